```python
import math
import jax, jax.numpy as jnp
from jax import lax
import numpy as np

D_MODEL = 2048
BATCH = 8
SEQ = 2048
DEPTH = 4
DEC_BATCH = 8
DEC_SEQ = 4096
PAST_LEN = 128

D_MIX = D_MODEL
SSD_WIDTH = D_MIX // 2
ATT_WIDTH = D_MIX // 4
HY_WIDTH = D_MIX - SSD_WIDTH - ATT_WIDTH

SSD_HEAD_DIM = 64
SSD_HEADS = SSD_WIDTH // SSD_HEAD_DIM
SSD_GROUPS = 2
SSD_STATE = 128
SSD_CONV = 5
SSD_CHUNK = 128
SSD_XBC = SSD_WIDTH + 2 * SSD_GROUPS * SSD_STATE

ATT_HEAD_DIM = 64
ATT_HEADS = ATT_WIDTH // ATT_HEAD_DIM
ATT_KV_HEADS = 2
ATT_WINDOW = 128
ATT_BLOCK = 128
REL_BUCKETS = 32
REL_MAX_DIST = 128

HY_CONV = 3
HY_EMB_BANDS = 16
HY_EMB = 1 + 2 * HY_EMB_BANDS
HY_FF = 64
HY_FAST_DECAY = 0.3
HY_SLOW_DECAY = 1.5
HY_DECAY_TARGET = 1e-2

EPS = 1e-6
IN_SIZES = [SSD_WIDTH, SSD_XBC, 2 * SSD_HEADS,
            ATT_WIDTH, ATT_KV_HEADS * ATT_HEAD_DIM, ATT_KV_HEADS * ATT_HEAD_DIM, ATT_WIDTH,
            3 * HY_WIDTH, HY_WIDTH]
IN_COLS = sum(IN_SIZES)
IN_SPLITS = [int(v) for v in np.cumsum(IN_SIZES)[:-1]]

kernel_name = 'hymba_ssd_window_hyena_encoder'

F32 = jnp.float32


def rms_norm(x, g):
    xf = x.astype(F32)
    y = xf * lax.rsqrt(jnp.mean(xf * xf, axis=-1, keepdims=True) + EPS)
    return (y * g.astype(F32)).astype(x.dtype)


def centred_dwconv(x, w, b):
    width = w.shape[0]
    y = lax.conv_general_dilated(x, w[:, None, :].astype(x.dtype), window_strides=(1,),
                                 padding=[(width // 2, width // 2)],
                                 dimension_numbers=('NWC', 'WIO', 'NWC'),
                                 feature_group_count=x.shape[-1])
    return y + b.astype(x.dtype)


def ssd_scan(x, dt, a, bm, cm):
    bsz, L, H, P = x.shape
    G, N = bm.shape[2], bm.shape[3]
    E = H // G
    nc = L // SSD_CHUNK
    x = x.reshape(bsz, nc, SSD_CHUNK, G, E, P)
    dt = dt.reshape(bsz, nc, SSD_CHUNK, G, E)
    bm = bm.reshape(bsz, nc, SSD_CHUNK, G, N)
    cm = cm.reshape(bsz, nc, SSD_CHUNK, G, N)
    xdt = x * dt[..., None]
    a_cs = jnp.cumsum(jnp.transpose(dt * a.reshape(G, E), (0, 1, 3, 4, 2)), axis=-1)
    seg = a_cs[..., :, None] - a_cs[..., None, :]
    tril = np.tril(np.ones((SSD_CHUNK, SSD_CHUNK), dtype=bool))
    lmat = jnp.exp(jnp.where(tril, seg, -jnp.inf))
    cb = jnp.einsum('bclgn,bcsgn->bcgls', cm, bm)
    y_diag = jnp.einsum('bcgels,bcsgep->bclgep', cb[:, :, :, None] * lmat, xdt)
    decay_states = jnp.exp(a_cs[..., -1:] - a_cs)
    states = jnp.einsum('bclgn,bcgel,bclgep->bcgepn', bm, decay_states, xdt)
    chunk_decay = jnp.exp(a_cs[..., -1])

    def step(h, inp):
        s, d = inp
        return h * d[..., None, None] + s, h

    h0 = jnp.zeros((bsz, G, E, P, N), F32)
    _, prev = lax.scan(step, h0, (jnp.moveaxis(states, 1, 0), jnp.moveaxis(chunk_decay, 1, 0)))
    prev = jnp.moveaxis(prev, 0, 1)
    y_off = jnp.einsum('bclgn,bcgepn,bcgel->bclgep', cm, prev, jnp.exp(a_cs))
    return (y_diag + y_off).reshape(bsz, L, H, P)


def ssd_branch(z, xbc, dt_raw, conv_w, conv_b, dt_bias, a_log, d_skip, norm_g):
    bsz, L, _ = xbc.shape
    xbc = jax.nn.silu(centred_dwconv(xbc, conv_w, conv_b)).astype(F32)
    xs, bm, cm = jnp.split(xbc, [SSD_WIDTH, SSD_WIDTH + SSD_GROUPS * SSD_STATE], axis=-1)
    xs = xs.reshape(bsz, L, SSD_HEADS, SSD_HEAD_DIM)
    bm = bm.reshape(bsz, L, SSD_GROUPS, SSD_STATE)
    cm = cm.reshape(bsz, L, SSD_GROUPS, SSD_STATE)
    dt = jax.nn.softplus(dt_raw.astype(F32).reshape(bsz, L, 2, SSD_HEADS) + dt_bias.astype(F32))
    a = -jnp.exp(a_log.astype(F32))
    flip = lambda t: jnp.flip(t, axis=1)
    y_f = ssd_scan(xs, dt[:, :, 0], a[0], bm, cm)
    y_b = flip(ssd_scan(flip(xs), flip(dt[:, :, 1]), a[1], flip(bm), flip(cm)))
    y = (y_f + y_b + xs * d_skip.astype(F32)[:, None]).reshape(bsz, L, SSD_WIDTH)
    return rms_norm(y * jax.nn.silu(z.astype(F32)), norm_g).astype(z.dtype)


def t5_buckets(rel):
    nb = REL_BUCKETS // 2
    max_exact = nb // 2
    ret = (rel > 0).astype(np.int32) * nb
    n = np.abs(rel)
    large = max_exact + (np.log(np.maximum(n, 1) / max_exact) / math.log(REL_MAX_DIST / max_exact)
                         * (nb - max_exact)).astype(np.int32)
    large = np.minimum(large, nb - 1)
    return ret + np.where(n < max_exact, n, large)


def attention_branch(q, k, v, gate, rel_bias, sink, norm_g):
    bsz, L, _ = q.shape
    nb = L // ATT_BLOCK
    R = ATT_HEADS // ATT_KV_HEADS
    qb = q.reshape(bsz, nb, ATT_BLOCK, ATT_KV_HEADS, R, ATT_HEAD_DIM)

    def band(t):
        t = t.reshape(bsz, L, ATT_KV_HEADS, ATT_HEAD_DIM)
        t = jnp.pad(t, ((0, 0), (ATT_BLOCK, ATT_BLOCK), (0, 0), (0, 0)))
        t = t.reshape(bsz, nb + 2, ATT_BLOCK, ATT_KV_HEADS, ATT_HEAD_DIM)
        return jnp.concatenate([t[:, :-2], t[:, 1:-1], t[:, 2:]], axis=2)

    kb, vb = band(k), band(v)
    qi = np.arange(ATT_BLOCK)[:, None]
    kj = np.arange(3 * ATT_BLOCK)[None, :]
    rel = kj - ATT_BLOCK - qi
    bias = rel_bias.astype(F32)[t5_buckets(rel)]
    bias = jnp.transpose(bias, (2, 0, 1)).reshape(ATT_KV_HEADS, R, ATT_BLOCK, 3 * ATT_BLOCK)
    kpos = (np.arange(nb)[:, None, None] - 1) * ATT_BLOCK + kj[None]
    mask = (np.abs(rel) <= ATT_WINDOW)[None] & (kpos >= 0) & (kpos < L)
    s = jnp.einsum('bnqgrd,bnkgd->bngrqk', qb, kb, preferred_element_type=F32) * (ATT_HEAD_DIM ** -0.5) + bias
    s = jnp.where(mask[None, :, None, None], s, -jnp.inf)
    sk = sink.astype(F32).reshape(ATT_KV_HEADS, R)[None, None, :, :, None, None]
    m = jnp.maximum(jnp.max(s, axis=-1, keepdims=True), sk)
    p = jnp.exp(s - m)
    den = jnp.sum(p, axis=-1, keepdims=True) + jnp.exp(sk - m)
    o = jnp.einsum('bngrqk,bnkgd->bnqgrd', (p / den).astype(vb.dtype), vb)
    o = o.reshape(bsz, L, ATT_WIDTH).astype(F32)
    return rms_norm(o * jax.nn.silu(gate.astype(F32)), norm_g).astype(gate.dtype)


def hyena_filters(L, w1, b1, w2, b2, w3, b3, w4, freq):
    t = jnp.linspace(0.0, 1.0, L, dtype=F32)[:, None]
    pos = jnp.arange(L, dtype=F32)[:, None]
    bands = jnp.linspace(1e-4, HY_EMB_BANDS - 1, HY_EMB_BANDS, dtype=F32)[None, :]
    ang = 2.0 * math.pi * pos * bands / L
    zemb = jnp.concatenate([t, jnp.cos(ang), -jnp.sin(ang)], axis=-1)
    fr = freq.astype(F32)
    h = jnp.sin(fr * (zemb @ w1.astype(F32) + b1.astype(F32)))
    h = jnp.sin(fr * (h @ w2.astype(F32) + b2.astype(F32)))
    h = jnp.sin(fr * (h @ w3.astype(F32) + b3.astype(F32)))
    h = h @ w4.astype(F32)
    max_decay = math.log(HY_DECAY_TARGET) / HY_FAST_DECAY
    min_decay = math.log(HY_DECAY_TARGET) / HY_SLOW_DECAY
    deltas = jnp.linspace(min_decay, max_decay, HY_WIDTH, dtype=F32)
    decay = jnp.exp(-t * jnp.abs(deltas))
    h = h.reshape(L, 2, HY_WIDTH) * decay[:, None, :]
    return h[:, 0], h[:, 1]


def hyena_branch(proj, gate, conv_w, conv_b, w1, b1, w2, b2, w3, b3, w4, freq, d_bias, norm_g):
    bsz, L, _ = proj.shape
    proj = centred_dwconv(proj, conv_w, conv_b).astype(F32)
    xa, xb, v = jnp.split(proj, 3, axis=-1)
    h_f, h_b = hyena_filters(L, w1, b1, w2, b2, w3, b3, w4, freq)
    kern = jnp.concatenate([h_f, jnp.zeros((1, HY_WIDTH), F32), h_b[1:][::-1]], axis=0)
    u = xb * v
    y = jnp.fft.irfft(jnp.fft.rfft(u, n=2 * L, axis=1) * jnp.fft.rfft(kern, axis=0)[None],
                      n=2 * L, axis=1)[:, :L]
    y = xa * (y + u * d_bias.astype(F32))
    return rms_norm(y * jax.nn.silu(gate.astype(F32)), norm_g).astype(gate.dtype)


def encoder_layer(x, rel_bias, norm_g, w_in, ssd_conv_w, ssd_conv_b, ssd_dt_bias, ssd_a_log, ssd_d,
                  ssd_norm_g, att_sink, att_norm_g, hy_conv_w, hy_conv_b, hy_w1, hy_b1, hy_w2, hy_b2,
                  hy_w3, hy_b3, hy_w4, hy_freq, hy_d, hy_norm_g, w_out):
    h = rms_norm(x, norm_g)
    u = h @ w_in.astype(h.dtype)
    z, xbc, dt_raw, q, k, v, g_att, hy_proj, g_hy = jnp.split(u, IN_SPLITS, axis=-1)
    y_ssd = ssd_branch(z, xbc, dt_raw, ssd_conv_w, ssd_conv_b, ssd_dt_bias, ssd_a_log, ssd_d, ssd_norm_g)
    y_att = attention_branch(q, k, v, g_att, rel_bias, att_sink, att_norm_g)
    y_hy = hyena_branch(hy_proj, g_hy, hy_conv_w, hy_conv_b, hy_w1, hy_b1, hy_w2, hy_b2, hy_w3, hy_b3,
                        hy_w4, hy_freq, hy_d, hy_norm_g)
    y = jnp.concatenate([y_ssd, y_att, y_hy], axis=-1) @ w_out.astype(h.dtype)
    return x + y.astype(x.dtype)


def setup_inputs(seed: int = 0) -> dict:
    key = jax.random.key(seed)
    ks = jax.random.split(key, 32)

    def nrm(k, shape, scale):
        return jax.random.normal(k, shape, F32) * scale

    dt0 = jnp.exp(jax.random.uniform(ks[7], (DEPTH, 2, SSD_HEADS), F32, math.log(1e-3), math.log(1e-1)))
    return {
        'x_prompt': nrm(ks[0], (BATCH, SEQ, D_MODEL), 1.0),
        'x_sample': nrm(ks[1], (DEC_BATCH, DEC_SEQ, D_MODEL), 1.0),
        'rel_bias': nrm(ks[2], (REL_BUCKETS, ATT_HEADS), 0.5),
        'norm_g': 1.0 + nrm(ks[3], (DEPTH, D_MODEL), 0.02),
        'w_in': nrm(ks[4], (DEPTH, D_MODEL, IN_COLS), D_MODEL ** -0.5),
        'ssd_conv_w': nrm(ks[5], (DEPTH, SSD_CONV, SSD_XBC), SSD_CONV ** -0.5),
        'ssd_conv_b': nrm(ks[6], (DEPTH, SSD_XBC), 0.02),
        'ssd_dt_bias': dt0 + jnp.log(-jnp.expm1(-dt0)),
        'ssd_a_log': jnp.log(jax.random.uniform(ks[8], (DEPTH, 2, SSD_HEADS), F32, 1.0, 16.0)),
        'ssd_d': 1.0 + nrm(ks[9], (DEPTH, SSD_HEADS), 0.1),
        'ssd_norm_g': 1.0 + nrm(ks[10], (DEPTH, SSD_WIDTH), 0.02),
        'att_sink': nrm(ks[11], (DEPTH, ATT_HEADS), 0.5),
        'att_norm_g': 1.0 + nrm(ks[12], (DEPTH, ATT_WIDTH), 0.02),
        'hy_conv_w': nrm(ks[13], (DEPTH, HY_CONV, 3 * HY_WIDTH), HY_CONV ** -0.5),
        'hy_conv_b': nrm(ks[14], (DEPTH, 3 * HY_WIDTH), 0.02),
        'hy_w1': nrm(ks[15], (DEPTH, HY_EMB, HY_FF), HY_EMB ** -0.5),
        'hy_b1': nrm(ks[16], (DEPTH, HY_FF), 0.02),
        'hy_w2': nrm(ks[17], (DEPTH, HY_FF, HY_FF), HY_FF ** -0.5),
        'hy_b2': nrm(ks[18], (DEPTH, HY_FF), 0.02),
        'hy_w3': nrm(ks[19], (DEPTH, HY_FF, HY_FF), HY_FF ** -0.5),
        'hy_b3': nrm(ks[20], (DEPTH, HY_FF), 0.02),
        'hy_w4': nrm(ks[21], (DEPTH, HY_FF, 2 * HY_WIDTH), HY_FF ** -0.5),
        'hy_freq': 1.0 + nrm(ks[22], (DEPTH, HY_FF), 0.1),
        'hy_d': nrm(ks[23], (DEPTH, HY_WIDTH), 1.0),
        'hy_norm_g': 1.0 + nrm(ks[24], (DEPTH, HY_WIDTH), 0.02),
        'w_out': nrm(ks[25], (DEPTH, D_MIX, D_MODEL), D_MIX ** -0.5),
        'final_norm_g': 1.0 + nrm(ks[26], (D_MODEL,), 0.02),
    }


def reference(x_prompt, x_sample, rel_bias, norm_g, w_in, ssd_conv_w, ssd_conv_b, ssd_dt_bias, ssd_a_log,
              ssd_d, ssd_norm_g, att_sink, att_norm_g, hy_conv_w, hy_conv_b, hy_w1, hy_b1, hy_w2, hy_b2,
              hy_w3, hy_b3, hy_w4, hy_freq, hy_d, hy_norm_g, w_out, final_norm_g):
    def trunk(x):
        for i in range(DEPTH):
            x = encoder_layer(x, rel_bias, norm_g[i], w_in[i], ssd_conv_w[i], ssd_conv_b[i], ssd_dt_bias[i],
                              ssd_a_log[i], ssd_d[i], ssd_norm_g[i], att_sink[i], att_norm_g[i],
                              hy_conv_w[i], hy_conv_b[i], hy_w1[i], hy_b1[i], hy_w2[i], hy_b2[i],
                              hy_w3[i], hy_b3[i], hy_w4[i], hy_freq[i], hy_d[i], hy_norm_g[i], w_out[i])
        return rms_norm(x, final_norm_g)

    y_prompt = trunk(x_prompt)
    y_sample = trunk(x_sample)
    return (y_prompt, y_sample)
```

```python
import functools
import math

import jax
import jax.numpy as jnp
import numpy as np
from jax import lax
from jax.experimental import pallas as pl
from jax.experimental.pallas import tpu as pltpu

F32 = jnp.float32
BF16 = jnp.bfloat16
HIGHEST = lax.Precision.HIGHEST

D_MODEL = 2048
SSD_WIDTH = 1024
ATT_WIDTH = 512
HY_WIDTH = 512
SSD_HEAD_DIM = 64
SSD_HEADS = 16
SSD_GROUPS = 2
SSD_STATE = 128
SSD_CONV = 5
CHUNK = 128
SSD_XBC = SSD_WIDTH + 2 * SSD_GROUPS * SSD_STATE
ATT_HEAD_DIM = 64
ATT_HEADS = 8
ATT_KV_HEADS = 2
ATT_REP = ATT_HEADS // ATT_KV_HEADS
ATT_WINDOW = 128
ATT_BLOCK = 128
REL_BUCKETS = 32
REL_MAX_DIST = 128
HY_CONV = 3
HY_EMB_BANDS = 16
HY_FF = 64
HY_FAST_DECAY = 0.3
HY_SLOW_DECAY = 1.5
HY_DECAY_TARGET = 1e-2
EPS = 1e-6
NEG_BIG = -1e30

LANES = 128
BF16_SUBLANES = 16
VMEM_LIMIT = 56 * 1024 * 1024

C_XBC = 0
C_HY = 1536
C_Z = 3072
C_Q = 4096
C_GATT = 4608
C_GHY = 5120
C_K = 5632
C_V = 5760
C_DT = 5888
N_IN = 6144
DT_PAD = 128

_OLD_SIZES = [SSD_WIDTH, SSD_XBC, 2 * SSD_HEADS, ATT_WIDTH, ATT_KV_HEADS * ATT_HEAD_DIM,
              ATT_KV_HEADS * ATT_HEAD_DIM, ATT_WIDTH, 3 * HY_WIDTH, HY_WIDTH]
_OLD_OFF = np.concatenate([[0], np.cumsum(_OLD_SIZES)])
IN_COLS = int(_OLD_OFF[-1])


def _in_perm():
    perm = np.full((N_IN,), IN_COLS, np.int32)
    o = {n: int(_OLD_OFF[i]) for i, n in enumerate(
        ["z", "xbc", "dt", "q", "k", "v", "gatt", "hy", "ghy"])}
    def put(new, old, width):
        perm[new:new + width] = np.arange(old, old + width)
    put(C_XBC, o["xbc"], SSD_XBC)
    put(C_HY, o["hy"], 3 * HY_WIDTH)
    put(C_Z, o["z"], SSD_WIDTH)
    put(C_Q, o["q"], ATT_WIDTH)
    put(C_GATT, o["gatt"], ATT_WIDTH)
    put(C_GHY, o["ghy"], HY_WIDTH)
    put(C_K, o["k"], 128)
    put(C_V, o["v"], 128)
    put(C_DT, o["dt"], 2 * SSD_HEADS)
    return perm


def _cparams(sem):
    return pltpu.CompilerParams(dimension_semantics=sem, vmem_limit_bytes=VMEM_LIMIT)


def _silu(x):
    return x * (1.0 / (1.0 + jnp.exp(-x)))


def _softplus(x):
    return jnp.maximum(x, 0.0) + jnp.log1p(jnp.exp(-jnp.abs(x)))


def _split_dot(a, b):
    hi = a.astype(BF16)
    lo = (a - hi.astype(F32)).astype(BF16)
    return (jnp.dot(hi, b, preferred_element_type=F32)
            + jnp.dot(lo, b, preferred_element_type=F32))


def _inproj_kernel(x_ref, g_ref, w_ref, u_ref, dt_ref, h_ref, *, rows, dt_tile, dt_off):
    j = pl.program_id(1)

    @pl.when(j == 0)
    def _():
        def body(r, carry):
            sl = pl.ds(pl.multiple_of(r * rows, rows), rows)
            x = x_ref[sl, :]
            ms = jnp.mean(x * x, axis=-1, keepdims=True)
            h_ref[sl, :] = (x * lax.rsqrt(ms + EPS) * g_ref[...]).astype(BF16)
            return carry
        lax.fori_loop(0, x_ref.shape[0] // rows, body, 0)

    acc = jnp.dot(h_ref[...], w_ref[...], preferred_element_type=F32)
    u_ref[...] = acc.astype(BF16)

    @pl.when(j == dt_tile)
    def _():
        dt_ref[...] = acc[:, dt_off:dt_off + DT_PAD]


def _inproj(x2, g, w, *, tm, tn):
    T = x2.shape[0]
    assert T % tm == 0 and N_IN % tn == 0
    kern = functools.partial(_inproj_kernel, rows=min(256, tm), dt_tile=C_DT // tn, dt_off=C_DT % tn)
    return pl.pallas_call(
        kern,
        grid=(T // tm, N_IN // tn),
        in_specs=[pl.BlockSpec((tm, D_MODEL), lambda i, j: (i, 0)),
                  pl.BlockSpec((1, D_MODEL), lambda i, j: (0, 0)),
                  pl.BlockSpec((D_MODEL, tn), lambda i, j: (0, j))],
        out_specs=[pl.BlockSpec((tm, tn), lambda i, j: (i, j)),
                   pl.BlockSpec((tm, DT_PAD), lambda i, j: (i, 0))],
        out_shape=[jax.ShapeDtypeStruct((T, N_IN), BF16),
                   jax.ShapeDtypeStruct((T, DT_PAD), F32)],
        scratch_shapes=[pltpu.VMEM((tm, D_MODEL), BF16)],
        compiler_params=_cparams(("parallel", "arbitrary")),
        name="inproj",
    )(x2, g, w)


def _outproj_kernel(x_ref, ys_ref, ya_ref, yh_ref, w_ref, fg_ref, o_ref, *, final):
    acc = x_ref[...]
    acc = acc + jnp.dot(ys_ref[...], w_ref[0:SSD_WIDTH, :], preferred_element_type=F32)
    acc = acc + jnp.dot(ya_ref[...], w_ref[SSD_WIDTH:SSD_WIDTH + ATT_WIDTH, :],
                        preferred_element_type=F32)
    acc = acc + jnp.dot(yh_ref[...], w_ref[SSD_WIDTH + ATT_WIDTH:, :], preferred_element_type=F32)
    if final:
        ms = jnp.mean(acc * acc, axis=-1, keepdims=True)
        acc = acc * lax.rsqrt(ms + EPS) * fg_ref[...]
    o_ref[...] = acc


def _outproj(x2, ys, ya, yh, w, fg, *, tm, final):
    T = x2.shape[0]
    assert T % tm == 0
    return pl.pallas_call(
        functools.partial(_outproj_kernel, final=final),
        grid=(T // tm,),
        in_specs=[pl.BlockSpec((tm, D_MODEL), lambda i: (i, 0)),
                  pl.BlockSpec((tm, SSD_WIDTH), lambda i: (i, 0)),
                  pl.BlockSpec((tm, ATT_WIDTH), lambda i: (i, 0)),
                  pl.BlockSpec((tm, HY_WIDTH), lambda i: (i, 0)),
                  pl.BlockSpec((D_MODEL, D_MODEL), lambda i: (0, 0)),
                  pl.BlockSpec((1, D_MODEL), lambda i: (0, 0))],
        out_specs=pl.BlockSpec((tm, D_MODEL), lambda i: (i, 0)),
        out_shape=jax.ShapeDtypeStruct((T, D_MODEL), F32),
        compiler_params=_cparams(("parallel",)),
        name="outproj",
    )(x2, ys, ya, yh, w, fg)


def _t5_buckets(rel):
    nb = REL_BUCKETS // 2
    max_exact = nb // 2
    ret = (rel > 0).astype(np.int32) * nb
    n = np.abs(rel)
    large = max_exact + (np.log(np.maximum(n, 1) / max_exact) / math.log(REL_MAX_DIST / max_exact)
                         * (nb - max_exact)).astype(np.int32)
    large = np.minimum(large, nb - 1)
    return ret + np.where(n < max_exact, n, large)


def _attn_bias(rel_bias):
    qi = np.arange(ATT_BLOCK)[:, None]
    kj = np.arange(3 * ATT_BLOCK)[None, :]
    rel = kj - ATT_BLOCK - qi
    bias = rel_bias.astype(F32)[_t5_buckets(rel)]
    bias = jnp.transpose(bias, (2, 0, 1))
    return jnp.where((np.abs(rel) <= ATT_WINDOW)[None], bias, NEG_BIG)


def _attn_kernel(q_ref, kp_ref, kc_ref, kn_ref, vp_ref, vc_ref, vn_ref, g_ref, bias_ref, sink_ref,
                 ng_ref, o_ref, *, nb):
    n = pl.program_id(1)
    q = q_ref[0]
    kcat = jnp.concatenate([kp_ref[0], kc_ref[0], kn_ref[0]], axis=0)
    vcat = jnp.concatenate([vp_ref[0], vc_ref[0], vn_ref[0]], axis=0)
    col = lax.broadcasted_iota(jnp.int32, (1, 3 * ATT_BLOCK), 1)
    invalid = ((n == 0) & (col < ATT_BLOCK)) | ((n == nb - 1) & (col >= 2 * ATT_BLOCK))
    scale = ATT_HEAD_DIM ** -0.5
    outs = []
    for h in range(ATT_HEADS):
        g = h // ATT_REP
        qh = q[:, h * ATT_HEAD_DIM:(h + 1) * ATT_HEAD_DIM]
        kg = kcat[:, g * ATT_HEAD_DIM:(g + 1) * ATT_HEAD_DIM]
        vg = vcat[:, g * ATT_HEAD_DIM:(g + 1) * ATT_HEAD_DIM]
        s = lax.dot_general(qh, kg, (((1,), (1,)), ((), ())), preferred_element_type=F32)
        s = s * scale + bias_ref[h]
        s = jnp.where(invalid, NEG_BIG, s)
        sk = sink_ref[h:h + 1, 0:1]
        m = jnp.maximum(jnp.max(s, axis=-1, keepdims=True), sk)
        p = jnp.exp(s - m)
        den = jnp.sum(p, axis=-1, keepdims=True) + jnp.exp(sk - m)
        o = jnp.dot(p.astype(BF16), vg, preferred_element_type=F32)
        outs.append(o * (1.0 / den))
    o = jnp.concatenate(outs, axis=-1)
    y = o * _silu(g_ref[0].astype(F32))
    ms = jnp.mean(y * y, axis=-1, keepdims=True)
    o_ref[0] = (y * lax.rsqrt(ms + EPS) * ng_ref[...]).astype(BF16)


def _attention(u3, bias, sink, ng):
    B, L, _ = u3.shape
    nb = L // ATT_BLOCK
    kcol, vcol = C_K // 128, C_V // 128
    def kv_spec(colblk, off):
        return pl.BlockSpec((1, ATT_BLOCK, 128),
                            lambda b, n: (b, jnp.clip(n + off, 0, nb - 1), colblk))
    return pl.pallas_call(
        functools.partial(_attn_kernel, nb=nb),
        grid=(B, nb),
        in_specs=[pl.BlockSpec((1, ATT_BLOCK, ATT_WIDTH), lambda b, n: (b, n, C_Q // ATT_WIDTH)),
                  kv_spec(kcol, -1), kv_spec(kcol, 0), kv_spec(kcol, 1),
                  kv_spec(vcol, -1), kv_spec(vcol, 0), kv_spec(vcol, 1),
                  pl.BlockSpec((1, ATT_BLOCK, ATT_WIDTH), lambda b, n: (b, n, C_GATT // ATT_WIDTH)),
                  pl.BlockSpec((ATT_HEADS, ATT_BLOCK, 3 * ATT_BLOCK), lambda b, n: (0, 0, 0)),
                  pl.BlockSpec((ATT_HEADS, LANES), lambda b, n: (0, 0)),
                  pl.BlockSpec((1, ATT_WIDTH), lambda b, n: (0, 0))],
        out_specs=pl.BlockSpec((1, ATT_BLOCK, ATT_WIDTH), lambda b, n: (b, n, 0)),
        out_shape=jax.ShapeDtypeStruct((B, L, ATT_WIDTH), BF16),
        compiler_params=_cparams(("parallel", "parallel")),
        name="attn",
    )(u3, u3, u3, u3, u3, u3, u3, u3, bias, sink, ng)


HALO = BF16_SUBLANES


def _load_padded(xpad_ref, xm_ref, xp_ref, xn_ref, has_prev, has_next, rows):
    xpad_ref[pl.ds(0, HALO), :] = jnp.where(has_prev, xp_ref[0].astype(F32), 0.0)
    xpad_ref[pl.ds(HALO, rows), :] = xm_ref[0].astype(F32)
    xpad_ref[pl.ds(HALO + rows, HALO), :] = jnp.where(has_next, xn_ref[0].astype(F32), 0.0)


def _dwconv(xpad_ref, w_ref, b_ref, rows, width, c0, c1):
    acc = None
    for k in range(width):
        t = xpad_ref[pl.ds(HALO - width // 2 + k, rows), c0:c1] * w_ref[k:k + 1, c0:c1]
        acc = t if acc is None else acc + t
    return acc + b_ref[:, c0:c1]


def _ssd_kernel(*refs, rev, nc, final):
    if final:
        (xm_ref, xp_ref, xn_ref, dt_ref, cw_ref, cb_ref, dtb_ref, alog_ref, rexp_ref,
         z_ref, yb_ref, dsk_ref, ng_ref, o_ref, xpad_ref, st_ref) = refs
    else:
        (xm_ref, xp_ref, xn_ref, dt_ref, cw_ref, cb_ref, dtb_ref, alog_ref, rexp_ref,
         o_ref, xpad_ref, st_ref) = refs
    c = pl.program_id(1)
    cc = (nc - 1 - c) if rev else c

    @pl.when(c == 0)
    def _():
        st_ref[...] = jnp.zeros(st_ref.shape, F32)

    _load_padded(xpad_ref, xm_ref, xp_ref, xn_ref, cc > 0, cc < nc - 1, CHUNK)
    xact = _silu(_dwconv(xpad_ref, cw_ref, cb_ref, CHUNK, SSD_CONV, 0, SSD_XBC))
    xs = xact[:, :SSD_WIDTH]
    nbc = SSD_GROUPS * SSD_STATE
    bm = xact[:, SSD_WIDTH:SSD_WIDTH + nbc].astype(BF16)
    cm = xact[:, SSD_WIDTH + nbc:].astype(BF16)

    dt = _softplus(dt_ref[0] + dtb_ref[...])
    dta = dt * (-jnp.exp(alog_ref[...]))
    row = lax.broadcasted_iota(jnp.int32, (CHUNK, CHUNK), 0)
    colm = lax.broadcasted_iota(jnp.int32, (CHUNK, CHUNK), 1)
    causal = (row <= colm) if rev else (row >= colm)
    acs = jnp.dot(causal.astype(F32), dta, preferred_element_type=F32, precision=HIGHEST)
    tot = acs[0:1, :] if rev else acs[CHUNK - 1:CHUNK, :]
    acs_t = acs.T
    dt_t = dt.T
    e_in = jnp.concatenate([jnp.exp(tot - acs) * dt, jnp.exp(acs),
                            jnp.broadcast_to(jnp.exp(tot), (8, LANES))], axis=0)
    e_out = _split_dot(e_in, rexp_ref[...])
    ds_exp = e_out[0:CHUNK]
    eacs_exp = e_out[CHUNK:2 * CHUNK]
    cdec_exp = e_out[2 * CHUNK:2 * CHUNK + 1]

    hoff = SSD_HEADS if rev else 0
    heads_per_group = SSD_HEADS // SSD_GROUPS
    gw = heads_per_group * SSD_HEAD_DIM
    xs_b = xs.astype(BF16)
    xd_b = (xs * ds_exp).astype(BF16)
    lane = lax.broadcasted_iota(jnp.int32, (CHUNK, LANES), 1)
    ys = []
    for g in range(SSD_GROUPS):
        bg = bm[:, g * SSD_STATE:(g + 1) * SSD_STATE]
        cg = cm[:, g * SSD_STATE:(g + 1) * SSD_STATE]
        cb = lax.dot_general(cg, bg, (((1,), (1,)), ((), ())), preferred_element_type=F32)
        prev = st_ref[g]
        y_off = jnp.dot(cg, prev.astype(BF16), preferred_element_type=F32)
        y_off = y_off * eacs_exp[:, g * gw:(g + 1) * gw]
        s_new = lax.dot_general(bg, xd_b[:, g * gw:(g + 1) * gw], (((0,), (0,)), ((), ())),
                                preferred_element_type=F32)
        st_ref[g] = prev * cdec_exp[:, g * gw:(g + 1) * gw] + s_new
        for pr in range(heads_per_group // 2):
            col0 = g * gw + pr * LANES
            xpair = xs_b[:, col0:col0 + LANES]
            halves = []
            for e in range(2):
                h = hoff + g * heads_per_group + pr * 2 + e
                seg = acs[:, h:h + 1] - acs_t[h:h + 1, :]
                lm = jnp.exp(jnp.where(causal, seg, -jnp.inf))
                mat = (cb * lm * dt_t[h:h + 1, :]).astype(BF16)
                halves.append(jnp.dot(mat, xpair, preferred_element_type=F32))
            yd = jnp.where(lane < SSD_HEAD_DIM, halves[0], halves[1])
            ys.append(yd + y_off[:, pr * LANES:(pr + 1) * LANES])
    y = jnp.concatenate(ys, axis=-1)

    if final:
        y = y + yb_ref[0].astype(F32) + xs * dsk_ref[...]
        y = y * _silu(z_ref[0].astype(F32))
        ms = jnp.mean(y * y, axis=-1, keepdims=True)
        o_ref[0] = (y * lax.rsqrt(ms + EPS) * ng_ref[...]).astype(BF16)
    else:
        o_ref[0] = y.astype(BF16)


def _ssd_sweep(u3, dt3, cw, cb, dtb, alog, rexp, *, rev, extra=None):
    B, L, _ = u3.shape
    nc = L // CHUNK
    hb = CHUNK // HALO
    nhb = L // HALO
    final = extra is not None

    def cmap(c):
        return (nc - 1 - c) if rev else c

    in_specs = [
        pl.BlockSpec((1, CHUNK, SSD_XBC), lambda b, c: (b, cmap(c), C_XBC // SSD_XBC)),
        pl.BlockSpec((1, HALO, SSD_XBC),
                     lambda b, c: (b, jnp.maximum(cmap(c) * hb - 1, 0), C_XBC // SSD_XBC)),
        pl.BlockSpec((1, HALO, SSD_XBC),
                     lambda b, c: (b, jnp.minimum((cmap(c) + 1) * hb, nhb - 1), C_XBC // SSD_XBC)),
        pl.BlockSpec((1, CHUNK, DT_PAD), lambda b, c: (b, cmap(c), 0)),
        pl.BlockSpec((8, SSD_XBC), lambda b, c: (0, 0)),
        pl.BlockSpec((1, SSD_XBC), lambda b, c: (0, 0)),
        pl.BlockSpec((1, DT_PAD), lambda b, c: (0, 0)),
        pl.BlockSpec((1, DT_PAD), lambda b, c: (0, 0)),
        pl.BlockSpec((LANES, SSD_WIDTH), lambda b, c: (0, 0)),
    ]
    args = [u3, u3, u3, dt3, cw, cb, dtb, alog, rexp]
    if final:
        yb, dsk, ng = extra
        in_specs += [
            pl.BlockSpec((1, CHUNK, SSD_WIDTH), lambda b, c: (b, cmap(c), C_Z // SSD_WIDTH)),
            pl.BlockSpec((1, CHUNK, SSD_WIDTH), lambda b, c: (b, cmap(c), 0)),
            pl.BlockSpec((1, SSD_WIDTH), lambda b, c: (0, 0)),
            pl.BlockSpec((1, SSD_WIDTH), lambda b, c: (0, 0)),
        ]
        args += [u3, yb, dsk, ng]
    return pl.pallas_call(
        functools.partial(_ssd_kernel, rev=rev, nc=nc, final=final),
        grid=(B, nc),
        in_specs=in_specs,
        out_specs=pl.BlockSpec((1, CHUNK, SSD_WIDTH), lambda b, c: (b, cmap(c), 0)),
        out_shape=jax.ShapeDtypeStruct((B, L, SSD_WIDTH), BF16),
        scratch_shapes=[pltpu.VMEM((CHUNK + 2 * HALO, SSD_XBC), F32),
                        pltpu.VMEM((SSD_GROUPS, SSD_STATE, SSD_WIDTH // SSD_GROUPS), F32)],
        compiler_params=_cparams(("parallel", "arbitrary")),
        name="ssd_bwd" if rev else "ssd_fwd",
    )(*args)


def _ssd_expand_matrix(rev):
    m = np.zeros((LANES, SSD_WIDTH), np.float32)
    hoff = SSD_HEADS if rev else 0
    for h in range(SSD_HEADS):
        m[hoff + h, h * SSD_HEAD_DIM:(h + 1) * SSD_HEAD_DIM] = 1.0
    return jnp.asarray(m, BF16)


def _hy_filter_kernel(t_ref, bands_ref, w1t_ref, w1c_ref, w1s_ref, b1_ref, w2_ref, b2_ref, w3_ref,
                      b3_ref, w4_ref, fr_ref, absd_ref, x_ref, *, L, tl):
    i = pl.program_id(0)
    pos = (lax.broadcasted_iota(jnp.int32, (tl, 1), 0) + i * tl).astype(F32)
    t = t_ref[...]
    ang = 2.0 * math.pi * pos * bands_ref[...] / L
    fr = fr_ref[...]
    dot = functools.partial(jnp.dot, preferred_element_type=F32, precision=HIGHEST)
    pre = t * w1t_ref[...] + dot(jnp.cos(ang), w1c_ref[...]) + dot(-jnp.sin(ang), w1s_ref[...])
    h = jnp.sin(fr * (pre + b1_ref[...]))
    h = jnp.sin(fr * (dot(h, w2_ref[...]) + b2_ref[...]))
    h = jnp.sin(fr * (dot(h, w3_ref[...]) + b3_ref[...]))
    h = dot(h, w4_ref[...])
    decay = jnp.exp(-t * absd_ref[...])
    hf = h[:, :HY_WIDTH] * decay
    hb = jnp.where(pos == 0.0, 0.0, h[:, HY_WIDTH:] * decay)
    x_ref[:, :HY_WIDTH] = (hf + hb).astype(BF16)
    x_ref[:, HY_WIDTH:] = (hb - hf).astype(BF16)


def _hy_filter(L, w1, b1, w2, b2, w3, b3, w4, freq):
    tl = min(512, L)
    t = jnp.linspace(0.0, 1.0, L, dtype=F32)[:, None]
    bands = jnp.linspace(1e-4, HY_EMB_BANDS - 1, HY_EMB_BANDS, dtype=F32)[None, :]
    max_decay = math.log(HY_DECAY_TARGET) / HY_FAST_DECAY
    min_decay = math.log(HY_DECAY_TARGET) / HY_SLOW_DECAY
    absd = jnp.abs(jnp.linspace(min_decay, max_decay, HY_WIDTH, dtype=F32))[None, :]
    w1 = w1.astype(F32)
    full = lambda a: pl.BlockSpec(a.shape, lambda i: (0,) * a.ndim)
    args = [t, bands, w1[0:1], w1[1:1 + HY_EMB_BANDS], w1[1 + HY_EMB_BANDS:], b1[None], w2, b2[None],
            w3, b3[None], w4, freq[None], absd]
    in_specs = [pl.BlockSpec((tl, 1), lambda i: (i, 0))] + [full(a) for a in args[1:]]
    return pl.pallas_call(
        functools.partial(_hy_filter_kernel, L=L, tl=tl),
        grid=(L // tl,),
        in_specs=in_specs,
        out_specs=pl.BlockSpec((tl, 2 * HY_WIDTH), lambda i: (i, 0)),
        out_shape=jax.ShapeDtypeStruct((L, 2 * HY_WIDTH), BF16),
        compiler_params=_cparams(("parallel",)),
        name="hy_filter",
    )(*args)


def _dft_tables(L, hm):
    n = 2 * L
    f = jnp.arange(L, dtype=jnp.int32)
    s = jnp.arange(L, dtype=jnp.int32)
    ph = ((2 * f + 1)[:, None] * s[None, :]) % (2 * n)
    ang = ph.astype(F32) * (math.pi / n)
    c = jnp.cos(ang).astype(BF16).reshape(L // hm, 1, hm, L)
    sn = jnp.sin(ang).astype(BF16).reshape(L // hm, 1, hm, L)
    a_fwd = jnp.concatenate([c, sn], axis=1).reshape(2 * L, L)
    a_inv = jnp.concatenate([c, -sn], axis=1).reshape(2 * L, L).T
    return a_fwd, a_inv


def _hy_kspec_kernel(a_ref, x_ref, k_ref, *, hm, scale):
    pq = jnp.dot(a_ref[...], x_ref[...], preferred_element_type=F32)
    k_ref[0:hm, :] = pq[0:hm, :HY_WIDTH] * scale
    k_ref[hm:, :] = pq[hm:, HY_WIDTH:] * scale


def _hy_kspec(a_fwd, xf, *, hm):
    L = xf.shape[0]
    tm = 2 * hm
    return pl.pallas_call(
        functools.partial(_hy_kspec_kernel, hm=hm, scale=1.0 / L),
        grid=(2 * L // tm,),
        in_specs=[pl.BlockSpec((tm, L), lambda i: (i, 0)),
                  pl.BlockSpec((L, 2 * HY_WIDTH), lambda i: (0, 0))],
        out_specs=pl.BlockSpec((tm, HY_WIDTH), lambda i: (i, 0)),
        out_shape=jax.ShapeDtypeStruct((2 * L, HY_WIDTH), F32),
        compiler_params=_cparams(("parallel",)),
        name="hy_kspec",
    )(a_fwd, xf)


def _hy_pre_kernel(xm_ref, xp_ref, xn_ref, cw_ref, cb_ref, u_ref, xpad_ref, *, nt, tl):
    i = pl.program_id(1)
    _load_padded(xpad_ref, xm_ref, xp_ref, xn_ref, i > 0, i < nt - 1, tl)
    xb = _dwconv(xpad_ref, cw_ref, cb_ref, tl, HY_CONV, HY_WIDTH, 2 * HY_WIDTH)
    v = _dwconv(xpad_ref, cw_ref, cb_ref, tl, HY_CONV, 2 * HY_WIDTH, 3 * HY_WIDTH)
    u_ref[0] = (xb * v).astype(BF16)


def _hy_tile_specs(L, tl, order):
    hb = tl // HALO
    nhb = L // HALO
    w = 3 * HY_WIDTH
    if order == "bt":
        pick = lambda a, b: (a, b)
    else:
        pick = lambda a, b: (b, a)
    def main(a, b):
        bb, i = pick(a, b)
        return (bb, i, C_HY // w)
    def prev(a, b):
        bb, i = pick(a, b)
        return (bb, jnp.maximum(i * hb - 1, 0), C_HY // w)
    def nxt(a, b):
        bb, i = pick(a, b)
        return (bb, jnp.minimum((i + 1) * hb, nhb - 1), C_HY // w)
    return [pl.BlockSpec((1, tl, w), main), pl.BlockSpec((1, HALO, w), prev),
            pl.BlockSpec((1, HALO, w), nxt)]


def _hy_pre(u3, cw, cb, *, tl):
    B, L, _ = u3.shape
    nt = L // tl
    w = 3 * HY_WIDTH
    return pl.pallas_call(
        functools.partial(_hy_pre_kernel, nt=nt, tl=tl),
        grid=(B, nt),
        in_specs=_hy_tile_specs(L, tl, "bt") + [pl.BlockSpec((8, w), lambda b, i: (0, 0)),
                                                 pl.BlockSpec((1, w), lambda b, i: (0, 0))],
        out_specs=pl.BlockSpec((1, tl, HY_WIDTH), lambda b, i: (b, i, 0)),
        out_shape=jax.ShapeDtypeStruct((B, L, HY_WIDTH), BF16),
        scratch_shapes=[pltpu.VMEM((tl + 2 * HALO, w), F32)],
        compiler_params=_cparams(("parallel", "parallel")),
        name="hy_pre",
    )(u3, u3, u3, cw, cb)


def _hy_fwd_kernel(a_ref, u_ref, k_ref, y_ref, *, hm):
    pq = jnp.dot(a_ref[...], u_ref[0], preferred_element_type=F32)
    p, q = pq[0:hm], pq[hm:]
    kr, ki = k_ref[0:hm, :], k_ref[hm:, :]
    y_ref[0, 0:hm, :] = (p * kr + q * ki).astype(BF16)
    y_ref[0, hm:, :] = (p * ki - q * kr).astype(BF16)


def _hy_fwd(a_fwd, uc, kspec, *, hm):
    B, L, _ = uc.shape
    tm = 2 * hm
    return pl.pallas_call(
        functools.partial(_hy_fwd_kernel, hm=hm),
        grid=(2 * L // tm, B),
        in_specs=[pl.BlockSpec((tm, L), lambda i, b: (i, 0)),
                  pl.BlockSpec((1, L, HY_WIDTH), lambda i, b: (b, 0, 0)),
                  pl.BlockSpec((tm, HY_WIDTH), lambda i, b: (i, 0))],
        out_specs=pl.BlockSpec((1, tm, HY_WIDTH), lambda i, b: (b, i, 0)),
        out_shape=jax.ShapeDtypeStruct((B, 2 * L, HY_WIDTH), BF16),
        compiler_params=_cparams(("parallel", "parallel")),
        name="hy_fwd",
    )(a_fwd, uc, kspec)


def _hy_inv_kernel(a_ref, y_ref, xm_ref, xp_ref, xn_ref, g_ref, cw_ref, cb_ref, d_ref, ng_ref,
                   o_ref, xpad_ref, *, nt, tl):
    i = pl.program_id(0)
    conv = jnp.dot(a_ref[...], y_ref[0], preferred_element_type=F32)
    _load_padded(xpad_ref, xm_ref, xp_ref, xn_ref, i > 0, i < nt - 1, tl)
    xa = _dwconv(xpad_ref, cw_ref, cb_ref, tl, HY_CONV, 0, HY_WIDTH)
    xb = _dwconv(xpad_ref, cw_ref, cb_ref, tl, HY_CONV, HY_WIDTH, 2 * HY_WIDTH)
    v = _dwconv(xpad_ref, cw_ref, cb_ref, tl, HY_CONV, 2 * HY_WIDTH, 3 * HY_WIDTH)
    u = xb * v
    y = xa * (conv + u * d_ref[...])
    y = y * _silu(g_ref[0].astype(F32))
    ms = jnp.mean(y * y, axis=-1, keepdims=True)
    o_ref[0] = (y * lax.rsqrt(ms + EPS) * ng_ref[...]).astype(BF16)


def _hy_inv(a_inv, yspec, u3, cw, cb, d, ng, *, tl):
    B, L, _ = u3.shape
    nt = L // tl
    w = 3 * HY_WIDTH
    return pl.pallas_call(
        functools.partial(_hy_inv_kernel, nt=nt, tl=tl),
        grid=(nt, B),
        in_specs=[pl.BlockSpec((tl, 2 * L), lambda i, b: (i, 0)),
                  pl.BlockSpec((1, 2 * L, HY_WIDTH), lambda i, b: (b, 0, 0))]
                 + _hy_tile_specs(L, tl, "tb")
                 + [pl.BlockSpec((1, tl, HY_WIDTH), lambda i, b: (b, i, C_GHY // HY_WIDTH)),
                    pl.BlockSpec((8, w), lambda i, b: (0, 0)),
                    pl.BlockSpec((1, w), lambda i, b: (0, 0)),
                    pl.BlockSpec((1, HY_WIDTH), lambda i, b: (0, 0)),
                    pl.BlockSpec((1, HY_WIDTH), lambda i, b: (0, 0))],
        out_specs=pl.BlockSpec((1, tl, HY_WIDTH), lambda i, b: (b, i, 0)),
        out_shape=jax.ShapeDtypeStruct((B, L, HY_WIDTH), BF16),
        scratch_shapes=[pltpu.VMEM((tl + 2 * HALO, w), F32)],
        compiler_params=_cparams(("parallel", "parallel")),
        name="hy_inv",
    )(a_inv, yspec, u3, u3, u3, u3, cw, cb, d, ng)


def _pad_rows(a, rows):
    return jnp.concatenate([a, jnp.zeros((rows - a.shape[0],) + a.shape[1:], a.dtype)], axis=0)


def _pad_cols(a, cols):
    return jnp.concatenate([a, jnp.zeros(a.shape[:-1] + (cols - a.shape[-1],), a.dtype)], axis=-1)


def _tile(n, pref):
    t = min(pref, n)
    assert n % t == 0
    return t


def _layer(x3, p, tables, *, final, final_g):
    B, L, _ = x3.shape
    T = B * L
    x2 = x3.reshape(T, D_MODEL)
    u, dt = _inproj(x2, p["norm_g"], p["w_in"], tm=_tile(T, 1024), tn=512)
    u3 = u.reshape(B, L, N_IN)
    dt3 = dt.reshape(B, L, DT_PAD)

    yb = _ssd_sweep(u3, dt3, p["ssd_cw"], p["ssd_cb"], p["ssd_dtb"], p["ssd_alog"], p["rexp_b"],
                    rev=True)
    ys = _ssd_sweep(u3, dt3, p["ssd_cw"], p["ssd_cb"], p["ssd_dtb"], p["ssd_alog"], p["rexp_f"],
                    rev=False, extra=(yb, p["ssd_dskip"], p["ssd_ng"]))

    ya = _attention(u3, p["att_bias"], p["att_sink"], p["att_ng"])

    a_fwd, a_inv, kspec, hm = tables
    tl = _tile(L, 512)
    uc = _hy_pre(u3, p["hy_cw"], p["hy_cb"], tl=tl)
    yspec = _hy_fwd(a_fwd, uc, kspec, hm=hm)
    yh = _hy_inv(a_inv, yspec, u3, p["hy_cw"], p["hy_cb"], p["hy_d"], p["hy_ng"], tl=tl)

    out = _outproj(x2, ys.reshape(T, SSD_WIDTH), ya.reshape(T, ATT_WIDTH), yh.reshape(T, HY_WIDTH),
                   p["w_out"], final_g, tm=_tile(T, 512), final=final)
    return out.reshape(B, L, D_MODEL)


def kernel(x_prompt, x_sample, rel_bias, norm_g, w_in, ssd_conv_w, ssd_conv_b, ssd_dt_bias, ssd_a_log, ssd_d, ssd_norm_g, att_sink, att_norm_g, hy_conv_w, hy_conv_b, hy_w1, hy_b1, hy_w2, hy_b2, hy_w3, hy_b3, hy_w4, hy_freq, hy_d, hy_norm_g, w_out, final_norm_g):
    depth = w_in.shape[0]
    perm = jnp.asarray(_in_perm())
    w_in_p = jnp.take(_pad_cols(w_in, IN_COLS + 1), perm, axis=2).astype(BF16)
    w_out_b = w_out.astype(BF16)
    att_bias = _attn_bias(rel_bias)
    rexp_f, rexp_b = _ssd_expand_matrix(False), _ssd_expand_matrix(True)
    final_g = final_norm_g.astype(F32)[None, :]

    layers = []
    for i in range(depth):
        layers.append(dict(
            norm_g=norm_g[i].astype(F32)[None, :],
            w_in=w_in_p[i],
            ssd_cw=_pad_rows(ssd_conv_w[i].astype(F32), 8),
            ssd_cb=ssd_conv_b[i].astype(F32)[None, :],
            ssd_dtb=_pad_cols(ssd_dt_bias[i].astype(F32).reshape(1, 2 * SSD_HEADS), DT_PAD),
            ssd_alog=_pad_cols(ssd_a_log[i].astype(F32).reshape(1, 2 * SSD_HEADS), DT_PAD),
            ssd_dskip=jnp.repeat(ssd_d[i].astype(F32), SSD_HEAD_DIM)[None, :],
            ssd_ng=ssd_norm_g[i].astype(F32)[None, :],
            rexp_f=rexp_f, rexp_b=rexp_b,
            att_bias=att_bias,
            att_sink=jnp.broadcast_to(att_sink[i].astype(F32)[:, None], (ATT_HEADS, LANES)),
            att_ng=att_norm_g[i].astype(F32)[None, :],
            hy_cw=_pad_rows(hy_conv_w[i].astype(F32), 8),
            hy_cb=hy_conv_b[i].astype(F32)[None, :],
            hy_d=hy_d[i].astype(F32)[None, :],
            hy_ng=hy_norm_g[i].astype(F32)[None, :],
            w_out=w_out_b[i],
        ))

    def trunk(x):
        L = x.shape[1]
        hm = min(256, L // 2)
        a_fwd, a_inv = _dft_tables(L, hm)
        for i in range(depth):
            xf = _hy_filter(L, hy_w1[i], hy_b1[i].astype(F32), hy_w2[i].astype(F32),
                            hy_b2[i].astype(F32), hy_w3[i].astype(F32), hy_b3[i].astype(F32),
                            hy_w4[i].astype(F32), hy_freq[i].astype(F32))
            kspec = _hy_kspec(a_fwd, xf, hm=hm)
            x = _layer(x, layers[i], (a_fwd, a_inv, kspec, hm), final=(i == depth - 1),
                       final_g=final_g)
        return x

    return (trunk(x_prompt), trunk(x_sample))
```

```python
import functools
import math

import jax
import jax.numpy as jnp
import numpy as np
from jax import lax
from jax.experimental import pallas as pl
from jax.experimental.pallas import tpu as pltpu

F32 = jnp.float32
BF16 = jnp.bfloat16
HIGHEST = lax.Precision.HIGHEST

D_MODEL = 2048
SSD_WIDTH = 1024
ATT_WIDTH = 512
HY_WIDTH = 512
SSD_HEAD_DIM = 64
SSD_HEADS = 16
SSD_GROUPS = 2
SSD_STATE = 128
SSD_CONV = 5
CHUNK = 128
SSD_XBC = SSD_WIDTH + 2 * SSD_GROUPS * SSD_STATE
ATT_HEAD_DIM = 64
ATT_HEADS = 8
ATT_KV_HEADS = 2
ATT_REP = ATT_HEADS // ATT_KV_HEADS
ATT_WINDOW = 128
ATT_BLOCK = 128
REL_BUCKETS = 32
REL_MAX_DIST = 128
HY_CONV = 3
HY_EMB_BANDS = 16
HY_FF = 64
HY_FAST_DECAY = 0.3
HY_SLOW_DECAY = 1.5
HY_DECAY_TARGET = 1e-2
EPS = 1e-6
NEG_BIG = -1e30

LANES = 128
BF16_SUBLANES = 16
VMEM_LIMIT = 56 * 1024 * 1024

C_XBC = 0
C_HY = 1536
C_Z = 3072
C_Q = 4096
C_GATT = 4608
C_GHY = 5120
C_K = 5632
C_V = 5760
C_DT = 5888
N_IN = 6144
DT_PAD = 128

_OLD_SIZES = [SSD_WIDTH, SSD_XBC, 2 * SSD_HEADS, ATT_WIDTH, ATT_KV_HEADS * ATT_HEAD_DIM,
              ATT_KV_HEADS * ATT_HEAD_DIM, ATT_WIDTH, 3 * HY_WIDTH, HY_WIDTH]
_OLD_OFF = np.concatenate([[0], np.cumsum(_OLD_SIZES)])
IN_COLS = int(_OLD_OFF[-1])


LOG2E = math.log2(math.e)
Q_SCALE = ATT_HEAD_DIM ** -0.5 * LOG2E


def _att_col_perm():
    order = [h for j in range(ATT_REP) for h in (j, ATT_REP + j)]
    return np.concatenate([np.arange(h * ATT_HEAD_DIM, (h + 1) * ATT_HEAD_DIM) for h in order])


def _in_perm():
    perm = np.full((N_IN,), IN_COLS, np.int32)
    scale = np.ones((N_IN,), np.float32)
    o = {n: int(_OLD_OFF[i]) for i, n in enumerate(
        ["z", "xbc", "dt", "q", "k", "v", "gatt", "hy", "ghy"])}
    def put(new, old, width):
        perm[new:new + width] = np.arange(old, old + width)
    put(C_XBC, o["xbc"], SSD_XBC)
    put(C_HY, o["hy"], 3 * HY_WIDTH)
    put(C_Z, o["z"], SSD_WIDTH)
    perm[C_Q:C_Q + ATT_WIDTH] = o["q"] + _att_col_perm()
    scale[C_Q:C_Q + ATT_WIDTH] = Q_SCALE
    perm[C_GATT:C_GATT + ATT_WIDTH] = o["gatt"] + _att_col_perm()
    put(C_GHY, o["ghy"], HY_WIDTH)
    put(C_K, o["k"], 128)
    put(C_V, o["v"], 128)
    put(C_DT, o["dt"], 2 * SSD_HEADS)
    return perm, scale


def _cparams(sem):
    return pltpu.CompilerParams(dimension_semantics=sem, vmem_limit_bytes=VMEM_LIMIT)


def _silu(x):
    return x * (1.0 / (1.0 + jnp.exp(-x)))


def _softplus(x):
    return jnp.maximum(x, 0.0) + jnp.log1p(jnp.exp(-jnp.abs(x)))


def _inproj_kernel(x_ref, g_ref, w_ref, u_ref, dt_ref, h_ref, *, rows, dt_tile, dt_off):
    j = pl.program_id(1)

    @pl.when(j == 0)
    def _():
        def body(r, carry):
            sl = pl.ds(pl.multiple_of(r * rows, rows), rows)
            x = x_ref[sl, :]
            ms = jnp.mean(x * x, axis=-1, keepdims=True)
            h_ref[sl, :] = (x * lax.rsqrt(ms + EPS) * g_ref[...]).astype(BF16)
            return carry
        lax.fori_loop(0, x_ref.shape[0] // rows, body, 0)

    acc = jnp.dot(h_ref[...], w_ref[...], preferred_element_type=F32)
    u_ref[...] = acc.astype(BF16)

    @pl.when(j == dt_tile)
    def _():
        dt_ref[...] = acc[:, dt_off:dt_off + DT_PAD]


def _inproj(x2, g, w, *, tm, tn):
    T = x2.shape[0]
    assert T % tm == 0 and N_IN % tn == 0
    kern = functools.partial(_inproj_kernel, rows=min(256, tm), dt_tile=C_DT // tn, dt_off=C_DT % tn)
    return pl.pallas_call(
        kern,
        grid=(T // tm, N_IN // tn),
        in_specs=[pl.BlockSpec((tm, D_MODEL), lambda i, j: (i, 0)),
                  pl.BlockSpec((1, D_MODEL), lambda i, j: (0, 0)),
                  pl.BlockSpec((D_MODEL, tn), lambda i, j: (0, j))],
        out_specs=[pl.BlockSpec((tm, tn), lambda i, j: (i, j)),
                   pl.BlockSpec((tm, DT_PAD), lambda i, j: (i, 0))],
        out_shape=[jax.ShapeDtypeStruct((T, N_IN), BF16),
                   jax.ShapeDtypeStruct((T, DT_PAD), F32)],
        scratch_shapes=[pltpu.VMEM((tm, D_MODEL), BF16)],
        compiler_params=_cparams(("parallel", "arbitrary")),
        name="inproj",
    )(x2, g, w)


def _outproj_kernel(x_ref, ys_ref, ya_ref, yh_ref, w_ref, fg_ref, o_ref, *, final):
    acc = x_ref[...]
    acc = acc + jnp.dot(ys_ref[...], w_ref[0:SSD_WIDTH, :], preferred_element_type=F32)
    acc = acc + jnp.dot(ya_ref[...], w_ref[SSD_WIDTH:SSD_WIDTH + ATT_WIDTH, :],
                        preferred_element_type=F32)
    acc = acc + jnp.dot(yh_ref[...], w_ref[SSD_WIDTH + ATT_WIDTH:, :], preferred_element_type=F32)
    if final:
        ms = jnp.mean(acc * acc, axis=-1, keepdims=True)
        acc = acc * lax.rsqrt(ms + EPS) * fg_ref[...]
    o_ref[...] = acc


def _outproj(x2, ys, ya, yh, w, fg, *, tm, final):
    T = x2.shape[0]
    assert T % tm == 0
    return pl.pallas_call(
        functools.partial(_outproj_kernel, final=final),
        grid=(T // tm,),
        in_specs=[pl.BlockSpec((tm, D_MODEL), lambda i: (i, 0)),
                  pl.BlockSpec((tm, SSD_WIDTH), lambda i: (i, 0)),
                  pl.BlockSpec((tm, ATT_WIDTH), lambda i: (i, 0)),
                  pl.BlockSpec((tm, HY_WIDTH), lambda i: (i, 0)),
                  pl.BlockSpec((D_MODEL, D_MODEL), lambda i: (0, 0)),
                  pl.BlockSpec((1, D_MODEL), lambda i: (0, 0))],
        out_specs=pl.BlockSpec((tm, D_MODEL), lambda i: (i, 0)),
        out_shape=jax.ShapeDtypeStruct((T, D_MODEL), F32),
        compiler_params=_cparams(("parallel",)),
        name="outproj",
    )(x2, ys, ya, yh, w, fg)


def _t5_buckets(rel):
    nb = REL_BUCKETS // 2
    max_exact = nb // 2
    ret = (rel > 0).astype(np.int32) * nb
    n = np.abs(rel)
    large = max_exact + (np.log(np.maximum(n, 1) / max_exact) / math.log(REL_MAX_DIST / max_exact)
                         * (nb - max_exact)).astype(np.int32)
    large = np.minimum(large, nb - 1)
    return ret + np.where(n < max_exact, n, large)


def _attn_bias(rel_bias):
    qi = np.arange(ATT_BLOCK)[:, None]
    kj = np.arange(3 * ATT_BLOCK)[None, :]
    rel = kj - ATT_BLOCK - qi
    bias = rel_bias.astype(F32)[_t5_buckets(rel)]
    bias = jnp.transpose(bias, (2, 0, 1)) * LOG2E
    window = np.abs(rel) <= ATT_WINDOW
    variants = []
    for last in (False, True):
        for first in (False, True):
            ok = window & ~(first & (kj < ATT_BLOCK)) & ~(last & (kj >= 2 * ATT_BLOCK))
            variants.append(jnp.where(ok[None], bias, NEG_BIG))
    return jnp.transpose(jnp.stack(variants), (0, 1, 3, 2))


def _attn_kernel(q_ref, kp_ref, kc_ref, kn_ref, vp_ref, vc_ref, vn_ref, g_ref, bias_ref, sink_ref,
                 ng_ref, o_ref, klo_ref, khi_ref, vt_ref, *, nt, qb):
    n = pl.program_id(1)
    lo = lax.broadcasted_iota(jnp.int32, (1, LANES), 1) < ATT_HEAD_DIM
    zero = jnp.zeros((), BF16)
    kext = jnp.concatenate([kp_ref[0], kc_ref[0], kn_ref[0]], axis=0)
    klo_ref[...] = jnp.where(lo, kext, zero)
    khi_ref[...] = jnp.where(lo, zero, kext)
    for t, ref, cnt in ((0, vp_ref, 1), (1, vc_ref, qb), (qb + 1, vn_ref, 1)):
        for i in range(cnt):
            blk = ref[0, i * ATT_BLOCK:(i + 1) * ATT_BLOCK, :]
            vt_ref[t + i] = blk.astype(F32).T.astype(BF16)
    row_lo = lax.broadcasted_iota(jnp.int32, (LANES, ATT_BLOCK), 0) < ATT_HEAD_DIM
    nk = 3 * ATT_BLOCK

    def body(i, carry):
        r0 = pl.multiple_of(i * ATT_BLOCK, ATT_BLOCK)
        q = q_ref[0, pl.ds(r0, ATT_BLOCK), :]
        kst = jnp.concatenate([klo_ref[pl.ds(r0, nk), :], khi_ref[pl.ds(r0, nk), :]], axis=0)
        vt = jnp.concatenate([vt_ref[i], vt_ref[i + 1], vt_ref[i + 2]], axis=1)
        first = jnp.logical_and(n == 0, i == 0)
        last = jnp.logical_and(n == nt - 1, i == qb - 1)
        variant = first.astype(jnp.int32) + 2 * last.astype(jnp.int32)
        outs = []
        for j in range(ATT_REP):
            qp = q[:, j * LANES:(j + 1) * LANES]
            st = lax.dot_general(kst, qp, (((1,), (1,)), ((), ())), preferred_element_type=F32)
            halves = []
            for e, h in enumerate((j, ATT_REP + j)):
                s = st[e * nk:(e + 1) * nk] + bias_ref[variant, h]
                sk = sink_ref[h:h + 1, :]
                m = jnp.maximum(jnp.max(s, axis=0, keepdims=True), sk)
                p = jnp.exp2(s - m)
                den = jnp.sum(p, axis=0, keepdims=True) + jnp.exp2(sk - m)
                ot = jnp.dot(vt, p.astype(BF16), preferred_element_type=F32)
                halves.append(ot * (1.0 / den))
            outs.append(jnp.where(row_lo, halves[0], halves[1]).T)
        o = jnp.concatenate(outs, axis=-1)
        y = o * _silu(g_ref[0, pl.ds(r0, ATT_BLOCK), :].astype(F32))
        ms = jnp.mean(y * y, axis=-1, keepdims=True)
        o_ref[0, pl.ds(r0, ATT_BLOCK), :] = (y * lax.rsqrt(ms + EPS) * ng_ref[...]).astype(BF16)
        return carry

    lax.fori_loop(0, qb, body, 0, unroll=True)


def _attention(u3, bias, sink, ng, *, qb):
    B, L, _ = u3.shape
    tq = qb * ATT_BLOCK
    assert L % tq == 0
    nt = L // tq
    nb = L // ATT_BLOCK
    kcol, vcol = C_K // 128, C_V // 128
    def kv_specs(colblk):
        return [pl.BlockSpec((1, ATT_BLOCK, 128), lambda b, n: (b, jnp.maximum(n * qb - 1, 0), colblk)),
                pl.BlockSpec((1, tq, 128), lambda b, n: (b, n, colblk)),
                pl.BlockSpec((1, ATT_BLOCK, 128),
                             lambda b, n: (b, jnp.minimum((n + 1) * qb, nb - 1), colblk))]
    return pl.pallas_call(
        functools.partial(_attn_kernel, nt=nt, qb=qb),
        grid=(B, nt),
        in_specs=[pl.BlockSpec((1, tq, ATT_WIDTH), lambda b, n: (b, n, C_Q // ATT_WIDTH))]
                 + kv_specs(kcol) + kv_specs(vcol)
                 + [pl.BlockSpec((1, tq, ATT_WIDTH), lambda b, n: (b, n, C_GATT // ATT_WIDTH)),
                    pl.BlockSpec((4, ATT_HEADS, 3 * ATT_BLOCK, ATT_BLOCK), lambda b, n: (0, 0, 0, 0)),
                    pl.BlockSpec((ATT_HEADS, LANES), lambda b, n: (0, 0)),
                    pl.BlockSpec((1, ATT_WIDTH), lambda b, n: (0, 0))],
        out_specs=pl.BlockSpec((1, tq, ATT_WIDTH), lambda b, n: (b, n, 0)),
        out_shape=jax.ShapeDtypeStruct((B, L, ATT_WIDTH), BF16),
        scratch_shapes=[pltpu.VMEM(((qb + 2) * ATT_BLOCK, LANES), BF16),
                        pltpu.VMEM(((qb + 2) * ATT_BLOCK, LANES), BF16),
                        pltpu.VMEM((qb + 2, LANES, ATT_BLOCK), BF16)],
        compiler_params=_cparams(("parallel", "parallel")),
        name="attn",
    )(u3, u3, u3, u3, u3, u3, u3, u3, bias, sink, ng)


HALO = BF16_SUBLANES


def _load_padded(xpad_ref, xm_ref, xp_ref, xn_ref, has_prev, has_next, rows):
    xpad_ref[pl.ds(0, HALO), :] = jnp.where(has_prev, xp_ref[0].astype(F32), 0.0)
    xpad_ref[pl.ds(HALO, rows), :] = xm_ref[0].astype(F32)
    xpad_ref[pl.ds(HALO + rows, HALO), :] = jnp.where(has_next, xn_ref[0].astype(F32), 0.0)


def _dwconv(xpad_ref, w_ref, b_ref, rows, width, c0, c1):
    x = xpad_ref[:, c0:c1]
    n = rows + 2 * HALO
    acc = None
    for k in range(width):
        d = k - width // 2
        xk = x if d == 0 else pltpu.roll(x, (-d) % n, axis=0)
        t = xk[HALO:HALO + rows] * w_ref[k:k + 1, c0:c1]
        acc = t if acc is None else acc + t
    return acc + b_ref[:, c0:c1]


SSD_HPG = SSD_HEADS // SSD_GROUPS
SSD_GW = SSD_HPG * SSD_HEAD_DIM
SSD_BC = SSD_GROUPS * SSD_STATE


def _ssd_decay(dt_raw, dtb_ref, alog_ref):
    dt = _softplus(dt_raw + dtb_ref[...])
    dta = dt * (-jnp.exp(alog_ref[...]))
    row = lax.broadcasted_iota(jnp.int32, (CHUNK, CHUNK), 0)
    col = lax.broadcasted_iota(jnp.int32, (CHUNK, CHUNK), 1)
    tril = (row >= col).astype(BF16)
    hi = dta.astype(BF16)
    r1 = dta - hi.astype(F32)
    mid = r1.astype(BF16)
    lo = (r1 - mid.astype(F32)).astype(BF16)
    pre = (jnp.dot(tril, hi, preferred_element_type=F32) + jnp.dot(tril, mid, preferred_element_type=F32)
           + jnp.dot(tril, lo, preferred_element_type=F32))
    tot = pre[CHUNK - 1:CHUNK, :]
    is_bwd = lax.broadcasted_iota(jnp.int32, (1, LANES), 1) >= SSD_HEADS
    acs = jnp.where(is_bwd, tot - pre + dta, pre)
    return dt, acs, tot


def _ssd_bwd_kernel(xm_ref, xp_ref, xn_ref, dt_ref, cw_ref, cb_ref, dtb_ref, alog_ref, rexp_ref,
                    xact_ref, prev_ref, xpad_ref, xf_ref, st_ref, *, nt, qb):
    n = pl.program_id(1)
    tile = nt - 1 - n
    rows = qb * CHUNK

    @pl.when(n == 0)
    def _():
        st_ref[...] = jnp.zeros(st_ref.shape, F32)

    _load_padded(xpad_ref, xm_ref, xp_ref, xn_ref, tile > 0, tile < nt - 1, rows)
    xact = _silu(_dwconv(xpad_ref, cw_ref, cb_ref, rows, SSD_CONV, 0, SSD_XBC))
    xf_ref[...] = xact
    xact_ref[0] = xact.astype(BF16)

    def body(j, carry):
        i = qb - 1 - j
        r0 = pl.multiple_of(i * CHUNK, CHUNK)
        xs = xf_ref[pl.ds(r0, CHUNK), 0:SSD_WIDTH]
        bm = xf_ref[pl.ds(r0, CHUNK), SSD_WIDTH:SSD_WIDTH + SSD_BC].astype(BF16)
        dt, acs, tot = _ssd_decay(dt_ref[0, pl.ds(r0, CHUNK), :], dtb_ref, alog_ref)
        e_in = jnp.concatenate([jnp.exp(tot - acs) * dt, jnp.broadcast_to(jnp.exp(tot), (8, LANES))],
                               axis=0).astype(BF16)
        e_out = jnp.dot(e_in, rexp_ref[...], preferred_element_type=F32)
        xd = (xs * e_out[0:CHUNK]).astype(BF16)
        cdec = e_out[CHUNK:CHUNK + 1]
        for g in range(SSD_GROUPS):
            prev = st_ref[g]
            prev_ref[0, i, g] = prev.astype(BF16)
            s_new = lax.dot_general(bm[:, g * SSD_STATE:(g + 1) * SSD_STATE],
                                    xd[:, g * SSD_GW:(g + 1) * SSD_GW], (((0,), (0,)), ((), ())),
                                    preferred_element_type=F32)
            st_ref[g] = prev * cdec[:, g * SSD_GW:(g + 1) * SSD_GW] + s_new
        return carry

    lax.fori_loop(0, qb, body, 0, unroll=True)


def _ssd_fwd_kernel(xa_ref, dt_ref, prev_ref, z_ref, dtb_ref, alog_ref, rexpf_ref, rexpb_ref,
                    dsk_ref, ng_ref, o_ref, st_ref, *, qb):
    n = pl.program_id(1)

    @pl.when(n == 0)
    def _():
        st_ref[...] = jnp.zeros(st_ref.shape, F32)

    row = lax.broadcasted_iota(jnp.int32, (CHUNK, CHUNK), 0)
    col = lax.broadcasted_iota(jnp.int32, (CHUNK, CHUNK), 1)
    fwd_part = row > col
    diag = row == col
    lane_lo = lax.broadcasted_iota(jnp.int32, (1, LANES), 1) < SSD_HEAD_DIM
    zero_b = jnp.zeros((), BF16)

    def body(i, carry):
        r0 = pl.multiple_of(i * CHUNK, CHUNK)
        xs_b = xa_ref[0, pl.ds(r0, CHUNK), 0:SSD_WIDTH]
        xs = xs_b.astype(F32)
        bm = xa_ref[0, pl.ds(r0, CHUNK), SSD_WIDTH:SSD_WIDTH + SSD_BC]
        cm = xa_ref[0, pl.ds(r0, CHUNK), SSD_WIDTH + SSD_BC:SSD_XBC]
        dt, acs, tot = _ssd_decay(dt_ref[0, pl.ds(r0, CHUNK), :], dtb_ref, alog_ref)
        acs2 = acs * LOG2E
        rt = (acs2 - jnp.log2(dt)).T
        dsum_t = jnp.log2(dt + pltpu.roll(dt, LANES - SSD_HEADS, axis=1)).T
        eacs = jnp.exp(acs)
        ef_in = jnp.concatenate([jnp.exp(tot - acs) * dt, eacs,
                                 jnp.broadcast_to(jnp.exp(tot), (8, LANES))], axis=0).astype(BF16)
        ef = jnp.dot(ef_in, rexpf_ref[...], preferred_element_type=F32)
        eb = jnp.dot(eacs.astype(BF16), rexpb_ref[...], preferred_element_type=F32)
        xd = (xs * ef[0:CHUNK]).astype(BF16)
        eacs_f = ef[CHUNK:2 * CHUNK]
        cdec = ef[2 * CHUNK:2 * CHUNK + 1]
        ys = []
        for g in range(SSD_GROUPS):
            bg = bm[:, g * SSD_STATE:(g + 1) * SSD_STATE]
            cg = cm[:, g * SSD_STATE:(g + 1) * SSD_STATE]
            gs = slice(g * SSD_GW, (g + 1) * SSD_GW)
            cb = lax.dot_general(cg, bg, (((1,), (1,)), ((), ())), preferred_element_type=F32)
            prev = st_ref[g]
            y_off = (jnp.dot(cg, prev.astype(BF16), preferred_element_type=F32) * eacs_f[:, gs]
                     + jnp.dot(cg, prev_ref[0, i, g], preferred_element_type=F32) * eb[:, gs])
            s_new = lax.dot_general(bg, xd[:, gs], (((0,), (0,)), ((), ())),
                                    preferred_element_type=F32)
            st_ref[g] = prev * cdec[:, gs] + s_new
            for pr in range(SSD_HPG // 2):
                c0 = g * SSD_GW + pr * LANES
                xpair = xs_b[:, c0:c0 + LANES]
                xbd = jnp.concatenate([jnp.where(lane_lo, xpair, zero_b),
                                       jnp.where(lane_lo, zero_b, xpair)], axis=0)
                mats = []
                for e in range(2):
                    h = g * SSD_HPG + pr * 2 + e
                    hb = SSD_HEADS + h
                    sel = jnp.where(fwd_part, acs2[:, h:h + 1] - rt[h:h + 1, :],
                                    acs2[:, hb:hb + 1] - rt[hb:hb + 1, :])
                    sel = jnp.where(diag, dsum_t[h:h + 1, :], sel)
                    mats.append((cb * jnp.exp2(sel)).astype(BF16))
                yd = jnp.dot(jnp.concatenate(mats, axis=1), xbd, preferred_element_type=F32)
                ys.append(yd + y_off[:, pr * LANES:(pr + 1) * LANES])
        y = jnp.concatenate(ys, axis=-1) + xs * dsk_ref[...]
        y = y * _silu(z_ref[0, pl.ds(r0, CHUNK), :].astype(F32))
        ms = jnp.mean(y * y, axis=-1, keepdims=True)
        o_ref[0, pl.ds(r0, CHUNK), :] = (y * lax.rsqrt(ms + EPS) * ng_ref[...]).astype(BF16)
        return carry

    lax.fori_loop(0, qb, body, 0, unroll=2 if qb % 2 == 0 else 1)


def _ssd(u3, dt3, cw, cb, dtb, alog, rexp_f, rexp_b, dsk, ng, *, qb):
    B, L, _ = u3.shape
    rows = qb * CHUNK
    assert L % rows == 0
    nt = L // rows
    nc = L // CHUNK
    hb = rows // HALO
    nhb = L // HALO
    state = pltpu.VMEM((SSD_GROUPS, SSD_STATE, SSD_GW), F32)
    const = lambda shape: pl.BlockSpec(shape, lambda b, n: (0,) * len(shape))
    rt = lambda n: nt - 1 - n

    xact, prevb = pl.pallas_call(
        functools.partial(_ssd_bwd_kernel, nt=nt, qb=qb),
        grid=(B, nt),
        in_specs=[
            pl.BlockSpec((1, rows, SSD_XBC), lambda b, n: (b, rt(n), C_XBC // SSD_XBC)),
            pl.BlockSpec((1, HALO, SSD_XBC),
                         lambda b, n: (b, jnp.maximum(rt(n) * hb - 1, 0), C_XBC // SSD_XBC)),
            pl.BlockSpec((1, HALO, SSD_XBC),
                         lambda b, n: (b, jnp.minimum((rt(n) + 1) * hb, nhb - 1), C_XBC // SSD_XBC)),
            pl.BlockSpec((1, rows, DT_PAD), lambda b, n: (b, rt(n), 0)),
            const((8, SSD_XBC)), const((1, SSD_XBC)), const((1, DT_PAD)), const((1, DT_PAD)),
            const((LANES, SSD_WIDTH)),
        ],
        out_specs=[pl.BlockSpec((1, rows, SSD_XBC), lambda b, n: (b, rt(n), 0)),
                   pl.BlockSpec((1, qb, SSD_GROUPS, SSD_STATE, SSD_GW),
                                lambda b, n: (b, rt(n), 0, 0, 0))],
        out_shape=[jax.ShapeDtypeStruct((B, L, SSD_XBC), BF16),
                   jax.ShapeDtypeStruct((B, nc, SSD_GROUPS, SSD_STATE, SSD_GW), BF16)],
        scratch_shapes=[pltpu.VMEM((rows + 2 * HALO, SSD_XBC), F32),
                        pltpu.VMEM((rows, SSD_XBC), F32), state],
        compiler_params=_cparams(("parallel", "arbitrary")),
        name="ssd_bwd",
    )(u3, u3, u3, dt3, cw, cb, dtb, alog, rexp_b)

    return pl.pallas_call(
        functools.partial(_ssd_fwd_kernel, qb=qb),
        grid=(B, nt),
        in_specs=[
            pl.BlockSpec((1, rows, SSD_XBC), lambda b, n: (b, n, 0)),
            pl.BlockSpec((1, rows, DT_PAD), lambda b, n: (b, n, 0)),
            pl.BlockSpec((1, qb, SSD_GROUPS, SSD_STATE, SSD_GW), lambda b, n: (b, n, 0, 0, 0)),
            pl.BlockSpec((1, rows, SSD_WIDTH), lambda b, n: (b, n, C_Z // SSD_WIDTH)),
            const((1, DT_PAD)), const((1, DT_PAD)),
            const((LANES, SSD_WIDTH)), const((LANES, SSD_WIDTH)),
            const((1, SSD_WIDTH)), const((1, SSD_WIDTH)),
        ],
        out_specs=pl.BlockSpec((1, rows, SSD_WIDTH), lambda b, n: (b, n, 0)),
        out_shape=jax.ShapeDtypeStruct((B, L, SSD_WIDTH), BF16),
        scratch_shapes=[state],
        compiler_params=_cparams(("parallel", "arbitrary")),
        name="ssd_fwd",
    )(xact, dt3, prevb, u3, dtb, alog, rexp_f, rexp_b, dsk, ng)


def _ssd_expand_matrix(rev):
    m = np.zeros((LANES, SSD_WIDTH), np.float32)
    hoff = SSD_HEADS if rev else 0
    for h in range(SSD_HEADS):
        m[hoff + h, h * SSD_HEAD_DIM:(h + 1) * SSD_HEAD_DIM] = 1.0
    return jnp.asarray(m, BF16)


def _hy_filter_kernel(t_ref, bands_ref, w1t_ref, w1c_ref, w1s_ref, b1_ref, w2_ref, b2_ref, w3_ref,
                      b3_ref, w4_ref, fr_ref, absd_ref, x_ref, *, L, tl):
    i = pl.program_id(0)
    pos = (lax.broadcasted_iota(jnp.int32, (tl, 1), 0) + i * tl).astype(F32)
    t = t_ref[...]
    ang = 2.0 * math.pi * pos * bands_ref[...] / L
    fr = fr_ref[...]
    dot = functools.partial(jnp.dot, preferred_element_type=F32, precision=HIGHEST)
    pre = t * w1t_ref[...] + dot(jnp.cos(ang), w1c_ref[...]) + dot(-jnp.sin(ang), w1s_ref[...])
    h = jnp.sin(fr * (pre + b1_ref[...]))
    h = jnp.sin(fr * (dot(h, w2_ref[...]) + b2_ref[...]))
    h = jnp.sin(fr * (dot(h, w3_ref[...]) + b3_ref[...]))
    h = dot(h, w4_ref[...])
    decay = jnp.exp(-t * absd_ref[...])
    hf = h[:, :HY_WIDTH] * decay
    hb = jnp.where(pos == 0.0, 0.0, h[:, HY_WIDTH:] * decay)
    x_ref[:, :HY_WIDTH] = (hf + hb).astype(BF16)
    x_ref[:, HY_WIDTH:] = (hb - hf).astype(BF16)


def _hy_filter(L, w1, b1, w2, b2, w3, b3, w4, freq):
    tl = min(512, L)
    t = jnp.linspace(0.0, 1.0, L, dtype=F32)[:, None]
    bands = jnp.linspace(1e-4, HY_EMB_BANDS - 1, HY_EMB_BANDS, dtype=F32)[None, :]
    max_decay = math.log(HY_DECAY_TARGET) / HY_FAST_DECAY
    min_decay = math.log(HY_DECAY_TARGET) / HY_SLOW_DECAY
    absd = jnp.abs(jnp.linspace(min_decay, max_decay, HY_WIDTH, dtype=F32))[None, :]
    w1 = w1.astype(F32)
    full = lambda a: pl.BlockSpec(a.shape, lambda i: (0,) * a.ndim)
    args = [t, bands, w1[0:1], w1[1:1 + HY_EMB_BANDS], w1[1 + HY_EMB_BANDS:], b1[None], w2, b2[None],
            w3, b3[None], w4, freq[None], absd]
    in_specs = [pl.BlockSpec((tl, 1), lambda i: (i, 0))] + [full(a) for a in args[1:]]
    return pl.pallas_call(
        functools.partial(_hy_filter_kernel, L=L, tl=tl),
        grid=(L // tl,),
        in_specs=in_specs,
        out_specs=pl.BlockSpec((tl, 2 * HY_WIDTH), lambda i: (i, 0)),
        out_shape=jax.ShapeDtypeStruct((L, 2 * HY_WIDTH), BF16),
        compiler_params=_cparams(("parallel",)),
        name="hy_filter",
    )(*args)


def _dft_tables(L, hm):
    n = 2 * L
    f = jnp.arange(L, dtype=jnp.int32)
    s = jnp.arange(L, dtype=jnp.int32)
    ph = ((2 * f + 1)[:, None] * s[None, :]) % (2 * n)
    ang = ph.astype(F32) * (math.pi / n)
    c = jnp.cos(ang).astype(BF16).reshape(L // hm, 1, hm, L)
    sn = jnp.sin(ang).astype(BF16).reshape(L // hm, 1, hm, L)
    a_fwd = jnp.concatenate([c, sn], axis=1).reshape(2 * L, L)
    a_inv = jnp.concatenate([c, -sn], axis=1).reshape(2 * L, L).T
    return a_fwd, a_inv


def _hy_kspec_kernel(a_ref, x_ref, k_ref, *, hm, scale):
    pq = jnp.dot(a_ref[...], x_ref[...], preferred_element_type=F32)
    k_ref[0:hm, :] = pq[0:hm, :HY_WIDTH] * scale
    k_ref[hm:, :] = pq[hm:, HY_WIDTH:] * scale


def _hy_kspec(a_fwd, xf, *, hm):
    L = xf.shape[0]
    tm = 2 * hm
    return pl.pallas_call(
        functools.partial(_hy_kspec_kernel, hm=hm, scale=1.0 / L),
        grid=(2 * L // tm,),
        in_specs=[pl.BlockSpec((tm, L), lambda i: (i, 0)),
                  pl.BlockSpec((L, 2 * HY_WIDTH), lambda i: (0, 0))],
        out_specs=pl.BlockSpec((tm, HY_WIDTH), lambda i: (i, 0)),
        out_shape=jax.ShapeDtypeStruct((2 * L, HY_WIDTH), F32),
        compiler_params=_cparams(("parallel",)),
        name="hy_kspec",
    )(a_fwd, xf)


def _hy_pre_kernel(xm_ref, xp_ref, xn_ref, cw_ref, cb_ref, u_ref, xpad_ref, *, nt, tl):
    i = pl.program_id(1)
    _load_padded(xpad_ref, xm_ref, xp_ref, xn_ref, i > 0, i < nt - 1, tl)
    xb = _dwconv(xpad_ref, cw_ref, cb_ref, tl, HY_CONV, HY_WIDTH, 2 * HY_WIDTH)
    v = _dwconv(xpad_ref, cw_ref, cb_ref, tl, HY_CONV, 2 * HY_WIDTH, 3 * HY_WIDTH)
    u_ref[0] = (xb * v).astype(BF16)


def _hy_tile_specs(L, tl, order):
    hb = tl // HALO
    nhb = L // HALO
    w = 3 * HY_WIDTH
    if order == "bt":
        pick = lambda a, b: (a, b)
    else:
        pick = lambda a, b: (b, a)
    def main(a, b):
        bb, i = pick(a, b)
        return (bb, i, C_HY // w)
    def prev(a, b):
        bb, i = pick(a, b)
        return (bb, jnp.maximum(i * hb - 1, 0), C_HY // w)
    def nxt(a, b):
        bb, i = pick(a, b)
        return (bb, jnp.minimum((i + 1) * hb, nhb - 1), C_HY // w)
    return [pl.BlockSpec((1, tl, w), main), pl.BlockSpec((1, HALO, w), prev),
            pl.BlockSpec((1, HALO, w), nxt)]


def _hy_pre(u3, cw, cb, *, tl):
    B, L, _ = u3.shape
    nt = L // tl
    w = 3 * HY_WIDTH
    return pl.pallas_call(
        functools.partial(_hy_pre_kernel, nt=nt, tl=tl),
        grid=(B, nt),
        in_specs=_hy_tile_specs(L, tl, "bt") + [pl.BlockSpec((8, w), lambda b, i: (0, 0)),
                                                 pl.BlockSpec((1, w), lambda b, i: (0, 0))],
        out_specs=pl.BlockSpec((1, tl, HY_WIDTH), lambda b, i: (b, i, 0)),
        out_shape=jax.ShapeDtypeStruct((B, L, HY_WIDTH), BF16),
        scratch_shapes=[pltpu.VMEM((tl + 2 * HALO, w), F32)],
        compiler_params=_cparams(("parallel", "parallel")),
        name="hy_pre",
    )(u3, u3, u3, cw, cb)


def _hy_fwd_kernel(a_ref, u_ref, k_ref, y_ref, *, hm):
    pq = jnp.dot(a_ref[...], u_ref[0], preferred_element_type=F32)
    p, q = pq[0:hm], pq[hm:]
    kr, ki = k_ref[0:hm, :], k_ref[hm:, :]
    y_ref[0, 0:hm, :] = (p * kr + q * ki).astype(BF16)
    y_ref[0, hm:, :] = (p * ki - q * kr).astype(BF16)


def _hy_fwd(a_fwd, uc, kspec, *, hm):
    B, L, _ = uc.shape
    tm = 2 * hm
    return pl.pallas_call(
        functools.partial(_hy_fwd_kernel, hm=hm),
        grid=(2 * L // tm, B),
        in_specs=[pl.BlockSpec((tm, L), lambda i, b: (i, 0)),
                  pl.BlockSpec((1, L, HY_WIDTH), lambda i, b: (b, 0, 0)),
                  pl.BlockSpec((tm, HY_WIDTH), lambda i, b: (i, 0))],
        out_specs=pl.BlockSpec((1, tm, HY_WIDTH), lambda i, b: (b, i, 0)),
        out_shape=jax.ShapeDtypeStruct((B, 2 * L, HY_WIDTH), BF16),
        compiler_params=_cparams(("parallel", "parallel")),
        name="hy_fwd",
    )(a_fwd, uc, kspec)


def _hy_inv_kernel(a_ref, y_ref, xm_ref, xp_ref, xn_ref, g_ref, cw_ref, cb_ref, d_ref, ng_ref,
                   o_ref, xpad_ref, *, nt, tl):
    i = pl.program_id(0)
    conv = jnp.dot(a_ref[...], y_ref[0], preferred_element_type=F32)
    _load_padded(xpad_ref, xm_ref, xp_ref, xn_ref, i > 0, i < nt - 1, tl)
    xa = _dwconv(xpad_ref, cw_ref, cb_ref, tl, HY_CONV, 0, HY_WIDTH)
    xb = _dwconv(xpad_ref, cw_ref, cb_ref, tl, HY_CONV, HY_WIDTH, 2 * HY_WIDTH)
    v = _dwconv(xpad_ref, cw_ref, cb_ref, tl, HY_CONV, 2 * HY_WIDTH, 3 * HY_WIDTH)
    u = xb * v
    y = xa * (conv + u * d_ref[...])
    y = y * _silu(g_ref[0].astype(F32))
    ms = jnp.mean(y * y, axis=-1, keepdims=True)
    o_ref[0] = (y * lax.rsqrt(ms + EPS) * ng_ref[...]).astype(BF16)


def _hy_inv(a_inv, yspec, u3, cw, cb, d, ng, *, tl):
    B, L, _ = u3.shape
    nt = L // tl
    w = 3 * HY_WIDTH
    return pl.pallas_call(
        functools.partial(_hy_inv_kernel, nt=nt, tl=tl),
        grid=(nt, B),
        in_specs=[pl.BlockSpec((tl, 2 * L), lambda i, b: (i, 0)),
                  pl.BlockSpec((1, 2 * L, HY_WIDTH), lambda i, b: (b, 0, 0))]
                 + _hy_tile_specs(L, tl, "tb")
                 + [pl.BlockSpec((1, tl, HY_WIDTH), lambda i, b: (b, i, C_GHY // HY_WIDTH)),
                    pl.BlockSpec((8, w), lambda i, b: (0, 0)),
                    pl.BlockSpec((1, w), lambda i, b: (0, 0)),
                    pl.BlockSpec((1, HY_WIDTH), lambda i, b: (0, 0)),
                    pl.BlockSpec((1, HY_WIDTH), lambda i, b: (0, 0))],
        out_specs=pl.BlockSpec((1, tl, HY_WIDTH), lambda i, b: (b, i, 0)),
        out_shape=jax.ShapeDtypeStruct((B, L, HY_WIDTH), BF16),
        scratch_shapes=[pltpu.VMEM((tl + 2 * HALO, w), F32)],
        compiler_params=_cparams(("parallel", "parallel")),
        name="hy_inv",
    )(a_inv, yspec, u3, u3, u3, u3, cw, cb, d, ng)


def _pad_rows(a, rows):
    return jnp.concatenate([a, jnp.zeros((rows - a.shape[0],) + a.shape[1:], a.dtype)], axis=0)


def _pad_cols(a, cols):
    return jnp.concatenate([a, jnp.zeros(a.shape[:-1] + (cols - a.shape[-1],), a.dtype)], axis=-1)


def _tile(n, pref):
    t = min(pref, n)
    assert n % t == 0
    return t


def _layer(x3, p, tables, *, final, final_g):
    B, L, _ = x3.shape
    T = B * L
    x2 = x3.reshape(T, D_MODEL)
    u, dt = _inproj(x2, p["norm_g"], p["w_in"], tm=_tile(T, 1024), tn=512)
    u3 = u.reshape(B, L, N_IN)
    dt3 = dt.reshape(B, L, DT_PAD)

    ys = _ssd(u3, dt3, p["ssd_cw"], p["ssd_cb"], p["ssd_dtb"], p["ssd_alog"], p["rexp_f"],
              p["rexp_b"], p["ssd_dskip"], p["ssd_ng"], qb=_tile(L // CHUNK, 4))

    ya = _attention(u3, p["att_bias"], p["att_sink"], p["att_ng"], qb=_tile(L // ATT_BLOCK, 4))

    a_fwd, a_inv, kspec, hm = tables
    tl = _tile(L, 512)
    uc = _hy_pre(u3, p["hy_cw"], p["hy_cb"], tl=tl)
    yspec = _hy_fwd(a_fwd, uc, kspec, hm=hm)
    yh = _hy_inv(a_inv, yspec, u3, p["hy_cw"], p["hy_cb"], p["hy_d"], p["hy_ng"], tl=tl)

    out = _outproj(x2, ys.reshape(T, SSD_WIDTH), ya.reshape(T, ATT_WIDTH), yh.reshape(T, HY_WIDTH),
                   p["w_out"], final_g, tm=_tile(T, 512), final=final)
    return out.reshape(B, L, D_MODEL)


def kernel(x_prompt, x_sample, rel_bias, norm_g, w_in, ssd_conv_w, ssd_conv_b, ssd_dt_bias, ssd_a_log, ssd_d, ssd_norm_g, att_sink, att_norm_g, hy_conv_w, hy_conv_b, hy_w1, hy_b1, hy_w2, hy_b2, hy_w3, hy_b3, hy_w4, hy_freq, hy_d, hy_norm_g, w_out, final_norm_g):
    depth = w_in.shape[0]
    perm, col_scale = _in_perm()
    w_in_p = (jnp.take(_pad_cols(w_in, IN_COLS + 1), jnp.asarray(perm), axis=2)
              * jnp.asarray(col_scale)).astype(BF16)
    att_perm = _att_col_perm()
    out_rows = np.concatenate([np.arange(SSD_WIDTH), SSD_WIDTH + att_perm,
                               np.arange(SSD_WIDTH + ATT_WIDTH, D_MODEL)])
    w_out_b = jnp.take(w_out, jnp.asarray(out_rows), axis=1).astype(BF16)
    att_bias = _attn_bias(rel_bias)
    rexp_f, rexp_b = _ssd_expand_matrix(False), _ssd_expand_matrix(True)
    final_g = final_norm_g.astype(F32)[None, :]

    layers = []
    for i in range(depth):
        layers.append(dict(
            norm_g=norm_g[i].astype(F32)[None, :],
            w_in=w_in_p[i],
            ssd_cw=_pad_rows(ssd_conv_w[i].astype(F32), 8),
            ssd_cb=ssd_conv_b[i].astype(F32)[None, :],
            ssd_dtb=_pad_cols(ssd_dt_bias[i].astype(F32).reshape(1, 2 * SSD_HEADS), DT_PAD),
            ssd_alog=_pad_cols(ssd_a_log[i].astype(F32).reshape(1, 2 * SSD_HEADS), DT_PAD),
            ssd_dskip=jnp.repeat(ssd_d[i].astype(F32), SSD_HEAD_DIM)[None, :],
            ssd_ng=ssd_norm_g[i].astype(F32)[None, :],
            rexp_f=rexp_f, rexp_b=rexp_b,
            att_bias=att_bias,
            att_sink=jnp.broadcast_to(att_sink[i].astype(F32)[:, None] * LOG2E, (ATT_HEADS, LANES)),
            att_ng=att_norm_g[i].astype(F32)[att_perm][None, :],
            hy_cw=_pad_rows(hy_conv_w[i].astype(F32), 8),
            hy_cb=hy_conv_b[i].astype(F32)[None, :],
            hy_d=hy_d[i].astype(F32)[None, :],
            hy_ng=hy_norm_g[i].astype(F32)[None, :],
            w_out=w_out_b[i],
        ))

    def trunk(x):
        L = x.shape[1]
        hm = min(256, L // 2)
        a_fwd, a_inv = _dft_tables(L, hm)
        for i in range(depth):
            xf = _hy_filter(L, hy_w1[i], hy_b1[i].astype(F32), hy_w2[i].astype(F32),
                            hy_b2[i].astype(F32), hy_w3[i].astype(F32), hy_b3[i].astype(F32),
                            hy_w4[i].astype(F32), hy_freq[i].astype(F32))
            kspec = _hy_kspec(a_fwd, xf, hm=hm)
            x = _layer(x, layers[i], (a_fwd, a_inv, kspec, hm), final=(i == depth - 1),
                       final_g=final_g)
        return x

    return (trunk(x_prompt), trunk(x_sample))
```

```python
import functools
import math

import jax
import jax.numpy as jnp
import numpy as np
from jax import lax
from jax.experimental import pallas as pl
from jax.experimental.pallas import tpu as pltpu

F32 = jnp.float32
BF16 = jnp.bfloat16
HIGHEST = lax.Precision.HIGHEST

D_MODEL = 2048
SSD_WIDTH = 1024
ATT_WIDTH = 512
HY_WIDTH = 512
SSD_HEAD_DIM = 64
SSD_HEADS = 16
SSD_GROUPS = 2
SSD_STATE = 128
SSD_CONV = 5
CHUNK = 128
SSD_XBC = SSD_WIDTH + 2 * SSD_GROUPS * SSD_STATE
ATT_HEAD_DIM = 64
ATT_HEADS = 8
ATT_KV_HEADS = 2
ATT_REP = ATT_HEADS // ATT_KV_HEADS
ATT_WINDOW = 128
ATT_BLOCK = 128
REL_BUCKETS = 32
REL_MAX_DIST = 128
HY_CONV = 3
HY_EMB_BANDS = 16
HY_FF = 64
HY_FAST_DECAY = 0.3
HY_SLOW_DECAY = 1.5
HY_DECAY_TARGET = 1e-2
EPS = 1e-6
NEG_BIG = -1e30

LANES = 128
BF16_SUBLANES = 16
VMEM_LIMIT = 56 * 1024 * 1024

C_XBC = 0
C_HY = 1536
C_Z = 3072
C_Q = 4096
C_GATT = 4608
C_GHY = 5120
C_K = 5632
C_V = 5760
C_DT = 5888
N_IN = 6144
DT_PAD = 128

_OLD_SIZES = [SSD_WIDTH, SSD_XBC, 2 * SSD_HEADS, ATT_WIDTH, ATT_KV_HEADS * ATT_HEAD_DIM,
              ATT_KV_HEADS * ATT_HEAD_DIM, ATT_WIDTH, 3 * HY_WIDTH, HY_WIDTH]
_OLD_OFF = np.concatenate([[0], np.cumsum(_OLD_SIZES)])
IN_COLS = int(_OLD_OFF[-1])


LOG2E = math.log2(math.e)
Q_SCALE = ATT_HEAD_DIM ** -0.5 * LOG2E


def _att_col_perm():
    order = [h for j in range(ATT_REP) for h in (j, ATT_REP + j)]
    return np.concatenate([np.arange(h * ATT_HEAD_DIM, (h + 1) * ATT_HEAD_DIM) for h in order])


def _in_perm():
    perm = np.full((N_IN,), IN_COLS, np.int32)
    scale = np.ones((N_IN,), np.float32)
    o = {n: int(_OLD_OFF[i]) for i, n in enumerate(
        ["z", "xbc", "dt", "q", "k", "v", "gatt", "hy", "ghy"])}
    def put(new, old, width):
        perm[new:new + width] = np.arange(old, old + width)
    put(C_XBC, o["xbc"], SSD_XBC)
    put(C_HY, o["hy"], 3 * HY_WIDTH)
    put(C_Z, o["z"], SSD_WIDTH)
    perm[C_Q:C_Q + ATT_WIDTH] = o["q"] + _att_col_perm()
    scale[C_Q:C_Q + ATT_WIDTH] = Q_SCALE
    perm[C_GATT:C_GATT + ATT_WIDTH] = o["gatt"] + _att_col_perm()
    put(C_GHY, o["ghy"], HY_WIDTH)
    put(C_K, o["k"], 128)
    put(C_V, o["v"], 128)
    put(C_DT, o["dt"], 2 * SSD_HEADS)
    return perm, scale


def _cparams(sem):
    return pltpu.CompilerParams(dimension_semantics=sem, vmem_limit_bytes=VMEM_LIMIT)


def _silu(x):
    return x * (1.0 / (1.0 + jnp.exp(-x)))


def _softplus(x):
    return jnp.maximum(x, 0.0) + jnp.log1p(jnp.exp(-jnp.abs(x)))


def _inproj_kernel(x_ref, g_ref, w_ref, u_ref, dt_ref, h_ref, *, rows, tn):
    def body(r, carry):
        sl = pl.ds(pl.multiple_of(r * rows, rows), rows)
        x = x_ref[sl, :]
        ms = jnp.mean(x * x, axis=-1, keepdims=True)
        h_ref[sl, :] = (x * lax.rsqrt(ms + EPS) * g_ref[...]).astype(BF16)
        return carry
    lax.fori_loop(0, x_ref.shape[0] // rows, body, 0)

    h = h_ref[...]
    for j in range(N_IN // tn):
        acc = jnp.dot(h, w_ref[:, j * tn:(j + 1) * tn], preferred_element_type=F32)
        u_ref[:, j * tn:(j + 1) * tn] = acc.astype(BF16)
        if j == C_DT // tn:
            dt_ref[...] = acc[:, C_DT % tn:C_DT % tn + DT_PAD]


def _inproj(x2, g, w, *, tm, tn):
    T = x2.shape[0]
    assert T % tm == 0 and N_IN % tn == 0
    return pl.pallas_call(
        functools.partial(_inproj_kernel, rows=min(256, tm), tn=tn),
        grid=(T // tm,),
        in_specs=[pl.BlockSpec((tm, D_MODEL), lambda i: (i, 0)),
                  pl.BlockSpec((1, D_MODEL), lambda i: (0, 0)),
                  pl.BlockSpec((D_MODEL, N_IN), lambda i: (0, 0), pipeline_mode=pl.Buffered(1))],
        out_specs=[pl.BlockSpec((tm, N_IN), lambda i: (i, 0)),
                   pl.BlockSpec((tm, DT_PAD), lambda i: (i, 0))],
        out_shape=[jax.ShapeDtypeStruct((T, N_IN), BF16),
                   jax.ShapeDtypeStruct((T, DT_PAD), F32)],
        scratch_shapes=[pltpu.VMEM((tm, D_MODEL), BF16)],
        compiler_params=_cparams(("parallel",)),
        name="inproj",
    )(x2, g, w)


def _outproj_kernel(x_ref, ys_ref, ya_ref, yh_ref, w_ref, fg_ref, o_ref, *, final):
    acc = x_ref[...]
    acc = acc + jnp.dot(ys_ref[...], w_ref[0:SSD_WIDTH, :], preferred_element_type=F32)
    acc = acc + jnp.dot(ya_ref[...], w_ref[SSD_WIDTH:SSD_WIDTH + ATT_WIDTH, :],
                        preferred_element_type=F32)
    acc = acc + jnp.dot(yh_ref[...], w_ref[SSD_WIDTH + ATT_WIDTH:, :], preferred_element_type=F32)
    if final:
        ms = jnp.mean(acc * acc, axis=-1, keepdims=True)
        acc = acc * lax.rsqrt(ms + EPS) * fg_ref[...]
    o_ref[...] = acc


def _outproj(x2, ys, ya, yh, w, fg, *, tm, final):
    T = x2.shape[0]
    assert T % tm == 0
    return pl.pallas_call(
        functools.partial(_outproj_kernel, final=final),
        grid=(T // tm,),
        in_specs=[pl.BlockSpec((tm, D_MODEL), lambda i: (i, 0)),
                  pl.BlockSpec((tm, SSD_WIDTH), lambda i: (i, 0)),
                  pl.BlockSpec((tm, ATT_WIDTH), lambda i: (i, 0)),
                  pl.BlockSpec((tm, HY_WIDTH), lambda i: (i, 0)),
                  pl.BlockSpec((D_MODEL, D_MODEL), lambda i: (0, 0)),
                  pl.BlockSpec((1, D_MODEL), lambda i: (0, 0))],
        out_specs=pl.BlockSpec((tm, D_MODEL), lambda i: (i, 0)),
        out_shape=jax.ShapeDtypeStruct((T, D_MODEL), F32),
        compiler_params=_cparams(("parallel",)),
        name="outproj",
    )(x2, ys, ya, yh, w, fg)


def _t5_buckets(rel):
    nb = REL_BUCKETS // 2
    max_exact = nb // 2
    ret = (rel > 0).astype(np.int32) * nb
    n = np.abs(rel)
    large = max_exact + (np.log(np.maximum(n, 1) / max_exact) / math.log(REL_MAX_DIST / max_exact)
                         * (nb - max_exact)).astype(np.int32)
    large = np.minimum(large, nb - 1)
    return ret + np.where(n < max_exact, n, large)


def _attn_bias(rel_bias):
    qi = np.arange(ATT_BLOCK)[:, None]
    kj = np.arange(3 * ATT_BLOCK)[None, :]
    rel = kj - ATT_BLOCK - qi
    bias = rel_bias.astype(F32)[_t5_buckets(rel)]
    bias = jnp.transpose(bias, (2, 0, 1)) * LOG2E
    window = np.abs(rel) <= ATT_WINDOW
    variants = []
    for last in (False, True):
        for first in (False, True):
            ok = window & ~(first & (kj < ATT_BLOCK)) & ~(last & (kj >= 2 * ATT_BLOCK))
            variants.append(jnp.where(ok[None], bias, NEG_BIG))
    return jnp.transpose(jnp.stack(variants), (0, 1, 3, 2))


def _attn_kernel(q_ref, kp_ref, kc_ref, kn_ref, vp_ref, vc_ref, vn_ref, g_ref, bias_ref, sink_ref,
                 ng_ref, o_ref, klo_ref, khi_ref, vt_ref, *, nt, qb):
    n = pl.program_id(1)
    lo = lax.broadcasted_iota(jnp.int32, (1, LANES), 1) < ATT_HEAD_DIM
    zero = jnp.zeros((), BF16)
    kext = jnp.concatenate([kp_ref[0], kc_ref[0], kn_ref[0]], axis=0)
    klo_ref[...] = jnp.where(lo, kext, zero)
    khi_ref[...] = jnp.where(lo, zero, kext)
    for t, ref, cnt in ((0, vp_ref, 1), (1, vc_ref, qb), (qb + 1, vn_ref, 1)):
        for i in range(cnt):
            blk = ref[0, i * ATT_BLOCK:(i + 1) * ATT_BLOCK, :]
            vt_ref[t + i] = blk.astype(F32).T.astype(BF16)
    row_lo = lax.broadcasted_iota(jnp.int32, (LANES, ATT_BLOCK), 0) < ATT_HEAD_DIM
    nk = 3 * ATT_BLOCK

    def body(i, carry):
        r0 = pl.multiple_of(i * ATT_BLOCK, ATT_BLOCK)
        q = q_ref[0, pl.ds(r0, ATT_BLOCK), :]
        kst = jnp.concatenate([klo_ref[pl.ds(r0, nk), :], khi_ref[pl.ds(r0, nk), :]], axis=0)
        vt = jnp.concatenate([vt_ref[i], vt_ref[i + 1], vt_ref[i + 2]], axis=1)
        first = jnp.logical_and(n == 0, i == 0)
        last = jnp.logical_and(n == nt - 1, i == qb - 1)
        variant = first.astype(jnp.int32) + 2 * last.astype(jnp.int32)
        outs = []
        for j in range(ATT_REP):
            qp = q[:, j * LANES:(j + 1) * LANES]
            st = lax.dot_general(kst, qp, (((1,), (1,)), ((), ())), preferred_element_type=F32)
            halves = []
            for e, h in enumerate((j, ATT_REP + j)):
                s = st[e * nk:(e + 1) * nk] + bias_ref[variant, h]
                sk = sink_ref[h:h + 1, :]
                m = jnp.maximum(jnp.max(s, axis=0, keepdims=True), sk)
                p = jnp.exp2(s - m)
                den = jnp.sum(p, axis=0, keepdims=True) + jnp.exp2(sk - m)
                ot = jnp.dot(vt, p.astype(BF16), preferred_element_type=F32)
                halves.append(ot * (1.0 / den))
            outs.append(jnp.where(row_lo, halves[0], halves[1]).T)
        o = jnp.concatenate(outs, axis=-1)
        y = o * _silu(g_ref[0, pl.ds(r0, ATT_BLOCK), :].astype(F32))
        ms = jnp.mean(y * y, axis=-1, keepdims=True)
        o_ref[0, pl.ds(r0, ATT_BLOCK), :] = (y * lax.rsqrt(ms + EPS) * ng_ref[...]).astype(BF16)
        return carry

    lax.fori_loop(0, qb, body, 0, unroll=True)


def _attention(u3, bias, sink, ng, *, qb):
    B, L, _ = u3.shape
    tq = qb * ATT_BLOCK
    assert L % tq == 0
    nt = L // tq
    nb = L // ATT_BLOCK
    kcol, vcol = C_K // 128, C_V // 128
    def kv_specs(colblk):
        return [pl.BlockSpec((1, ATT_BLOCK, 128), lambda b, n: (b, jnp.maximum(n * qb - 1, 0), colblk)),
                pl.BlockSpec((1, tq, 128), lambda b, n: (b, n, colblk)),
                pl.BlockSpec((1, ATT_BLOCK, 128),
                             lambda b, n: (b, jnp.minimum((n + 1) * qb, nb - 1), colblk))]
    return pl.pallas_call(
        functools.partial(_attn_kernel, nt=nt, qb=qb),
        grid=(B, nt),
        in_specs=[pl.BlockSpec((1, tq, ATT_WIDTH), lambda b, n: (b, n, C_Q // ATT_WIDTH))]
                 + kv_specs(kcol) + kv_specs(vcol)
                 + [pl.BlockSpec((1, tq, ATT_WIDTH), lambda b, n: (b, n, C_GATT // ATT_WIDTH)),
                    pl.BlockSpec((4, ATT_HEADS, 3 * ATT_BLOCK, ATT_BLOCK), lambda b, n: (0, 0, 0, 0)),
                    pl.BlockSpec((ATT_HEADS, LANES), lambda b, n: (0, 0)),
                    pl.BlockSpec((1, ATT_WIDTH), lambda b, n: (0, 0))],
        out_specs=pl.BlockSpec((1, tq, ATT_WIDTH), lambda b, n: (b, n, 0)),
        out_shape=jax.ShapeDtypeStruct((B, L, ATT_WIDTH), BF16),
        scratch_shapes=[pltpu.VMEM(((qb + 2) * ATT_BLOCK, LANES), BF16),
                        pltpu.VMEM(((qb + 2) * ATT_BLOCK, LANES), BF16),
                        pltpu.VMEM((qb + 2, LANES, ATT_BLOCK), BF16)],
        compiler_params=_cparams(("parallel", "parallel")),
        name="attn",
    )(u3, u3, u3, u3, u3, u3, u3, u3, bias, sink, ng)


HALO = BF16_SUBLANES


def _load_padded(xpad_ref, xm_ref, xp_ref, xn_ref, has_prev, has_next, rows):
    xpad_ref[pl.ds(0, HALO), :] = jnp.where(has_prev, xp_ref[0].astype(F32), 0.0)
    xpad_ref[pl.ds(HALO, rows), :] = xm_ref[0].astype(F32)
    xpad_ref[pl.ds(HALO + rows, HALO), :] = jnp.where(has_next, xn_ref[0].astype(F32), 0.0)


def _dwconv(xpad_ref, w_ref, b_ref, rows, width, c0, c1):
    x = xpad_ref[:, c0:c1]
    n = rows + 2 * HALO
    acc = None
    for k in range(width):
        d = k - width // 2
        xk = x if d == 0 else pltpu.roll(x, (-d) % n, axis=0)
        t = xk[HALO:HALO + rows] * w_ref[k:k + 1, c0:c1]
        acc = t if acc is None else acc + t
    return acc + b_ref[:, c0:c1]


SSD_HPG = SSD_HEADS // SSD_GROUPS
SSD_GW = SSD_HPG * SSD_HEAD_DIM
SSD_BC = SSD_GROUPS * SSD_STATE


def _ssd_decay(dt_raw, dtb_ref, alog_ref):
    dt = _softplus(dt_raw + dtb_ref[...])
    dta = dt * (-jnp.exp(alog_ref[...]))
    row = lax.broadcasted_iota(jnp.int32, (CHUNK, CHUNK), 0)
    col = lax.broadcasted_iota(jnp.int32, (CHUNK, CHUNK), 1)
    tril = (row >= col).astype(BF16)
    hi = dta.astype(BF16)
    r1 = dta - hi.astype(F32)
    mid = r1.astype(BF16)
    lo = (r1 - mid.astype(F32)).astype(BF16)
    pre = (jnp.dot(tril, hi, preferred_element_type=F32) + jnp.dot(tril, mid, preferred_element_type=F32)
           + jnp.dot(tril, lo, preferred_element_type=F32))
    tot = pre[CHUNK - 1:CHUNK, :]
    is_bwd = lax.broadcasted_iota(jnp.int32, (1, LANES), 1) >= SSD_HEADS
    acs = jnp.where(is_bwd, tot - pre + dta, pre)
    return dt, acs, tot


def _ssd_bwd_kernel(xm_ref, xp_ref, xn_ref, dt_ref, cw_ref, cb_ref, dtb_ref, alog_ref, rexp_ref,
                    xact_ref, prev_ref, xpad_ref, xf_ref, st_ref, *, nt, qb):
    n = pl.program_id(1)
    tile = nt - 1 - n
    rows = qb * CHUNK

    @pl.when(n == 0)
    def _():
        st_ref[...] = jnp.zeros(st_ref.shape, F32)

    _load_padded(xpad_ref, xm_ref, xp_ref, xn_ref, tile > 0, tile < nt - 1, rows)
    xact = _silu(_dwconv(xpad_ref, cw_ref, cb_ref, rows, SSD_CONV, 0, SSD_XBC))
    xf_ref[...] = xact
    xact_ref[0] = xact.astype(BF16)

    def body(j, carry):
        i = qb - 1 - j
        r0 = pl.multiple_of(i * CHUNK, CHUNK)
        xs = xf_ref[pl.ds(r0, CHUNK), 0:SSD_WIDTH]
        bm = xf_ref[pl.ds(r0, CHUNK), SSD_WIDTH:SSD_WIDTH + SSD_BC].astype(BF16)
        dt, acs, tot = _ssd_decay(dt_ref[0, pl.ds(r0, CHUNK), :], dtb_ref, alog_ref)
        e_in = jnp.concatenate([jnp.exp(tot - acs) * dt, jnp.broadcast_to(jnp.exp(tot), (8, LANES))],
                               axis=0).astype(BF16)
        e_out = jnp.dot(e_in, rexp_ref[...], preferred_element_type=F32)
        xd = (xs * e_out[0:CHUNK]).astype(BF16)
        cdec = e_out[CHUNK:CHUNK + 1]
        for g in range(SSD_GROUPS):
            prev = st_ref[g]
            prev_ref[0, i, g] = prev.astype(BF16)
            s_new = lax.dot_general(bm[:, g * SSD_STATE:(g + 1) * SSD_STATE],
                                    xd[:, g * SSD_GW:(g + 1) * SSD_GW], (((0,), (0,)), ((), ())),
                                    preferred_element_type=F32)
            st_ref[g] = prev * cdec[:, g * SSD_GW:(g + 1) * SSD_GW] + s_new
        return carry

    lax.fori_loop(0, qb, body, 0, unroll=True)


def _ssd_fwd_kernel(xa_ref, dt_ref, prev_ref, z_ref, dtb_ref, alog_ref, rexpf_ref, rexpb_ref,
                    dsk_ref, ng_ref, o_ref, st_ref, *, qb):
    n = pl.program_id(1)

    @pl.when(n == 0)
    def _():
        st_ref[...] = jnp.zeros(st_ref.shape, F32)

    row = lax.broadcasted_iota(jnp.int32, (CHUNK, CHUNK), 0)
    col = lax.broadcasted_iota(jnp.int32, (CHUNK, CHUNK), 1)
    fwd_part = row > col
    diag = row == col
    lane_lo = lax.broadcasted_iota(jnp.int32, (1, LANES), 1) < SSD_HEAD_DIM
    zero_b = jnp.zeros((), BF16)

    def body(i, carry):
        r0 = pl.multiple_of(i * CHUNK, CHUNK)
        xs_b = xa_ref[0, pl.ds(r0, CHUNK), 0:SSD_WIDTH]
        xs = xs_b.astype(F32)
        bm = xa_ref[0, pl.ds(r0, CHUNK), SSD_WIDTH:SSD_WIDTH + SSD_BC]
        cm = xa_ref[0, pl.ds(r0, CHUNK), SSD_WIDTH + SSD_BC:SSD_XBC]
        dt, acs, tot = _ssd_decay(dt_ref[0, pl.ds(r0, CHUNK), :], dtb_ref, alog_ref)
        acs2 = acs * LOG2E
        rt = (acs2 - jnp.log2(dt)).T
        dsum_t = jnp.log2(dt + pltpu.roll(dt, LANES - SSD_HEADS, axis=1)).T
        eacs = jnp.exp(acs)
        ef_in = jnp.concatenate([jnp.exp(tot - acs) * dt, eacs,
                                 jnp.broadcast_to(jnp.exp(tot), (8, LANES))], axis=0).astype(BF16)
        ef = jnp.dot(ef_in, rexpf_ref[...], preferred_element_type=F32)
        eb = jnp.dot(eacs.astype(BF16), rexpb_ref[...], preferred_element_type=F32)
        xd = (xs * ef[0:CHUNK]).astype(BF16)
        eacs_f = ef[CHUNK:2 * CHUNK]
        cdec = ef[2 * CHUNK:2 * CHUNK + 1]
        ys = []
        for g in range(SSD_GROUPS):
            bg = bm[:, g * SSD_STATE:(g + 1) * SSD_STATE]
            cg = cm[:, g * SSD_STATE:(g + 1) * SSD_STATE]
            gs = slice(g * SSD_GW, (g + 1) * SSD_GW)
            cb = lax.dot_general(cg, bg, (((1,), (1,)), ((), ())), preferred_element_type=F32)
            prev = st_ref[g]
            y_off = (jnp.dot(cg, prev.astype(BF16), preferred_element_type=F32) * eacs_f[:, gs]
                     + jnp.dot(cg, prev_ref[0, i, g], preferred_element_type=F32) * eb[:, gs])
            s_new = lax.dot_general(bg, xd[:, gs], (((0,), (0,)), ((), ())),
                                    preferred_element_type=F32)
            st_ref[g] = prev * cdec[:, gs] + s_new
            for pr in range(SSD_HPG // 2):
                c0 = g * SSD_GW + pr * LANES
                xpair = xs_b[:, c0:c0 + LANES]
                xbd = jnp.concatenate([jnp.where(lane_lo, xpair, zero_b),
                                       jnp.where(lane_lo, zero_b, xpair)], axis=0)
                mats = []
                for e in range(2):
                    h = g * SSD_HPG + pr * 2 + e
                    hb = SSD_HEADS + h
                    sel = jnp.where(fwd_part, acs2[:, h:h + 1] - rt[h:h + 1, :],
                                    acs2[:, hb:hb + 1] - rt[hb:hb + 1, :])
                    sel = jnp.where(diag, dsum_t[h:h + 1, :], sel)
                    mats.append((cb * jnp.exp2(sel)).astype(BF16))
                yd = jnp.dot(jnp.concatenate(mats, axis=1), xbd, preferred_element_type=F32)
                ys.append(yd + y_off[:, pr * LANES:(pr + 1) * LANES])
        y = jnp.concatenate(ys, axis=-1) + xs * dsk_ref[...]
        y = y * _silu(z_ref[0, pl.ds(r0, CHUNK), :].astype(F32))
        ms = jnp.mean(y * y, axis=-1, keepdims=True)
        o_ref[0, pl.ds(r0, CHUNK), :] = (y * lax.rsqrt(ms + EPS) * ng_ref[...]).astype(BF16)
        return carry

    lax.fori_loop(0, qb, body, 0, unroll=2 if qb % 2 == 0 else 1)


def _ssd(u3, dt3, cw, cb, dtb, alog, rexp_f, rexp_b, dsk, ng, *, qb):
    B, L, _ = u3.shape
    rows = qb * CHUNK
    assert L % rows == 0
    nt = L // rows
    nc = L // CHUNK
    hb = rows // HALO
    nhb = L // HALO
    state = pltpu.VMEM((SSD_GROUPS, SSD_STATE, SSD_GW), F32)
    const = lambda shape: pl.BlockSpec(shape, lambda b, n: (0,) * len(shape))
    rt = lambda n: nt - 1 - n

    xact, prevb = pl.pallas_call(
        functools.partial(_ssd_bwd_kernel, nt=nt, qb=qb),
        grid=(B, nt),
        in_specs=[
            pl.BlockSpec((1, rows, SSD_XBC), lambda b, n: (b, rt(n), C_XBC // SSD_XBC)),
            pl.BlockSpec((1, HALO, SSD_XBC),
                         lambda b, n: (b, jnp.maximum(rt(n) * hb - 1, 0), C_XBC // SSD_XBC)),
            pl.BlockSpec((1, HALO, SSD_XBC),
                         lambda b, n: (b, jnp.minimum((rt(n) + 1) * hb, nhb - 1), C_XBC // SSD_XBC)),
            pl.BlockSpec((1, rows, DT_PAD), lambda b, n: (b, rt(n), 0)),
            const((8, SSD_XBC)), const((1, SSD_XBC)), const((1, DT_PAD)), const((1, DT_PAD)),
            const((LANES, SSD_WIDTH)),
        ],
        out_specs=[pl.BlockSpec((1, rows, SSD_XBC), lambda b, n: (b, rt(n), 0)),
                   pl.BlockSpec((1, qb, SSD_GROUPS, SSD_STATE, SSD_GW),
                                lambda b, n: (b, rt(n), 0, 0, 0))],
        out_shape=[jax.ShapeDtypeStruct((B, L, SSD_XBC), BF16),
                   jax.ShapeDtypeStruct((B, nc, SSD_GROUPS, SSD_STATE, SSD_GW), BF16)],
        scratch_shapes=[pltpu.VMEM((rows + 2 * HALO, SSD_XBC), F32),
                        pltpu.VMEM((rows, SSD_XBC), F32), state],
        compiler_params=_cparams(("parallel", "arbitrary")),
        name="ssd_bwd",
    )(u3, u3, u3, dt3, cw, cb, dtb, alog, rexp_b)

    return pl.pallas_call(
        functools.partial(_ssd_fwd_kernel, qb=qb),
        grid=(B, nt),
        in_specs=[
            pl.BlockSpec((1, rows, SSD_XBC), lambda b, n: (b, n, 0)),
            pl.BlockSpec((1, rows, DT_PAD), lambda b, n: (b, n, 0)),
            pl.BlockSpec((1, qb, SSD_GROUPS, SSD_STATE, SSD_GW), lambda b, n: (b, n, 0, 0, 0)),
            pl.BlockSpec((1, rows, SSD_WIDTH), lambda b, n: (b, n, C_Z // SSD_WIDTH)),
            const((1, DT_PAD)), const((1, DT_PAD)),
            const((LANES, SSD_WIDTH)), const((LANES, SSD_WIDTH)),
            const((1, SSD_WIDTH)), const((1, SSD_WIDTH)),
        ],
        out_specs=pl.BlockSpec((1, rows, SSD_WIDTH), lambda b, n: (b, n, 0)),
        out_shape=jax.ShapeDtypeStruct((B, L, SSD_WIDTH), BF16),
        scratch_shapes=[state],
        compiler_params=_cparams(("parallel", "arbitrary")),
        name="ssd_fwd",
    )(xact, dt3, prevb, u3, dtb, alog, rexp_f, rexp_b, dsk, ng)


def _ssd_expand_matrix(rev):
    m = np.zeros((LANES, SSD_WIDTH), np.float32)
    hoff = SSD_HEADS if rev else 0
    for h in range(SSD_HEADS):
        m[hoff + h, h * SSD_HEAD_DIM:(h + 1) * SSD_HEAD_DIM] = 1.0
    return jnp.asarray(m, BF16)


def _hy_filter_kernel(t_ref, bands_ref, w1t_ref, w1c_ref, w1s_ref, b1_ref, w2_ref, b2_ref, w3_ref,
                      b3_ref, w4_ref, fr_ref, absd_ref, x_ref, *, L, tl):
    i = pl.program_id(0)
    pos = (lax.broadcasted_iota(jnp.int32, (tl, 1), 0) + i * tl).astype(F32)
    t = t_ref[...]
    ang = 2.0 * math.pi * pos * bands_ref[...] / L
    fr = fr_ref[...]
    dot = functools.partial(jnp.dot, preferred_element_type=F32, precision=HIGHEST)
    pre = t * w1t_ref[...] + dot(jnp.cos(ang), w1c_ref[...]) + dot(-jnp.sin(ang), w1s_ref[...])
    h = jnp.sin(fr * (pre + b1_ref[...]))
    h = jnp.sin(fr * (dot(h, w2_ref[...]) + b2_ref[...]))
    h = jnp.sin(fr * (dot(h, w3_ref[...]) + b3_ref[...]))
    h = dot(h, w4_ref[...])
    decay = jnp.exp(-t * absd_ref[...])
    hf = h[:, :HY_WIDTH] * decay
    hb = jnp.where(pos == 0.0, 0.0, h[:, HY_WIDTH:] * decay)
    x_ref[:, :HY_WIDTH] = (hf + hb).astype(BF16)
    x_ref[:, HY_WIDTH:] = (hb - hf).astype(BF16)


def _hy_filter(L, w1, b1, w2, b2, w3, b3, w4, freq):
    tl = min(512, L)
    t = jnp.linspace(0.0, 1.0, L, dtype=F32)[:, None]
    bands = jnp.linspace(1e-4, HY_EMB_BANDS - 1, HY_EMB_BANDS, dtype=F32)[None, :]
    max_decay = math.log(HY_DECAY_TARGET) / HY_FAST_DECAY
    min_decay = math.log(HY_DECAY_TARGET) / HY_SLOW_DECAY
    absd = jnp.abs(jnp.linspace(min_decay, max_decay, HY_WIDTH, dtype=F32))[None, :]
    w1 = w1.astype(F32)
    full = lambda a: pl.BlockSpec(a.shape, lambda i: (0,) * a.ndim)
    args = [t, bands, w1[0:1], w1[1:1 + HY_EMB_BANDS], w1[1 + HY_EMB_BANDS:], b1[None], w2, b2[None],
            w3, b3[None], w4, freq[None], absd]
    in_specs = [pl.BlockSpec((tl, 1), lambda i: (i, 0))] + [full(a) for a in args[1:]]
    return pl.pallas_call(
        functools.partial(_hy_filter_kernel, L=L, tl=tl),
        grid=(L // tl,),
        in_specs=in_specs,
        out_specs=pl.BlockSpec((tl, 2 * HY_WIDTH), lambda i: (i, 0)),
        out_shape=jax.ShapeDtypeStruct((L, 2 * HY_WIDTH), BF16),
        compiler_params=_cparams(("parallel",)),
        name="hy_filter",
    )(*args)


def _dft_tables(L, hm):
    L2 = L // 2
    g = jnp.arange(L2, dtype=jnp.int32)
    s = jnp.arange(L2, dtype=jnp.int32)
    ph = ((2 * g + 1)[:, None] * s[None, :]) % (2 * L)
    ang = ph.astype(F32) * (math.pi / L)
    c = jnp.cos(ang).astype(BF16).reshape(L2 // hm, 1, hm, L2)
    sn = jnp.sin(ang).astype(BF16).reshape(L2 // hm, 1, hm, L2)
    a_fwd = jnp.concatenate([c, sn], axis=1).reshape(L, L2)
    a_inv = jnp.concatenate([c, -sn], axis=1).reshape(L, L2).T
    return a_fwd, a_inv


def _hy_kraw_kernel(a_ref, x_ref, k_ref):
    k_ref[...] = jnp.dot(a_ref[...], x_ref[...], preferred_element_type=F32)


def _hy_kraw(a_fwd, xf2, *, hm):
    L, L2 = a_fwd.shape
    tm = 2 * hm
    nx = xf2.shape[1]
    return pl.pallas_call(
        _hy_kraw_kernel,
        grid=(L // tm,),
        in_specs=[pl.BlockSpec((tm, L2), lambda i: (i, 0)),
                  pl.BlockSpec((L2, nx), lambda i: (0, 0))],
        out_specs=pl.BlockSpec((tm, nx), lambda i: (i, 0)),
        out_shape=jax.ShapeDtypeStruct((L, nx), F32),
        compiler_params=_cparams(("parallel",)),
        name="hy_kspec",
    )(a_fwd, xf2)


def _hy_filter_spectrum(pq, L, hm):
    L2 = L // 2
    W = HY_WIDTH
    pq = pq.reshape(L2 // hm, 2, hm, 4 * W)
    p = pq[:, 0].reshape(L2, 4 * W)
    q = pq[:, 1].reshape(L2, 4 * W)
    ea, eb, oa, ob = [(p[:, i * W:(i + 1) * W], -q[:, i * W:(i + 1) * W]) for i in range(4)]
    theta = (2.0 * jnp.arange(L2, dtype=F32) + 1.0) * (math.pi / (2 * L))
    w = (jnp.cos(theta)[:, None], -jnp.sin(theta)[:, None])
    cmul = lambda x, y: (x[0] * y[0] - x[1] * y[1], x[0] * y[1] + x[1] * y[0])
    woa, wob = cmul(w, oa), cmul(w, ob)
    k1 = (ea[0] + woa[0], -(eb[1] + wob[1]))
    k2 = (ea[0] - woa[0], eb[1] - wob[1])
    kp = (k1[0] + k2[0], k1[1] - k2[1])
    km = (k1[0] - k2[0], k1[1] + k2[1])
    wkm = cmul(w, km)
    vkm = cmul((w[0], -w[1]), km)
    filt = jnp.stack([kp[0], kp[1], wkm[0], wkm[1], vkm[0], vkm[1]]) * (1.0 / L)
    return jnp.transpose(filt.reshape(6, L2 // hm, hm, W), (1, 0, 2, 3))


def _hy_pre_kernel(xm_ref, xp_ref, xn_ref, cw_ref, cb_ref, u_ref, xpad_ref, *, nt, tl):
    i = pl.program_id(1)
    _load_padded(xpad_ref, xm_ref, xp_ref, xn_ref, i > 0, i < nt - 1, tl)
    xb = _dwconv(xpad_ref, cw_ref, cb_ref, tl, HY_CONV, HY_WIDTH, 2 * HY_WIDTH)
    v = _dwconv(xpad_ref, cw_ref, cb_ref, tl, HY_CONV, 2 * HY_WIDTH, 3 * HY_WIDTH)
    u_ref[0] = (xb * v).astype(BF16)


def _hy_tile_specs(L, tl):
    hb = tl // HALO
    nhb = L // HALO
    w = 3 * HY_WIDTH
    return [pl.BlockSpec((1, tl, w), lambda b, i: (b, i, C_HY // w)),
            pl.BlockSpec((1, HALO, w), lambda b, i: (b, jnp.maximum(i * hb - 1, 0), C_HY // w)),
            pl.BlockSpec((1, HALO, w), lambda b, i: (b, jnp.minimum((i + 1) * hb, nhb - 1), C_HY // w))]


def _hy_pre(u3, cw, cb, *, tl):
    B, L, _ = u3.shape
    nt = L // tl
    w = 3 * HY_WIDTH
    return pl.pallas_call(
        functools.partial(_hy_pre_kernel, nt=nt, tl=tl),
        grid=(B, nt),
        in_specs=_hy_tile_specs(L, tl) + [pl.BlockSpec((8, w), lambda b, i: (0, 0)),
                                                 pl.BlockSpec((1, w), lambda b, i: (0, 0))],
        out_specs=pl.BlockSpec((1, tl, HY_WIDTH), lambda b, i: (b, i, 0)),
        out_shape=jax.ShapeDtypeStruct((B, L, HY_WIDTH), BF16),
        scratch_shapes=[pltpu.VMEM((tl + 2 * HALO, w), F32)],
        compiler_params=_cparams(("parallel", "parallel")),
        name="hy_pre",
    )(u3, u3, u3, cw, cb)


def _hy_fwd_kernel(a_ref, u_ref, f_ref, y_ref, *, hm):
    W = HY_WIDTH
    pq = jnp.dot(a_ref[...], u_ref[0], preferred_element_type=F32)
    pe, po = pq[0:hm, :W], pq[0:hm, W:]
    qe, qo = pq[hm:, :W], pq[hm:, W:]
    kpr, kpi, wr, wi, vr, vi = [f_ref[0, i] for i in range(6)]
    y_ref[0, 0:hm, :W] = (pe * kpr + qe * kpi + po * wr + qo * wi).astype(BF16)
    y_ref[0, hm:, :W] = (pe * kpi - qe * kpr + po * wi - qo * wr).astype(BF16)
    y_ref[0, 0:hm, W:] = (pe * vr + qe * vi + po * kpr + qo * kpi).astype(BF16)
    y_ref[0, hm:, W:] = (pe * vi - qe * vr + po * kpi - qo * kpr).astype(BF16)


def _hy_fwd(a_fwd, ueo, filt, *, hm):
    B, L2, W2 = ueo.shape
    L = 2 * L2
    tm = 2 * hm
    return pl.pallas_call(
        functools.partial(_hy_fwd_kernel, hm=hm),
        grid=(L // tm, B),
        in_specs=[pl.BlockSpec((tm, L2), lambda i, b: (i, 0)),
                  pl.BlockSpec((1, L2, W2), lambda i, b: (b, 0, 0)),
                  pl.BlockSpec((1, 6, hm, HY_WIDTH), lambda i, b: (i, 0, 0, 0))],
        out_specs=pl.BlockSpec((1, tm, W2), lambda i, b: (b, i, 0)),
        out_shape=jax.ShapeDtypeStruct((B, L, W2), BF16),
        compiler_params=_cparams(("parallel", "parallel")),
        name="hy_fwd",
    )(a_fwd, ueo, filt)


def _hy_inv_kernel(a_ref, y_ref, o_ref):
    o_ref[0] = jnp.dot(a_ref[...], y_ref[0], preferred_element_type=F32)


def _hy_inv(a_inv, yspec, *, tm):
    B, L, W2 = yspec.shape
    L2 = L // 2
    return pl.pallas_call(
        _hy_inv_kernel,
        grid=(L2 // tm, B),
        in_specs=[pl.BlockSpec((tm, L), lambda i, b: (i, 0)),
                  pl.BlockSpec((1, L, W2), lambda i, b: (b, 0, 0))],
        out_specs=pl.BlockSpec((1, tm, W2), lambda i, b: (b, i, 0)),
        out_shape=jax.ShapeDtypeStruct((B, L2, W2), F32),
        compiler_params=_cparams(("parallel", "parallel")),
        name="hy_inv",
    )(a_inv, yspec)


def _hy_post_kernel(c_ref, xm_ref, xp_ref, xn_ref, g_ref, cw_ref, cb_ref, d_ref, ng_ref,
                    o_ref, xpad_ref, *, nt, tl):
    i = pl.program_id(1)
    _load_padded(xpad_ref, xm_ref, xp_ref, xn_ref, i > 0, i < nt - 1, tl)
    xa = _dwconv(xpad_ref, cw_ref, cb_ref, tl, HY_CONV, 0, HY_WIDTH)
    xb = _dwconv(xpad_ref, cw_ref, cb_ref, tl, HY_CONV, HY_WIDTH, 2 * HY_WIDTH)
    v = _dwconv(xpad_ref, cw_ref, cb_ref, tl, HY_CONV, 2 * HY_WIDTH, 3 * HY_WIDTH)
    u = xb * v
    y = xa * (c_ref[0] + u * d_ref[...])
    y = y * _silu(g_ref[0].astype(F32))
    ms = jnp.mean(y * y, axis=-1, keepdims=True)
    o_ref[0] = (y * lax.rsqrt(ms + EPS) * ng_ref[...]).astype(BF16)


def _hy_post(conv, u3, cw, cb, d, ng, *, tl):
    B, L, _ = u3.shape
    nt = L // tl
    w = 3 * HY_WIDTH
    return pl.pallas_call(
        functools.partial(_hy_post_kernel, nt=nt, tl=tl),
        grid=(B, nt),
        in_specs=[pl.BlockSpec((1, tl, HY_WIDTH), lambda b, i: (b, i, 0))]
                 + _hy_tile_specs(L, tl)
                 + [pl.BlockSpec((1, tl, HY_WIDTH), lambda b, i: (b, i, C_GHY // HY_WIDTH)),
                    pl.BlockSpec((8, w), lambda b, i: (0, 0)),
                    pl.BlockSpec((1, w), lambda b, i: (0, 0)),
                    pl.BlockSpec((1, HY_WIDTH), lambda b, i: (0, 0)),
                    pl.BlockSpec((1, HY_WIDTH), lambda b, i: (0, 0))],
        out_specs=pl.BlockSpec((1, tl, HY_WIDTH), lambda b, i: (b, i, 0)),
        out_shape=jax.ShapeDtypeStruct((B, L, HY_WIDTH), BF16),
        scratch_shapes=[pltpu.VMEM((tl + 2 * HALO, w), F32)],
        compiler_params=_cparams(("parallel", "parallel")),
        name="hy_post",
    )(conv, u3, u3, u3, u3, cw, cb, d, ng)


def _pad_rows(a, rows):
    return jnp.concatenate([a, jnp.zeros((rows - a.shape[0],) + a.shape[1:], a.dtype)], axis=0)


def _pad_cols(a, cols):
    return jnp.concatenate([a, jnp.zeros(a.shape[:-1] + (cols - a.shape[-1],), a.dtype)], axis=-1)


def _static_take(a, idx, axis):
    idx = np.asarray(idx)
    cuts = np.flatnonzero(np.diff(idx) != 1) + 1
    starts = np.concatenate([[0], cuts])
    ends = np.concatenate([cuts, [len(idx)]])
    parts = [lax.slice_in_dim(a, int(idx[s]), int(idx[e - 1]) + 1, axis=axis)
             for s, e in zip(starts, ends)]
    return jnp.concatenate(parts, axis=axis)


def _tile(n, pref):
    t = min(pref, n)
    assert n % t == 0
    return t


def _layer(x3, p, tables, *, final, final_g):
    B, L, _ = x3.shape
    T = B * L
    x2 = x3.reshape(T, D_MODEL)
    u, dt = _inproj(x2, p["norm_g"], p["w_in"], tm=_tile(T, 512), tn=512)
    u3 = u.reshape(B, L, N_IN)
    dt3 = dt.reshape(B, L, DT_PAD)

    ys = _ssd(u3, dt3, p["ssd_cw"], p["ssd_cb"], p["ssd_dtb"], p["ssd_alog"], p["rexp_f"],
              p["rexp_b"], p["ssd_dskip"], p["ssd_ng"], qb=_tile(L // CHUNK, 4))

    ya = _attention(u3, p["att_bias"], p["att_sink"], p["att_ng"], qb=_tile(L // ATT_BLOCK, 4))

    a_fwd, a_inv, filt, hm = tables
    tl = _tile(L, 512)
    uc = _hy_pre(u3, p["hy_cw"], p["hy_cb"], tl=tl)
    yspec = _hy_fwd(a_fwd, uc.reshape(B, L // 2, 2 * HY_WIDTH), filt, hm=hm)
    conv = _hy_inv(a_inv, yspec, tm=_tile(L // 2, 512)).reshape(B, L, HY_WIDTH)
    yh = _hy_post(conv, u3, p["hy_cw"], p["hy_cb"], p["hy_d"], p["hy_ng"], tl=tl)

    out = _outproj(x2, ys.reshape(T, SSD_WIDTH), ya.reshape(T, ATT_WIDTH), yh.reshape(T, HY_WIDTH),
                   p["w_out"], final_g, tm=_tile(T, 512), final=final)
    return out.reshape(B, L, D_MODEL)


def kernel(x_prompt, x_sample, rel_bias, norm_g, w_in, ssd_conv_w, ssd_conv_b, ssd_dt_bias, ssd_a_log, ssd_d, ssd_norm_g, att_sink, att_norm_g, hy_conv_w, hy_conv_b, hy_w1, hy_b1, hy_w2, hy_b2, hy_w3, hy_b3, hy_w4, hy_freq, hy_d, hy_norm_g, w_out, final_norm_g):
    depth = w_in.shape[0]
    perm, col_scale = _in_perm()
    n_real = C_DT + 2 * SSD_HEADS
    w_in_p = _pad_cols((_static_take(w_in, perm[:n_real], 2)
                        * jnp.asarray(col_scale[:n_real])).astype(BF16), N_IN)
    att_perm = _att_col_perm()
    out_rows = np.concatenate([np.arange(SSD_WIDTH), SSD_WIDTH + att_perm,
                               np.arange(SSD_WIDTH + ATT_WIDTH, D_MODEL)])
    w_out_b = _static_take(w_out, out_rows, 1).astype(BF16)
    att_bias = _attn_bias(rel_bias)
    rexp_f, rexp_b = _ssd_expand_matrix(False), _ssd_expand_matrix(True)
    final_g = final_norm_g.astype(F32)[None, :]

    layers = []
    for i in range(depth):
        layers.append(dict(
            norm_g=norm_g[i].astype(F32)[None, :],
            w_in=w_in_p[i],
            ssd_cw=_pad_rows(ssd_conv_w[i].astype(F32), 8),
            ssd_cb=ssd_conv_b[i].astype(F32)[None, :],
            ssd_dtb=_pad_cols(ssd_dt_bias[i].astype(F32).reshape(1, 2 * SSD_HEADS), DT_PAD),
            ssd_alog=_pad_cols(ssd_a_log[i].astype(F32).reshape(1, 2 * SSD_HEADS), DT_PAD),
            ssd_dskip=jnp.repeat(ssd_d[i].astype(F32), SSD_HEAD_DIM)[None, :],
            ssd_ng=ssd_norm_g[i].astype(F32)[None, :],
            rexp_f=rexp_f, rexp_b=rexp_b,
            att_bias=att_bias,
            att_sink=jnp.broadcast_to(att_sink[i].astype(F32)[:, None] * LOG2E, (ATT_HEADS, LANES)),
            att_ng=att_norm_g[i].astype(F32)[att_perm][None, :],
            hy_cw=_pad_rows(hy_conv_w[i].astype(F32), 8),
            hy_cb=hy_conv_b[i].astype(F32)[None, :],
            hy_d=hy_d[i].astype(F32)[None, :],
            hy_ng=hy_norm_g[i].astype(F32)[None, :],
            w_out=w_out_b[i],
        ))

    def trunk(x):
        L = x.shape[1]
        hm = min(256, L // 4)
        a_fwd, a_inv = _dft_tables(L, hm)
        for i in range(depth):
            xf = _hy_filter(L, hy_w1[i], hy_b1[i].astype(F32), hy_w2[i].astype(F32),
                            hy_b2[i].astype(F32), hy_w3[i].astype(F32), hy_b3[i].astype(F32),
                            hy_w4[i].astype(F32), hy_freq[i].astype(F32))
            pq = _hy_kraw(a_fwd, xf.reshape(L // 2, 4 * HY_WIDTH), hm=hm)
            filt = _hy_filter_spectrum(pq, L, hm)
            x = _layer(x, layers[i], (a_fwd, a_inv, filt, hm), final=(i == depth - 1),
                       final_g=final_g)
        return x

    return (trunk(x_prompt), trunk(x_sample))
```

```python
import functools
import math

import jax
import jax.numpy as jnp
import numpy as np
from jax import lax
from jax.experimental import pallas as pl
from jax.experimental.pallas import tpu as pltpu

F32 = jnp.float32
BF16 = jnp.bfloat16
HIGHEST = lax.Precision.HIGHEST

D_MODEL = 2048
SSD_WIDTH = 1024
ATT_WIDTH = 512
HY_WIDTH = 512
SSD_HEAD_DIM = 64
SSD_HEADS = 16
SSD_GROUPS = 2
SSD_STATE = 128
SSD_CONV = 5
CHUNK = 128
SSD_XBC = SSD_WIDTH + 2 * SSD_GROUPS * SSD_STATE
ATT_HEAD_DIM = 64
ATT_HEADS = 8
ATT_KV_HEADS = 2
ATT_REP = ATT_HEADS // ATT_KV_HEADS
ATT_WINDOW = 128
ATT_BLOCK = 128
REL_BUCKETS = 32
REL_MAX_DIST = 128
HY_CONV = 3
HY_EMB_BANDS = 16
HY_FF = 64
HY_FAST_DECAY = 0.3
HY_SLOW_DECAY = 1.5
HY_DECAY_TARGET = 1e-2
EPS = 1e-6
NEG_BIG = -1e30

LANES = 128
BF16_SUBLANES = 16
VMEM_LIMIT = 56 * 1024 * 1024

C_XBC = 0
C_HY = 1536
C_Z = 3072
C_Q = 4096
C_GATT = 4608
C_GHY = 5120
C_K = 5632
C_V = 5760
C_DT = 5888
N_IN = 6144
DT_PAD = 128

_OLD_SIZES = [SSD_WIDTH, SSD_XBC, 2 * SSD_HEADS, ATT_WIDTH, ATT_KV_HEADS * ATT_HEAD_DIM,
              ATT_KV_HEADS * ATT_HEAD_DIM, ATT_WIDTH, 3 * HY_WIDTH, HY_WIDTH]
_OLD_OFF = np.concatenate([[0], np.cumsum(_OLD_SIZES)])
IN_COLS = int(_OLD_OFF[-1])


LOG2E = math.log2(math.e)
Q_SCALE = ATT_HEAD_DIM ** -0.5 * LOG2E


def _att_col_perm():
    order = [h for j in range(ATT_REP) for h in (j, ATT_REP + j)]
    return np.concatenate([np.arange(h * ATT_HEAD_DIM, (h + 1) * ATT_HEAD_DIM) for h in order])


def _in_perm():
    perm = np.full((N_IN,), IN_COLS, np.int32)
    scale = np.ones((N_IN,), np.float32)
    o = {n: int(_OLD_OFF[i]) for i, n in enumerate(
        ["z", "xbc", "dt", "q", "k", "v", "gatt", "hy", "ghy"])}
    def put(new, old, width):
        perm[new:new + width] = np.arange(old, old + width)
    put(C_XBC, o["xbc"], SSD_XBC)
    put(C_HY, o["hy"], 3 * HY_WIDTH)
    put(C_Z, o["z"], SSD_WIDTH)
    perm[C_Q:C_Q + ATT_WIDTH] = o["q"] + _att_col_perm()
    scale[C_Q:C_Q + ATT_WIDTH] = Q_SCALE
    perm[C_GATT:C_GATT + ATT_WIDTH] = o["gatt"] + _att_col_perm()
    put(C_GHY, o["ghy"], HY_WIDTH)
    put(C_K, o["k"], 128)
    put(C_V, o["v"], 128)
    put(C_DT, o["dt"], 2 * SSD_HEADS)
    return perm, scale


def _cparams(sem):
    return pltpu.CompilerParams(dimension_semantics=sem, vmem_limit_bytes=VMEM_LIMIT)


def _silu(x):
    return x * (1.0 / (1.0 + jnp.exp(-x)))


def _softplus(x):
    return jnp.maximum(x, 0.0) + jnp.log1p(jnp.exp(-jnp.abs(x)))


def _inproj_kernel(x_ref, g_ref, w_ref, u_ref, dt_ref, h_ref, *, rows, tn):
    for r in range(x_ref.shape[0] // rows):
        sl = slice(r * rows, (r + 1) * rows)
        x = x_ref[sl, :]
        ms = jnp.mean(x * x, axis=-1, keepdims=True)
        h_ref[sl, :] = (x * lax.rsqrt(ms + EPS) * g_ref[...]).astype(BF16)
        h = h_ref[sl, :]
        for j in range(N_IN // tn):
            acc = jnp.dot(h, w_ref[:, j * tn:(j + 1) * tn], preferred_element_type=F32)
            u_ref[sl, j * tn:(j + 1) * tn] = acc.astype(BF16)
            if j == C_DT // tn:
                dt_ref[sl, :] = acc[:, C_DT % tn:C_DT % tn + DT_PAD]


def _inproj(x2, g, w, *, tm, tn):
    T = x2.shape[0]
    assert T % tm == 0 and N_IN % tn == 0
    return pl.pallas_call(
        functools.partial(_inproj_kernel, rows=min(256, tm), tn=tn),
        grid=(T // tm,),
        in_specs=[pl.BlockSpec((tm, D_MODEL), lambda i: (i, 0)),
                  pl.BlockSpec((1, D_MODEL), lambda i: (0, 0)),
                  pl.BlockSpec((D_MODEL, N_IN), lambda i: (0, 0), pipeline_mode=pl.Buffered(1))],
        out_specs=[pl.BlockSpec((tm, N_IN), lambda i: (i, 0)),
                   pl.BlockSpec((tm, DT_PAD), lambda i: (i, 0))],
        out_shape=[jax.ShapeDtypeStruct((T, N_IN), BF16),
                   jax.ShapeDtypeStruct((T, DT_PAD), F32)],
        scratch_shapes=[pltpu.VMEM((tm, D_MODEL), BF16)],
        compiler_params=_cparams(("parallel",)),
        name="inproj",
    )(x2, g, w)


def _outproj_kernel(x_ref, ys_ref, ya_ref, yh_ref, w_ref, fg_ref, o_ref, *, final):
    acc = x_ref[...]
    acc = acc + jnp.dot(ys_ref[...], w_ref[0:SSD_WIDTH, :], preferred_element_type=F32)
    acc = acc + jnp.dot(ya_ref[...], w_ref[SSD_WIDTH:SSD_WIDTH + ATT_WIDTH, :],
                        preferred_element_type=F32)
    acc = acc + jnp.dot(yh_ref[...], w_ref[SSD_WIDTH + ATT_WIDTH:, :], preferred_element_type=F32)
    if final:
        ms = jnp.mean(acc * acc, axis=-1, keepdims=True)
        acc = acc * lax.rsqrt(ms + EPS) * fg_ref[...]
    o_ref[...] = acc


def _outproj(x2, ys, ya, yh, w, fg, *, tm, final):
    T = x2.shape[0]
    assert T % tm == 0
    return pl.pallas_call(
        functools.partial(_outproj_kernel, final=final),
        grid=(T // tm,),
        in_specs=[pl.BlockSpec((tm, D_MODEL), lambda i: (i, 0)),
                  pl.BlockSpec((tm, SSD_WIDTH), lambda i: (i, 0)),
                  pl.BlockSpec((tm, ATT_WIDTH), lambda i: (i, 0)),
                  pl.BlockSpec((tm, HY_WIDTH), lambda i: (i, 0)),
                  pl.BlockSpec((D_MODEL, D_MODEL), lambda i: (0, 0)),
                  pl.BlockSpec((1, D_MODEL), lambda i: (0, 0))],
        out_specs=pl.BlockSpec((tm, D_MODEL), lambda i: (i, 0)),
        out_shape=jax.ShapeDtypeStruct((T, D_MODEL), F32),
        compiler_params=_cparams(("parallel",)),
        name="outproj",
    )(x2, ys, ya, yh, w, fg)


def _t5_buckets(rel):
    nb = REL_BUCKETS // 2
    max_exact = nb // 2
    ret = (rel > 0).astype(np.int32) * nb
    n = np.abs(rel)
    large = max_exact + (np.log(np.maximum(n, 1) / max_exact) / math.log(REL_MAX_DIST / max_exact)
                         * (nb - max_exact)).astype(np.int32)
    large = np.minimum(large, nb - 1)
    return ret + np.where(n < max_exact, n, large)


def _attn_bias(rel_bias):
    qi = np.arange(ATT_BLOCK)[:, None]
    kj = np.arange(3 * ATT_BLOCK)[None, :]
    rel = kj - ATT_BLOCK - qi
    onehot = (_t5_buckets(rel)[None] == np.arange(REL_BUCKETS)[:, None, None]).astype(np.float32)
    bias = jnp.einsum("bqk,bh->hqk", jnp.asarray(onehot), rel_bias.astype(F32),
                      precision=HIGHEST) * LOG2E
    window = np.abs(rel) <= ATT_WINDOW
    variants = []
    for last in (False, True):
        for first in (False, True):
            ok = window & ~(first & (kj < ATT_BLOCK)) & ~(last & (kj >= 2 * ATT_BLOCK))
            variants.append(jnp.where(ok[None], bias, NEG_BIG))
    return jnp.transpose(jnp.stack(variants), (0, 1, 3, 2))


def _attn_kernel(q_ref, kp_ref, kc_ref, kn_ref, vp_ref, vc_ref, vn_ref, g_ref, bias_ref, sink_ref,
                 ng_ref, o_ref, klo_ref, khi_ref, vt_ref, *, nt, qb):
    n = pl.program_id(1)
    lo = lax.broadcasted_iota(jnp.int32, (1, LANES), 1) < ATT_HEAD_DIM
    zero = jnp.zeros((), BF16)
    kext = jnp.concatenate([kp_ref[0], kc_ref[0], kn_ref[0]], axis=0)
    klo_ref[...] = jnp.where(lo, kext, zero)
    khi_ref[...] = jnp.where(lo, zero, kext)
    for t, ref, cnt in ((0, vp_ref, 1), (1, vc_ref, qb), (qb + 1, vn_ref, 1)):
        for i in range(cnt):
            blk = ref[0, i * ATT_BLOCK:(i + 1) * ATT_BLOCK, :]
            vt_ref[t + i] = blk.astype(F32).T.astype(BF16)
    row_lo = lax.broadcasted_iota(jnp.int32, (LANES, ATT_BLOCK), 0) < ATT_HEAD_DIM
    nk = 3 * ATT_BLOCK

    def body(i, carry):
        r0 = pl.multiple_of(i * ATT_BLOCK, ATT_BLOCK)
        q = q_ref[0, pl.ds(r0, ATT_BLOCK), :]
        kst = jnp.concatenate([klo_ref[pl.ds(r0, nk), :], khi_ref[pl.ds(r0, nk), :]], axis=0)
        vt = jnp.concatenate([vt_ref[i], vt_ref[i + 1], vt_ref[i + 2]], axis=1)
        first = jnp.logical_and(n == 0, i == 0)
        last = jnp.logical_and(n == nt - 1, i == qb - 1)
        variant = first.astype(jnp.int32) + 2 * last.astype(jnp.int32)
        outs = []
        for j in range(ATT_REP):
            qp = q[:, j * LANES:(j + 1) * LANES]
            st = lax.dot_general(kst, qp, (((1,), (1,)), ((), ())), preferred_element_type=F32)
            halves = []
            for e, h in enumerate((j, ATT_REP + j)):
                s = st[e * nk:(e + 1) * nk] + bias_ref[variant, h]
                sk = sink_ref[h:h + 1, :]
                m = jnp.maximum(jnp.max(s, axis=0, keepdims=True), sk)
                p = jnp.exp2(s - m)
                den = jnp.sum(p, axis=0, keepdims=True) + jnp.exp2(sk - m)
                ot = jnp.dot(vt, p.astype(BF16), preferred_element_type=F32)
                halves.append(ot * (1.0 / den))
            outs.append(jnp.where(row_lo, halves[0], halves[1]).T)
        o = jnp.concatenate(outs, axis=-1)
        y = o * _silu(g_ref[0, pl.ds(r0, ATT_BLOCK), :].astype(F32))
        ms = jnp.mean(y * y, axis=-1, keepdims=True)
        o_ref[0, pl.ds(r0, ATT_BLOCK), :] = (y * lax.rsqrt(ms + EPS) * ng_ref[...]).astype(BF16)
        return carry

    lax.fori_loop(0, qb, body, 0, unroll=True)


def _attention(u3, bias, sink, ng, *, qb):
    B, L, _ = u3.shape
    tq = qb * ATT_BLOCK
    assert L % tq == 0
    nt = L // tq
    nb = L // ATT_BLOCK
    kcol, vcol = C_K // 128, C_V // 128
    def kv_specs(colblk):
        return [pl.BlockSpec((1, ATT_BLOCK, 128), lambda b, n: (b, jnp.maximum(n * qb - 1, 0), colblk)),
                pl.BlockSpec((1, tq, 128), lambda b, n: (b, n, colblk)),
                pl.BlockSpec((1, ATT_BLOCK, 128),
                             lambda b, n: (b, jnp.minimum((n + 1) * qb, nb - 1), colblk))]
    return pl.pallas_call(
        functools.partial(_attn_kernel, nt=nt, qb=qb),
        grid=(B, nt),
        in_specs=[pl.BlockSpec((1, tq, ATT_WIDTH), lambda b, n: (b, n, C_Q // ATT_WIDTH))]
                 + kv_specs(kcol) + kv_specs(vcol)
                 + [pl.BlockSpec((1, tq, ATT_WIDTH), lambda b, n: (b, n, C_GATT // ATT_WIDTH)),
                    pl.BlockSpec((4, ATT_HEADS, 3 * ATT_BLOCK, ATT_BLOCK), lambda b, n: (0, 0, 0, 0)),
                    pl.BlockSpec((ATT_HEADS, LANES), lambda b, n: (0, 0)),
                    pl.BlockSpec((1, ATT_WIDTH), lambda b, n: (0, 0))],
        out_specs=pl.BlockSpec((1, tq, ATT_WIDTH), lambda b, n: (b, n, 0)),
        out_shape=jax.ShapeDtypeStruct((B, L, ATT_WIDTH), BF16),
        scratch_shapes=[pltpu.VMEM(((qb + 2) * ATT_BLOCK, LANES), BF16),
                        pltpu.VMEM(((qb + 2) * ATT_BLOCK, LANES), BF16),
                        pltpu.VMEM((qb + 2, LANES, ATT_BLOCK), BF16)],
        compiler_params=_cparams(("parallel", "parallel")),
        name="attn",
    )(u3, u3, u3, u3, u3, u3, u3, u3, bias, sink, ng)


HALO = BF16_SUBLANES


def _load_padded(xpad_ref, xm_ref, xp_ref, xn_ref, has_prev, has_next, rows):
    xpad_ref[pl.ds(0, HALO), :] = jnp.where(has_prev, xp_ref[0].astype(F32), 0.0)
    xpad_ref[pl.ds(HALO, rows), :] = xm_ref[0].astype(F32)
    xpad_ref[pl.ds(HALO + rows, HALO), :] = jnp.where(has_next, xn_ref[0].astype(F32), 0.0)


def _dwconv(xpad_ref, w_ref, b_ref, rows, width, c0, c1):
    x = xpad_ref[:, c0:c1]
    n = rows + 2 * HALO
    acc = None
    for k in range(width):
        d = k - width // 2
        xk = x if d == 0 else pltpu.roll(x, (-d) % n, axis=0)
        t = xk[HALO:HALO + rows] * w_ref[k:k + 1, c0:c1]
        acc = t if acc is None else acc + t
    return acc + b_ref[:, c0:c1]


SSD_HPG = SSD_HEADS // SSD_GROUPS
SSD_GW = SSD_HPG * SSD_HEAD_DIM
SSD_BC = SSD_GROUPS * SSD_STATE


def _ssd_decay(dt_raw, dtb_ref, alog_ref):
    dt = _softplus(dt_raw + dtb_ref[...])
    dta = dt * (-jnp.exp(alog_ref[...]))
    row = lax.broadcasted_iota(jnp.int32, (CHUNK, CHUNK), 0)
    col = lax.broadcasted_iota(jnp.int32, (CHUNK, CHUNK), 1)
    tril = (row >= col).astype(BF16)
    hi = dta.astype(BF16)
    r1 = dta - hi.astype(F32)
    mid = r1.astype(BF16)
    lo = (r1 - mid.astype(F32)).astype(BF16)
    pre = (jnp.dot(tril, hi, preferred_element_type=F32) + jnp.dot(tril, mid, preferred_element_type=F32)
           + jnp.dot(tril, lo, preferred_element_type=F32))
    tot = pre[CHUNK - 1:CHUNK, :]
    is_bwd = lax.broadcasted_iota(jnp.int32, (1, LANES), 1) >= SSD_HEADS
    acs = jnp.where(is_bwd, tot - pre + dta, pre)
    return dt, acs, tot


def _ssd_bwd_kernel(xm_ref, xp_ref, xn_ref, dt_ref, cw_ref, cb_ref, dtb_ref, alog_ref, rexp_ref,
                    xact_ref, prev_ref, xpad_ref, xf_ref, st_ref, *, nt, qb):
    n = pl.program_id(1)
    tile = nt - 1 - n
    rows = qb * CHUNK

    @pl.when(n == 0)
    def _():
        st_ref[...] = jnp.zeros(st_ref.shape, F32)

    _load_padded(xpad_ref, xm_ref, xp_ref, xn_ref, tile > 0, tile < nt - 1, rows)
    xact = _silu(_dwconv(xpad_ref, cw_ref, cb_ref, rows, SSD_CONV, 0, SSD_XBC))
    xf_ref[...] = xact
    xact_ref[0] = xact.astype(BF16)

    def body(j, carry):
        i = qb - 1 - j
        r0 = pl.multiple_of(i * CHUNK, CHUNK)
        xs = xf_ref[pl.ds(r0, CHUNK), 0:SSD_WIDTH]
        bm = xf_ref[pl.ds(r0, CHUNK), SSD_WIDTH:SSD_WIDTH + SSD_BC].astype(BF16)
        dt, acs, tot = _ssd_decay(dt_ref[0, pl.ds(r0, CHUNK), :], dtb_ref, alog_ref)
        e_in = jnp.concatenate([jnp.exp(tot - acs) * dt, jnp.broadcast_to(jnp.exp(tot), (8, LANES))],
                               axis=0).astype(BF16)
        e_out = jnp.dot(e_in, rexp_ref[...], preferred_element_type=F32)
        xd = (xs * e_out[0:CHUNK]).astype(BF16)
        cdec = e_out[CHUNK:CHUNK + 1]
        for g in range(SSD_GROUPS):
            prev = st_ref[g]
            prev_ref[0, i, g] = prev.astype(BF16)
            s_new = lax.dot_general(bm[:, g * SSD_STATE:(g + 1) * SSD_STATE],
                                    xd[:, g * SSD_GW:(g + 1) * SSD_GW], (((0,), (0,)), ((), ())),
                                    preferred_element_type=F32)
            st_ref[g] = prev * cdec[:, g * SSD_GW:(g + 1) * SSD_GW] + s_new
        return carry

    lax.fori_loop(0, qb, body, 0, unroll=True)


def _ssd_fwd_kernel(xa_ref, dt_ref, prev_ref, z_ref, dtb_ref, alog_ref, rexpf_ref, rexpb_ref,
                    dsk_ref, ng_ref, o_ref, st_ref, *, qb):
    n = pl.program_id(1)

    @pl.when(n == 0)
    def _():
        st_ref[...] = jnp.zeros(st_ref.shape, F32)

    row = lax.broadcasted_iota(jnp.int32, (CHUNK, CHUNK), 0)
    col = lax.broadcasted_iota(jnp.int32, (CHUNK, CHUNK), 1)
    fwd_part = row > col
    diag = row == col
    lane_lo = lax.broadcasted_iota(jnp.int32, (1, LANES), 1) < SSD_HEAD_DIM
    zero_b = jnp.zeros((), BF16)

    def body(i, carry):
        r0 = pl.multiple_of(i * CHUNK, CHUNK)
        xs_b = xa_ref[0, pl.ds(r0, CHUNK), 0:SSD_WIDTH]
        xs = xs_b.astype(F32)
        bm = xa_ref[0, pl.ds(r0, CHUNK), SSD_WIDTH:SSD_WIDTH + SSD_BC]
        cm = xa_ref[0, pl.ds(r0, CHUNK), SSD_WIDTH + SSD_BC:SSD_XBC]
        dt, acs, tot = _ssd_decay(dt_ref[0, pl.ds(r0, CHUNK), :], dtb_ref, alog_ref)
        acs2 = acs * LOG2E
        rt = (acs2 - jnp.log2(dt)).T
        dsum_t = jnp.log2(dt + pltpu.roll(dt, LANES - SSD_HEADS, axis=1)).T
        eacs = jnp.exp(acs)
        ef_in = jnp.concatenate([jnp.exp(tot - acs) * dt, eacs,
                                 jnp.broadcast_to(jnp.exp(tot), (8, LANES))], axis=0).astype(BF16)
        ef = jnp.dot(ef_in, rexpf_ref[...], preferred_element_type=F32)
        eb = jnp.dot(eacs.astype(BF16), rexpb_ref[...], preferred_element_type=F32)
        xd = (xs * ef[0:CHUNK]).astype(BF16)
        eacs_f = ef[CHUNK:2 * CHUNK]
        cdec = ef[2 * CHUNK:2 * CHUNK + 1]
        ys = []
        for g in range(SSD_GROUPS):
            bg = bm[:, g * SSD_STATE:(g + 1) * SSD_STATE]
            cg = cm[:, g * SSD_STATE:(g + 1) * SSD_STATE]
            gs = slice(g * SSD_GW, (g + 1) * SSD_GW)
            cb = lax.dot_general(cg, bg, (((1,), (1,)), ((), ())), preferred_element_type=F32)
            prev = st_ref[g]
            y_off = (jnp.dot(cg, prev.astype(BF16), preferred_element_type=F32) * eacs_f[:, gs]
                     + jnp.dot(cg, prev_ref[0, i, g], preferred_element_type=F32) * eb[:, gs])
            s_new = lax.dot_general(bg, xd[:, gs], (((0,), (0,)), ((), ())),
                                    preferred_element_type=F32)
            st_ref[g] = prev * cdec[:, gs] + s_new
            for pr in range(SSD_HPG // 2):
                c0 = g * SSD_GW + pr * LANES
                xpair = xs_b[:, c0:c0 + LANES]
                xbd = jnp.concatenate([jnp.where(lane_lo, xpair, zero_b),
                                       jnp.where(lane_lo, zero_b, xpair)], axis=0)
                mats = []
                for e in range(2):
                    h = g * SSD_HPG + pr * 2 + e
                    hb = SSD_HEADS + h
                    sel = jnp.where(fwd_part, acs2[:, h:h + 1] - rt[h:h + 1, :],
                                    acs2[:, hb:hb + 1] - rt[hb:hb + 1, :])
                    sel = jnp.where(diag, dsum_t[h:h + 1, :], sel)
                    mats.append((cb * jnp.exp2(sel)).astype(BF16))
                yd = jnp.dot(jnp.concatenate(mats, axis=1), xbd, preferred_element_type=F32)
                ys.append(yd + y_off[:, pr * LANES:(pr + 1) * LANES])
        y = jnp.concatenate(ys, axis=-1) + xs * dsk_ref[...]
        y = y * _silu(z_ref[0, pl.ds(r0, CHUNK), :].astype(F32))
        ms = jnp.mean(y * y, axis=-1, keepdims=True)
        o_ref[0, pl.ds(r0, CHUNK), :] = (y * lax.rsqrt(ms + EPS) * ng_ref[...]).astype(BF16)
        return carry

    lax.fori_loop(0, qb, body, 0, unroll=2 if qb % 2 == 0 else 1)


def _ssd(u3, dt3, cw, cb, dtb, alog, rexp_f, rexp_b, dsk, ng, *, qb):
    B, L, _ = u3.shape
    rows = qb * CHUNK
    assert L % rows == 0
    nt = L // rows
    nc = L // CHUNK
    hb = rows // HALO
    nhb = L // HALO
    state = pltpu.VMEM((SSD_GROUPS, SSD_STATE, SSD_GW), F32)
    const = lambda shape: pl.BlockSpec(shape, lambda b, n: (0,) * len(shape))
    rt = lambda n: nt - 1 - n

    xact, prevb = pl.pallas_call(
        functools.partial(_ssd_bwd_kernel, nt=nt, qb=qb),
        grid=(B, nt),
        in_specs=[
            pl.BlockSpec((1, rows, SSD_XBC), lambda b, n: (b, rt(n), C_XBC // SSD_XBC)),
            pl.BlockSpec((1, HALO, SSD_XBC),
                         lambda b, n: (b, jnp.maximum(rt(n) * hb - 1, 0), C_XBC // SSD_XBC)),
            pl.BlockSpec((1, HALO, SSD_XBC),
                         lambda b, n: (b, jnp.minimum((rt(n) + 1) * hb, nhb - 1), C_XBC // SSD_XBC)),
            pl.BlockSpec((1, rows, DT_PAD), lambda b, n: (b, rt(n), 0)),
            const((8, SSD_XBC)), const((1, SSD_XBC)), const((1, DT_PAD)), const((1, DT_PAD)),
            const((LANES, SSD_WIDTH)),
        ],
        out_specs=[pl.BlockSpec((1, rows, SSD_XBC), lambda b, n: (b, rt(n), 0)),
                   pl.BlockSpec((1, qb, SSD_GROUPS, SSD_STATE, SSD_GW),
                                lambda b, n: (b, rt(n), 0, 0, 0))],
        out_shape=[jax.ShapeDtypeStruct((B, L, SSD_XBC), BF16),
                   jax.ShapeDtypeStruct((B, nc, SSD_GROUPS, SSD_STATE, SSD_GW), BF16)],
        scratch_shapes=[pltpu.VMEM((rows + 2 * HALO, SSD_XBC), F32),
                        pltpu.VMEM((rows, SSD_XBC), F32), state],
        compiler_params=_cparams(("parallel", "arbitrary")),
        name="ssd_bwd",
    )(u3, u3, u3, dt3, cw, cb, dtb, alog, rexp_b)

    return pl.pallas_call(
        functools.partial(_ssd_fwd_kernel, qb=qb),
        grid=(B, nt),
        in_specs=[
            pl.BlockSpec((1, rows, SSD_XBC), lambda b, n: (b, n, 0)),
            pl.BlockSpec((1, rows, DT_PAD), lambda b, n: (b, n, 0)),
            pl.BlockSpec((1, qb, SSD_GROUPS, SSD_STATE, SSD_GW), lambda b, n: (b, n, 0, 0, 0)),
            pl.BlockSpec((1, rows, SSD_WIDTH), lambda b, n: (b, n, C_Z // SSD_WIDTH)),
            const((1, DT_PAD)), const((1, DT_PAD)),
            const((LANES, SSD_WIDTH)), const((LANES, SSD_WIDTH)),
            const((1, SSD_WIDTH)), const((1, SSD_WIDTH)),
        ],
        out_specs=pl.BlockSpec((1, rows, SSD_WIDTH), lambda b, n: (b, n, 0)),
        out_shape=jax.ShapeDtypeStruct((B, L, SSD_WIDTH), BF16),
        scratch_shapes=[state],
        compiler_params=_cparams(("parallel", "arbitrary")),
        name="ssd_fwd",
    )(xact, dt3, prevb, u3, dtb, alog, rexp_f, rexp_b, dsk, ng)


def _ssd_expand_matrix(rev):
    m = np.zeros((LANES, SSD_WIDTH), np.float32)
    hoff = SSD_HEADS if rev else 0
    for h in range(SSD_HEADS):
        m[hoff + h, h * SSD_HEAD_DIM:(h + 1) * SSD_HEAD_DIM] = 1.0
    return jnp.asarray(m, BF16)


def _hy_filter_kernel(t_ref, bands_ref, w1t_ref, w1c_ref, w1s_ref, b1_ref, w2_ref, b2_ref, w3_ref,
                      b3_ref, w4_ref, fr_ref, absd_ref, x_ref, s_ref, *, L, tl):
    i = pl.program_id(0)
    pos = (lax.broadcasted_iota(jnp.int32, (tl, 1), 0) + i * tl).astype(F32)
    t = t_ref[...]
    ang = 2.0 * math.pi * pos * bands_ref[...] / L
    fr = fr_ref[...]
    dot = functools.partial(jnp.dot, preferred_element_type=F32, precision=HIGHEST)
    pre = t * w1t_ref[...] + dot(jnp.cos(ang), w1c_ref[...]) + dot(-jnp.sin(ang), w1s_ref[...])
    h = jnp.sin(fr * (pre + b1_ref[...]))
    h = jnp.sin(fr * (dot(h, w2_ref[...]) + b2_ref[...]))
    h = jnp.sin(fr * (dot(h, w3_ref[...]) + b3_ref[...]))
    h = dot(h, w4_ref[...])
    decay = jnp.exp(-t * absd_ref[...])
    hf = h[:, :HY_WIDTH] * decay
    hb = jnp.where(pos == 0.0, 0.0, h[:, HY_WIDTH:] * decay)
    _to_lane_blocks(s_ref, jnp.concatenate([hf + hb, hb - hf], axis=1))
    x_ref[...] = _split_even_odd(s_ref).astype(BF16)


def _hy_filter(L, w1, b1, w2, b2, w3, b3, w4, freq):
    tl = min(512, L)
    t = jnp.linspace(0.0, 1.0, L, dtype=F32)[:, None]
    bands = jnp.linspace(1e-4, HY_EMB_BANDS - 1, HY_EMB_BANDS, dtype=F32)[None, :]
    max_decay = math.log(HY_DECAY_TARGET) / HY_FAST_DECAY
    min_decay = math.log(HY_DECAY_TARGET) / HY_SLOW_DECAY
    absd = jnp.abs(jnp.linspace(min_decay, max_decay, HY_WIDTH, dtype=F32))[None, :]
    w1 = w1.astype(F32)
    full = lambda a: pl.BlockSpec(a.shape, lambda i: (0,) * a.ndim)
    args = [t, bands, w1[0:1], w1[1:1 + HY_EMB_BANDS], w1[1 + HY_EMB_BANDS:], b1[None], w2, b2[None],
            w3, b3[None], w4, freq[None], absd]
    in_specs = [pl.BlockSpec((tl, 1), lambda i: (i, 0))] + [full(a) for a in args[1:]]
    return pl.pallas_call(
        functools.partial(_hy_filter_kernel, L=L, tl=tl),
        grid=(L // tl,),
        in_specs=in_specs,
        out_specs=pl.BlockSpec((tl // 2, 4 * HY_WIDTH), lambda i: (i, 0)),
        out_shape=jax.ShapeDtypeStruct((L // 2, 4 * HY_WIDTH), BF16),
        scratch_shapes=[pltpu.VMEM((2 * HY_WIDTH // LANES, tl, LANES), F32)],
        compiler_params=_cparams(("parallel",)),
        name="hy_filter",
    )(*args)


def _dft_tables(L, hm):
    L2 = L // 2
    g = jnp.arange(L2, dtype=jnp.int32)
    s = jnp.arange(L2, dtype=jnp.int32)
    ph = ((2 * g + 1)[:, None] * s[None, :]) % (2 * L)
    ang = ph.astype(F32) * (math.pi / L)
    c = jnp.cos(ang).astype(BF16).reshape(L2 // hm, 1, hm, L2)
    sn = jnp.sin(ang).astype(BF16).reshape(L2 // hm, 1, hm, L2)
    a_fwd = jnp.concatenate([c, sn], axis=1).reshape(L, L2)
    a_inv = jnp.concatenate([c, -sn], axis=1).reshape(L, L2).T
    return a_fwd, a_inv


def _hy_kraw_kernel(a_ref, x_ref, k_ref):
    k_ref[...] = jnp.dot(a_ref[...], x_ref[...], preferred_element_type=F32)


def _hy_kraw(a_fwd, xf2, *, hm):
    L, L2 = a_fwd.shape
    tm = 2 * hm
    nx = xf2.shape[1]
    return pl.pallas_call(
        _hy_kraw_kernel,
        grid=(L // tm,),
        in_specs=[pl.BlockSpec((tm, L2), lambda i: (i, 0)),
                  pl.BlockSpec((L2, nx), lambda i: (0, 0))],
        out_specs=pl.BlockSpec((tm, nx), lambda i: (i, 0)),
        out_shape=jax.ShapeDtypeStruct((L, nx), F32),
        compiler_params=_cparams(("parallel",)),
        name="hy_kspec",
    )(a_fwd, xf2)


def _hy_filter_spectrum(pq, L, hm):
    L2 = L // 2
    W = HY_WIDTH
    pq = pq.reshape(L2 // hm, 2, hm, 4 * W)
    p = pq[:, 0].reshape(L2, 4 * W)
    q = pq[:, 1].reshape(L2, 4 * W)
    ea, eb, oa, ob = [(p[:, i * W:(i + 1) * W], -q[:, i * W:(i + 1) * W]) for i in range(4)]
    theta = (2.0 * jnp.arange(L2, dtype=F32) + 1.0) * (math.pi / (2 * L))
    w = (jnp.cos(theta)[:, None], -jnp.sin(theta)[:, None])
    cmul = lambda x, y: (x[0] * y[0] - x[1] * y[1], x[0] * y[1] + x[1] * y[0])
    woa, wob = cmul(w, oa), cmul(w, ob)
    k1 = (ea[0] + woa[0], -(eb[1] + wob[1]))
    k2 = (ea[0] - woa[0], eb[1] - wob[1])
    kp = (k1[0] + k2[0], k1[1] - k2[1])
    km = (k1[0] - k2[0], k1[1] + k2[1])
    wkm = cmul(w, km)
    vkm = cmul((w[0], -w[1]), km)
    filt = jnp.stack([kp[0], kp[1], wkm[0], wkm[1], vkm[0], vkm[1]]) * (1.0 / L)
    return jnp.transpose(filt.reshape(6, L2 // hm, hm, W), (1, 0, 2, 3))


def _to_lane_blocks(s_ref, x):
    for c in range(s_ref.shape[0]):
        s_ref[c] = x[:, c * LANES:(c + 1) * LANES]


def _split_even_odd(s_ref):
    k, rows, _ = s_ref.shape
    return jnp.concatenate([s_ref[c, pl.ds(p, rows // 2, stride=2), :]
                            for p in range(2) for c in range(k)], axis=1)


def _hy_pre_kernel(xm_ref, xp_ref, xn_ref, cw_ref, cb_ref, u_ref, xau_ref, xpad_ref, s_ref, *, nt, tl):
    i = pl.program_id(1)
    _load_padded(xpad_ref, xm_ref, xp_ref, xn_ref, i > 0, i < nt - 1, tl)
    xa = _dwconv(xpad_ref, cw_ref, cb_ref, tl, HY_CONV, 0, HY_WIDTH)
    xb = _dwconv(xpad_ref, cw_ref, cb_ref, tl, HY_CONV, HY_WIDTH, 2 * HY_WIDTH)
    v = _dwconv(xpad_ref, cw_ref, cb_ref, tl, HY_CONV, 2 * HY_WIDTH, 3 * HY_WIDTH)
    u = xb * v
    xau_ref[0, :, :HY_WIDTH] = xa.astype(BF16)
    xau_ref[0, :, HY_WIDTH:] = u.astype(BF16)
    _to_lane_blocks(s_ref, u)
    u_ref[0] = _split_even_odd(s_ref).astype(BF16)


def _hy_tile_specs(L, tl):
    hb = tl // HALO
    nhb = L // HALO
    w = 3 * HY_WIDTH
    return [pl.BlockSpec((1, tl, w), lambda b, i: (b, i, C_HY // w)),
            pl.BlockSpec((1, HALO, w), lambda b, i: (b, jnp.maximum(i * hb - 1, 0), C_HY // w)),
            pl.BlockSpec((1, HALO, w), lambda b, i: (b, jnp.minimum((i + 1) * hb, nhb - 1), C_HY // w))]


def _hy_pre(u3, cw, cb, *, tl):
    B, L, _ = u3.shape
    nt = L // tl
    w = 3 * HY_WIDTH
    return pl.pallas_call(
        functools.partial(_hy_pre_kernel, nt=nt, tl=tl),
        grid=(B, nt),
        in_specs=_hy_tile_specs(L, tl) + [pl.BlockSpec((8, w), lambda b, i: (0, 0)),
                                          pl.BlockSpec((1, w), lambda b, i: (0, 0))],
        out_specs=[pl.BlockSpec((1, tl // 2, 2 * HY_WIDTH), lambda b, i: (b, i, 0)),
                   pl.BlockSpec((1, tl, 2 * HY_WIDTH), lambda b, i: (b, i, 0))],
        out_shape=[jax.ShapeDtypeStruct((B, L // 2, 2 * HY_WIDTH), BF16),
                   jax.ShapeDtypeStruct((B, L, 2 * HY_WIDTH), BF16)],
        scratch_shapes=[pltpu.VMEM((tl + 2 * HALO, w), F32),
                        pltpu.VMEM((HY_WIDTH // LANES, tl, LANES), F32)],
        compiler_params=_cparams(("parallel", "parallel")),
        name="hy_pre",
    )(u3, u3, u3, cw, cb)


def _hy_fwd_kernel(a_ref, u_ref, f_ref, y_ref, *, hm):
    W = HY_WIDTH
    pq = jnp.dot(a_ref[...], u_ref[0], preferred_element_type=F32)
    pe, po = pq[0:hm, :W], pq[0:hm, W:]
    qe, qo = pq[hm:, :W], pq[hm:, W:]
    kpr, kpi, wr, wi, vr, vi = [f_ref[0, i] for i in range(6)]
    y_ref[0, 0:hm, :W] = (pe * kpr + qe * kpi + po * wr + qo * wi).astype(BF16)
    y_ref[0, hm:, :W] = (pe * kpi - qe * kpr + po * wi - qo * wr).astype(BF16)
    y_ref[0, 0:hm, W:] = (pe * vr + qe * vi + po * kpr + qo * kpi).astype(BF16)
    y_ref[0, hm:, W:] = (pe * vi - qe * vr + po * kpi - qo * kpr).astype(BF16)


def _hy_fwd(a_fwd, ueo, filt, *, hm):
    B, L2, W2 = ueo.shape
    L = 2 * L2
    tm = 2 * hm
    return pl.pallas_call(
        functools.partial(_hy_fwd_kernel, hm=hm),
        grid=(L // tm, B),
        in_specs=[pl.BlockSpec((tm, L2), lambda i, b: (i, 0)),
                  pl.BlockSpec((1, L2, W2), lambda i, b: (b, 0, 0)),
                  pl.BlockSpec((1, 6, hm, HY_WIDTH), lambda i, b: (i, 0, 0, 0))],
        out_specs=pl.BlockSpec((1, tm, W2), lambda i, b: (b, i, 0)),
        out_shape=jax.ShapeDtypeStruct((B, L, W2), BF16),
        compiler_params=_cparams(("parallel", "parallel")),
        name="hy_fwd",
    )(a_fwd, ueo, filt)


def _hy_inv_kernel(a_ref, y_ref, o_ref):
    o_ref[0] = jnp.dot(a_ref[...], y_ref[0], preferred_element_type=F32)


def _hy_inv(a_inv, yspec, *, tm):
    B, L, W2 = yspec.shape
    L2 = L // 2
    return pl.pallas_call(
        _hy_inv_kernel,
        grid=(L2 // tm, B),
        in_specs=[pl.BlockSpec((tm, L), lambda i, b: (i, 0)),
                  pl.BlockSpec((1, L, W2), lambda i, b: (b, 0, 0))],
        out_specs=pl.BlockSpec((1, tm, W2), lambda i, b: (b, i, 0)),
        out_shape=jax.ShapeDtypeStruct((B, L2, W2), F32),
        compiler_params=_cparams(("parallel", "parallel")),
        name="hy_inv",
    )(a_inv, yspec)


def _hy_post_kernel(c_ref, xau_ref, g_ref, d_ref, ng_ref, o_ref, s_ref, *, tl):
    nblk = HY_WIDTH // LANES
    for p in range(2):
        for c in range(nblk):
            s_ref[c, pl.ds(p, tl // 2, stride=2), :] = c_ref[0, :, (p * nblk + c) * LANES:
                                                             (p * nblk + c + 1) * LANES]
    conv = jnp.concatenate([s_ref[c] for c in range(nblk)], axis=1)
    xa = xau_ref[0, :, :HY_WIDTH].astype(F32)
    u = xau_ref[0, :, HY_WIDTH:].astype(F32)
    y = xa * (conv + u * d_ref[...])
    y = y * _silu(g_ref[0].astype(F32))
    ms = jnp.mean(y * y, axis=-1, keepdims=True)
    o_ref[0] = (y * lax.rsqrt(ms + EPS) * ng_ref[...]).astype(BF16)


def _hy_post(conv, xau, u3, d, ng, *, tl):
    B, L, _ = u3.shape
    return pl.pallas_call(
        functools.partial(_hy_post_kernel, tl=tl),
        grid=(B, L // tl),
        in_specs=[pl.BlockSpec((1, tl // 2, 2 * HY_WIDTH), lambda b, i: (b, i, 0)),
                  pl.BlockSpec((1, tl, 2 * HY_WIDTH), lambda b, i: (b, i, 0)),
                  pl.BlockSpec((1, tl, HY_WIDTH), lambda b, i: (b, i, C_GHY // HY_WIDTH)),
                  pl.BlockSpec((1, HY_WIDTH), lambda b, i: (0, 0)),
                  pl.BlockSpec((1, HY_WIDTH), lambda b, i: (0, 0))],
        out_specs=pl.BlockSpec((1, tl, HY_WIDTH), lambda b, i: (b, i, 0)),
        out_shape=jax.ShapeDtypeStruct((B, L, HY_WIDTH), BF16),
        scratch_shapes=[pltpu.VMEM((HY_WIDTH // LANES, tl, LANES), F32)],
        compiler_params=_cparams(("parallel", "parallel")),
        name="hy_post",
    )(conv, xau, u3, d, ng)


def _pad_rows(a, rows):
    return jnp.concatenate([a, jnp.zeros((rows - a.shape[0],) + a.shape[1:], a.dtype)], axis=0)


def _pad_cols(a, cols):
    return jnp.concatenate([a, jnp.zeros(a.shape[:-1] + (cols - a.shape[-1],), a.dtype)], axis=-1)


def _static_take(a, idx, axis):
    idx = np.asarray(idx)
    cuts = np.flatnonzero(np.diff(idx) != 1) + 1
    starts = np.concatenate([[0], cuts])
    ends = np.concatenate([cuts, [len(idx)]])
    parts = [lax.slice_in_dim(a, int(idx[s]), int(idx[e - 1]) + 1, axis=axis)
             for s, e in zip(starts, ends)]
    return jnp.concatenate(parts, axis=axis)


def _tile(n, pref):
    t = min(pref, n)
    assert n % t == 0
    return t


def _layer(x3, p, tables, *, final, final_g):
    B, L, _ = x3.shape
    T = B * L
    x2 = x3.reshape(T, D_MODEL)
    u, dt = _inproj(x2, p["norm_g"], p["w_in"], tm=_tile(T, 512), tn=512)
    u3 = u.reshape(B, L, N_IN)
    dt3 = dt.reshape(B, L, DT_PAD)

    ys = _ssd(u3, dt3, p["ssd_cw"], p["ssd_cb"], p["ssd_dtb"], p["ssd_alog"], p["rexp_f"],
              p["rexp_b"], p["ssd_dskip"], p["ssd_ng"], qb=_tile(L // CHUNK, 4))

    ya = _attention(u3, p["att_bias"], p["att_sink"], p["att_ng"], qb=_tile(L // ATT_BLOCK, 4))

    a_fwd, a_inv, filt, hm = tables
    tl = _tile(L, 512)
    ueo, xau = _hy_pre(u3, p["hy_cw"], p["hy_cb"], tl=tl)
    yspec = _hy_fwd(a_fwd, ueo, filt, hm=hm)
    conv = _hy_inv(a_inv, yspec, tm=_tile(L // 2, 512))
    yh = _hy_post(conv, xau, u3, p["hy_d"], p["hy_ng"], tl=tl)

    out = _outproj(x2, ys.reshape(T, SSD_WIDTH), ya.reshape(T, ATT_WIDTH), yh.reshape(T, HY_WIDTH),
                   p["w_out"], final_g, tm=_tile(T, 512), final=final)
    return out.reshape(B, L, D_MODEL)


def kernel(x_prompt, x_sample, rel_bias, norm_g, w_in, ssd_conv_w, ssd_conv_b, ssd_dt_bias, ssd_a_log, ssd_d, ssd_norm_g, att_sink, att_norm_g, hy_conv_w, hy_conv_b, hy_w1, hy_b1, hy_w2, hy_b2, hy_w3, hy_b3, hy_w4, hy_freq, hy_d, hy_norm_g, w_out, final_norm_g):
    depth = w_in.shape[0]
    perm, col_scale = _in_perm()
    n_real = C_DT + 2 * SSD_HEADS
    w_in_p = _pad_cols((_static_take(w_in, perm[:n_real], 2)
                        * jnp.asarray(col_scale[:n_real])).astype(BF16), N_IN)
    att_perm = _att_col_perm()
    out_rows = np.concatenate([np.arange(SSD_WIDTH), SSD_WIDTH + att_perm,
                               np.arange(SSD_WIDTH + ATT_WIDTH, D_MODEL)])
    w_out_b = _static_take(w_out, out_rows, 1).astype(BF16)
    att_bias = _attn_bias(rel_bias)
    rexp_f, rexp_b = _ssd_expand_matrix(False), _ssd_expand_matrix(True)
    final_g = final_norm_g.astype(F32)[None, :]

    layers = []
    for i in range(depth):
        layers.append(dict(
            norm_g=norm_g[i].astype(F32)[None, :],
            w_in=w_in_p[i],
            ssd_cw=_pad_rows(ssd_conv_w[i].astype(F32), 8),
            ssd_cb=ssd_conv_b[i].astype(F32)[None, :],
            ssd_dtb=_pad_cols(ssd_dt_bias[i].astype(F32).reshape(1, 2 * SSD_HEADS), DT_PAD),
            ssd_alog=_pad_cols(ssd_a_log[i].astype(F32).reshape(1, 2 * SSD_HEADS), DT_PAD),
            ssd_dskip=jnp.repeat(ssd_d[i].astype(F32), SSD_HEAD_DIM)[None, :],
            ssd_ng=ssd_norm_g[i].astype(F32)[None, :],
            rexp_f=rexp_f, rexp_b=rexp_b,
            att_bias=att_bias,
            att_sink=jnp.broadcast_to(att_sink[i].astype(F32)[:, None] * LOG2E, (ATT_HEADS, LANES)),
            att_ng=att_norm_g[i].astype(F32)[att_perm][None, :],
            hy_cw=_pad_rows(hy_conv_w[i].astype(F32), 8),
            hy_cb=hy_conv_b[i].astype(F32)[None, :],
            hy_d=hy_d[i].astype(F32)[None, :],
            hy_ng=hy_norm_g[i].astype(F32)[None, :],
            w_out=w_out_b[i],
        ))

    def trunk(x):
        L = x.shape[1]
        hm = min(256, L // 4)
        a_fwd, a_inv = _dft_tables(L, hm)
        for i in range(depth):
            xf = _hy_filter(L, hy_w1[i], hy_b1[i].astype(F32), hy_w2[i].astype(F32),
                            hy_b2[i].astype(F32), hy_w3[i].astype(F32), hy_b3[i].astype(F32),
                            hy_w4[i].astype(F32), hy_freq[i].astype(F32))
            pq = _hy_kraw(a_fwd, xf, hm=hm)
            filt = _hy_filter_spectrum(pq, L, hm)
            x = _layer(x, layers[i], (a_fwd, a_inv, filt, hm), final=(i == depth - 1),
                       final_g=final_g)
        return x

    return (trunk(x_prompt), trunk(x_sample))
```

```python
import functools
import math

import jax
import jax.numpy as jnp
import numpy as np
from jax import lax
from jax.experimental import pallas as pl
from jax.experimental.pallas import tpu as pltpu

F32 = jnp.float32
BF16 = jnp.bfloat16
HIGHEST = lax.Precision.HIGHEST

D_MODEL = 2048
SSD_WIDTH = 1024
ATT_WIDTH = 512
HY_WIDTH = 512
SSD_HEAD_DIM = 64
SSD_HEADS = 16
SSD_GROUPS = 2
SSD_STATE = 128
SSD_CONV = 5
CHUNK = 128
SSD_XBC = SSD_WIDTH + 2 * SSD_GROUPS * SSD_STATE
ATT_HEAD_DIM = 64
ATT_HEADS = 8
ATT_KV_HEADS = 2
ATT_REP = ATT_HEADS // ATT_KV_HEADS
ATT_WINDOW = 128
ATT_BLOCK = 128
REL_BUCKETS = 32
REL_MAX_DIST = 128
HY_CONV = 3
HY_EMB_BANDS = 16
HY_FF = 64
HY_FAST_DECAY = 0.3
HY_SLOW_DECAY = 1.5
HY_DECAY_TARGET = 1e-2
EPS = 1e-6
NEG_BIG = -1e30

LANES = 128
BF16_SUBLANES = 16
VMEM_LIMIT = 56 * 1024 * 1024

C_XBC = 0
C_HY = 1536
C_Z = 3072
C_Q = 4096
C_GATT = 4608
C_GHY = 5120
C_K = 5632
C_V = 5760
C_DT = 5888
DT_PAD = 128
N_IN = C_DT + DT_PAD

_OLD_SIZES = [SSD_WIDTH, SSD_XBC, 2 * SSD_HEADS, ATT_WIDTH, ATT_KV_HEADS * ATT_HEAD_DIM,
              ATT_KV_HEADS * ATT_HEAD_DIM, ATT_WIDTH, 3 * HY_WIDTH, HY_WIDTH]
_OLD_OFF = np.concatenate([[0], np.cumsum(_OLD_SIZES)])
IN_COLS = int(_OLD_OFF[-1])


LOG2E = math.log2(math.e)
Q_SCALE = ATT_HEAD_DIM ** -0.5 * LOG2E


def _att_col_perm():
    order = [h for j in range(ATT_REP) for h in (j, ATT_REP + j)]
    return np.concatenate([np.arange(h * ATT_HEAD_DIM, (h + 1) * ATT_HEAD_DIM) for h in order])


def _in_perm():
    perm = np.full((N_IN,), IN_COLS, np.int32)
    scale = np.ones((N_IN,), np.float32)
    o = {n: int(_OLD_OFF[i]) for i, n in enumerate(
        ["z", "xbc", "dt", "q", "k", "v", "gatt", "hy", "ghy"])}
    def put(new, old, width):
        perm[new:new + width] = np.arange(old, old + width)
    put(C_XBC, o["xbc"], SSD_XBC)
    put(C_HY, o["hy"], 3 * HY_WIDTH)
    put(C_Z, o["z"], SSD_WIDTH)
    perm[C_Q:C_Q + ATT_WIDTH] = o["q"] + _att_col_perm()
    scale[C_Q:C_Q + ATT_WIDTH] = Q_SCALE
    perm[C_GATT:C_GATT + ATT_WIDTH] = o["gatt"] + _att_col_perm()
    put(C_GHY, o["ghy"], HY_WIDTH)
    put(C_K, o["k"], 128)
    put(C_V, o["v"], 128)
    put(C_DT, o["dt"], 2 * SSD_HEADS)
    return perm, scale


def _cparams(sem):
    return pltpu.CompilerParams(dimension_semantics=sem, vmem_limit_bytes=VMEM_LIMIT)


def _silu(x):
    return x * (1.0 / (1.0 + jnp.exp(-x)))


def _softplus(x):
    return jnp.maximum(x, 0.0) + jnp.log1p(jnp.exp(-jnp.abs(x)))


def _inproj_kernel(x_ref, g_ref, w_ref, u_ref, dt_ref, h_ref, *, rows, tn):
    for r in range(x_ref.shape[0] // rows):
        sl = slice(r * rows, (r + 1) * rows)
        x = x_ref[sl, :]
        ms = jnp.mean(x * x, axis=-1, keepdims=True)
        h_ref[sl, :] = (x * lax.rsqrt(ms + EPS) * g_ref[...]).astype(BF16)
        h = h_ref[sl, :]
        for c0 in range(0, N_IN, tn):
            c1 = min(c0 + tn, N_IN)
            acc = jnp.dot(h, w_ref[:, c0:c1], preferred_element_type=F32)
            u_ref[sl, c0:c1] = acc.astype(BF16)
            if c0 <= C_DT < c1:
                dt_ref[sl, :] = acc[:, C_DT - c0:C_DT - c0 + DT_PAD]


def _inproj(x2, g, w, *, tm, tn):
    T = x2.shape[0]
    assert T % tm == 0
    return pl.pallas_call(
        functools.partial(_inproj_kernel, rows=min(256, tm), tn=tn),
        grid=(T // tm,),
        in_specs=[pl.BlockSpec((tm, D_MODEL), lambda i: (i, 0)),
                  pl.BlockSpec((1, D_MODEL), lambda i: (0, 0)),
                  pl.BlockSpec((D_MODEL, N_IN), lambda i: (0, 0), pipeline_mode=pl.Buffered(1))],
        out_specs=[pl.BlockSpec((tm, N_IN), lambda i: (i, 0)),
                   pl.BlockSpec((tm, DT_PAD), lambda i: (i, 0))],
        out_shape=[jax.ShapeDtypeStruct((T, N_IN), BF16),
                   jax.ShapeDtypeStruct((T, DT_PAD), F32)],
        scratch_shapes=[pltpu.VMEM((tm, D_MODEL), BF16)],
        compiler_params=_cparams(("parallel",)),
        name="inproj",
    )(x2, g, w)


def _outproj_kernel(x_ref, ys_ref, ya_ref, yh_ref, w_ref, fg_ref, o_ref, *, final):
    acc = x_ref[...]
    acc = acc + jnp.dot(ys_ref[...], w_ref[0:SSD_WIDTH, :], preferred_element_type=F32)
    acc = acc + jnp.dot(ya_ref[...], w_ref[SSD_WIDTH:SSD_WIDTH + ATT_WIDTH, :],
                        preferred_element_type=F32)
    acc = acc + jnp.dot(yh_ref[...], w_ref[SSD_WIDTH + ATT_WIDTH:, :], preferred_element_type=F32)
    if final:
        ms = jnp.mean(acc * acc, axis=-1, keepdims=True)
        acc = acc * lax.rsqrt(ms + EPS) * fg_ref[...]
    o_ref[...] = acc


def _outproj(x2, ys, ya, yh, w, fg, *, tm, final):
    T = x2.shape[0]
    assert T % tm == 0
    return pl.pallas_call(
        functools.partial(_outproj_kernel, final=final),
        grid=(T // tm,),
        in_specs=[pl.BlockSpec((tm, D_MODEL), lambda i: (i, 0)),
                  pl.BlockSpec((tm, SSD_WIDTH), lambda i: (i, 0)),
                  pl.BlockSpec((tm, ATT_WIDTH), lambda i: (i, 0)),
                  pl.BlockSpec((tm, HY_WIDTH), lambda i: (i, 0)),
                  pl.BlockSpec((D_MODEL, D_MODEL), lambda i: (0, 0)),
                  pl.BlockSpec((1, D_MODEL), lambda i: (0, 0))],
        out_specs=pl.BlockSpec((tm, D_MODEL), lambda i: (i, 0)),
        out_shape=jax.ShapeDtypeStruct((T, D_MODEL), F32),
        compiler_params=_cparams(("parallel",)),
        name="outproj",
    )(x2, ys, ya, yh, w, fg)


def _t5_buckets(rel):
    nb = REL_BUCKETS // 2
    max_exact = nb // 2
    ret = (rel > 0).astype(np.int32) * nb
    n = np.abs(rel)
    large = max_exact + (np.log(np.maximum(n, 1) / max_exact) / math.log(REL_MAX_DIST / max_exact)
                         * (nb - max_exact)).astype(np.int32)
    large = np.minimum(large, nb - 1)
    return ret + np.where(n < max_exact, n, large)


def _attn_bias(rel_bias):
    qi = np.arange(ATT_BLOCK)[:, None]
    kj = np.arange(3 * ATT_BLOCK)[None, :]
    rel = kj - ATT_BLOCK - qi
    onehot = (_t5_buckets(rel)[None] == np.arange(REL_BUCKETS)[:, None, None]).astype(np.float32)
    bias = jnp.einsum("bqk,bh->hqk", jnp.asarray(onehot), rel_bias.astype(F32),
                      precision=HIGHEST) * LOG2E
    window = np.abs(rel) <= ATT_WINDOW
    variants = []
    for last in (False, True):
        for first in (False, True):
            ok = window & ~(first & (kj < ATT_BLOCK)) & ~(last & (kj >= 2 * ATT_BLOCK))
            variants.append(jnp.where(ok[None], bias, NEG_BIG))
    return jnp.transpose(jnp.stack(variants), (0, 1, 3, 2))


def _attn_kernel(q_ref, kp_ref, kc_ref, kn_ref, vp_ref, vc_ref, vn_ref, g_ref, bias_ref, sink_ref,
                 ng_ref, o_ref, klo_ref, khi_ref, vt_ref, *, nt, qb):
    n = pl.program_id(1)
    lo = lax.broadcasted_iota(jnp.int32, (1, LANES), 1) < ATT_HEAD_DIM
    zero = jnp.zeros((), BF16)
    kext = jnp.concatenate([kp_ref[0], kc_ref[0], kn_ref[0]], axis=0)
    klo_ref[...] = jnp.where(lo, kext, zero)
    khi_ref[...] = jnp.where(lo, zero, kext)
    for t, ref, cnt in ((0, vp_ref, 1), (1, vc_ref, qb), (qb + 1, vn_ref, 1)):
        for i in range(cnt):
            blk = ref[0, i * ATT_BLOCK:(i + 1) * ATT_BLOCK, :]
            vt_ref[t + i] = blk.astype(F32).T.astype(BF16)
    row_lo = lax.broadcasted_iota(jnp.int32, (LANES, ATT_BLOCK), 0) < ATT_HEAD_DIM
    nk = 3 * ATT_BLOCK

    def body(i, carry):
        r0 = pl.multiple_of(i * ATT_BLOCK, ATT_BLOCK)
        q = q_ref[0, pl.ds(r0, ATT_BLOCK), :]
        kst = jnp.concatenate([klo_ref[pl.ds(r0, nk), :], khi_ref[pl.ds(r0, nk), :]], axis=0)
        vt = jnp.concatenate([vt_ref[i], vt_ref[i + 1], vt_ref[i + 2]], axis=1)
        first = jnp.logical_and(n == 0, i == 0)
        last = jnp.logical_and(n == nt - 1, i == qb - 1)
        variant = first.astype(jnp.int32) + 2 * last.astype(jnp.int32)
        outs = []
        for j in range(ATT_REP):
            qp = q[:, j * LANES:(j + 1) * LANES]
            st = lax.dot_general(kst, qp, (((1,), (1,)), ((), ())), preferred_element_type=F32)
            halves = []
            for e, h in enumerate((j, ATT_REP + j)):
                s = st[e * nk:(e + 1) * nk] + bias_ref[variant, h]
                sk = sink_ref[h:h + 1, :]
                m = jnp.maximum(jnp.max(s, axis=0, keepdims=True), sk)
                p = jnp.exp2(s - m)
                den = jnp.sum(p, axis=0, keepdims=True) + jnp.exp2(sk - m)
                ot = jnp.dot(vt, p.astype(BF16), preferred_element_type=F32)
                halves.append(ot * (1.0 / den))
            outs.append(jnp.where(row_lo, halves[0], halves[1]).T)
        o = jnp.concatenate(outs, axis=-1)
        y = o * _silu(g_ref[0, pl.ds(r0, ATT_BLOCK), :].astype(F32))
        ms = jnp.mean(y * y, axis=-1, keepdims=True)
        o_ref[0, pl.ds(r0, ATT_BLOCK), :] = (y * lax.rsqrt(ms + EPS) * ng_ref[...]).astype(BF16)
        return carry

    lax.fori_loop(0, qb, body, 0, unroll=True)


def _attention(u3, bias, sink, ng, *, qb):
    B, L, _ = u3.shape
    tq = qb * ATT_BLOCK
    assert L % tq == 0
    nt = L // tq
    nb = L // ATT_BLOCK
    kcol, vcol = C_K // 128, C_V // 128
    def kv_specs(colblk):
        return [pl.BlockSpec((1, ATT_BLOCK, 128), lambda b, n: (b, jnp.maximum(n * qb - 1, 0), colblk)),
                pl.BlockSpec((1, tq, 128), lambda b, n: (b, n, colblk)),
                pl.BlockSpec((1, ATT_BLOCK, 128),
                             lambda b, n: (b, jnp.minimum((n + 1) * qb, nb - 1), colblk))]
    return pl.pallas_call(
        functools.partial(_attn_kernel, nt=nt, qb=qb),
        grid=(B, nt),
        in_specs=[pl.BlockSpec((1, tq, ATT_WIDTH), lambda b, n: (b, n, C_Q // ATT_WIDTH))]
                 + kv_specs(kcol) + kv_specs(vcol)
                 + [pl.BlockSpec((1, tq, ATT_WIDTH), lambda b, n: (b, n, C_GATT // ATT_WIDTH)),
                    pl.BlockSpec((4, ATT_HEADS, 3 * ATT_BLOCK, ATT_BLOCK), lambda b, n: (0, 0, 0, 0)),
                    pl.BlockSpec((ATT_HEADS, LANES), lambda b, n: (0, 0)),
                    pl.BlockSpec((1, ATT_WIDTH), lambda b, n: (0, 0))],
        out_specs=pl.BlockSpec((1, tq, ATT_WIDTH), lambda b, n: (b, n, 0)),
        out_shape=jax.ShapeDtypeStruct((B, L, ATT_WIDTH), BF16),
        scratch_shapes=[pltpu.VMEM(((qb + 2) * ATT_BLOCK, LANES), BF16),
                        pltpu.VMEM(((qb + 2) * ATT_BLOCK, LANES), BF16),
                        pltpu.VMEM((qb + 2, LANES, ATT_BLOCK), BF16)],
        compiler_params=_cparams(("parallel", "parallel")),
        name="attn",
    )(u3, u3, u3, u3, u3, u3, u3, u3, bias, sink, ng)


HALO = BF16_SUBLANES


CONV_BLK = 128


def _load_padded(xpad_ref, xm_ref, xp_ref, xn_ref, has_prev, has_next, rows):
    zero = jnp.zeros((), BF16)
    xpad_ref[pl.ds(0, HALO), :] = jnp.where(has_prev, xp_ref[0], zero)
    xpad_ref[pl.ds(HALO, rows), :] = xm_ref[0]
    xpad_ref[pl.ds(HALO + rows, HALO), :] = jnp.where(has_next, xn_ref[0], zero)


def _shift_matrix(width):
    offs = [k - width // 2 for k in range(width) if k != width // 2]
    r = lax.broadcasted_iota(jnp.int32, (CONV_BLK, CONV_BLK + 2 * HALO), 0)
    c = lax.broadcasted_iota(jnp.int32, (CONV_BLK, CONV_BLK + 2 * HALO), 1)
    return jnp.concatenate([(c == r + HALO + d).astype(BF16) for d in offs], axis=0)


def _dwconv(xpad_ref, w_ref, b_ref, rows, width, c0, c1):
    shifts = _shift_matrix(width)
    blocks = []
    for j in range(rows // CONV_BLK):
        win = xpad_ref[j * CONV_BLK:(j + 1) * CONV_BLK + 2 * HALO, c0:c1]
        moved = jnp.dot(shifts, win, preferred_element_type=F32)
        acc = win[HALO:HALO + CONV_BLK].astype(F32) * w_ref[width // 2:width // 2 + 1, c0:c1]
        i = 0
        for k in range(width):
            if k == width // 2:
                continue
            acc = acc + moved[i * CONV_BLK:(i + 1) * CONV_BLK] * w_ref[k:k + 1, c0:c1]
            i += 1
        blocks.append(acc + b_ref[:, c0:c1])
    return jnp.concatenate(blocks, axis=0)


SSD_HPG = SSD_HEADS // SSD_GROUPS
SSD_GW = SSD_HPG * SSD_HEAD_DIM
SSD_BC = SSD_GROUPS * SSD_STATE


def _ssd_decay(dt_raw, dtb_ref, alog_ref):
    dt = _softplus(dt_raw + dtb_ref[...])
    dta = dt * (-jnp.exp(alog_ref[...]))
    row = lax.broadcasted_iota(jnp.int32, (CHUNK, CHUNK), 0)
    col = lax.broadcasted_iota(jnp.int32, (CHUNK, CHUNK), 1)
    tril = (row >= col).astype(BF16)
    hi = dta.astype(BF16)
    r1 = dta - hi.astype(F32)
    mid = r1.astype(BF16)
    lo = (r1 - mid.astype(F32)).astype(BF16)
    pre = (jnp.dot(tril, hi, preferred_element_type=F32) + jnp.dot(tril, mid, preferred_element_type=F32)
           + jnp.dot(tril, lo, preferred_element_type=F32))
    tot = pre[CHUNK - 1:CHUNK, :]
    is_bwd = lax.broadcasted_iota(jnp.int32, (1, LANES), 1) >= SSD_HEADS
    acs = jnp.where(is_bwd, tot - pre + dta, pre)
    return dt, acs, tot


def _ssd_bwd_kernel(xm_ref, xp_ref, xn_ref, dt_ref, cw_ref, cb_ref, dtb_ref, alog_ref, rexp_ref,
                    xact_ref, prev_ref, xpad_ref, xf_ref, st_ref, *, nt, qb):
    n = pl.program_id(1)
    tile = nt - 1 - n
    rows = qb * CHUNK

    @pl.when(n == 0)
    def _():
        st_ref[...] = jnp.zeros(st_ref.shape, F32)

    _load_padded(xpad_ref, xm_ref, xp_ref, xn_ref, tile > 0, tile < nt - 1, rows)
    xact = _silu(_dwconv(xpad_ref, cw_ref, cb_ref, rows, SSD_CONV, 0, SSD_XBC))
    xf_ref[...] = xact
    xact_ref[0] = xact.astype(BF16)

    def body(j, carry):
        i = qb - 1 - j
        r0 = pl.multiple_of(i * CHUNK, CHUNK)
        xs = xf_ref[pl.ds(r0, CHUNK), 0:SSD_WIDTH]
        bm = xf_ref[pl.ds(r0, CHUNK), SSD_WIDTH:SSD_WIDTH + SSD_BC].astype(BF16)
        dt, acs, tot = _ssd_decay(dt_ref[0, pl.ds(r0, CHUNK), :], dtb_ref, alog_ref)
        e_in = jnp.concatenate([jnp.exp(tot - acs) * dt, jnp.broadcast_to(jnp.exp(tot), (8, LANES))],
                               axis=0).astype(BF16)
        e_out = jnp.dot(e_in, rexp_ref[...], preferred_element_type=F32)
        xd = (xs * e_out[0:CHUNK]).astype(BF16)
        cdec = e_out[CHUNK:CHUNK + 1]
        for g in range(SSD_GROUPS):
            prev = st_ref[g]
            prev_ref[0, i, g] = prev.astype(BF16)
            s_new = lax.dot_general(bm[:, g * SSD_STATE:(g + 1) * SSD_STATE],
                                    xd[:, g * SSD_GW:(g + 1) * SSD_GW], (((0,), (0,)), ((), ())),
                                    preferred_element_type=F32)
            st_ref[g] = prev * cdec[:, g * SSD_GW:(g + 1) * SSD_GW] + s_new
        return carry

    lax.fori_loop(0, qb, body, 0, unroll=True)


def _ssd_fwd_kernel(xa_ref, dt_ref, prev_ref, z_ref, dtb_ref, alog_ref, rexpf_ref, rexpb_ref,
                    dsk_ref, ng_ref, o_ref, st_ref, *, qb):
    n = pl.program_id(1)

    @pl.when(n == 0)
    def _():
        st_ref[...] = jnp.zeros(st_ref.shape, F32)

    row = lax.broadcasted_iota(jnp.int32, (CHUNK, CHUNK), 0)
    col = lax.broadcasted_iota(jnp.int32, (CHUNK, CHUNK), 1)
    fwd_part = row > col
    diag = row == col
    lane_lo = lax.broadcasted_iota(jnp.int32, (1, LANES), 1) < SSD_HEAD_DIM
    zero_b = jnp.zeros((), BF16)

    def body(i, carry):
        r0 = pl.multiple_of(i * CHUNK, CHUNK)
        xs_b = xa_ref[0, pl.ds(r0, CHUNK), 0:SSD_WIDTH]
        xs = xs_b.astype(F32)
        bm = xa_ref[0, pl.ds(r0, CHUNK), SSD_WIDTH:SSD_WIDTH + SSD_BC]
        cm = xa_ref[0, pl.ds(r0, CHUNK), SSD_WIDTH + SSD_BC:SSD_XBC]
        dt, acs, tot = _ssd_decay(dt_ref[0, pl.ds(r0, CHUNK), :], dtb_ref, alog_ref)
        acs2 = acs * LOG2E
        rt = (acs2 - jnp.log2(dt)).T
        dsum_t = jnp.log2(dt + pltpu.roll(dt, LANES - SSD_HEADS, axis=1)).T
        eacs = jnp.exp(acs)
        ef_in = jnp.concatenate([jnp.exp(tot - acs) * dt, eacs,
                                 jnp.broadcast_to(jnp.exp(tot), (8, LANES))], axis=0).astype(BF16)
        ef = jnp.dot(ef_in, rexpf_ref[...], preferred_element_type=F32)
        eb = jnp.dot(eacs.astype(BF16), rexpb_ref[...], preferred_element_type=F32)
        xd = (xs * ef[0:CHUNK]).astype(BF16)
        eacs_f = ef[CHUNK:2 * CHUNK]
        cdec = ef[2 * CHUNK:2 * CHUNK + 1]
        ys = []
        for g in range(SSD_GROUPS):
            bg = bm[:, g * SSD_STATE:(g + 1) * SSD_STATE]
            cg = cm[:, g * SSD_STATE:(g + 1) * SSD_STATE]
            gs = slice(g * SSD_GW, (g + 1) * SSD_GW)
            cb = lax.dot_general(cg, bg, (((1,), (1,)), ((), ())), preferred_element_type=F32)
            prev = st_ref[g]
            y_off = (jnp.dot(cg, prev.astype(BF16), preferred_element_type=F32) * eacs_f[:, gs]
                     + jnp.dot(cg, prev_ref[0, i, g], preferred_element_type=F32) * eb[:, gs])
            s_new = lax.dot_general(bg, xd[:, gs], (((0,), (0,)), ((), ())),
                                    preferred_element_type=F32)
            st_ref[g] = prev * cdec[:, gs] + s_new
            for pr in range(SSD_HPG // 2):
                c0 = g * SSD_GW + pr * LANES
                xpair = xs_b[:, c0:c0 + LANES]
                xbd = jnp.concatenate([jnp.where(lane_lo, xpair, zero_b),
                                       jnp.where(lane_lo, zero_b, xpair)], axis=0)
                mats = []
                for e in range(2):
                    h = g * SSD_HPG + pr * 2 + e
                    hb = SSD_HEADS + h
                    sel = jnp.where(fwd_part, acs2[:, h:h + 1] - rt[h:h + 1, :],
                                    acs2[:, hb:hb + 1] - rt[hb:hb + 1, :])
                    sel = jnp.where(diag, dsum_t[h:h + 1, :], sel)
                    mats.append((cb * jnp.exp2(sel)).astype(BF16))
                yd = jnp.dot(jnp.concatenate(mats, axis=1), xbd, preferred_element_type=F32)
                ys.append(yd + y_off[:, pr * LANES:(pr + 1) * LANES])
        y = jnp.concatenate(ys, axis=-1) + xs * dsk_ref[...]
        y = y * _silu(z_ref[0, pl.ds(r0, CHUNK), :].astype(F32))
        ms = jnp.mean(y * y, axis=-1, keepdims=True)
        o_ref[0, pl.ds(r0, CHUNK), :] = (y * lax.rsqrt(ms + EPS) * ng_ref[...]).astype(BF16)
        return carry

    lax.fori_loop(0, qb, body, 0, unroll=True)


def _ssd(u3, dt3, cw, cb, dtb, alog, rexp_f, rexp_b, dsk, ng, *, qb):
    B, L, _ = u3.shape
    rows = qb * CHUNK
    assert L % rows == 0
    nt = L // rows
    nc = L // CHUNK
    hb = rows // HALO
    nhb = L // HALO
    state = pltpu.VMEM((SSD_GROUPS, SSD_STATE, SSD_GW), F32)
    const = lambda shape: pl.BlockSpec(shape, lambda b, n: (0,) * len(shape))
    rt = lambda n: nt - 1 - n

    xact, prevb = pl.pallas_call(
        functools.partial(_ssd_bwd_kernel, nt=nt, qb=qb),
        grid=(B, nt),
        in_specs=[
            pl.BlockSpec((1, rows, SSD_XBC), lambda b, n: (b, rt(n), C_XBC // SSD_XBC)),
            pl.BlockSpec((1, HALO, SSD_XBC),
                         lambda b, n: (b, jnp.maximum(rt(n) * hb - 1, 0), C_XBC // SSD_XBC)),
            pl.BlockSpec((1, HALO, SSD_XBC),
                         lambda b, n: (b, jnp.minimum((rt(n) + 1) * hb, nhb - 1), C_XBC // SSD_XBC)),
            pl.BlockSpec((1, rows, DT_PAD), lambda b, n: (b, rt(n), 0)),
            const((8, SSD_XBC)), const((1, SSD_XBC)), const((1, DT_PAD)), const((1, DT_PAD)),
            const((LANES, SSD_WIDTH)),
        ],
        out_specs=[pl.BlockSpec((1, rows, SSD_XBC), lambda b, n: (b, rt(n), 0)),
                   pl.BlockSpec((1, qb, SSD_GROUPS, SSD_STATE, SSD_GW),
                                lambda b, n: (b, rt(n), 0, 0, 0))],
        out_shape=[jax.ShapeDtypeStruct((B, L, SSD_XBC), BF16),
                   jax.ShapeDtypeStruct((B, nc, SSD_GROUPS, SSD_STATE, SSD_GW), BF16)],
        scratch_shapes=[pltpu.VMEM((rows + 2 * HALO, SSD_XBC), BF16),
                        pltpu.VMEM((rows, SSD_XBC), F32), state],
        compiler_params=_cparams(("parallel", "arbitrary")),
        name="ssd_bwd",
    )(u3, u3, u3, dt3, cw, cb, dtb, alog, rexp_b)

    return pl.pallas_call(
        functools.partial(_ssd_fwd_kernel, qb=qb),
        grid=(B, nt),
        in_specs=[
            pl.BlockSpec((1, rows, SSD_XBC), lambda b, n: (b, n, 0)),
            pl.BlockSpec((1, rows, DT_PAD), lambda b, n: (b, n, 0)),
            pl.BlockSpec((1, qb, SSD_GROUPS, SSD_STATE, SSD_GW), lambda b, n: (b, n, 0, 0, 0)),
            pl.BlockSpec((1, rows, SSD_WIDTH), lambda b, n: (b, n, C_Z // SSD_WIDTH)),
            const((1, DT_PAD)), const((1, DT_PAD)),
            const((LANES, SSD_WIDTH)), const((LANES, SSD_WIDTH)),
            const((1, SSD_WIDTH)), const((1, SSD_WIDTH)),
        ],
        out_specs=pl.BlockSpec((1, rows, SSD_WIDTH), lambda b, n: (b, n, 0)),
        out_shape=jax.ShapeDtypeStruct((B, L, SSD_WIDTH), BF16),
        scratch_shapes=[state],
        compiler_params=_cparams(("parallel", "arbitrary")),
        name="ssd_fwd",
    )(xact, dt3, prevb, u3, dtb, alog, rexp_f, rexp_b, dsk, ng)


def _ssd_expand_matrix(rev):
    m = np.zeros((LANES, SSD_WIDTH), np.float32)
    hoff = SSD_HEADS if rev else 0
    for h in range(SSD_HEADS):
        m[hoff + h, h * SSD_HEAD_DIM:(h + 1) * SSD_HEAD_DIM] = 1.0
    return jnp.asarray(m, BF16)


def _hy_filter_kernel(t_ref, bands_ref, w1t_ref, w1c_ref, w1s_ref, b1_ref, w2_ref, b2_ref, w3_ref,
                      b3_ref, w4_ref, fr_ref, absd_ref, x_ref, s_ref, *, L, tl):
    i = pl.program_id(0)
    pos = (lax.broadcasted_iota(jnp.int32, (tl, 1), 0) + i * tl).astype(F32)
    t = t_ref[...]
    ang = 2.0 * math.pi * pos * bands_ref[...] / L
    fr = fr_ref[...]
    dot = functools.partial(jnp.dot, preferred_element_type=F32, precision=HIGHEST)
    pre = t * w1t_ref[...] + dot(jnp.cos(ang), w1c_ref[...]) + dot(-jnp.sin(ang), w1s_ref[...])
    h = jnp.sin(fr * (pre + b1_ref[...]))
    h = jnp.sin(fr * (dot(h, w2_ref[...]) + b2_ref[...]))
    h = jnp.sin(fr * (dot(h, w3_ref[...]) + b3_ref[...]))
    h = dot(h, w4_ref[...])
    decay = jnp.exp(-t * absd_ref[...])
    hf = h[:, :HY_WIDTH] * decay
    hb = jnp.where(pos == 0.0, 0.0, h[:, HY_WIDTH:] * decay)
    _to_lane_blocks(s_ref, jnp.concatenate([hf + hb, hb - hf], axis=1))
    x_ref[...] = _split_even_odd(s_ref).astype(BF16)


def _hy_filter(L, w1, b1, w2, b2, w3, b3, w4, freq):
    tl = min(512, L)
    t = jnp.linspace(0.0, 1.0, L, dtype=F32)[:, None]
    bands = jnp.linspace(1e-4, HY_EMB_BANDS - 1, HY_EMB_BANDS, dtype=F32)[None, :]
    max_decay = math.log(HY_DECAY_TARGET) / HY_FAST_DECAY
    min_decay = math.log(HY_DECAY_TARGET) / HY_SLOW_DECAY
    absd = jnp.abs(jnp.linspace(min_decay, max_decay, HY_WIDTH, dtype=F32))[None, :]
    w1 = w1.astype(F32)
    full = lambda a: pl.BlockSpec(a.shape, lambda i: (0,) * a.ndim)
    args = [t, bands, w1[0:1], w1[1:1 + HY_EMB_BANDS], w1[1 + HY_EMB_BANDS:], b1[None], w2, b2[None],
            w3, b3[None], w4, freq[None], absd]
    in_specs = [pl.BlockSpec((tl, 1), lambda i: (i, 0))] + [full(a) for a in args[1:]]
    return pl.pallas_call(
        functools.partial(_hy_filter_kernel, L=L, tl=tl),
        grid=(L // tl,),
        in_specs=in_specs,
        out_specs=pl.BlockSpec((tl // 2, 4 * HY_WIDTH), lambda i: (i, 0)),
        out_shape=jax.ShapeDtypeStruct((L // 2, 4 * HY_WIDTH), BF16),
        scratch_shapes=[pltpu.VMEM((2 * HY_WIDTH // LANES, tl, LANES), F32)],
        compiler_params=_cparams(("parallel",)),
        name="hy_filter",
    )(*args)


def _dft_tables(L, hm):
    L2 = L // 2
    g = jnp.arange(L2, dtype=jnp.int32)
    s = jnp.arange(L2, dtype=jnp.int32)
    ph = ((2 * g + 1)[:, None] * s[None, :]) % (2 * L)
    ang = ph.astype(F32) * (math.pi / L)
    c = jnp.cos(ang).astype(BF16).reshape(L2 // hm, 1, hm, L2)
    sn = jnp.sin(ang).astype(BF16).reshape(L2 // hm, 1, hm, L2)
    a_fwd = jnp.concatenate([c, sn], axis=1).reshape(L, L2)
    a_inv = jnp.concatenate([c, -sn], axis=1).reshape(L, L2).T
    return a_fwd, a_inv


def _hy_kraw_kernel(a_ref, x_ref, k_ref):
    k_ref[...] = jnp.dot(a_ref[...], x_ref[...], preferred_element_type=F32)


def _hy_kraw(a_fwd, xf2, *, hm):
    L, L2 = a_fwd.shape
    tm = 2 * hm
    nx = xf2.shape[1]
    return pl.pallas_call(
        _hy_kraw_kernel,
        grid=(L // tm,),
        in_specs=[pl.BlockSpec((tm, L2), lambda i: (i, 0)),
                  pl.BlockSpec((L2, nx), lambda i: (0, 0))],
        out_specs=pl.BlockSpec((tm, nx), lambda i: (i, 0)),
        out_shape=jax.ShapeDtypeStruct((L, nx), F32),
        compiler_params=_cparams(("parallel",)),
        name="hy_kspec",
    )(a_fwd, xf2)


def _hy_filter_spectrum(pq, L, hm):
    L2 = L // 2
    W = HY_WIDTH
    pq = pq.reshape(L2 // hm, 2, hm, 4 * W)
    p = pq[:, 0].reshape(L2, 4 * W)
    q = pq[:, 1].reshape(L2, 4 * W)
    ea, eb, oa, ob = [(p[:, i * W:(i + 1) * W], -q[:, i * W:(i + 1) * W]) for i in range(4)]
    theta = (2.0 * jnp.arange(L2, dtype=F32) + 1.0) * (math.pi / (2 * L))
    w = (jnp.cos(theta)[:, None], -jnp.sin(theta)[:, None])
    cmul = lambda x, y: (x[0] * y[0] - x[1] * y[1], x[0] * y[1] + x[1] * y[0])
    woa, wob = cmul(w, oa), cmul(w, ob)
    k1 = (ea[0] + woa[0], -(eb[1] + wob[1]))
    k2 = (ea[0] - woa[0], eb[1] - wob[1])
    kp = (k1[0] + k2[0], k1[1] - k2[1])
    km = (k1[0] - k2[0], k1[1] + k2[1])
    wkm = cmul(w, km)
    vkm = cmul((w[0], -w[1]), km)
    filt = jnp.stack([kp[0], kp[1], wkm[0], wkm[1], vkm[0], vkm[1]]) * (1.0 / L)
    return jnp.transpose(filt.reshape(6, L2 // hm, hm, W), (1, 0, 2, 3))


def _to_lane_blocks(s_ref, x):
    for c in range(s_ref.shape[0]):
        s_ref[c] = x[:, c * LANES:(c + 1) * LANES]


def _split_even_odd(s_ref):
    k, rows, _ = s_ref.shape
    return jnp.concatenate([s_ref[c, pl.ds(p, rows // 2, stride=2), :]
                            for p in range(2) for c in range(k)], axis=1)


def _hy_pre_kernel(xm_ref, xp_ref, xn_ref, cw_ref, cb_ref, u_ref, xau_ref, xpad_ref, s_ref, *, nt, tl):
    i = pl.program_id(1)
    _load_padded(xpad_ref, xm_ref, xp_ref, xn_ref, i > 0, i < nt - 1, tl)
    xa = _dwconv(xpad_ref, cw_ref, cb_ref, tl, HY_CONV, 0, HY_WIDTH)
    xb = _dwconv(xpad_ref, cw_ref, cb_ref, tl, HY_CONV, HY_WIDTH, 2 * HY_WIDTH)
    v = _dwconv(xpad_ref, cw_ref, cb_ref, tl, HY_CONV, 2 * HY_WIDTH, 3 * HY_WIDTH)
    u = xb * v
    xau_ref[0, :, :HY_WIDTH] = xa.astype(BF16)
    xau_ref[0, :, HY_WIDTH:] = u.astype(BF16)
    _to_lane_blocks(s_ref, u)
    u_ref[0] = _split_even_odd(s_ref).astype(BF16)


def _hy_tile_specs(L, tl):
    hb = tl // HALO
    nhb = L // HALO
    w = 3 * HY_WIDTH
    return [pl.BlockSpec((1, tl, w), lambda b, i: (b, i, C_HY // w)),
            pl.BlockSpec((1, HALO, w), lambda b, i: (b, jnp.maximum(i * hb - 1, 0), C_HY // w)),
            pl.BlockSpec((1, HALO, w), lambda b, i: (b, jnp.minimum((i + 1) * hb, nhb - 1), C_HY // w))]


def _hy_pre(u3, cw, cb, *, tl):
    B, L, _ = u3.shape
    nt = L // tl
    w = 3 * HY_WIDTH
    return pl.pallas_call(
        functools.partial(_hy_pre_kernel, nt=nt, tl=tl),
        grid=(B, nt),
        in_specs=_hy_tile_specs(L, tl) + [pl.BlockSpec((8, w), lambda b, i: (0, 0)),
                                          pl.BlockSpec((1, w), lambda b, i: (0, 0))],
        out_specs=[pl.BlockSpec((1, tl // 2, 2 * HY_WIDTH), lambda b, i: (b, i, 0)),
                   pl.BlockSpec((1, tl, 2 * HY_WIDTH), lambda b, i: (b, i, 0))],
        out_shape=[jax.ShapeDtypeStruct((B, L // 2, 2 * HY_WIDTH), BF16),
                   jax.ShapeDtypeStruct((B, L, 2 * HY_WIDTH), BF16)],
        scratch_shapes=[pltpu.VMEM((tl + 2 * HALO, w), BF16),
                        pltpu.VMEM((HY_WIDTH // LANES, tl, LANES), F32)],
        compiler_params=_cparams(("parallel", "parallel")),
        name="hy_pre",
    )(u3, u3, u3, cw, cb)


def _hy_fwd_kernel(a_ref, u_ref, f_ref, y_ref, *, hm):
    W = HY_WIDTH
    pq = jnp.dot(a_ref[...], u_ref[0], preferred_element_type=F32)
    pe, po = pq[0:hm, :W], pq[0:hm, W:]
    qe, qo = pq[hm:, :W], pq[hm:, W:]
    kpr, kpi, wr, wi, vr, vi = [f_ref[0, i] for i in range(6)]
    y_ref[0, 0:hm, :W] = (pe * kpr + qe * kpi + po * wr + qo * wi).astype(BF16)
    y_ref[0, hm:, :W] = (pe * kpi - qe * kpr + po * wi - qo * wr).astype(BF16)
    y_ref[0, 0:hm, W:] = (pe * vr + qe * vi + po * kpr + qo * kpi).astype(BF16)
    y_ref[0, hm:, W:] = (pe * vi - qe * vr + po * kpi - qo * kpr).astype(BF16)


def _hy_fwd(a_fwd, ueo, filt, *, hm):
    B, L2, W2 = ueo.shape
    L = 2 * L2
    tm = 2 * hm
    return pl.pallas_call(
        functools.partial(_hy_fwd_kernel, hm=hm),
        grid=(L // tm, B),
        in_specs=[pl.BlockSpec((tm, L2), lambda i, b: (i, 0)),
                  pl.BlockSpec((1, L2, W2), lambda i, b: (b, 0, 0)),
                  pl.BlockSpec((1, 6, hm, HY_WIDTH), lambda i, b: (i, 0, 0, 0))],
        out_specs=pl.BlockSpec((1, tm, W2), lambda i, b: (b, i, 0)),
        out_shape=jax.ShapeDtypeStruct((B, L, W2), BF16),
        compiler_params=_cparams(("parallel", "parallel")),
        name="hy_fwd",
    )(a_fwd, ueo, filt)


def _hy_inv_kernel(a_ref, y_ref, o_ref):
    o_ref[0] = jnp.dot(a_ref[...], y_ref[0], preferred_element_type=F32)


def _hy_inv(a_inv, yspec, *, tm):
    B, L, W2 = yspec.shape
    L2 = L // 2
    return pl.pallas_call(
        _hy_inv_kernel,
        grid=(L2 // tm, B),
        in_specs=[pl.BlockSpec((tm, L), lambda i, b: (i, 0)),
                  pl.BlockSpec((1, L, W2), lambda i, b: (b, 0, 0))],
        out_specs=pl.BlockSpec((1, tm, W2), lambda i, b: (b, i, 0)),
        out_shape=jax.ShapeDtypeStruct((B, L2, W2), F32),
        compiler_params=_cparams(("parallel", "parallel")),
        name="hy_inv",
    )(a_inv, yspec)


def _hy_post_kernel(c_ref, xau_ref, g_ref, d_ref, ng_ref, o_ref, s_ref, *, tl):
    nblk = HY_WIDTH // LANES
    for p in range(2):
        for c in range(nblk):
            s_ref[c, pl.ds(p, tl // 2, stride=2), :] = c_ref[0, :, (p * nblk + c) * LANES:
                                                             (p * nblk + c + 1) * LANES]
    conv = jnp.concatenate([s_ref[c] for c in range(nblk)], axis=1)
    xa = xau_ref[0, :, :HY_WIDTH].astype(F32)
    u = xau_ref[0, :, HY_WIDTH:].astype(F32)
    y = xa * (conv + u * d_ref[...])
    y = y * _silu(g_ref[0].astype(F32))
    ms = jnp.mean(y * y, axis=-1, keepdims=True)
    o_ref[0] = (y * lax.rsqrt(ms + EPS) * ng_ref[...]).astype(BF16)


def _hy_post(conv, xau, u3, d, ng, *, tl):
    B, L, _ = u3.shape
    return pl.pallas_call(
        functools.partial(_hy_post_kernel, tl=tl),
        grid=(B, L // tl),
        in_specs=[pl.BlockSpec((1, tl // 2, 2 * HY_WIDTH), lambda b, i: (b, i, 0)),
                  pl.BlockSpec((1, tl, 2 * HY_WIDTH), lambda b, i: (b, i, 0)),
                  pl.BlockSpec((1, tl, HY_WIDTH), lambda b, i: (b, i, C_GHY // HY_WIDTH)),
                  pl.BlockSpec((1, HY_WIDTH), lambda b, i: (0, 0)),
                  pl.BlockSpec((1, HY_WIDTH), lambda b, i: (0, 0))],
        out_specs=pl.BlockSpec((1, tl, HY_WIDTH), lambda b, i: (b, i, 0)),
        out_shape=jax.ShapeDtypeStruct((B, L, HY_WIDTH), BF16),
        scratch_shapes=[pltpu.VMEM((HY_WIDTH // LANES, tl, LANES), F32)],
        compiler_params=_cparams(("parallel", "parallel")),
        name="hy_post",
    )(conv, xau, u3, d, ng)


def _pad_rows(a, rows):
    return jnp.concatenate([a, jnp.zeros((rows - a.shape[0],) + a.shape[1:], a.dtype)], axis=0)


def _pad_cols(a, cols):
    return jnp.concatenate([a, jnp.zeros(a.shape[:-1] + (cols - a.shape[-1],), a.dtype)], axis=-1)


def _static_take(a, idx, axis):
    idx = np.asarray(idx)
    cuts = np.flatnonzero(np.diff(idx) != 1) + 1
    starts = np.concatenate([[0], cuts])
    ends = np.concatenate([cuts, [len(idx)]])
    parts = [lax.slice_in_dim(a, int(idx[s]), int(idx[e - 1]) + 1, axis=axis)
             for s, e in zip(starts, ends)]
    return jnp.concatenate(parts, axis=axis)


def _tile(n, pref):
    t = min(pref, n)
    assert n % t == 0
    return t


def _layer(x3, p, tables, *, final, final_g):
    B, L, _ = x3.shape
    T = B * L
    x2 = x3.reshape(T, D_MODEL)
    u, dt = _inproj(x2, p["norm_g"], p["w_in"], tm=_tile(T, 512), tn=512)
    u3 = u.reshape(B, L, N_IN)
    dt3 = dt.reshape(B, L, DT_PAD)

    ys = _ssd(u3, dt3, p["ssd_cw"], p["ssd_cb"], p["ssd_dtb"], p["ssd_alog"], p["rexp_f"],
              p["rexp_b"], p["ssd_dskip"], p["ssd_ng"], qb=_tile(L // CHUNK, 4))

    ya = _attention(u3, p["att_bias"], p["att_sink"], p["att_ng"], qb=_tile(L // ATT_BLOCK, 4))

    a_fwd, a_inv, filt, hm = tables
    tl = _tile(L, 1024)
    ueo, xau = _hy_pre(u3, p["hy_cw"], p["hy_cb"], tl=tl)
    yspec = _hy_fwd(a_fwd, ueo, filt, hm=hm)
    conv = _hy_inv(a_inv, yspec, tm=_tile(L // 2, 512))
    yh = _hy_post(conv, xau, u3, p["hy_d"], p["hy_ng"], tl=tl)

    out = _outproj(x2, ys.reshape(T, SSD_WIDTH), ya.reshape(T, ATT_WIDTH), yh.reshape(T, HY_WIDTH),
                   p["w_out"], final_g, tm=_tile(T, 512), final=final)
    return out.reshape(B, L, D_MODEL)


def kernel(x_prompt, x_sample, rel_bias, norm_g, w_in, ssd_conv_w, ssd_conv_b, ssd_dt_bias, ssd_a_log, ssd_d, ssd_norm_g, att_sink, att_norm_g, hy_conv_w, hy_conv_b, hy_w1, hy_b1, hy_w2, hy_b2, hy_w3, hy_b3, hy_w4, hy_freq, hy_d, hy_norm_g, w_out, final_norm_g):
    depth = w_in.shape[0]
    perm, col_scale = _in_perm()
    n_real = C_DT + 2 * SSD_HEADS
    w_in_p = _pad_cols((_static_take(w_in, perm[:n_real], 2)
                        * jnp.asarray(col_scale[:n_real])).astype(BF16), N_IN)
    att_perm = _att_col_perm()
    out_rows = np.concatenate([np.arange(SSD_WIDTH), SSD_WIDTH + att_perm,
                               np.arange(SSD_WIDTH + ATT_WIDTH, D_MODEL)])
    w_out_b = _static_take(w_out, out_rows, 1).astype(BF16)
    att_bias = _attn_bias(rel_bias)
    rexp_f, rexp_b = _ssd_expand_matrix(False), _ssd_expand_matrix(True)
    final_g = final_norm_g.astype(F32)[None, :]

    layers = []
    for i in range(depth):
        layers.append(dict(
            norm_g=norm_g[i].astype(F32)[None, :],
            w_in=w_in_p[i],
            ssd_cw=_pad_rows(ssd_conv_w[i].astype(F32), 8),
            ssd_cb=ssd_conv_b[i].astype(F32)[None, :],
            ssd_dtb=_pad_cols(ssd_dt_bias[i].astype(F32).reshape(1, 2 * SSD_HEADS), DT_PAD),
            ssd_alog=_pad_cols(ssd_a_log[i].astype(F32).reshape(1, 2 * SSD_HEADS), DT_PAD),
            ssd_dskip=jnp.repeat(ssd_d[i].astype(F32), SSD_HEAD_DIM)[None, :],
            ssd_ng=ssd_norm_g[i].astype(F32)[None, :],
            rexp_f=rexp_f, rexp_b=rexp_b,
            att_bias=att_bias,
            att_sink=jnp.broadcast_to(att_sink[i].astype(F32)[:, None] * LOG2E, (ATT_HEADS, LANES)),
            att_ng=att_norm_g[i].astype(F32)[att_perm][None, :],
            hy_cw=_pad_rows(hy_conv_w[i].astype(F32), 8),
            hy_cb=hy_conv_b[i].astype(F32)[None, :],
            hy_d=hy_d[i].astype(F32)[None, :],
            hy_ng=hy_norm_g[i].astype(F32)[None, :],
            w_out=w_out_b[i],
        ))

    def trunk(x):
        L = x.shape[1]
        hm = min(256, L // 4)
        a_fwd, a_inv = _dft_tables(L, hm)
        for i in range(depth):
            xf = _hy_filter(L, hy_w1[i], hy_b1[i].astype(F32), hy_w2[i].astype(F32),
                            hy_b2[i].astype(F32), hy_w3[i].astype(F32), hy_b3[i].astype(F32),
                            hy_w4[i].astype(F32), hy_freq[i].astype(F32))
            pq = _hy_kraw(a_fwd, xf, hm=hm)
            filt = _hy_filter_spectrum(pq, L, hm)
            x = _layer(x, layers[i], (a_fwd, a_inv, filt, hm), final=(i == depth - 1),
                       final_g=final_g)
        return x

    return (trunk(x_prompt), trunk(x_sample))
```

```python
import functools
import math

import jax
import jax.numpy as jnp
import numpy as np
from jax import lax
from jax.experimental import pallas as pl
from jax.experimental.pallas import tpu as pltpu

F32 = jnp.float32
BF16 = jnp.bfloat16
HIGHEST = lax.Precision.HIGHEST

D_MODEL = 2048
SSD_WIDTH = 1024
ATT_WIDTH = 512
HY_WIDTH = 512
SSD_HEAD_DIM = 64
SSD_HEADS = 16
SSD_GROUPS = 2
SSD_STATE = 128
SSD_CONV = 5
CHUNK = 128
SSD_XBC = SSD_WIDTH + 2 * SSD_GROUPS * SSD_STATE
ATT_HEAD_DIM = 64
ATT_HEADS = 8
ATT_KV_HEADS = 2
ATT_REP = ATT_HEADS // ATT_KV_HEADS
ATT_WINDOW = 128
ATT_BLOCK = 128
REL_BUCKETS = 32
REL_MAX_DIST = 128
HY_CONV = 3
HY_EMB_BANDS = 16
HY_FF = 64
HY_FAST_DECAY = 0.3
HY_SLOW_DECAY = 1.5
HY_DECAY_TARGET = 1e-2
EPS = 1e-6
NEG_BIG = -1e30

LANES = 128
BF16_SUBLANES = 16
VMEM_LIMIT = 56 * 1024 * 1024

C_XBC = 0
C_HY = 1536
C_Z = 3072
C_Q = 4096
C_GATT = 4608
C_GHY = 5120
C_K = 5632
C_V = 5760
C_DT = 5888
DT_PAD = 128
N_IN = C_DT + DT_PAD

_OLD_SIZES = [SSD_WIDTH, SSD_XBC, 2 * SSD_HEADS, ATT_WIDTH, ATT_KV_HEADS * ATT_HEAD_DIM,
              ATT_KV_HEADS * ATT_HEAD_DIM, ATT_WIDTH, 3 * HY_WIDTH, HY_WIDTH]
_OLD_OFF = np.concatenate([[0], np.cumsum(_OLD_SIZES)])
IN_COLS = int(_OLD_OFF[-1])


LOG2E = math.log2(math.e)
Q_SCALE = ATT_HEAD_DIM ** -0.5 * LOG2E


def _att_col_perm():
    order = [h for j in range(ATT_REP) for h in (j, ATT_REP + j)]
    return np.concatenate([np.arange(h * ATT_HEAD_DIM, (h + 1) * ATT_HEAD_DIM) for h in order])


def _in_perm():
    perm = np.full((N_IN,), IN_COLS, np.int32)
    scale = np.ones((N_IN,), np.float32)
    o = {n: int(_OLD_OFF[i]) for i, n in enumerate(
        ["z", "xbc", "dt", "q", "k", "v", "gatt", "hy", "ghy"])}
    def put(new, old, width):
        perm[new:new + width] = np.arange(old, old + width)
    put(C_XBC, o["xbc"], SSD_XBC)
    put(C_HY, o["hy"], 3 * HY_WIDTH)
    put(C_Z, o["z"], SSD_WIDTH)
    perm[C_Q:C_Q + ATT_WIDTH] = o["q"] + _att_col_perm()
    scale[C_Q:C_Q + ATT_WIDTH] = Q_SCALE
    perm[C_GATT:C_GATT + ATT_WIDTH] = o["gatt"] + _att_col_perm()
    put(C_GHY, o["ghy"], HY_WIDTH)
    put(C_K, o["k"], 128)
    put(C_V, o["v"], 128)
    put(C_DT, o["dt"], 2 * SSD_HEADS)
    return perm, scale


def _cparams(sem):
    return pltpu.CompilerParams(dimension_semantics=sem, vmem_limit_bytes=VMEM_LIMIT)


def _silu(x):
    return x * (1.0 / (1.0 + jnp.exp(-x)))


def _softplus(x):
    return jnp.maximum(x, 0.0) + jnp.log1p(jnp.exp(-jnp.abs(x)))


def _inproj_kernel(x_ref, g_ref, w_ref, u_ref, dt_ref, h_ref, *, rows, tn):
    for r in range(x_ref.shape[0] // rows):
        sl = slice(r * rows, (r + 1) * rows)
        x = x_ref[sl, :]
        ms = jnp.mean(x * x, axis=-1, keepdims=True)
        h_ref[sl, :] = (x * lax.rsqrt(ms + EPS) * g_ref[...]).astype(BF16)
        h = h_ref[sl, :]
        for c0 in range(0, N_IN, tn):
            c1 = min(c0 + tn, N_IN)
            acc = jnp.dot(h, w_ref[:, c0:c1], preferred_element_type=F32)
            u_ref[sl, c0:c1] = acc.astype(BF16)
            if c0 <= C_DT < c1:
                dt_ref[sl, :] = acc[:, C_DT - c0:C_DT - c0 + DT_PAD]


def _inproj(x2, g, w, *, tm, tn):
    T = x2.shape[0]
    assert T % tm == 0
    return pl.pallas_call(
        functools.partial(_inproj_kernel, rows=min(256, tm), tn=tn),
        grid=(T // tm,),
        in_specs=[pl.BlockSpec((tm, D_MODEL), lambda i: (i, 0)),
                  pl.BlockSpec((1, D_MODEL), lambda i: (0, 0)),
                  pl.BlockSpec((D_MODEL, N_IN), lambda i: (0, 0), pipeline_mode=pl.Buffered(1))],
        out_specs=[pl.BlockSpec((tm, N_IN), lambda i: (i, 0)),
                   pl.BlockSpec((tm, DT_PAD), lambda i: (i, 0))],
        out_shape=[jax.ShapeDtypeStruct((T, N_IN), BF16),
                   jax.ShapeDtypeStruct((T, DT_PAD), F32)],
        scratch_shapes=[pltpu.VMEM((tm, D_MODEL), BF16)],
        compiler_params=_cparams(("parallel",)),
        name="inproj",
    )(x2, g, w)


def _outproj_kernel(x_ref, ys_ref, ya_ref, yh_ref, w_ref, fg_ref, o_ref, *, final):
    acc = x_ref[...]
    acc = acc + jnp.dot(ys_ref[...], w_ref[0:SSD_WIDTH, :], preferred_element_type=F32)
    acc = acc + jnp.dot(ya_ref[...], w_ref[SSD_WIDTH:SSD_WIDTH + ATT_WIDTH, :],
                        preferred_element_type=F32)
    acc = acc + jnp.dot(yh_ref[...], w_ref[SSD_WIDTH + ATT_WIDTH:, :], preferred_element_type=F32)
    if final:
        ms = jnp.mean(acc * acc, axis=-1, keepdims=True)
        acc = acc * lax.rsqrt(ms + EPS) * fg_ref[...]
    o_ref[...] = acc


def _outproj(x2, ys, ya, yh, w, fg, *, tm, final):
    T = x2.shape[0]
    assert T % tm == 0
    return pl.pallas_call(
        functools.partial(_outproj_kernel, final=final),
        grid=(T // tm,),
        in_specs=[pl.BlockSpec((tm, D_MODEL), lambda i: (i, 0)),
                  pl.BlockSpec((tm, SSD_WIDTH), lambda i: (i, 0)),
                  pl.BlockSpec((tm, ATT_WIDTH), lambda i: (i, 0)),
                  pl.BlockSpec((tm, HY_WIDTH), lambda i: (i, 0)),
                  pl.BlockSpec((D_MODEL, D_MODEL), lambda i: (0, 0)),
                  pl.BlockSpec((1, D_MODEL), lambda i: (0, 0))],
        out_specs=pl.BlockSpec((tm, D_MODEL), lambda i: (i, 0)),
        out_shape=jax.ShapeDtypeStruct((T, D_MODEL), F32),
        compiler_params=_cparams(("parallel",)),
        name="outproj",
    )(x2, ys, ya, yh, w, fg)


def _t5_buckets(rel):
    nb = REL_BUCKETS // 2
    max_exact = nb // 2
    ret = (rel > 0).astype(np.int32) * nb
    n = np.abs(rel)
    large = max_exact + (np.log(np.maximum(n, 1) / max_exact) / math.log(REL_MAX_DIST / max_exact)
                         * (nb - max_exact)).astype(np.int32)
    large = np.minimum(large, nb - 1)
    return ret + np.where(n < max_exact, n, large)


def _attn_bias(rel_bias):
    qi = np.arange(ATT_BLOCK)[:, None]
    kj = np.arange(3 * ATT_BLOCK)[None, :]
    rel = kj - ATT_BLOCK - qi
    onehot = (_t5_buckets(rel)[None] == np.arange(REL_BUCKETS)[:, None, None]).astype(np.float32)
    bias = jnp.einsum("bqk,bh->hqk", jnp.asarray(onehot), rel_bias.astype(F32),
                      precision=HIGHEST) * LOG2E
    window = np.abs(rel) <= ATT_WINDOW
    variants = []
    for last in (False, True):
        for first in (False, True):
            ok = window & ~(first & (kj < ATT_BLOCK)) & ~(last & (kj >= 2 * ATT_BLOCK))
            variants.append(jnp.where(ok[None], bias, NEG_BIG))
    return jnp.transpose(jnp.stack(variants), (0, 1, 3, 2))


def _attn_kernel(q_ref, kp_ref, kc_ref, kn_ref, vp_ref, vc_ref, vn_ref, g_ref, bias_ref, sink_ref,
                 ng_ref, o_ref, klo_ref, khi_ref, vt_ref, *, nt, qb):
    n = pl.program_id(1)
    lo = lax.broadcasted_iota(jnp.int32, (1, LANES), 1) < ATT_HEAD_DIM
    zero = jnp.zeros((), BF16)
    kext = jnp.concatenate([kp_ref[0], kc_ref[0], kn_ref[0]], axis=0)
    klo_ref[...] = jnp.where(lo, kext, zero)
    khi_ref[...] = jnp.where(lo, zero, kext)
    for t, ref, cnt in ((0, vp_ref, 1), (1, vc_ref, qb), (qb + 1, vn_ref, 1)):
        for i in range(cnt):
            blk = ref[0, i * ATT_BLOCK:(i + 1) * ATT_BLOCK, :]
            vt_ref[t + i] = blk.astype(F32).T.astype(BF16)
    row_lo = lax.broadcasted_iota(jnp.int32, (LANES, ATT_BLOCK), 0) < ATT_HEAD_DIM
    nk = 3 * ATT_BLOCK

    def body(i, carry):
        r0 = pl.multiple_of(i * ATT_BLOCK, ATT_BLOCK)
        q = q_ref[0, pl.ds(r0, ATT_BLOCK), :]
        kst = jnp.concatenate([klo_ref[pl.ds(r0, nk), :], khi_ref[pl.ds(r0, nk), :]], axis=0)
        vt = jnp.concatenate([vt_ref[i], vt_ref[i + 1], vt_ref[i + 2]], axis=1)
        first = jnp.logical_and(n == 0, i == 0)
        last = jnp.logical_and(n == nt - 1, i == qb - 1)
        variant = first.astype(jnp.int32) + 2 * last.astype(jnp.int32)
        outs = []
        for j in range(ATT_REP):
            qp = q[:, j * LANES:(j + 1) * LANES]
            st = lax.dot_general(kst, qp, (((1,), (1,)), ((), ())), preferred_element_type=F32)
            halves = []
            for e, h in enumerate((j, ATT_REP + j)):
                s = st[e * nk:(e + 1) * nk] + bias_ref[variant, h]
                sk = sink_ref[h:h + 1, :]
                m = jnp.maximum(jnp.max(s, axis=0, keepdims=True), sk)
                p = jnp.exp2(s - m)
                den = jnp.sum(p, axis=0, keepdims=True) + jnp.exp2(sk - m)
                ot = jnp.dot(vt, p.astype(BF16), preferred_element_type=F32)
                halves.append(ot * (1.0 / den))
            outs.append(jnp.where(row_lo, halves[0], halves[1]).T)
        o = jnp.concatenate(outs, axis=-1)
        y = o * _silu(g_ref[0, pl.ds(r0, ATT_BLOCK), :].astype(F32))
        ms = jnp.mean(y * y, axis=-1, keepdims=True)
        o_ref[0, pl.ds(r0, ATT_BLOCK), :] = (y * lax.rsqrt(ms + EPS) * ng_ref[...]).astype(BF16)
        return carry

    lax.fori_loop(0, qb, body, 0, unroll=True)


def _attention(u3, bias, sink, ng, *, qb):
    B, L, _ = u3.shape
    tq = qb * ATT_BLOCK
    assert L % tq == 0
    nt = L // tq
    nb = L // ATT_BLOCK
    kcol, vcol = C_K // 128, C_V // 128
    def kv_specs(colblk):
        return [pl.BlockSpec((1, ATT_BLOCK, 128), lambda b, n: (b, jnp.maximum(n * qb - 1, 0), colblk)),
                pl.BlockSpec((1, tq, 128), lambda b, n: (b, n, colblk)),
                pl.BlockSpec((1, ATT_BLOCK, 128),
                             lambda b, n: (b, jnp.minimum((n + 1) * qb, nb - 1), colblk))]
    return pl.pallas_call(
        functools.partial(_attn_kernel, nt=nt, qb=qb),
        grid=(B, nt),
        in_specs=[pl.BlockSpec((1, tq, ATT_WIDTH), lambda b, n: (b, n, C_Q // ATT_WIDTH))]
                 + kv_specs(kcol) + kv_specs(vcol)
                 + [pl.BlockSpec((1, tq, ATT_WIDTH), lambda b, n: (b, n, C_GATT // ATT_WIDTH)),
                    pl.BlockSpec((4, ATT_HEADS, 3 * ATT_BLOCK, ATT_BLOCK), lambda b, n: (0, 0, 0, 0)),
                    pl.BlockSpec((ATT_HEADS, LANES), lambda b, n: (0, 0)),
                    pl.BlockSpec((1, ATT_WIDTH), lambda b, n: (0, 0))],
        out_specs=pl.BlockSpec((1, tq, ATT_WIDTH), lambda b, n: (b, n, 0)),
        out_shape=jax.ShapeDtypeStruct((B, L, ATT_WIDTH), BF16),
        scratch_shapes=[pltpu.VMEM(((qb + 2) * ATT_BLOCK, LANES), BF16),
                        pltpu.VMEM(((qb + 2) * ATT_BLOCK, LANES), BF16),
                        pltpu.VMEM((qb + 2, LANES, ATT_BLOCK), BF16)],
        compiler_params=_cparams(("parallel", "parallel")),
        name="attn",
    )(u3, u3, u3, u3, u3, u3, u3, u3, bias, sink, ng)


HALO = BF16_SUBLANES


CONV_BLK = 128


def _load_padded(xpad_ref, xm_ref, xp_ref, xn_ref, has_prev, has_next, rows):
    zero = jnp.zeros((), BF16)
    xpad_ref[pl.ds(0, HALO), :] = jnp.where(has_prev, xp_ref[0], zero)
    xpad_ref[pl.ds(HALO, rows), :] = xm_ref[0]
    xpad_ref[pl.ds(HALO + rows, HALO), :] = jnp.where(has_next, xn_ref[0], zero)


def _shift_matrix(width):
    offs = [k - width // 2 for k in range(width) if k != width // 2]
    r = lax.broadcasted_iota(jnp.int32, (CONV_BLK, CONV_BLK + 2 * HALO), 0)
    c = lax.broadcasted_iota(jnp.int32, (CONV_BLK, CONV_BLK + 2 * HALO), 1)
    return jnp.concatenate([(c == r + HALO + d).astype(BF16) for d in offs], axis=0)


def _dwconv_block(xpad_ref, shifts, w_ref, b_ref, j, width, c0, c1):
    win = xpad_ref[j * CONV_BLK:(j + 1) * CONV_BLK + 2 * HALO, c0:c1]
    moved = jnp.dot(shifts, win, preferred_element_type=F32)
    acc = win[HALO:HALO + CONV_BLK].astype(F32) * w_ref[width // 2:width // 2 + 1, c0:c1]
    i = 0
    for k in range(width):
        if k == width // 2:
            continue
        acc = acc + moved[i * CONV_BLK:(i + 1) * CONV_BLK] * w_ref[k:k + 1, c0:c1]
        i += 1
    return acc + b_ref[:, c0:c1]


def _dwconv(xpad_ref, w_ref, b_ref, rows, width, c0, c1):
    shifts = _shift_matrix(width)
    return jnp.concatenate([_dwconv_block(xpad_ref, shifts, w_ref, b_ref, j, width, c0, c1)
                            for j in range(rows // CONV_BLK)], axis=0)


SSD_HPG = SSD_HEADS // SSD_GROUPS
SSD_GW = SSD_HPG * SSD_HEAD_DIM
SSD_BC = SSD_GROUPS * SSD_STATE


def _ssd_decay(dt_raw, dtb_ref, alog_ref):
    dt = _softplus(dt_raw + dtb_ref[...])
    dta = dt * (-jnp.exp(alog_ref[...]))
    row = lax.broadcasted_iota(jnp.int32, (CHUNK, CHUNK), 0)
    col = lax.broadcasted_iota(jnp.int32, (CHUNK, CHUNK), 1)
    tril = (row >= col).astype(BF16)
    hi = dta.astype(BF16)
    r1 = dta - hi.astype(F32)
    mid = r1.astype(BF16)
    lo = (r1 - mid.astype(F32)).astype(BF16)
    pre = (jnp.dot(tril, hi, preferred_element_type=F32) + jnp.dot(tril, mid, preferred_element_type=F32)
           + jnp.dot(tril, lo, preferred_element_type=F32))
    tot = pre[CHUNK - 1:CHUNK, :]
    is_bwd = lax.broadcasted_iota(jnp.int32, (1, LANES), 1) >= SSD_HEADS
    acs = jnp.where(is_bwd, tot - pre + dta, pre)
    return dt, acs, tot


def _ssd_bwd_kernel(xm_ref, xp_ref, xn_ref, dt_ref, cw_ref, cb_ref, dtb_ref, alog_ref, rexp_ref,
                    xact_ref, prev_ref, xpad_ref, xf_ref, st_ref, *, nt, qb):
    n = pl.program_id(1)
    tile = nt - 1 - n
    rows = qb * CHUNK

    @pl.when(n == 0)
    def _():
        st_ref[...] = jnp.zeros(st_ref.shape, F32)

    _load_padded(xpad_ref, xm_ref, xp_ref, xn_ref, tile > 0, tile < nt - 1, rows)
    xact = _silu(_dwconv(xpad_ref, cw_ref, cb_ref, rows, SSD_CONV, 0, SSD_XBC))
    xf_ref[...] = xact
    xact_ref[0] = xact.astype(BF16)

    def body(j, carry):
        i = qb - 1 - j
        r0 = pl.multiple_of(i * CHUNK, CHUNK)
        xs = xf_ref[pl.ds(r0, CHUNK), 0:SSD_WIDTH]
        bm = xf_ref[pl.ds(r0, CHUNK), SSD_WIDTH:SSD_WIDTH + SSD_BC].astype(BF16)
        dt, acs, tot = _ssd_decay(dt_ref[0, pl.ds(r0, CHUNK), :], dtb_ref, alog_ref)
        e_in = jnp.concatenate([jnp.exp(tot - acs) * dt, jnp.broadcast_to(jnp.exp(tot), (8, LANES))],
                               axis=0).astype(BF16)
        e_out = jnp.dot(e_in, rexp_ref[...], preferred_element_type=F32)
        xd = (xs * e_out[0:CHUNK]).astype(BF16)
        cdec = e_out[CHUNK:CHUNK + 1]
        for g in range(SSD_GROUPS):
            prev = st_ref[g]
            prev_ref[0, i, g] = prev.astype(BF16)
            s_new = lax.dot_general(bm[:, g * SSD_STATE:(g + 1) * SSD_STATE],
                                    xd[:, g * SSD_GW:(g + 1) * SSD_GW], (((0,), (0,)), ((), ())),
                                    preferred_element_type=F32)
            st_ref[g] = prev * cdec[:, g * SSD_GW:(g + 1) * SSD_GW] + s_new
        return carry

    lax.fori_loop(0, qb, body, 0, unroll=True)


def _ssd_fwd_kernel(xa_ref, dt_ref, prev_ref, z_ref, dtb_ref, alog_ref, rexpf_ref, rexpb_ref,
                    dsk_ref, ng_ref, o_ref, st_ref, *, qb):
    n = pl.program_id(1)

    @pl.when(n == 0)
    def _():
        st_ref[...] = jnp.zeros(st_ref.shape, F32)

    row = lax.broadcasted_iota(jnp.int32, (CHUNK, CHUNK), 0)
    col = lax.broadcasted_iota(jnp.int32, (CHUNK, CHUNK), 1)
    fwd_part = row > col
    diag = row == col
    lane_lo = lax.broadcasted_iota(jnp.int32, (1, LANES), 1) < SSD_HEAD_DIM
    zero_b = jnp.zeros((), BF16)

    def body(i, carry):
        r0 = pl.multiple_of(i * CHUNK, CHUNK)
        xs_b = xa_ref[0, pl.ds(r0, CHUNK), 0:SSD_WIDTH]
        xs = xs_b.astype(F32)
        bm = xa_ref[0, pl.ds(r0, CHUNK), SSD_WIDTH:SSD_WIDTH + SSD_BC]
        cm = xa_ref[0, pl.ds(r0, CHUNK), SSD_WIDTH + SSD_BC:SSD_XBC]
        dt, acs, tot = _ssd_decay(dt_ref[0, pl.ds(r0, CHUNK), :], dtb_ref, alog_ref)
        acs2 = acs * LOG2E
        rt = (acs2 - jnp.log2(dt)).T
        dsum_t = jnp.log2(dt + pltpu.roll(dt, LANES - SSD_HEADS, axis=1)).T
        eacs = jnp.exp(acs)
        ef_in = jnp.concatenate([jnp.exp(tot - acs) * dt, eacs,
                                 jnp.broadcast_to(jnp.exp(tot), (8, LANES))], axis=0).astype(BF16)
        ef = jnp.dot(ef_in, rexpf_ref[...], preferred_element_type=F32)
        eb = jnp.dot(eacs.astype(BF16), rexpb_ref[...], preferred_element_type=F32)
        xd = (xs * ef[0:CHUNK]).astype(BF16)
        eacs_f = ef[CHUNK:2 * CHUNK]
        cdec = ef[2 * CHUNK:2 * CHUNK + 1]
        ys = []
        for g in range(SSD_GROUPS):
            bg = bm[:, g * SSD_STATE:(g + 1) * SSD_STATE]
            cg = cm[:, g * SSD_STATE:(g + 1) * SSD_STATE]
            gs = slice(g * SSD_GW, (g + 1) * SSD_GW)
            cb = lax.dot_general(cg, bg, (((1,), (1,)), ((), ())), preferred_element_type=F32)
            prev = st_ref[g]
            y_off = (jnp.dot(cg, prev.astype(BF16), preferred_element_type=F32) * eacs_f[:, gs]
                     + jnp.dot(cg, prev_ref[0, i, g], preferred_element_type=F32) * eb[:, gs])
            s_new = lax.dot_general(bg, xd[:, gs], (((0,), (0,)), ((), ())),
                                    preferred_element_type=F32)
            st_ref[g] = prev * cdec[:, gs] + s_new
            for pr in range(SSD_HPG // 2):
                c0 = g * SSD_GW + pr * LANES
                xpair = xs_b[:, c0:c0 + LANES]
                xbd = jnp.concatenate([jnp.where(lane_lo, xpair, zero_b),
                                       jnp.where(lane_lo, zero_b, xpair)], axis=0)
                mats = []
                for e in range(2):
                    h = g * SSD_HPG + pr * 2 + e
                    hb = SSD_HEADS + h
                    sel = jnp.where(fwd_part, acs2[:, h:h + 1] - rt[h:h + 1, :],
                                    acs2[:, hb:hb + 1] - rt[hb:hb + 1, :])
                    sel = jnp.where(diag, dsum_t[h:h + 1, :], sel)
                    mats.append((cb * jnp.exp2(sel)).astype(BF16))
                yd = jnp.dot(jnp.concatenate(mats, axis=1), xbd, preferred_element_type=F32)
                ys.append(yd + y_off[:, pr * LANES:(pr + 1) * LANES])
        y = jnp.concatenate(ys, axis=-1) + xs * dsk_ref[...]
        y = y * _silu(z_ref[0, pl.ds(r0, CHUNK), :].astype(F32))
        ms = jnp.mean(y * y, axis=-1, keepdims=True)
        o_ref[0, pl.ds(r0, CHUNK), :] = (y * lax.rsqrt(ms + EPS) * ng_ref[...]).astype(BF16)
        return carry

    lax.fori_loop(0, qb, body, 0, unroll=True)


def _ssd(u3, dt3, cw, cb, dtb, alog, rexp_f, rexp_b, dsk, ng, *, qb):
    B, L, _ = u3.shape
    rows = qb * CHUNK
    assert L % rows == 0
    nt = L // rows
    nc = L // CHUNK
    hb = rows // HALO
    nhb = L // HALO
    state = pltpu.VMEM((SSD_GROUPS, SSD_STATE, SSD_GW), F32)
    const = lambda shape: pl.BlockSpec(shape, lambda b, n: (0,) * len(shape))
    rt = lambda n: nt - 1 - n

    xact, prevb = pl.pallas_call(
        functools.partial(_ssd_bwd_kernel, nt=nt, qb=qb),
        grid=(B, nt),
        in_specs=[
            pl.BlockSpec((1, rows, SSD_XBC), lambda b, n: (b, rt(n), C_XBC // SSD_XBC)),
            pl.BlockSpec((1, HALO, SSD_XBC),
                         lambda b, n: (b, jnp.maximum(rt(n) * hb - 1, 0), C_XBC // SSD_XBC)),
            pl.BlockSpec((1, HALO, SSD_XBC),
                         lambda b, n: (b, jnp.minimum((rt(n) + 1) * hb, nhb - 1), C_XBC // SSD_XBC)),
            pl.BlockSpec((1, rows, DT_PAD), lambda b, n: (b, rt(n), 0)),
            const((8, SSD_XBC)), const((1, SSD_XBC)), const((1, DT_PAD)), const((1, DT_PAD)),
            const((LANES, SSD_WIDTH)),
        ],
        out_specs=[pl.BlockSpec((1, rows, SSD_XBC), lambda b, n: (b, rt(n), 0)),
                   pl.BlockSpec((1, qb, SSD_GROUPS, SSD_STATE, SSD_GW),
                                lambda b, n: (b, rt(n), 0, 0, 0))],
        out_shape=[jax.ShapeDtypeStruct((B, L, SSD_XBC), BF16),
                   jax.ShapeDtypeStruct((B, nc, SSD_GROUPS, SSD_STATE, SSD_GW), BF16)],
        scratch_shapes=[pltpu.VMEM((rows + 2 * HALO, SSD_XBC), BF16),
                        pltpu.VMEM((rows, SSD_XBC), F32), state],
        compiler_params=_cparams(("parallel", "arbitrary")),
        name="ssd_bwd",
    )(u3, u3, u3, dt3, cw, cb, dtb, alog, rexp_b)

    return pl.pallas_call(
        functools.partial(_ssd_fwd_kernel, qb=qb),
        grid=(B, nt),
        in_specs=[
            pl.BlockSpec((1, rows, SSD_XBC), lambda b, n: (b, n, 0)),
            pl.BlockSpec((1, rows, DT_PAD), lambda b, n: (b, n, 0)),
            pl.BlockSpec((1, qb, SSD_GROUPS, SSD_STATE, SSD_GW), lambda b, n: (b, n, 0, 0, 0)),
            pl.BlockSpec((1, rows, SSD_WIDTH), lambda b, n: (b, n, C_Z // SSD_WIDTH)),
            const((1, DT_PAD)), const((1, DT_PAD)),
            const((LANES, SSD_WIDTH)), const((LANES, SSD_WIDTH)),
            const((1, SSD_WIDTH)), const((1, SSD_WIDTH)),
        ],
        out_specs=pl.BlockSpec((1, rows, SSD_WIDTH), lambda b, n: (b, n, 0)),
        out_shape=jax.ShapeDtypeStruct((B, L, SSD_WIDTH), BF16),
        scratch_shapes=[state],
        compiler_params=_cparams(("parallel", "arbitrary")),
        name="ssd_fwd",
    )(xact, dt3, prevb, u3, dtb, alog, rexp_f, rexp_b, dsk, ng)


def _ssd_expand_matrix(rev):
    m = np.zeros((LANES, SSD_WIDTH), np.float32)
    hoff = SSD_HEADS if rev else 0
    for h in range(SSD_HEADS):
        m[hoff + h, h * SSD_HEAD_DIM:(h + 1) * SSD_HEAD_DIM] = 1.0
    return jnp.asarray(m, BF16)


def _hy_filter_kernel(t_ref, bands_ref, w1t_ref, w1c_ref, w1s_ref, b1_ref, w2_ref, b2_ref, w3_ref,
                      b3_ref, w4_ref, fr_ref, absd_ref, x_ref, s_ref, *, L, tl):
    i = pl.program_id(0)
    pos = (lax.broadcasted_iota(jnp.int32, (tl, 1), 0) + i * tl).astype(F32)
    t = t_ref[...]
    ang = 2.0 * math.pi * pos * bands_ref[...] / L
    fr = fr_ref[...]
    dot = functools.partial(jnp.dot, preferred_element_type=F32, precision=HIGHEST)
    pre = t * w1t_ref[...] + dot(jnp.cos(ang), w1c_ref[...]) + dot(-jnp.sin(ang), w1s_ref[...])
    h = jnp.sin(fr * (pre + b1_ref[...]))
    h = jnp.sin(fr * (dot(h, w2_ref[...]) + b2_ref[...]))
    h = jnp.sin(fr * (dot(h, w3_ref[...]) + b3_ref[...]))
    w4 = w4_ref[...]
    h_hi, w_hi = h.astype(BF16), w4.astype(BF16)
    h_lo, w_lo = (h - h_hi.astype(F32)).astype(BF16), (w4 - w_hi.astype(F32)).astype(BF16)
    bdot = functools.partial(jnp.dot, preferred_element_type=F32)
    h = bdot(h_hi, w_hi) + bdot(h_hi, w_lo) + bdot(h_lo, w_hi)
    decay = jnp.exp(-t * absd_ref[...])
    hf = h[:, :HY_WIDTH] * decay
    hb = jnp.where(pos == 0.0, 0.0, h[:, HY_WIDTH:] * decay)
    _to_lane_blocks(s_ref, jnp.concatenate([hf + hb, hb - hf], axis=1))
    x_ref[...] = _split_even_odd(s_ref).astype(BF16)


def _hy_filter(L, w1, b1, w2, b2, w3, b3, w4, freq):
    tl = min(512, L)
    t = jnp.linspace(0.0, 1.0, L, dtype=F32)[:, None]
    bands = jnp.linspace(1e-4, HY_EMB_BANDS - 1, HY_EMB_BANDS, dtype=F32)[None, :]
    max_decay = math.log(HY_DECAY_TARGET) / HY_FAST_DECAY
    min_decay = math.log(HY_DECAY_TARGET) / HY_SLOW_DECAY
    absd = jnp.abs(jnp.linspace(min_decay, max_decay, HY_WIDTH, dtype=F32))[None, :]
    w1 = w1.astype(F32)
    full = lambda a: pl.BlockSpec(a.shape, lambda i: (0,) * a.ndim)
    args = [t, bands, w1[0:1], w1[1:1 + HY_EMB_BANDS], w1[1 + HY_EMB_BANDS:], b1[None], w2, b2[None],
            w3, b3[None], w4, freq[None], absd]
    in_specs = [pl.BlockSpec((tl, 1), lambda i: (i, 0))] + [full(a) for a in args[1:]]
    return pl.pallas_call(
        functools.partial(_hy_filter_kernel, L=L, tl=tl),
        grid=(L // tl,),
        in_specs=in_specs,
        out_specs=pl.BlockSpec((tl // 2, 4 * HY_WIDTH), lambda i: (i, 0)),
        out_shape=jax.ShapeDtypeStruct((L // 2, 4 * HY_WIDTH), BF16),
        scratch_shapes=[pltpu.VMEM((2 * HY_WIDTH // LANES, tl, LANES), F32)],
        compiler_params=_cparams(("parallel",)),
        name="hy_filter",
    )(*args)


def _dft_tables(L, hm):
    L2 = L // 2
    g = jnp.arange(L2, dtype=jnp.int32)
    s = jnp.arange(L2, dtype=jnp.int32)
    ph = ((2 * g + 1)[:, None] * s[None, :]) % (2 * L)
    ang = ph.astype(F32) * (math.pi / L)
    c = jnp.cos(ang).astype(BF16).reshape(L2 // hm, 1, hm, L2)
    sn = jnp.sin(ang).astype(BF16).reshape(L2 // hm, 1, hm, L2)
    a_fwd = jnp.concatenate([c, sn], axis=1).reshape(L, L2)
    a_inv = jnp.concatenate([c, -sn], axis=1).reshape(L, L2).T
    return a_fwd, a_inv


def _hy_kraw_kernel(a_ref, x_ref, k_ref):
    k_ref[...] = jnp.dot(a_ref[...], x_ref[...], preferred_element_type=F32)


def _hy_kraw(a_fwd, xf2, *, hm):
    L, L2 = a_fwd.shape
    tm = 2 * hm
    nx = xf2.shape[1]
    return pl.pallas_call(
        _hy_kraw_kernel,
        grid=(L // tm,),
        in_specs=[pl.BlockSpec((tm, L2), lambda i: (i, 0)),
                  pl.BlockSpec((L2, nx), lambda i: (0, 0))],
        out_specs=pl.BlockSpec((tm, nx), lambda i: (i, 0)),
        out_shape=jax.ShapeDtypeStruct((L, nx), F32),
        compiler_params=_cparams(("parallel",)),
        name="hy_kspec",
    )(a_fwd, xf2)


def _hy_filter_spectrum(pq, L, hm):
    L2 = L // 2
    W = HY_WIDTH
    pq = pq.reshape(L2 // hm, 2, hm, 4 * W)
    p = pq[:, 0].reshape(L2, 4 * W)
    q = pq[:, 1].reshape(L2, 4 * W)
    ea, eb, oa, ob = [(p[:, i * W:(i + 1) * W], -q[:, i * W:(i + 1) * W]) for i in range(4)]
    theta = (2.0 * jnp.arange(L2, dtype=F32) + 1.0) * (math.pi / (2 * L))
    w = (jnp.cos(theta)[:, None], -jnp.sin(theta)[:, None])
    cmul = lambda x, y: (x[0] * y[0] - x[1] * y[1], x[0] * y[1] + x[1] * y[0])
    woa, wob = cmul(w, oa), cmul(w, ob)
    k1 = (ea[0] + woa[0], -(eb[1] + wob[1]))
    k2 = (ea[0] - woa[0], eb[1] - wob[1])
    kp = (k1[0] + k2[0], k1[1] - k2[1])
    km = (k1[0] - k2[0], k1[1] + k2[1])
    wkm = cmul(w, km)
    vkm = cmul((w[0], -w[1]), km)
    filt = jnp.stack([kp[0], kp[1], wkm[0], wkm[1], vkm[0], vkm[1]]) * (1.0 / L)
    return jnp.transpose(filt.reshape(6, L2 // hm, hm, W), (1, 0, 2, 3))


def _to_lane_blocks(s_ref, x):
    for c in range(s_ref.shape[0]):
        s_ref[c] = x[:, c * LANES:(c + 1) * LANES]


def _split_even_odd(s_ref):
    k, rows, _ = s_ref.shape
    return jnp.concatenate([s_ref[c, pl.ds(p, rows // 2, stride=2), :]
                            for p in range(2) for c in range(k)], axis=1)


def _hy_pre_kernel(xm_ref, xp_ref, xn_ref, cw_ref, cb_ref, u_ref, xau_ref, xpad_ref, s_ref, *, nt, tl):
    i = pl.program_id(1)
    _load_padded(xpad_ref, xm_ref, xp_ref, xn_ref, i > 0, i < nt - 1, tl)
    xa = _dwconv(xpad_ref, cw_ref, cb_ref, tl, HY_CONV, 0, HY_WIDTH)
    xb = _dwconv(xpad_ref, cw_ref, cb_ref, tl, HY_CONV, HY_WIDTH, 2 * HY_WIDTH)
    v = _dwconv(xpad_ref, cw_ref, cb_ref, tl, HY_CONV, 2 * HY_WIDTH, 3 * HY_WIDTH)
    u = xb * v
    xau_ref[0, :, :HY_WIDTH] = xa.astype(BF16)
    xau_ref[0, :, HY_WIDTH:] = u.astype(BF16)
    _to_lane_blocks(s_ref, u)
    u_ref[0] = _split_even_odd(s_ref).astype(BF16)


def _hy_tile_specs(L, tl):
    hb = tl // HALO
    nhb = L // HALO
    w = 3 * HY_WIDTH
    return [pl.BlockSpec((1, tl, w), lambda b, i: (b, i, C_HY // w)),
            pl.BlockSpec((1, HALO, w), lambda b, i: (b, jnp.maximum(i * hb - 1, 0), C_HY // w)),
            pl.BlockSpec((1, HALO, w), lambda b, i: (b, jnp.minimum((i + 1) * hb, nhb - 1), C_HY // w))]


def _hy_pre(u3, cw, cb, *, tl):
    B, L, _ = u3.shape
    nt = L // tl
    w = 3 * HY_WIDTH
    return pl.pallas_call(
        functools.partial(_hy_pre_kernel, nt=nt, tl=tl),
        grid=(B, nt),
        in_specs=_hy_tile_specs(L, tl) + [pl.BlockSpec((8, w), lambda b, i: (0, 0)),
                                          pl.BlockSpec((1, w), lambda b, i: (0, 0))],
        out_specs=[pl.BlockSpec((1, tl // 2, 2 * HY_WIDTH), lambda b, i: (b, i, 0)),
                   pl.BlockSpec((1, tl, 2 * HY_WIDTH), lambda b, i: (b, i, 0))],
        out_shape=[jax.ShapeDtypeStruct((B, L // 2, 2 * HY_WIDTH), BF16),
                   jax.ShapeDtypeStruct((B, L, 2 * HY_WIDTH), BF16)],
        scratch_shapes=[pltpu.VMEM((tl + 2 * HALO, w), BF16),
                        pltpu.VMEM((HY_WIDTH // LANES, tl, LANES), F32)],
        compiler_params=_cparams(("parallel", "parallel")),
        name="hy_pre",
    )(u3, u3, u3, cw, cb)


def _hy_fwd_kernel(a_ref, u_ref, f_ref, y_ref, *, hm):
    W = HY_WIDTH
    pq = jnp.dot(a_ref[...], u_ref[0], preferred_element_type=F32)
    pe, po = pq[0:hm, :W], pq[0:hm, W:]
    qe, qo = pq[hm:, :W], pq[hm:, W:]
    kpr, kpi, wr, wi, vr, vi = [f_ref[0, i] for i in range(6)]
    y_ref[0, 0:hm, :W] = (pe * kpr + qe * kpi + po * wr + qo * wi).astype(BF16)
    y_ref[0, hm:, :W] = (pe * kpi - qe * kpr + po * wi - qo * wr).astype(BF16)
    y_ref[0, 0:hm, W:] = (pe * vr + qe * vi + po * kpr + qo * kpi).astype(BF16)
    y_ref[0, hm:, W:] = (pe * vi - qe * vr + po * kpi - qo * kpr).astype(BF16)


def _hy_fwd(a_fwd, ueo, filt, *, hm):
    B, L2, W2 = ueo.shape
    L = 2 * L2
    tm = 2 * hm
    return pl.pallas_call(
        functools.partial(_hy_fwd_kernel, hm=hm),
        grid=(L // tm, B),
        in_specs=[pl.BlockSpec((tm, L2), lambda i, b: (i, 0)),
                  pl.BlockSpec((1, L2, W2), lambda i, b: (b, 0, 0)),
                  pl.BlockSpec((1, 6, hm, HY_WIDTH), lambda i, b: (i, 0, 0, 0))],
        out_specs=pl.BlockSpec((1, tm, W2), lambda i, b: (b, i, 0)),
        out_shape=jax.ShapeDtypeStruct((B, L, W2), BF16),
        compiler_params=_cparams(("parallel", "parallel")),
        name="hy_fwd",
    )(a_fwd, ueo, filt)


def _hy_inv_kernel(a_ref, y_ref, o_ref):
    o_ref[0] = jnp.dot(a_ref[...], y_ref[0], preferred_element_type=F32)


def _hy_inv(a_inv, yspec, *, tm):
    B, L, W2 = yspec.shape
    L2 = L // 2
    return pl.pallas_call(
        _hy_inv_kernel,
        grid=(L2 // tm, B),
        in_specs=[pl.BlockSpec((tm, L), lambda i, b: (i, 0)),
                  pl.BlockSpec((1, L, W2), lambda i, b: (b, 0, 0))],
        out_specs=pl.BlockSpec((1, tm, W2), lambda i, b: (b, i, 0)),
        out_shape=jax.ShapeDtypeStruct((B, L2, W2), F32),
        compiler_params=_cparams(("parallel", "parallel")),
        name="hy_inv",
    )(a_inv, yspec)


def _hy_post_kernel(c_ref, xau_ref, g_ref, d_ref, ng_ref, o_ref, s_ref, *, tl):
    nblk = HY_WIDTH // LANES
    for p in range(2):
        for c in range(nblk):
            s_ref[c, pl.ds(p, tl // 2, stride=2), :] = c_ref[0, :, (p * nblk + c) * LANES:
                                                             (p * nblk + c + 1) * LANES]
    conv = jnp.concatenate([s_ref[c] for c in range(nblk)], axis=1)
    xa = xau_ref[0, :, :HY_WIDTH].astype(F32)
    u = xau_ref[0, :, HY_WIDTH:].astype(F32)
    y = xa * (conv + u * d_ref[...])
    y = y * _silu(g_ref[0].astype(F32))
    ms = jnp.mean(y * y, axis=-1, keepdims=True)
    o_ref[0] = (y * lax.rsqrt(ms + EPS) * ng_ref[...]).astype(BF16)


def _hy_post(conv, xau, u3, d, ng, *, tl):
    B, L, _ = u3.shape
    return pl.pallas_call(
        functools.partial(_hy_post_kernel, tl=tl),
        grid=(B, L // tl),
        in_specs=[pl.BlockSpec((1, tl // 2, 2 * HY_WIDTH), lambda b, i: (b, i, 0)),
                  pl.BlockSpec((1, tl, 2 * HY_WIDTH), lambda b, i: (b, i, 0)),
                  pl.BlockSpec((1, tl, HY_WIDTH), lambda b, i: (b, i, C_GHY // HY_WIDTH)),
                  pl.BlockSpec((1, HY_WIDTH), lambda b, i: (0, 0)),
                  pl.BlockSpec((1, HY_WIDTH), lambda b, i: (0, 0))],
        out_specs=pl.BlockSpec((1, tl, HY_WIDTH), lambda b, i: (b, i, 0)),
        out_shape=jax.ShapeDtypeStruct((B, L, HY_WIDTH), BF16),
        scratch_shapes=[pltpu.VMEM((HY_WIDTH // LANES, tl, LANES), F32)],
        compiler_params=_cparams(("parallel", "parallel")),
        name="hy_post",
    )(conv, xau, u3, d, ng)


def _pad_rows(a, rows):
    return jnp.concatenate([a, jnp.zeros((rows - a.shape[0],) + a.shape[1:], a.dtype)], axis=0)


def _pad_cols(a, cols):
    return jnp.concatenate([a, jnp.zeros(a.shape[:-1] + (cols - a.shape[-1],), a.dtype)], axis=-1)


def _static_take(a, idx, axis):
    idx = np.asarray(idx)
    cuts = np.flatnonzero(np.diff(idx) != 1) + 1
    starts = np.concatenate([[0], cuts])
    ends = np.concatenate([cuts, [len(idx)]])
    parts = [lax.slice_in_dim(a, int(idx[s]), int(idx[e - 1]) + 1, axis=axis)
             for s, e in zip(starts, ends)]
    return jnp.concatenate(parts, axis=axis)


def _tile(n, pref):
    t = min(pref, n)
    assert n % t == 0
    return t


def _layer(x3, p, tables, *, final, final_g):
    B, L, _ = x3.shape
    T = B * L
    x2 = x3.reshape(T, D_MODEL)
    u, dt = _inproj(x2, p["norm_g"], p["w_in"], tm=_tile(T, 512), tn=512)
    u3 = u.reshape(B, L, N_IN)
    dt3 = dt.reshape(B, L, DT_PAD)

    ys = _ssd(u3, dt3, p["ssd_cw"], p["ssd_cb"], p["ssd_dtb"], p["ssd_alog"], p["rexp_f"],
              p["rexp_b"], p["ssd_dskip"], p["ssd_ng"], qb=_tile(L // CHUNK, 4))

    ya = _attention(u3, p["att_bias"], p["att_sink"], p["att_ng"], qb=_tile(L // ATT_BLOCK, 8))

    a_fwd, a_inv, filt, hm = tables
    tl = _tile(L, 1024)
    ueo, xau = _hy_pre(u3, p["hy_cw"], p["hy_cb"], tl=tl)
    yspec = _hy_fwd(a_fwd, ueo, filt, hm=hm)
    conv = _hy_inv(a_inv, yspec, tm=_tile(L // 2, 512))
    yh = _hy_post(conv, xau, u3, p["hy_d"], p["hy_ng"], tl=tl)

    out = _outproj(x2, ys.reshape(T, SSD_WIDTH), ya.reshape(T, ATT_WIDTH), yh.reshape(T, HY_WIDTH),
                   p["w_out"], final_g, tm=_tile(T, 512), final=final)
    return out.reshape(B, L, D_MODEL)


def kernel(x_prompt, x_sample, rel_bias, norm_g, w_in, ssd_conv_w, ssd_conv_b, ssd_dt_bias, ssd_a_log, ssd_d, ssd_norm_g, att_sink, att_norm_g, hy_conv_w, hy_conv_b, hy_w1, hy_b1, hy_w2, hy_b2, hy_w3, hy_b3, hy_w4, hy_freq, hy_d, hy_norm_g, w_out, final_norm_g):
    depth = w_in.shape[0]
    perm, col_scale = _in_perm()
    n_real = C_DT + 2 * SSD_HEADS
    scale_old = np.ones((IN_COLS,), np.float32)
    scale_old[perm[:n_real]] = col_scale[:n_real]
    w_in_p = _pad_cols(_static_take((w_in * jnp.asarray(scale_old)).astype(BF16), perm[:n_real], 2),
                       N_IN)
    att_perm = _att_col_perm()
    out_rows = np.concatenate([np.arange(SSD_WIDTH), SSD_WIDTH + att_perm,
                               np.arange(SSD_WIDTH + ATT_WIDTH, D_MODEL)])
    w_out_b = _static_take(w_out, out_rows, 1).astype(BF16)
    att_bias = _attn_bias(rel_bias)
    rexp_f, rexp_b = _ssd_expand_matrix(False), _ssd_expand_matrix(True)
    final_g = final_norm_g.astype(F32)[None, :]

    layers = []
    for i in range(depth):
        layers.append(dict(
            norm_g=norm_g[i].astype(F32)[None, :],
            w_in=w_in_p[i],
            ssd_cw=_pad_rows(ssd_conv_w[i].astype(F32), 8),
            ssd_cb=ssd_conv_b[i].astype(F32)[None, :],
            ssd_dtb=_pad_cols(ssd_dt_bias[i].astype(F32).reshape(1, 2 * SSD_HEADS), DT_PAD),
            ssd_alog=_pad_cols(ssd_a_log[i].astype(F32).reshape(1, 2 * SSD_HEADS), DT_PAD),
            ssd_dskip=jnp.repeat(ssd_d[i].astype(F32), SSD_HEAD_DIM)[None, :],
            ssd_ng=ssd_norm_g[i].astype(F32)[None, :],
            rexp_f=rexp_f, rexp_b=rexp_b,
            att_bias=att_bias,
            att_sink=jnp.broadcast_to(att_sink[i].astype(F32)[:, None] * LOG2E, (ATT_HEADS, LANES)),
            att_ng=att_norm_g[i].astype(F32)[att_perm][None, :],
            hy_cw=_pad_rows(hy_conv_w[i].astype(F32), 8),
            hy_cb=hy_conv_b[i].astype(F32)[None, :],
            hy_d=hy_d[i].astype(F32)[None, :],
            hy_ng=hy_norm_g[i].astype(F32)[None, :],
            w_out=w_out_b[i],
        ))

    def trunk(x):
        L = x.shape[1]
        hm = min(256, L // 4)
        a_fwd, a_inv = _dft_tables(L, hm)
        for i in range(depth):
            xf = _hy_filter(L, hy_w1[i], hy_b1[i].astype(F32), hy_w2[i].astype(F32),
                            hy_b2[i].astype(F32), hy_w3[i].astype(F32), hy_b3[i].astype(F32),
                            hy_w4[i].astype(F32), hy_freq[i].astype(F32))
            pq = _hy_kraw(a_fwd, xf, hm=hm)
            filt = _hy_filter_spectrum(pq, L, hm)
            x = _layer(x, layers[i], (a_fwd, a_inv, filt, hm), final=(i == depth - 1),
                       final_g=final_g)
        return x

    return (trunk(x_prompt), trunk(x_sample))
```

```python
import functools
import math

import jax
import jax.numpy as jnp
import numpy as np
from jax import lax
from jax.experimental import pallas as pl
from jax.experimental.pallas import tpu as pltpu

F32 = jnp.float32
BF16 = jnp.bfloat16
HIGHEST = lax.Precision.HIGHEST

D_MODEL = 2048
SSD_WIDTH = 1024
ATT_WIDTH = 512
HY_WIDTH = 512
SSD_HEAD_DIM = 64
SSD_HEADS = 16
SSD_GROUPS = 2
SSD_STATE = 128
SSD_CONV = 5
CHUNK = 128
SSD_XBC = SSD_WIDTH + 2 * SSD_GROUPS * SSD_STATE
ATT_HEAD_DIM = 64
ATT_HEADS = 8
ATT_KV_HEADS = 2
ATT_REP = ATT_HEADS // ATT_KV_HEADS
ATT_WINDOW = 128
ATT_BLOCK = 128
REL_BUCKETS = 32
REL_MAX_DIST = 128
HY_CONV = 3
HY_EMB_BANDS = 16
HY_FF = 64
HY_FAST_DECAY = 0.3
HY_SLOW_DECAY = 1.5
HY_DECAY_TARGET = 1e-2
EPS = 1e-6
NEG_BIG = -1e30

LANES = 128
BF16_SUBLANES = 16
VMEM_LIMIT = 56 * 1024 * 1024

C_XBC = 0
C_HY = 1536
C_Z = 3072
C_Q = 4096
C_GATT = 4608
C_GHY = 5120
C_K = 5632
C_V = 5760
C_DT = 5888
DT_PAD = 128
N_IN = C_DT + DT_PAD

_OLD_SIZES = [SSD_WIDTH, SSD_XBC, 2 * SSD_HEADS, ATT_WIDTH, ATT_KV_HEADS * ATT_HEAD_DIM,
              ATT_KV_HEADS * ATT_HEAD_DIM, ATT_WIDTH, 3 * HY_WIDTH, HY_WIDTH]
_OLD_OFF = np.concatenate([[0], np.cumsum(_OLD_SIZES)])
IN_COLS = int(_OLD_OFF[-1])


LOG2E = math.log2(math.e)
Q_SCALE = ATT_HEAD_DIM ** -0.5 * LOG2E


def _att_col_perm():
    order = [h for j in range(ATT_REP) for h in (j, ATT_REP + j)]
    return np.concatenate([np.arange(h * ATT_HEAD_DIM, (h + 1) * ATT_HEAD_DIM) for h in order])


def _in_perm():
    perm = np.full((N_IN,), IN_COLS, np.int32)
    scale = np.ones((N_IN,), np.float32)
    o = {n: int(_OLD_OFF[i]) for i, n in enumerate(
        ["z", "xbc", "dt", "q", "k", "v", "gatt", "hy", "ghy"])}
    def put(new, old, width):
        perm[new:new + width] = np.arange(old, old + width)
    put(C_XBC, o["xbc"], SSD_XBC)
    put(C_HY, o["hy"], 3 * HY_WIDTH)
    put(C_Z, o["z"], SSD_WIDTH)
    perm[C_Q:C_Q + ATT_WIDTH] = o["q"] + _att_col_perm()
    scale[C_Q:C_Q + ATT_WIDTH] = Q_SCALE
    perm[C_GATT:C_GATT + ATT_WIDTH] = o["gatt"] + _att_col_perm()
    put(C_GHY, o["ghy"], HY_WIDTH)
    put(C_K, o["k"], 128)
    put(C_V, o["v"], 128)
    put(C_DT, o["dt"], 2 * SSD_HEADS)
    return perm, scale


def _cparams(sem):
    return pltpu.CompilerParams(dimension_semantics=sem, vmem_limit_bytes=VMEM_LIMIT)


def _silu(x):
    return x * (1.0 / (1.0 + jnp.exp(-x)))


def _softplus(x):
    return jnp.maximum(x, 0.0) + jnp.log1p(jnp.exp(-jnp.abs(x)))


def _inproj_kernel(x_ref, g_ref, w_ref, u_ref, dt_ref, h_ref, *, rows, tn):
    for r in range(x_ref.shape[0] // rows):
        sl = slice(r * rows, (r + 1) * rows)
        x = x_ref[sl, :]
        ms = jnp.mean(x * x, axis=-1, keepdims=True)
        h_ref[sl, :] = (x * lax.rsqrt(ms + EPS) * g_ref[...]).astype(BF16)
        h = h_ref[sl, :]
        for c0 in range(0, N_IN, tn):
            c1 = min(c0 + tn, N_IN)
            acc = jnp.dot(h, w_ref[:, c0:c1], preferred_element_type=F32)
            u_ref[sl, c0:c1] = acc.astype(BF16)
            if c0 <= C_DT < c1:
                dt_ref[sl, :] = acc[:, C_DT - c0:C_DT - c0 + DT_PAD]


def _inproj(x2, g, w, *, tm, tn):
    T = x2.shape[0]
    assert T % tm == 0
    return pl.pallas_call(
        functools.partial(_inproj_kernel, rows=min(256, tm), tn=tn),
        grid=(T // tm,),
        in_specs=[pl.BlockSpec((tm, D_MODEL), lambda i: (i, 0)),
                  pl.BlockSpec((1, D_MODEL), lambda i: (0, 0)),
                  pl.BlockSpec((D_MODEL, N_IN), lambda i: (0, 0), pipeline_mode=pl.Buffered(1))],
        out_specs=[pl.BlockSpec((tm, N_IN), lambda i: (i, 0)),
                   pl.BlockSpec((tm, DT_PAD), lambda i: (i, 0))],
        out_shape=[jax.ShapeDtypeStruct((T, N_IN), BF16),
                   jax.ShapeDtypeStruct((T, DT_PAD), F32)],
        scratch_shapes=[pltpu.VMEM((tm, D_MODEL), BF16)],
        compiler_params=_cparams(("parallel",)),
        name="inproj",
    )(x2, g, w)


def _outproj_kernel(x_ref, ys_ref, ya_ref, yh_ref, w_ref, fg_ref, o_ref, *, final):
    acc = x_ref[...]
    acc = acc + jnp.dot(ys_ref[...], w_ref[0:SSD_WIDTH, :], preferred_element_type=F32)
    acc = acc + jnp.dot(ya_ref[...], w_ref[SSD_WIDTH:SSD_WIDTH + ATT_WIDTH, :],
                        preferred_element_type=F32)
    acc = acc + jnp.dot(yh_ref[...], w_ref[SSD_WIDTH + ATT_WIDTH:, :], preferred_element_type=F32)
    if final:
        ms = jnp.mean(acc * acc, axis=-1, keepdims=True)
        acc = acc * lax.rsqrt(ms + EPS) * fg_ref[...]
    o_ref[...] = acc


def _outproj(x2, ys, ya, yh, w, fg, *, tm, final):
    T = x2.shape[0]
    assert T % tm == 0
    return pl.pallas_call(
        functools.partial(_outproj_kernel, final=final),
        grid=(T // tm,),
        in_specs=[pl.BlockSpec((tm, D_MODEL), lambda i: (i, 0)),
                  pl.BlockSpec((tm, SSD_WIDTH), lambda i: (i, 0)),
                  pl.BlockSpec((tm, ATT_WIDTH), lambda i: (i, 0)),
                  pl.BlockSpec((tm, HY_WIDTH), lambda i: (i, 0)),
                  pl.BlockSpec((D_MODEL, D_MODEL), lambda i: (0, 0)),
                  pl.BlockSpec((1, D_MODEL), lambda i: (0, 0))],
        out_specs=pl.BlockSpec((tm, D_MODEL), lambda i: (i, 0)),
        out_shape=jax.ShapeDtypeStruct((T, D_MODEL), F32),
        compiler_params=_cparams(("parallel",)),
        name="outproj",
    )(x2, ys, ya, yh, w, fg)


def _t5_buckets(rel):
    nb = REL_BUCKETS // 2
    max_exact = nb // 2
    ret = (rel > 0).astype(np.int32) * nb
    n = np.abs(rel)
    large = max_exact + (np.log(np.maximum(n, 1) / max_exact) / math.log(REL_MAX_DIST / max_exact)
                         * (nb - max_exact)).astype(np.int32)
    large = np.minimum(large, nb - 1)
    return ret + np.where(n < max_exact, n, large)


def _attn_bias(rel_bias):
    qi = np.arange(ATT_BLOCK)[:, None]
    kj = np.arange(3 * ATT_BLOCK)[None, :]
    rel = kj - ATT_BLOCK - qi
    onehot = (_t5_buckets(rel)[None] == np.arange(REL_BUCKETS)[:, None, None]).astype(np.float32)
    bias = jnp.einsum("bqk,bh->hqk", jnp.asarray(onehot), rel_bias.astype(F32),
                      precision=HIGHEST) * LOG2E
    window = np.abs(rel) <= ATT_WINDOW
    variants = []
    for last in (False, True):
        for first in (False, True):
            ok = window & ~(first & (kj < ATT_BLOCK)) & ~(last & (kj >= 2 * ATT_BLOCK))
            variants.append(jnp.where(ok[None], bias, NEG_BIG))
    return jnp.transpose(jnp.stack(variants), (0, 1, 3, 2))


def _attn_kernel(q_ref, kp_ref, kc_ref, kn_ref, vp_ref, vc_ref, vn_ref, g_ref, bias_ref, sink_ref,
                 ng_ref, o_ref, klo_ref, khi_ref, vt_ref, *, nt, qb):
    n = pl.program_id(1)
    lo = lax.broadcasted_iota(jnp.int32, (1, LANES), 1) < ATT_HEAD_DIM
    zero = jnp.zeros((), BF16)
    kext = jnp.concatenate([kp_ref[0], kc_ref[0], kn_ref[0]], axis=0)
    klo_ref[...] = jnp.where(lo, kext, zero)
    khi_ref[...] = jnp.where(lo, zero, kext)
    for t, ref, cnt in ((0, vp_ref, 1), (1, vc_ref, qb), (qb + 1, vn_ref, 1)):
        for i in range(cnt):
            blk = ref[0, i * ATT_BLOCK:(i + 1) * ATT_BLOCK, :]
            vt_ref[t + i] = blk.astype(F32).T.astype(BF16)
    row_lo = lax.broadcasted_iota(jnp.int32, (LANES, ATT_BLOCK), 0) < ATT_HEAD_DIM
    nk = 3 * ATT_BLOCK

    def body(i, carry):
        r0 = pl.multiple_of(i * ATT_BLOCK, ATT_BLOCK)
        q = q_ref[0, pl.ds(r0, ATT_BLOCK), :]
        kst = jnp.concatenate([klo_ref[pl.ds(r0, nk), :], khi_ref[pl.ds(r0, nk), :]], axis=0)
        vt = jnp.concatenate([vt_ref[i], vt_ref[i + 1], vt_ref[i + 2]], axis=1)
        first = jnp.logical_and(n == 0, i == 0)
        last = jnp.logical_and(n == nt - 1, i == qb - 1)
        variant = first.astype(jnp.int32) + 2 * last.astype(jnp.int32)
        outs = []
        for j in range(ATT_REP):
            qp = q[:, j * LANES:(j + 1) * LANES]
            st = lax.dot_general(kst, qp, (((1,), (1,)), ((), ())), preferred_element_type=F32)
            halves = []
            for e, h in enumerate((j, ATT_REP + j)):
                s = st[e * nk:(e + 1) * nk] + bias_ref[variant, h]
                sk = sink_ref[h:h + 1, :]
                m = jnp.maximum(jnp.max(s, axis=0, keepdims=True), sk)
                p = jnp.exp2(s - m)
                den = jnp.sum(p, axis=0, keepdims=True) + jnp.exp2(sk - m)
                ot = jnp.dot(vt, p.astype(BF16), preferred_element_type=F32)
                halves.append(ot * (1.0 / den))
            outs.append(jnp.where(row_lo, halves[0], halves[1]).T)
        o = jnp.concatenate(outs, axis=-1)
        y = o * _silu(g_ref[0, pl.ds(r0, ATT_BLOCK), :].astype(F32))
        ms = jnp.mean(y * y, axis=-1, keepdims=True)
        o_ref[0, pl.ds(r0, ATT_BLOCK), :] = (y * lax.rsqrt(ms + EPS) * ng_ref[...]).astype(BF16)
        return carry

    lax.fori_loop(0, qb, body, 0, unroll=True)


def _attention(u3, bias, sink, ng, *, qb):
    B, L, _ = u3.shape
    tq = qb * ATT_BLOCK
    assert L % tq == 0
    nt = L // tq
    nb = L // ATT_BLOCK
    kcol, vcol = C_K // 128, C_V // 128
    def kv_specs(colblk):
        return [pl.BlockSpec((1, ATT_BLOCK, 128), lambda b, n: (b, jnp.maximum(n * qb - 1, 0), colblk)),
                pl.BlockSpec((1, tq, 128), lambda b, n: (b, n, colblk)),
                pl.BlockSpec((1, ATT_BLOCK, 128),
                             lambda b, n: (b, jnp.minimum((n + 1) * qb, nb - 1), colblk))]
    return pl.pallas_call(
        functools.partial(_attn_kernel, nt=nt, qb=qb),
        grid=(B, nt),
        in_specs=[pl.BlockSpec((1, tq, ATT_WIDTH), lambda b, n: (b, n, C_Q // ATT_WIDTH))]
                 + kv_specs(kcol) + kv_specs(vcol)
                 + [pl.BlockSpec((1, tq, ATT_WIDTH), lambda b, n: (b, n, C_GATT // ATT_WIDTH)),
                    pl.BlockSpec((4, ATT_HEADS, 3 * ATT_BLOCK, ATT_BLOCK), lambda b, n: (0, 0, 0, 0)),
                    pl.BlockSpec((ATT_HEADS, LANES), lambda b, n: (0, 0)),
                    pl.BlockSpec((1, ATT_WIDTH), lambda b, n: (0, 0))],
        out_specs=pl.BlockSpec((1, tq, ATT_WIDTH), lambda b, n: (b, n, 0)),
        out_shape=jax.ShapeDtypeStruct((B, L, ATT_WIDTH), BF16),
        scratch_shapes=[pltpu.VMEM(((qb + 2) * ATT_BLOCK, LANES), BF16),
                        pltpu.VMEM(((qb + 2) * ATT_BLOCK, LANES), BF16),
                        pltpu.VMEM((qb + 2, LANES, ATT_BLOCK), BF16)],
        compiler_params=_cparams(("parallel", "parallel")),
        name="attn",
    )(u3, u3, u3, u3, u3, u3, u3, u3, bias, sink, ng)


HALO = BF16_SUBLANES


CONV_BLK = 128


def _load_padded(xpad_ref, xm_ref, xp_ref, xn_ref, has_prev, has_next, rows):
    zero = jnp.zeros((), BF16)
    xpad_ref[pl.ds(0, HALO), :] = jnp.where(has_prev, xp_ref[0], zero)
    xpad_ref[pl.ds(HALO, rows), :] = xm_ref[0]
    xpad_ref[pl.ds(HALO + rows, HALO), :] = jnp.where(has_next, xn_ref[0], zero)


def _shift_matrix(width):
    offs = [k - width // 2 for k in range(width) if k != width // 2]
    r = lax.broadcasted_iota(jnp.int32, (CONV_BLK, CONV_BLK + 2 * HALO), 0)
    c = lax.broadcasted_iota(jnp.int32, (CONV_BLK, CONV_BLK + 2 * HALO), 1)
    return jnp.concatenate([(c == r + HALO + d).astype(BF16) for d in offs], axis=0)


def _dwconv_block(xpad_ref, shifts, w_ref, b_ref, j, width, c0, c1):
    win = xpad_ref[j * CONV_BLK:(j + 1) * CONV_BLK + 2 * HALO, c0:c1]
    moved = jnp.dot(shifts, win, preferred_element_type=F32)
    acc = win[HALO:HALO + CONV_BLK].astype(F32) * w_ref[width // 2:width // 2 + 1, c0:c1]
    i = 0
    for k in range(width):
        if k == width // 2:
            continue
        acc = acc + moved[i * CONV_BLK:(i + 1) * CONV_BLK] * w_ref[k:k + 1, c0:c1]
        i += 1
    return acc + b_ref[:, c0:c1]


def _dwconv(xpad_ref, w_ref, b_ref, rows, width, c0, c1):
    shifts = _shift_matrix(width)
    return jnp.concatenate([_dwconv_block(xpad_ref, shifts, w_ref, b_ref, j, width, c0, c1)
                            for j in range(rows // CONV_BLK)], axis=0)


SSD_HPG = SSD_HEADS // SSD_GROUPS
SSD_GW = SSD_HPG * SSD_HEAD_DIM
SSD_BC = SSD_GROUPS * SSD_STATE


def _ssd_decay(dt_raw, dtb_ref, alog_ref):
    dt = _softplus(dt_raw + dtb_ref[...])
    dta = dt * (-jnp.exp(alog_ref[...]))
    row = lax.broadcasted_iota(jnp.int32, (CHUNK, CHUNK), 0)
    col = lax.broadcasted_iota(jnp.int32, (CHUNK, CHUNK), 1)
    tril = (row >= col).astype(BF16)
    hi = dta.astype(BF16)
    r1 = dta - hi.astype(F32)
    mid = r1.astype(BF16)
    lo = (r1 - mid.astype(F32)).astype(BF16)
    pre = (jnp.dot(tril, hi, preferred_element_type=F32) + jnp.dot(tril, mid, preferred_element_type=F32)
           + jnp.dot(tril, lo, preferred_element_type=F32))
    tot = pre[CHUNK - 1:CHUNK, :]
    is_bwd = lax.broadcasted_iota(jnp.int32, (1, LANES), 1) >= SSD_HEADS
    acs = jnp.where(is_bwd, tot - pre + dta, pre)
    return dt, acs, tot


def _ssd_bwd_kernel(xm_ref, xp_ref, xn_ref, dt_ref, cw_ref, cb_ref, dtb_ref, alog_ref, rexp_ref,
                    xact_ref, prev_ref, xpad_ref, xf_ref, st_ref, *, nt, qb):
    n = pl.program_id(1)
    tile = nt - 1 - n
    rows = qb * CHUNK

    @pl.when(n == 0)
    def _():
        st_ref[...] = jnp.zeros(st_ref.shape, F32)

    _load_padded(xpad_ref, xm_ref, xp_ref, xn_ref, tile > 0, tile < nt - 1, rows)
    xact = _silu(_dwconv(xpad_ref, cw_ref, cb_ref, rows, SSD_CONV, 0, SSD_XBC))
    xf_ref[...] = xact
    xact_ref[0] = xact.astype(BF16)

    def body(j, carry):
        i = qb - 1 - j
        r0 = pl.multiple_of(i * CHUNK, CHUNK)
        xs = xf_ref[pl.ds(r0, CHUNK), 0:SSD_WIDTH]
        bm = xf_ref[pl.ds(r0, CHUNK), SSD_WIDTH:SSD_WIDTH + SSD_BC].astype(BF16)
        dt, acs, tot = _ssd_decay(dt_ref[0, pl.ds(r0, CHUNK), :], dtb_ref, alog_ref)
        e_in = jnp.concatenate([jnp.exp(tot - acs) * dt, jnp.broadcast_to(jnp.exp(tot), (8, LANES))],
                               axis=0).astype(BF16)
        e_out = jnp.dot(e_in, rexp_ref[...], preferred_element_type=F32)
        xd = (xs * e_out[0:CHUNK]).astype(BF16)
        cdec = e_out[CHUNK:CHUNK + 1]
        for g in range(SSD_GROUPS):
            prev = st_ref[g]
            prev_ref[0, i, g] = prev.astype(BF16)
            s_new = lax.dot_general(bm[:, g * SSD_STATE:(g + 1) * SSD_STATE],
                                    xd[:, g * SSD_GW:(g + 1) * SSD_GW], (((0,), (0,)), ((), ())),
                                    preferred_element_type=F32)
            st_ref[g] = prev * cdec[:, g * SSD_GW:(g + 1) * SSD_GW] + s_new
        return carry

    lax.fori_loop(0, qb, body, 0, unroll=True)


def _ssd_fwd_kernel(xa_ref, dt_ref, prev_ref, z_ref, dtb_ref, alog_ref, rexpf_ref, rexpb_ref,
                    dsk_ref, ng_ref, o_ref, st_ref, *, qb):
    n = pl.program_id(1)

    @pl.when(n == 0)
    def _():
        st_ref[...] = jnp.zeros(st_ref.shape, F32)

    row = lax.broadcasted_iota(jnp.int32, (CHUNK, CHUNK), 0)
    col = lax.broadcasted_iota(jnp.int32, (CHUNK, CHUNK), 1)
    fwd_part = row > col
    diag = row == col
    lane_lo = lax.broadcasted_iota(jnp.int32, (1, LANES), 1) < SSD_HEAD_DIM
    zero_b = jnp.zeros((), BF16)

    def body(i, carry):
        r0 = pl.multiple_of(i * CHUNK, CHUNK)
        xs_b = xa_ref[0, pl.ds(r0, CHUNK), 0:SSD_WIDTH]
        xs = xs_b.astype(F32)
        bm = xa_ref[0, pl.ds(r0, CHUNK), SSD_WIDTH:SSD_WIDTH + SSD_BC]
        cm = xa_ref[0, pl.ds(r0, CHUNK), SSD_WIDTH + SSD_BC:SSD_XBC]
        dt, acs, tot = _ssd_decay(dt_ref[0, pl.ds(r0, CHUNK), :], dtb_ref, alog_ref)
        acs2 = acs * LOG2E
        rt = (acs2 - jnp.log2(dt)).T
        dsum_t = jnp.log2(dt + pltpu.roll(dt, LANES - SSD_HEADS, axis=1)).T
        eacs = jnp.exp(acs)
        ef_in = jnp.concatenate([jnp.exp(tot - acs) * dt, eacs,
                                 jnp.broadcast_to(jnp.exp(tot), (8, LANES))], axis=0).astype(BF16)
        ef = jnp.dot(ef_in, rexpf_ref[...], preferred_element_type=F32)
        eb = jnp.dot(eacs.astype(BF16), rexpb_ref[...], preferred_element_type=F32)
        xd = (xs * ef[0:CHUNK]).astype(BF16)
        eacs_f = ef[CHUNK:2 * CHUNK]
        cdec = ef[2 * CHUNK:2 * CHUNK + 1]
        ys = []
        for g in range(SSD_GROUPS):
            bg = bm[:, g * SSD_STATE:(g + 1) * SSD_STATE]
            cg = cm[:, g * SSD_STATE:(g + 1) * SSD_STATE]
            gs = slice(g * SSD_GW, (g + 1) * SSD_GW)
            cb = lax.dot_general(cg, bg, (((1,), (1,)), ((), ())), preferred_element_type=F32)
            prev = st_ref[g]
            y_off = (jnp.dot(cg, prev.astype(BF16), preferred_element_type=F32) * eacs_f[:, gs]
                     + jnp.dot(cg, prev_ref[0, i, g], preferred_element_type=F32) * eb[:, gs])
            s_new = lax.dot_general(bg, xd[:, gs], (((0,), (0,)), ((), ())),
                                    preferred_element_type=F32)
            st_ref[g] = prev * cdec[:, gs] + s_new
            for pr in range(SSD_HPG // 2):
                c0 = g * SSD_GW + pr * LANES
                xpair = xs_b[:, c0:c0 + LANES]
                xbd = jnp.concatenate([jnp.where(lane_lo, xpair, zero_b),
                                       jnp.where(lane_lo, zero_b, xpair)], axis=0)
                mats = []
                for e in range(2):
                    h = g * SSD_HPG + pr * 2 + e
                    hb = SSD_HEADS + h
                    sel = jnp.where(fwd_part, acs2[:, h:h + 1] - rt[h:h + 1, :],
                                    acs2[:, hb:hb + 1] - rt[hb:hb + 1, :])
                    sel = jnp.where(diag, dsum_t[h:h + 1, :], sel)
                    mats.append((cb * jnp.exp2(sel)).astype(BF16))
                yd = jnp.dot(jnp.concatenate(mats, axis=1), xbd, preferred_element_type=F32)
                ys.append(yd + y_off[:, pr * LANES:(pr + 1) * LANES])
        y = jnp.concatenate(ys, axis=-1) + xs * dsk_ref[...]
        y = y * _silu(z_ref[0, pl.ds(r0, CHUNK), :].astype(F32))
        ms = jnp.mean(y * y, axis=-1, keepdims=True)
        o_ref[0, pl.ds(r0, CHUNK), :] = (y * lax.rsqrt(ms + EPS) * ng_ref[...]).astype(BF16)
        return carry

    lax.fori_loop(0, qb, body, 0, unroll=True)


def _ssd(u3, dt3, cw, cb, dtb, alog, rexp_f, rexp_b, dsk, ng, *, qb):
    B, L, _ = u3.shape
    rows = qb * CHUNK
    assert L % rows == 0
    nt = L // rows
    nc = L // CHUNK
    hb = rows // HALO
    nhb = L // HALO
    state = pltpu.VMEM((SSD_GROUPS, SSD_STATE, SSD_GW), F32)
    const = lambda shape: pl.BlockSpec(shape, lambda b, n: (0,) * len(shape))
    rt = lambda n: nt - 1 - n

    xact, prevb = pl.pallas_call(
        functools.partial(_ssd_bwd_kernel, nt=nt, qb=qb),
        grid=(B, nt),
        in_specs=[
            pl.BlockSpec((1, rows, SSD_XBC), lambda b, n: (b, rt(n), C_XBC // SSD_XBC)),
            pl.BlockSpec((1, HALO, SSD_XBC),
                         lambda b, n: (b, jnp.maximum(rt(n) * hb - 1, 0), C_XBC // SSD_XBC)),
            pl.BlockSpec((1, HALO, SSD_XBC),
                         lambda b, n: (b, jnp.minimum((rt(n) + 1) * hb, nhb - 1), C_XBC // SSD_XBC)),
            pl.BlockSpec((1, rows, DT_PAD), lambda b, n: (b, rt(n), 0)),
            const((8, SSD_XBC)), const((1, SSD_XBC)), const((1, DT_PAD)), const((1, DT_PAD)),
            const((LANES, SSD_WIDTH)),
        ],
        out_specs=[pl.BlockSpec((1, rows, SSD_XBC), lambda b, n: (b, rt(n), 0)),
                   pl.BlockSpec((1, qb, SSD_GROUPS, SSD_STATE, SSD_GW),
                                lambda b, n: (b, rt(n), 0, 0, 0))],
        out_shape=[jax.ShapeDtypeStruct((B, L, SSD_XBC), BF16),
                   jax.ShapeDtypeStruct((B, nc, SSD_GROUPS, SSD_STATE, SSD_GW), BF16)],
        scratch_shapes=[pltpu.VMEM((rows + 2 * HALO, SSD_XBC), BF16),
                        pltpu.VMEM((rows, SSD_XBC), F32), state],
        compiler_params=_cparams(("parallel", "arbitrary")),
        name="ssd_bwd",
    )(u3, u3, u3, dt3, cw, cb, dtb, alog, rexp_b)

    return pl.pallas_call(
        functools.partial(_ssd_fwd_kernel, qb=qb),
        grid=(B, nt),
        in_specs=[
            pl.BlockSpec((1, rows, SSD_XBC), lambda b, n: (b, n, 0)),
            pl.BlockSpec((1, rows, DT_PAD), lambda b, n: (b, n, 0)),
            pl.BlockSpec((1, qb, SSD_GROUPS, SSD_STATE, SSD_GW), lambda b, n: (b, n, 0, 0, 0)),
            pl.BlockSpec((1, rows, SSD_WIDTH), lambda b, n: (b, n, C_Z // SSD_WIDTH)),
            const((1, DT_PAD)), const((1, DT_PAD)),
            const((LANES, SSD_WIDTH)), const((LANES, SSD_WIDTH)),
            const((1, SSD_WIDTH)), const((1, SSD_WIDTH)),
        ],
        out_specs=pl.BlockSpec((1, rows, SSD_WIDTH), lambda b, n: (b, n, 0)),
        out_shape=jax.ShapeDtypeStruct((B, L, SSD_WIDTH), BF16),
        scratch_shapes=[state],
        compiler_params=_cparams(("parallel", "arbitrary")),
        name="ssd_fwd",
    )(xact, dt3, prevb, u3, dtb, alog, rexp_f, rexp_b, dsk, ng)


def _ssd_expand_matrix(rev):
    m = np.zeros((LANES, SSD_WIDTH), np.float32)
    hoff = SSD_HEADS if rev else 0
    for h in range(SSD_HEADS):
        m[hoff + h, h * SSD_HEAD_DIM:(h + 1) * SSD_HEAD_DIM] = 1.0
    return jnp.asarray(m, BF16)


def _hy_filter_kernel(t_ref, bands_ref, w1t_ref, w1c_ref, w1s_ref, b1_ref, w2_ref, b2_ref, w3_ref,
                      b3_ref, w4_ref, fr_ref, absd_ref, x_ref, s_ref, *, L, tl):
    i = pl.program_id(0)
    pos = (lax.broadcasted_iota(jnp.int32, (tl, 1), 0) + i * tl).astype(F32)
    t = t_ref[...]
    pos_row = (lax.broadcasted_iota(jnp.int32, (1, tl), 1) + i * tl).astype(F32)
    ang_t = 2.0 * math.pi * pos_row * bands_ref[...] / L
    fr = fr_ref[...]
    dot = functools.partial(jnp.dot, preferred_element_type=F32, precision=HIGHEST)
    tdot = lambda a, b: lax.dot_general(a, b, (((0,), (0,)), ((), ())), preferred_element_type=F32,
                                        precision=HIGHEST)
    pre = t * w1t_ref[...] + tdot(jnp.cos(ang_t), w1c_ref[...]) + tdot(-jnp.sin(ang_t), w1s_ref[...])
    h = jnp.sin(fr * (pre + b1_ref[...]))
    h = jnp.sin(fr * (dot(h, w2_ref[...]) + b2_ref[...]))
    h = jnp.sin(fr * (dot(h, w3_ref[...]) + b3_ref[...]))
    w4 = w4_ref[...]
    h_hi, w_hi = h.astype(BF16), w4.astype(BF16)
    h_lo, w_lo = (h - h_hi.astype(F32)).astype(BF16), (w4 - w_hi.astype(F32)).astype(BF16)
    bdot = functools.partial(jnp.dot, preferred_element_type=F32)
    h = bdot(h_hi, w_hi) + bdot(h_hi, w_lo) + bdot(h_lo, w_hi)
    decay = jnp.exp(-t * absd_ref[...])
    hf = h[:, :HY_WIDTH] * decay
    hb = jnp.where(pos == 0.0, 0.0, h[:, HY_WIDTH:] * decay)
    _to_lane_blocks(s_ref, jnp.concatenate([hf + hb, hb - hf], axis=1))
    x_ref[...] = _split_even_odd(s_ref).astype(BF16)


def _hy_filter(L, w1, b1, w2, b2, w3, b3, w4, freq):
    tl = min(512, L)
    t = jnp.linspace(0.0, 1.0, L, dtype=F32)[:, None]
    bands = jnp.linspace(1e-4, HY_EMB_BANDS - 1, HY_EMB_BANDS, dtype=F32)[:, None]
    max_decay = math.log(HY_DECAY_TARGET) / HY_FAST_DECAY
    min_decay = math.log(HY_DECAY_TARGET) / HY_SLOW_DECAY
    absd = jnp.abs(jnp.linspace(min_decay, max_decay, HY_WIDTH, dtype=F32))[None, :]
    w1 = w1.astype(F32)
    full = lambda a: pl.BlockSpec(a.shape, lambda i: (0,) * a.ndim)
    args = [t, bands, w1[0:1], w1[1:1 + HY_EMB_BANDS], w1[1 + HY_EMB_BANDS:], b1[None], w2, b2[None],
            w3, b3[None], w4, freq[None], absd]
    in_specs = [pl.BlockSpec((tl, 1), lambda i: (i, 0))] + [full(a) for a in args[1:]]
    return pl.pallas_call(
        functools.partial(_hy_filter_kernel, L=L, tl=tl),
        grid=(L // tl,),
        in_specs=in_specs,
        out_specs=pl.BlockSpec((tl // 2, 4 * HY_WIDTH), lambda i: (i, 0)),
        out_shape=jax.ShapeDtypeStruct((L // 2, 4 * HY_WIDTH), BF16),
        scratch_shapes=[pltpu.VMEM((2 * HY_WIDTH // LANES, tl, LANES), F32)],
        compiler_params=_cparams(("parallel",)),
        name="hy_filter",
    )(*args)


def _dft_tables(L, hm):
    L2 = L // 2
    g = jnp.arange(L2, dtype=jnp.int32)
    s = jnp.arange(L2, dtype=jnp.int32)
    ph = ((2 * g + 1)[:, None] * s[None, :]) % (2 * L)
    ang = ph.astype(F32) * (math.pi / L)
    c = jnp.cos(ang).astype(BF16).reshape(L2 // hm, 1, hm, L2)
    sn = jnp.sin(ang).astype(BF16).reshape(L2 // hm, 1, hm, L2)
    a_fwd = jnp.concatenate([c, sn], axis=1).reshape(L, L2)
    a_inv = jnp.concatenate([c, -sn], axis=1).reshape(L, L2).T
    return a_fwd, a_inv


def _hy_kspec_kernel(a_ref, x_ref, wc_ref, ws_ref, f_ref, *, hm, scale):
    W = HY_WIDTH
    pq = jnp.dot(a_ref[...], x_ref[...], preferred_element_type=F32)
    ea, eb, oa, ob = [(pq[0:hm, i * W:(i + 1) * W], -pq[hm:, i * W:(i + 1) * W]) for i in range(4)]
    w = (wc_ref[...], -ws_ref[...])
    cmul = lambda x, y: (x[0] * y[0] - x[1] * y[1], x[0] * y[1] + x[1] * y[0])
    woa, wob = cmul(w, oa), cmul(w, ob)
    k1 = (ea[0] + woa[0], -(eb[1] + wob[1]))
    k2 = (ea[0] - woa[0], eb[1] - wob[1])
    kp = (k1[0] + k2[0], k1[1] - k2[1])
    km = (k1[0] - k2[0], k1[1] + k2[1])
    wkm = cmul(w, km)
    vkm = cmul((w[0], -w[1]), km)
    for i, part in enumerate((kp[0], kp[1], wkm[0], wkm[1], vkm[0], vkm[1])):
        f_ref[0, i] = part * scale


def _hy_kspec(a_fwd, xf2, *, hm):
    L, L2 = a_fwd.shape
    tm = 2 * hm
    nx = xf2.shape[1]
    theta = (2.0 * jnp.arange(L2, dtype=F32) + 1.0) * (math.pi / (2 * L))
    return pl.pallas_call(
        functools.partial(_hy_kspec_kernel, hm=hm, scale=1.0 / L),
        grid=(L // tm,),
        in_specs=[pl.BlockSpec((tm, L2), lambda i: (i, 0)),
                  pl.BlockSpec((L2, nx), lambda i: (0, 0)),
                  pl.BlockSpec((hm, 1), lambda i: (i, 0)),
                  pl.BlockSpec((hm, 1), lambda i: (i, 0))],
        out_specs=pl.BlockSpec((1, 6, hm, HY_WIDTH), lambda i: (i, 0, 0, 0)),
        out_shape=jax.ShapeDtypeStruct((L2 // hm, 6, hm, HY_WIDTH), F32),
        compiler_params=_cparams(("parallel",)),
        name="hy_kspec",
    )(a_fwd, xf2, jnp.cos(theta)[:, None], jnp.sin(theta)[:, None])


def _to_lane_blocks(s_ref, x):
    for c in range(s_ref.shape[0]):
        s_ref[c] = x[:, c * LANES:(c + 1) * LANES]


def _split_even_odd(s_ref):
    k, rows, _ = s_ref.shape
    return jnp.concatenate([s_ref[c, pl.ds(p, rows // 2, stride=2), :]
                            for p in range(2) for c in range(k)], axis=1)


def _hy_pre_kernel(xm_ref, xp_ref, xn_ref, cw_ref, cb_ref, u_ref, xau_ref, xpad_ref, s_ref, *, nt, tl):
    i = pl.program_id(1)
    _load_padded(xpad_ref, xm_ref, xp_ref, xn_ref, i > 0, i < nt - 1, tl)
    xa = _dwconv(xpad_ref, cw_ref, cb_ref, tl, HY_CONV, 0, HY_WIDTH)
    xb = _dwconv(xpad_ref, cw_ref, cb_ref, tl, HY_CONV, HY_WIDTH, 2 * HY_WIDTH)
    v = _dwconv(xpad_ref, cw_ref, cb_ref, tl, HY_CONV, 2 * HY_WIDTH, 3 * HY_WIDTH)
    u = xb * v
    xau_ref[0, :, :HY_WIDTH] = xa.astype(BF16)
    xau_ref[0, :, HY_WIDTH:] = u.astype(BF16)
    _to_lane_blocks(s_ref, u)
    u_ref[0] = _split_even_odd(s_ref).astype(BF16)


def _hy_tile_specs(L, tl):
    hb = tl // HALO
    nhb = L // HALO
    w = 3 * HY_WIDTH
    return [pl.BlockSpec((1, tl, w), lambda b, i: (b, i, C_HY // w)),
            pl.BlockSpec((1, HALO, w), lambda b, i: (b, jnp.maximum(i * hb - 1, 0), C_HY // w)),
            pl.BlockSpec((1, HALO, w), lambda b, i: (b, jnp.minimum((i + 1) * hb, nhb - 1), C_HY // w))]


def _hy_pre(u3, cw, cb, *, tl):
    B, L, _ = u3.shape
    nt = L // tl
    w = 3 * HY_WIDTH
    return pl.pallas_call(
        functools.partial(_hy_pre_kernel, nt=nt, tl=tl),
        grid=(B, nt),
        in_specs=_hy_tile_specs(L, tl) + [pl.BlockSpec((8, w), lambda b, i: (0, 0)),
                                          pl.BlockSpec((1, w), lambda b, i: (0, 0))],
        out_specs=[pl.BlockSpec((1, tl // 2, 2 * HY_WIDTH), lambda b, i: (b, i, 0)),
                   pl.BlockSpec((1, tl, 2 * HY_WIDTH), lambda b, i: (b, i, 0))],
        out_shape=[jax.ShapeDtypeStruct((B, L // 2, 2 * HY_WIDTH), BF16),
                   jax.ShapeDtypeStruct((B, L, 2 * HY_WIDTH), BF16)],
        scratch_shapes=[pltpu.VMEM((tl + 2 * HALO, w), BF16),
                        pltpu.VMEM((HY_WIDTH // LANES, tl, LANES), F32)],
        compiler_params=_cparams(("parallel", "parallel")),
        name="hy_pre",
    )(u3, u3, u3, cw, cb)


def _hy_fwd_kernel(a_ref, u_ref, f_ref, y_ref, *, hm):
    W = HY_WIDTH
    pq = jnp.dot(a_ref[...], u_ref[0], preferred_element_type=F32)
    pe, po = pq[0:hm, :W], pq[0:hm, W:]
    qe, qo = pq[hm:, :W], pq[hm:, W:]
    kpr, kpi, wr, wi, vr, vi = [f_ref[0, i] for i in range(6)]
    y_ref[0, 0:hm, :W] = (pe * kpr + qe * kpi + po * wr + qo * wi).astype(BF16)
    y_ref[0, hm:, :W] = (pe * kpi - qe * kpr + po * wi - qo * wr).astype(BF16)
    y_ref[0, 0:hm, W:] = (pe * vr + qe * vi + po * kpr + qo * kpi).astype(BF16)
    y_ref[0, hm:, W:] = (pe * vi - qe * vr + po * kpi - qo * kpr).astype(BF16)


def _hy_fwd(a_fwd, ueo, filt, *, hm):
    B, L2, W2 = ueo.shape
    L = 2 * L2
    tm = 2 * hm
    return pl.pallas_call(
        functools.partial(_hy_fwd_kernel, hm=hm),
        grid=(L // tm, B),
        in_specs=[pl.BlockSpec((tm, L2), lambda i, b: (i, 0)),
                  pl.BlockSpec((1, L2, W2), lambda i, b: (b, 0, 0)),
                  pl.BlockSpec((1, 6, hm, HY_WIDTH), lambda i, b: (i, 0, 0, 0))],
        out_specs=pl.BlockSpec((1, tm, W2), lambda i, b: (b, i, 0)),
        out_shape=jax.ShapeDtypeStruct((B, L, W2), BF16),
        compiler_params=_cparams(("parallel", "parallel")),
        name="hy_fwd",
    )(a_fwd, ueo, filt)


def _hy_inv_kernel(a_ref, y_ref, o_ref):
    o_ref[0] = jnp.dot(a_ref[...], y_ref[0], preferred_element_type=F32)


def _hy_inv(a_inv, yspec, *, tm):
    B, L, W2 = yspec.shape
    L2 = L // 2
    return pl.pallas_call(
        _hy_inv_kernel,
        grid=(L2 // tm, B),
        in_specs=[pl.BlockSpec((tm, L), lambda i, b: (i, 0)),
                  pl.BlockSpec((1, L, W2), lambda i, b: (b, 0, 0))],
        out_specs=pl.BlockSpec((1, tm, W2), lambda i, b: (b, i, 0)),
        out_shape=jax.ShapeDtypeStruct((B, L2, W2), F32),
        compiler_params=_cparams(("parallel", "parallel")),
        name="hy_inv",
    )(a_inv, yspec)


def _hy_post_kernel(c_ref, xau_ref, g_ref, d_ref, ng_ref, o_ref, s_ref, *, tl):
    nblk = HY_WIDTH // LANES
    for p in range(2):
        for c in range(nblk):
            s_ref[c, pl.ds(p, tl // 2, stride=2), :] = c_ref[0, :, (p * nblk + c) * LANES:
                                                             (p * nblk + c + 1) * LANES]
    conv = jnp.concatenate([s_ref[c] for c in range(nblk)], axis=1)
    xa = xau_ref[0, :, :HY_WIDTH].astype(F32)
    u = xau_ref[0, :, HY_WIDTH:].astype(F32)
    y = xa * (conv + u * d_ref[...])
    y = y * _silu(g_ref[0].astype(F32))
    ms = jnp.mean(y * y, axis=-1, keepdims=True)
    o_ref[0] = (y * lax.rsqrt(ms + EPS) * ng_ref[...]).astype(BF16)


def _hy_post(conv, xau, u3, d, ng, *, tl):
    B, L, _ = u3.shape
    return pl.pallas_call(
        functools.partial(_hy_post_kernel, tl=tl),
        grid=(B, L // tl),
        in_specs=[pl.BlockSpec((1, tl // 2, 2 * HY_WIDTH), lambda b, i: (b, i, 0)),
                  pl.BlockSpec((1, tl, 2 * HY_WIDTH), lambda b, i: (b, i, 0)),
                  pl.BlockSpec((1, tl, HY_WIDTH), lambda b, i: (b, i, C_GHY // HY_WIDTH)),
                  pl.BlockSpec((1, HY_WIDTH), lambda b, i: (0, 0)),
                  pl.BlockSpec((1, HY_WIDTH), lambda b, i: (0, 0))],
        out_specs=pl.BlockSpec((1, tl, HY_WIDTH), lambda b, i: (b, i, 0)),
        out_shape=jax.ShapeDtypeStruct((B, L, HY_WIDTH), BF16),
        scratch_shapes=[pltpu.VMEM((HY_WIDTH // LANES, tl, LANES), F32)],
        compiler_params=_cparams(("parallel", "parallel")),
        name="hy_post",
    )(conv, xau, u3, d, ng)


def _pad_rows(a, rows):
    return jnp.concatenate([a, jnp.zeros((rows - a.shape[0],) + a.shape[1:], a.dtype)], axis=0)


def _pad_cols(a, cols):
    return jnp.concatenate([a, jnp.zeros(a.shape[:-1] + (cols - a.shape[-1],), a.dtype)], axis=-1)


def _static_take(a, idx, axis):
    idx = np.asarray(idx)
    cuts = np.flatnonzero(np.diff(idx) != 1) + 1
    starts = np.concatenate([[0], cuts])
    ends = np.concatenate([cuts, [len(idx)]])
    parts = [lax.slice_in_dim(a, int(idx[s]), int(idx[e - 1]) + 1, axis=axis)
             for s, e in zip(starts, ends)]
    return jnp.concatenate(parts, axis=axis)


def _tile(n, pref):
    t = min(pref, n)
    assert n % t == 0
    return t


def _layer(x3, p, tables, *, final, final_g):
    B, L, _ = x3.shape
    T = B * L
    x2 = x3.reshape(T, D_MODEL)
    u, dt = _inproj(x2, p["norm_g"], p["w_in"], tm=_tile(T, 512), tn=512)
    u3 = u.reshape(B, L, N_IN)
    dt3 = dt.reshape(B, L, DT_PAD)

    ys = _ssd(u3, dt3, p["ssd_cw"], p["ssd_cb"], p["ssd_dtb"], p["ssd_alog"], p["rexp_f"],
              p["rexp_b"], p["ssd_dskip"], p["ssd_ng"], qb=_tile(L // CHUNK, 4))

    ya = _attention(u3, p["att_bias"], p["att_sink"], p["att_ng"], qb=_tile(L // ATT_BLOCK, 8))

    a_fwd, a_inv, filt, hm = tables
    tl = _tile(L, 1024)
    ueo, xau = _hy_pre(u3, p["hy_cw"], p["hy_cb"], tl=tl)
    yspec = _hy_fwd(a_fwd, ueo, filt, hm=hm)
    conv = _hy_inv(a_inv, yspec, tm=_tile(L // 2, 512))
    yh = _hy_post(conv, xau, u3, p["hy_d"], p["hy_ng"], tl=tl)

    out = _outproj(x2, ys.reshape(T, SSD_WIDTH), ya.reshape(T, ATT_WIDTH), yh.reshape(T, HY_WIDTH),
                   p["w_out"], final_g, tm=_tile(T, 512), final=final)
    return out.reshape(B, L, D_MODEL)


def kernel(x_prompt, x_sample, rel_bias, norm_g, w_in, ssd_conv_w, ssd_conv_b, ssd_dt_bias, ssd_a_log, ssd_d, ssd_norm_g, att_sink, att_norm_g, hy_conv_w, hy_conv_b, hy_w1, hy_b1, hy_w2, hy_b2, hy_w3, hy_b3, hy_w4, hy_freq, hy_d, hy_norm_g, w_out, final_norm_g):
    depth = w_in.shape[0]
    perm, col_scale = _in_perm()
    n_real = C_DT + 2 * SSD_HEADS
    scale_old = np.ones((IN_COLS,), np.float32)
    scale_old[perm[:n_real]] = col_scale[:n_real]
    w_in_p = _pad_cols(_static_take((w_in * jnp.asarray(scale_old)).astype(BF16), perm[:n_real], 2),
                       N_IN)
    att_perm = _att_col_perm()
    out_rows = np.concatenate([np.arange(SSD_WIDTH), SSD_WIDTH + att_perm,
                               np.arange(SSD_WIDTH + ATT_WIDTH, D_MODEL)])
    w_out_b = _static_take(w_out, out_rows, 1).astype(BF16)
    att_bias = _attn_bias(rel_bias)
    rexp_f, rexp_b = _ssd_expand_matrix(False), _ssd_expand_matrix(True)
    final_g = final_norm_g.astype(F32)[None, :]

    layers = []
    for i in range(depth):
        layers.append(dict(
            norm_g=norm_g[i].astype(F32)[None, :],
            w_in=w_in_p[i],
            ssd_cw=_pad_rows(ssd_conv_w[i].astype(F32), 8),
            ssd_cb=ssd_conv_b[i].astype(F32)[None, :],
            ssd_dtb=_pad_cols(ssd_dt_bias[i].astype(F32).reshape(1, 2 * SSD_HEADS), DT_PAD),
            ssd_alog=_pad_cols(ssd_a_log[i].astype(F32).reshape(1, 2 * SSD_HEADS), DT_PAD),
            ssd_dskip=jnp.repeat(ssd_d[i].astype(F32), SSD_HEAD_DIM)[None, :],
            ssd_ng=ssd_norm_g[i].astype(F32)[None, :],
            rexp_f=rexp_f, rexp_b=rexp_b,
            att_bias=att_bias,
            att_sink=jnp.broadcast_to(att_sink[i].astype(F32)[:, None] * LOG2E, (ATT_HEADS, LANES)),
            att_ng=att_norm_g[i].astype(F32)[att_perm][None, :],
            hy_cw=_pad_rows(hy_conv_w[i].astype(F32), 8),
            hy_cb=hy_conv_b[i].astype(F32)[None, :],
            hy_d=hy_d[i].astype(F32)[None, :],
            hy_ng=hy_norm_g[i].astype(F32)[None, :],
            w_out=w_out_b[i],
        ))

    def trunk(x):
        L = x.shape[1]
        hm = min(512, L // 4)
        a_fwd, a_inv = _dft_tables(L, hm)
        for i in range(depth):
            xf = _hy_filter(L, hy_w1[i], hy_b1[i].astype(F32), hy_w2[i].astype(F32),
                            hy_b2[i].astype(F32), hy_w3[i].astype(F32), hy_b3[i].astype(F32),
                            hy_w4[i].astype(F32), hy_freq[i].astype(F32))
            filt = _hy_kspec(a_fwd, xf, hm=hm)
            x = _layer(x, layers[i], (a_fwd, a_inv, filt, hm), final=(i == depth - 1),
                       final_g=final_g)
        return x

    return (trunk(x_prompt), trunk(x_sample))
```

```python
import functools
import math

import jax
import jax.numpy as jnp
import numpy as np
from jax import lax
from jax.experimental import pallas as pl
from jax.experimental.pallas import tpu as pltpu

F32 = jnp.float32
BF16 = jnp.bfloat16
HIGHEST = lax.Precision.HIGHEST

D_MODEL = 2048
SSD_WIDTH = 1024
ATT_WIDTH = 512
HY_WIDTH = 512
SSD_HEAD_DIM = 64
SSD_HEADS = 16
SSD_GROUPS = 2
SSD_STATE = 128
SSD_CONV = 5
CHUNK = 128
SSD_XBC = SSD_WIDTH + 2 * SSD_GROUPS * SSD_STATE
ATT_HEAD_DIM = 64
ATT_HEADS = 8
ATT_KV_HEADS = 2
ATT_REP = ATT_HEADS // ATT_KV_HEADS
ATT_WINDOW = 128
ATT_BLOCK = 128
REL_BUCKETS = 32
REL_MAX_DIST = 128
HY_CONV = 3
HY_EMB_BANDS = 16
HY_FF = 64
HY_FAST_DECAY = 0.3
HY_SLOW_DECAY = 1.5
HY_DECAY_TARGET = 1e-2
EPS = 1e-6
NEG_BIG = -1e30

LANES = 128
BF16_SUBLANES = 16
VMEM_LIMIT = 56 * 1024 * 1024

C_XBC = 0
C_HY = 1536
C_Z = 3072
C_Q = 4096
C_GATT = 4608
C_GHY = 5120
C_K = 5632
C_V = 5760
C_DT = 5888
DT_PAD = 128
N_IN = C_DT + DT_PAD

_OLD_SIZES = [SSD_WIDTH, SSD_XBC, 2 * SSD_HEADS, ATT_WIDTH, ATT_KV_HEADS * ATT_HEAD_DIM,
              ATT_KV_HEADS * ATT_HEAD_DIM, ATT_WIDTH, 3 * HY_WIDTH, HY_WIDTH]
_OLD_OFF = np.concatenate([[0], np.cumsum(_OLD_SIZES)])
IN_COLS = int(_OLD_OFF[-1])


LOG2E = math.log2(math.e)
Q_SCALE = ATT_HEAD_DIM ** -0.5 * LOG2E


def _att_col_perm():
    order = [h for j in range(ATT_REP) for h in (j, ATT_REP + j)]
    return np.concatenate([np.arange(h * ATT_HEAD_DIM, (h + 1) * ATT_HEAD_DIM) for h in order])


def _in_perm():
    perm = np.full((N_IN,), IN_COLS, np.int32)
    scale = np.ones((N_IN,), np.float32)
    o = {n: int(_OLD_OFF[i]) for i, n in enumerate(
        ["z", "xbc", "dt", "q", "k", "v", "gatt", "hy", "ghy"])}
    def put(new, old, width):
        perm[new:new + width] = np.arange(old, old + width)
    put(C_XBC, o["xbc"], SSD_XBC)
    put(C_HY, o["hy"], 3 * HY_WIDTH)
    put(C_Z, o["z"], SSD_WIDTH)
    perm[C_Q:C_Q + ATT_WIDTH] = o["q"] + _att_col_perm()
    scale[C_Q:C_Q + ATT_WIDTH] = Q_SCALE
    perm[C_GATT:C_GATT + ATT_WIDTH] = o["gatt"] + _att_col_perm()
    put(C_GHY, o["ghy"], HY_WIDTH)
    put(C_K, o["k"], 128)
    put(C_V, o["v"], 128)
    put(C_DT, o["dt"], 2 * SSD_HEADS)
    return perm, scale


def _cparams(sem):
    return pltpu.CompilerParams(dimension_semantics=sem, vmem_limit_bytes=VMEM_LIMIT)


def _silu(x):
    return x * (1.0 / (1.0 + jnp.exp(-x)))


def _softplus(x):
    return jnp.maximum(x, 0.0) + jnp.log1p(jnp.exp(-jnp.abs(x)))


def _inproj_kernel(x_ref, g_ref, w_ref, u_ref, dt_ref, h_ref, *, rows, tn):
    for r in range(x_ref.shape[0] // rows):
        sl = slice(r * rows, (r + 1) * rows)
        x = x_ref[sl, :]
        ms = jnp.mean(x * x, axis=-1, keepdims=True)
        h_ref[sl, :] = (x * lax.rsqrt(ms + EPS) * g_ref[...]).astype(BF16)
        h = h_ref[sl, :]
        for c0 in range(0, N_IN, tn):
            c1 = min(c0 + tn, N_IN)
            acc = jnp.dot(h, w_ref[:, c0:c1], preferred_element_type=F32)
            u_ref[sl, c0:c1] = acc.astype(BF16)
            if c0 <= C_DT < c1:
                dt_ref[sl, :] = acc[:, C_DT - c0:C_DT - c0 + DT_PAD]


def _inproj(x2, g, w, *, tm, tn):
    T = x2.shape[0]
    assert T % tm == 0
    return pl.pallas_call(
        functools.partial(_inproj_kernel, rows=min(256, tm), tn=tn),
        grid=(T // tm,),
        in_specs=[pl.BlockSpec((tm, D_MODEL), lambda i: (i, 0)),
                  pl.BlockSpec((1, D_MODEL), lambda i: (0, 0)),
                  pl.BlockSpec((D_MODEL, N_IN), lambda i: (0, 0), pipeline_mode=pl.Buffered(1))],
        out_specs=[pl.BlockSpec((tm, N_IN), lambda i: (i, 0)),
                   pl.BlockSpec((tm, DT_PAD), lambda i: (i, 0))],
        out_shape=[jax.ShapeDtypeStruct((T, N_IN), BF16),
                   jax.ShapeDtypeStruct((T, DT_PAD), F32)],
        scratch_shapes=[pltpu.VMEM((tm, D_MODEL), BF16)],
        compiler_params=_cparams(("parallel",)),
        name="inproj",
    )(x2, g, w)


def _outproj_kernel(x_ref, ys_ref, ya_ref, c_ref, xau_ref, g_ref, d_ref, hng_ref, w_ref, fg_ref,
                    o_ref, s_ref, *, rows, final):
    nblk = HY_WIDTH // LANES
    for r in range(x_ref.shape[0] // rows):
        sl = slice(r * rows, (r + 1) * rows)
        hs = slice(r * rows // 2, (r + 1) * rows // 2)
        for p in range(2):
            for c in range(nblk):
                s_ref[c, pl.ds(p, rows // 2, stride=2), :] = c_ref[hs, (p * nblk + c) * LANES:
                                                                   (p * nblk + c + 1) * LANES]
        conv = jnp.concatenate([s_ref[c] for c in range(nblk)], axis=1)
        xa = xau_ref[sl, :HY_WIDTH].astype(F32)
        u = xau_ref[sl, HY_WIDTH:].astype(F32)
        y = xa * (conv + u * d_ref[...])
        y = y * _silu(g_ref[sl, :].astype(F32))
        ms = jnp.mean(y * y, axis=-1, keepdims=True)
        yh = (y * lax.rsqrt(ms + EPS) * hng_ref[...]).astype(BF16)

        acc = x_ref[sl, :]
        acc = acc + jnp.dot(ys_ref[sl, :], w_ref[0:SSD_WIDTH, :], preferred_element_type=F32)
        acc = acc + jnp.dot(ya_ref[sl, :], w_ref[SSD_WIDTH:SSD_WIDTH + ATT_WIDTH, :],
                            preferred_element_type=F32)
        acc = acc + jnp.dot(yh, w_ref[SSD_WIDTH + ATT_WIDTH:, :], preferred_element_type=F32)
        if final:
            ms = jnp.mean(acc * acc, axis=-1, keepdims=True)
            acc = acc * lax.rsqrt(ms + EPS) * fg_ref[...]
        o_ref[sl, :] = acc


def _outproj(x2, ys, ya, conv2, xau2, u2, d, hng, w, fg, *, tm, final):
    T = x2.shape[0]
    assert T % tm == 0
    rows = min(256, tm)
    return pl.pallas_call(
        functools.partial(_outproj_kernel, rows=rows, final=final),
        grid=(T // tm,),
        in_specs=[pl.BlockSpec((tm, D_MODEL), lambda i: (i, 0)),
                  pl.BlockSpec((tm, SSD_WIDTH), lambda i: (i, 0)),
                  pl.BlockSpec((tm, ATT_WIDTH), lambda i: (i, 0)),
                  pl.BlockSpec((tm // 2, 2 * HY_WIDTH), lambda i: (i, 0)),
                  pl.BlockSpec((tm, 2 * HY_WIDTH), lambda i: (i, 0)),
                  pl.BlockSpec((tm, HY_WIDTH), lambda i: (i, C_GHY // HY_WIDTH)),
                  pl.BlockSpec((1, HY_WIDTH), lambda i: (0, 0)),
                  pl.BlockSpec((1, HY_WIDTH), lambda i: (0, 0)),
                  pl.BlockSpec((D_MODEL, D_MODEL), lambda i: (0, 0)),
                  pl.BlockSpec((1, D_MODEL), lambda i: (0, 0))],
        out_specs=pl.BlockSpec((tm, D_MODEL), lambda i: (i, 0)),
        out_shape=jax.ShapeDtypeStruct((T, D_MODEL), F32),
        scratch_shapes=[pltpu.VMEM((HY_WIDTH // LANES, rows, LANES), F32)],
        compiler_params=_cparams(("parallel",)),
        name="outproj",
    )(x2, ys, ya, conv2, xau2, u2, d, hng, w, fg)


def _t5_buckets(rel):
    nb = REL_BUCKETS // 2
    max_exact = nb // 2
    ret = (rel > 0).astype(np.int32) * nb
    n = np.abs(rel)
    large = max_exact + (np.log(np.maximum(n, 1) / max_exact) / math.log(REL_MAX_DIST / max_exact)
                         * (nb - max_exact)).astype(np.int32)
    large = np.minimum(large, nb - 1)
    return ret + np.where(n < max_exact, n, large)


def _attn_bias(rel_bias):
    qi = np.arange(ATT_BLOCK)[:, None]
    kj = np.arange(3 * ATT_BLOCK)[None, :]
    rel = kj - ATT_BLOCK - qi
    onehot = (_t5_buckets(rel)[None] == np.arange(REL_BUCKETS)[:, None, None]).astype(np.float32)
    bias = jnp.einsum("bqk,bh->hqk", jnp.asarray(onehot), rel_bias.astype(F32),
                      precision=HIGHEST) * LOG2E
    window = np.abs(rel) <= ATT_WINDOW
    variants = []
    for last in (False, True):
        for first in (False, True):
            ok = window & ~(first & (kj < ATT_BLOCK)) & ~(last & (kj >= 2 * ATT_BLOCK))
            variants.append(jnp.where(ok[None], bias, NEG_BIG))
    return jnp.transpose(jnp.stack(variants), (0, 1, 3, 2))


def _attn_kernel(q_ref, kp_ref, kc_ref, kn_ref, vp_ref, vc_ref, vn_ref, g_ref, bias_ref, sink_ref,
                 ng_ref, o_ref, klo_ref, khi_ref, vt_ref, *, nt, qb):
    n = pl.program_id(1)
    lo = lax.broadcasted_iota(jnp.int32, (1, LANES), 1) < ATT_HEAD_DIM
    zero = jnp.zeros((), BF16)
    kext = jnp.concatenate([kp_ref[0], kc_ref[0], kn_ref[0]], axis=0)
    klo_ref[...] = jnp.where(lo, kext, zero)
    khi_ref[...] = jnp.where(lo, zero, kext)
    for t, ref, cnt in ((0, vp_ref, 1), (1, vc_ref, qb), (qb + 1, vn_ref, 1)):
        for i in range(cnt):
            blk = ref[0, i * ATT_BLOCK:(i + 1) * ATT_BLOCK, :]
            vt_ref[t + i] = blk.astype(F32).T.astype(BF16)
    row_lo = lax.broadcasted_iota(jnp.int32, (LANES, ATT_BLOCK), 0) < ATT_HEAD_DIM
    nk = 3 * ATT_BLOCK

    def body(i, carry):
        r0 = pl.multiple_of(i * ATT_BLOCK, ATT_BLOCK)
        q = q_ref[0, pl.ds(r0, ATT_BLOCK), :]
        kst = jnp.concatenate([klo_ref[pl.ds(r0, nk), :], khi_ref[pl.ds(r0, nk), :]], axis=0)
        vt = jnp.concatenate([vt_ref[i], vt_ref[i + 1], vt_ref[i + 2]], axis=1)
        first = jnp.logical_and(n == 0, i == 0)
        last = jnp.logical_and(n == nt - 1, i == qb - 1)
        variant = first.astype(jnp.int32) + 2 * last.astype(jnp.int32)
        outs = []
        for j in range(ATT_REP):
            qp = q[:, j * LANES:(j + 1) * LANES]
            st = lax.dot_general(kst, qp, (((1,), (1,)), ((), ())), preferred_element_type=F32)
            halves = []
            for e, h in enumerate((j, ATT_REP + j)):
                s = st[e * nk:(e + 1) * nk] + bias_ref[variant, h]
                sk = sink_ref[h:h + 1, :]
                m = jnp.maximum(jnp.max(s, axis=0, keepdims=True), sk)
                p = jnp.exp2(s - m)
                den = jnp.sum(p, axis=0, keepdims=True) + jnp.exp2(sk - m)
                ot = jnp.dot(vt, p.astype(BF16), preferred_element_type=F32)
                halves.append(ot * (1.0 / den))
            outs.append(jnp.where(row_lo, halves[0], halves[1]).T)
        o = jnp.concatenate(outs, axis=-1)
        y = o * _silu(g_ref[0, pl.ds(r0, ATT_BLOCK), :].astype(F32))
        ms = jnp.mean(y * y, axis=-1, keepdims=True)
        o_ref[0, pl.ds(r0, ATT_BLOCK), :] = (y * lax.rsqrt(ms + EPS) * ng_ref[...]).astype(BF16)
        return carry

    lax.fori_loop(0, qb, body, 0, unroll=True)


def _attention(u3, bias, sink, ng, *, qb):
    B, L, _ = u3.shape
    tq = qb * ATT_BLOCK
    assert L % tq == 0
    nt = L // tq
    nb = L // ATT_BLOCK
    kcol, vcol = C_K // 128, C_V // 128
    def kv_specs(colblk):
        return [pl.BlockSpec((1, ATT_BLOCK, 128), lambda b, n: (b, jnp.maximum(n * qb - 1, 0), colblk)),
                pl.BlockSpec((1, tq, 128), lambda b, n: (b, n, colblk)),
                pl.BlockSpec((1, ATT_BLOCK, 128),
                             lambda b, n: (b, jnp.minimum((n + 1) * qb, nb - 1), colblk))]
    return pl.pallas_call(
        functools.partial(_attn_kernel, nt=nt, qb=qb),
        grid=(B, nt),
        in_specs=[pl.BlockSpec((1, tq, ATT_WIDTH), lambda b, n: (b, n, C_Q // ATT_WIDTH))]
                 + kv_specs(kcol) + kv_specs(vcol)
                 + [pl.BlockSpec((1, tq, ATT_WIDTH), lambda b, n: (b, n, C_GATT // ATT_WIDTH)),
                    pl.BlockSpec((4, ATT_HEADS, 3 * ATT_BLOCK, ATT_BLOCK), lambda b, n: (0, 0, 0, 0)),
                    pl.BlockSpec((ATT_HEADS, LANES), lambda b, n: (0, 0)),
                    pl.BlockSpec((1, ATT_WIDTH), lambda b, n: (0, 0))],
        out_specs=pl.BlockSpec((1, tq, ATT_WIDTH), lambda b, n: (b, n, 0)),
        out_shape=jax.ShapeDtypeStruct((B, L, ATT_WIDTH), BF16),
        scratch_shapes=[pltpu.VMEM(((qb + 2) * ATT_BLOCK, LANES), BF16),
                        pltpu.VMEM(((qb + 2) * ATT_BLOCK, LANES), BF16),
                        pltpu.VMEM((qb + 2, LANES, ATT_BLOCK), BF16)],
        compiler_params=_cparams(("parallel", "parallel")),
        name="attn",
    )(u3, u3, u3, u3, u3, u3, u3, u3, bias, sink, ng)


HALO = BF16_SUBLANES


CONV_BLK = 128


def _load_padded(xpad_ref, xm_ref, xp_ref, xn_ref, has_prev, has_next, rows):
    zero = jnp.zeros((), BF16)
    xpad_ref[pl.ds(0, HALO), :] = jnp.where(has_prev, xp_ref[0], zero)
    xpad_ref[pl.ds(HALO, rows), :] = xm_ref[0]
    xpad_ref[pl.ds(HALO + rows, HALO), :] = jnp.where(has_next, xn_ref[0], zero)


def _shift_matrix(width):
    offs = [k - width // 2 for k in range(width) if k != width // 2]
    r = lax.broadcasted_iota(jnp.int32, (CONV_BLK, CONV_BLK + 2 * HALO), 0)
    c = lax.broadcasted_iota(jnp.int32, (CONV_BLK, CONV_BLK + 2 * HALO), 1)
    return jnp.concatenate([(c == r + HALO + d).astype(BF16) for d in offs], axis=0)


def _dwconv_block(xpad_ref, shifts, w_ref, b_ref, j, width, c0, c1):
    win = xpad_ref[j * CONV_BLK:(j + 1) * CONV_BLK + 2 * HALO, c0:c1]
    moved = jnp.dot(shifts, win, preferred_element_type=F32)
    acc = win[HALO:HALO + CONV_BLK].astype(F32) * w_ref[width // 2:width // 2 + 1, c0:c1]
    i = 0
    for k in range(width):
        if k == width // 2:
            continue
        acc = acc + moved[i * CONV_BLK:(i + 1) * CONV_BLK] * w_ref[k:k + 1, c0:c1]
        i += 1
    return acc + b_ref[:, c0:c1]


def _dwconv(xpad_ref, w_ref, b_ref, rows, width, c0, c1):
    shifts = _shift_matrix(width)
    return jnp.concatenate([_dwconv_block(xpad_ref, shifts, w_ref, b_ref, j, width, c0, c1)
                            for j in range(rows // CONV_BLK)], axis=0)


SSD_HPG = SSD_HEADS // SSD_GROUPS
SSD_GW = SSD_HPG * SSD_HEAD_DIM
SSD_BC = SSD_GROUPS * SSD_STATE


def _ssd_decay(dt_raw, dtb_ref, alog_ref):
    dt = _softplus(dt_raw + dtb_ref[...])
    dta = dt * (-jnp.exp(alog_ref[...]))
    row = lax.broadcasted_iota(jnp.int32, (CHUNK, CHUNK), 0)
    col = lax.broadcasted_iota(jnp.int32, (CHUNK, CHUNK), 1)
    tril = (row >= col).astype(BF16)
    hi = dta.astype(BF16)
    r1 = dta - hi.astype(F32)
    mid = r1.astype(BF16)
    lo = (r1 - mid.astype(F32)).astype(BF16)
    pre = (jnp.dot(tril, hi, preferred_element_type=F32) + jnp.dot(tril, mid, preferred_element_type=F32)
           + jnp.dot(tril, lo, preferred_element_type=F32))
    tot = pre[CHUNK - 1:CHUNK, :]
    is_bwd = lax.broadcasted_iota(jnp.int32, (1, LANES), 1) >= SSD_HEADS
    acs = jnp.where(is_bwd, tot - pre + dta, pre)
    return dt, acs, tot


def _ssd_bwd_kernel(xm_ref, xp_ref, xn_ref, dt_ref, cw_ref, cb_ref, dtb_ref, alog_ref, rexp_ref,
                    xact_ref, prev_ref, xpad_ref, xf_ref, st_ref, *, nt, qb):
    n = pl.program_id(1)
    tile = nt - 1 - n
    rows = qb * CHUNK

    @pl.when(n == 0)
    def _():
        st_ref[...] = jnp.zeros(st_ref.shape, F32)

    _load_padded(xpad_ref, xm_ref, xp_ref, xn_ref, tile > 0, tile < nt - 1, rows)
    xact = _silu(_dwconv(xpad_ref, cw_ref, cb_ref, rows, SSD_CONV, 0, SSD_XBC))
    xf_ref[...] = xact
    xact_ref[0] = xact.astype(BF16)

    def body(j, carry):
        i = qb - 1 - j
        r0 = pl.multiple_of(i * CHUNK, CHUNK)
        xs = xf_ref[pl.ds(r0, CHUNK), 0:SSD_WIDTH]
        bm = xf_ref[pl.ds(r0, CHUNK), SSD_WIDTH:SSD_WIDTH + SSD_BC].astype(BF16)
        dt, acs, tot = _ssd_decay(dt_ref[0, pl.ds(r0, CHUNK), :], dtb_ref, alog_ref)
        e_in = jnp.concatenate([jnp.exp(tot - acs) * dt, jnp.broadcast_to(jnp.exp(tot), (8, LANES))],
                               axis=0).astype(BF16)
        e_out = jnp.dot(e_in, rexp_ref[...], preferred_element_type=F32)
        xd = (xs * e_out[0:CHUNK]).astype(BF16)
        cdec = e_out[CHUNK:CHUNK + 1]
        for g in range(SSD_GROUPS):
            prev = st_ref[g]
            prev_ref[0, i, g] = prev.astype(BF16)
            s_new = lax.dot_general(bm[:, g * SSD_STATE:(g + 1) * SSD_STATE],
                                    xd[:, g * SSD_GW:(g + 1) * SSD_GW], (((0,), (0,)), ((), ())),
                                    preferred_element_type=F32)
            st_ref[g] = prev * cdec[:, g * SSD_GW:(g + 1) * SSD_GW] + s_new
        return carry

    lax.fori_loop(0, qb, body, 0, unroll=True)


def _ssd_fwd_kernel(xa_ref, dt_ref, prev_ref, z_ref, dtb_ref, alog_ref, rexpf_ref, rexpb_ref,
                    dsk_ref, ng_ref, o_ref, st_ref, *, qb):
    n = pl.program_id(1)

    @pl.when(n == 0)
    def _():
        st_ref[...] = jnp.zeros(st_ref.shape, F32)

    row = lax.broadcasted_iota(jnp.int32, (CHUNK, CHUNK), 0)
    col = lax.broadcasted_iota(jnp.int32, (CHUNK, CHUNK), 1)
    fwd_part = row > col
    diag = row == col
    lane_lo = lax.broadcasted_iota(jnp.int32, (1, LANES), 1) < SSD_HEAD_DIM
    zero_b = jnp.zeros((), BF16)

    def body(i, carry):
        r0 = pl.multiple_of(i * CHUNK, CHUNK)
        xs_b = xa_ref[0, pl.ds(r0, CHUNK), 0:SSD_WIDTH]
        xs = xs_b.astype(F32)
        bm = xa_ref[0, pl.ds(r0, CHUNK), SSD_WIDTH:SSD_WIDTH + SSD_BC]
        cm = xa_ref[0, pl.ds(r0, CHUNK), SSD_WIDTH + SSD_BC:SSD_XBC]
        dt, acs, tot = _ssd_decay(dt_ref[0, pl.ds(r0, CHUNK), :], dtb_ref, alog_ref)
        acs2 = acs * LOG2E
        rt = (acs2 - jnp.log2(dt)).T
        dsum_t = jnp.log2(dt + pltpu.roll(dt, LANES - SSD_HEADS, axis=1)).T
        eacs = jnp.exp(acs)
        ef_in = jnp.concatenate([jnp.exp(tot - acs) * dt, eacs,
                                 jnp.broadcast_to(jnp.exp(tot), (8, LANES))], axis=0).astype(BF16)
        ef = jnp.dot(ef_in, rexpf_ref[...], preferred_element_type=F32)
        eb = jnp.dot(eacs.astype(BF16), rexpb_ref[...], preferred_element_type=F32)
        xd = (xs * ef[0:CHUNK]).astype(BF16)
        eacs_f = ef[CHUNK:2 * CHUNK]
        cdec = ef[2 * CHUNK:2 * CHUNK + 1]
        ys = []
        for g in range(SSD_GROUPS):
            bg = bm[:, g * SSD_STATE:(g + 1) * SSD_STATE]
            cg = cm[:, g * SSD_STATE:(g + 1) * SSD_STATE]
            gs = slice(g * SSD_GW, (g + 1) * SSD_GW)
            cb = lax.dot_general(cg, bg, (((1,), (1,)), ((), ())), preferred_element_type=F32)
            prev = st_ref[g]
            y_off = (jnp.dot(cg, prev.astype(BF16), preferred_element_type=F32) * eacs_f[:, gs]
                     + jnp.dot(cg, prev_ref[0, i, g], preferred_element_type=F32) * eb[:, gs])
            s_new = lax.dot_general(bg, xd[:, gs], (((0,), (0,)), ((), ())),
                                    preferred_element_type=F32)
            st_ref[g] = prev * cdec[:, gs] + s_new
            for pr in range(SSD_HPG // 2):
                c0 = g * SSD_GW + pr * LANES
                xpair = xs_b[:, c0:c0 + LANES]
                xbd = jnp.concatenate([jnp.where(lane_lo, xpair, zero_b),
                                       jnp.where(lane_lo, zero_b, xpair)], axis=0)
                mats = []
                for e in range(2):
                    h = g * SSD_HPG + pr * 2 + e
                    hb = SSD_HEADS + h
                    sel = jnp.where(fwd_part, acs2[:, h:h + 1] - rt[h:h + 1, :],
                                    acs2[:, hb:hb + 1] - rt[hb:hb + 1, :])
                    sel = jnp.where(diag, dsum_t[h:h + 1, :], sel)
                    mats.append((cb * jnp.exp2(sel)).astype(BF16))
                yd = jnp.dot(jnp.concatenate(mats, axis=1), xbd, preferred_element_type=F32)
                ys.append(yd + y_off[:, pr * LANES:(pr + 1) * LANES])
        y = jnp.concatenate(ys, axis=-1) + xs * dsk_ref[...]
        y = y * _silu(z_ref[0, pl.ds(r0, CHUNK), :].astype(F32))
        ms = jnp.mean(y * y, axis=-1, keepdims=True)
        o_ref[0, pl.ds(r0, CHUNK), :] = (y * lax.rsqrt(ms + EPS) * ng_ref[...]).astype(BF16)
        return carry

    lax.fori_loop(0, qb, body, 0, unroll=True)


def _ssd(u3, dt3, cw, cb, dtb, alog, rexp_f, rexp_b, dsk, ng, *, qb):
    B, L, _ = u3.shape
    rows = qb * CHUNK
    assert L % rows == 0
    nt = L // rows
    nc = L // CHUNK
    hb = rows // HALO
    nhb = L // HALO
    state = pltpu.VMEM((SSD_GROUPS, SSD_STATE, SSD_GW), F32)
    const = lambda shape: pl.BlockSpec(shape, lambda b, n: (0,) * len(shape))
    rt = lambda n: nt - 1 - n

    xact, prevb = pl.pallas_call(
        functools.partial(_ssd_bwd_kernel, nt=nt, qb=qb),
        grid=(B, nt),
        in_specs=[
            pl.BlockSpec((1, rows, SSD_XBC), lambda b, n: (b, rt(n), C_XBC // SSD_XBC)),
            pl.BlockSpec((1, HALO, SSD_XBC),
                         lambda b, n: (b, jnp.maximum(rt(n) * hb - 1, 0), C_XBC // SSD_XBC)),
            pl.BlockSpec((1, HALO, SSD_XBC),
                         lambda b, n: (b, jnp.minimum((rt(n) + 1) * hb, nhb - 1), C_XBC // SSD_XBC)),
            pl.BlockSpec((1, rows, DT_PAD), lambda b, n: (b, rt(n), 0)),
            const((8, SSD_XBC)), const((1, SSD_XBC)), const((1, DT_PAD)), const((1, DT_PAD)),
            const((LANES, SSD_WIDTH)),
        ],
        out_specs=[pl.BlockSpec((1, rows, SSD_XBC), lambda b, n: (b, rt(n), 0)),
                   pl.BlockSpec((1, qb, SSD_GROUPS, SSD_STATE, SSD_GW),
                                lambda b, n: (b, rt(n), 0, 0, 0))],
        out_shape=[jax.ShapeDtypeStruct((B, L, SSD_XBC), BF16),
                   jax.ShapeDtypeStruct((B, nc, SSD_GROUPS, SSD_STATE, SSD_GW), BF16)],
        scratch_shapes=[pltpu.VMEM((rows + 2 * HALO, SSD_XBC), BF16),
                        pltpu.VMEM((rows, SSD_XBC), F32), state],
        compiler_params=_cparams(("parallel", "arbitrary")),
        name="ssd_bwd",
    )(u3, u3, u3, dt3, cw, cb, dtb, alog, rexp_b)

    return pl.pallas_call(
        functools.partial(_ssd_fwd_kernel, qb=qb),
        grid=(B, nt),
        in_specs=[
            pl.BlockSpec((1, rows, SSD_XBC), lambda b, n: (b, n, 0)),
            pl.BlockSpec((1, rows, DT_PAD), lambda b, n: (b, n, 0)),
            pl.BlockSpec((1, qb, SSD_GROUPS, SSD_STATE, SSD_GW), lambda b, n: (b, n, 0, 0, 0)),
            pl.BlockSpec((1, rows, SSD_WIDTH), lambda b, n: (b, n, C_Z // SSD_WIDTH)),
            const((1, DT_PAD)), const((1, DT_PAD)),
            const((LANES, SSD_WIDTH)), const((LANES, SSD_WIDTH)),
            const((1, SSD_WIDTH)), const((1, SSD_WIDTH)),
        ],
        out_specs=pl.BlockSpec((1, rows, SSD_WIDTH), lambda b, n: (b, n, 0)),
        out_shape=jax.ShapeDtypeStruct((B, L, SSD_WIDTH), BF16),
        scratch_shapes=[state],
        compiler_params=_cparams(("parallel", "arbitrary")),
        name="ssd_fwd",
    )(xact, dt3, prevb, u3, dtb, alog, rexp_f, rexp_b, dsk, ng)


def _ssd_expand_matrix(rev):
    m = np.zeros((LANES, SSD_WIDTH), np.float32)
    hoff = SSD_HEADS if rev else 0
    for h in range(SSD_HEADS):
        m[hoff + h, h * SSD_HEAD_DIM:(h + 1) * SSD_HEAD_DIM] = 1.0
    return jnp.asarray(m, BF16)


def _hy_filter_kernel(t_ref, bands_ref, w1t_ref, w1c_ref, w1s_ref, b1_ref, w2_ref, b2_ref, w3_ref,
                      b3_ref, w4_ref, fr_ref, absd_ref, x_ref, s_ref, *, L, tl):
    i = pl.program_id(0)
    pos = (lax.broadcasted_iota(jnp.int32, (tl, 1), 0) + i * tl).astype(F32)
    t = t_ref[...]
    pos_row = (lax.broadcasted_iota(jnp.int32, (1, tl), 1) + i * tl).astype(F32)
    ang_t = 2.0 * math.pi * pos_row * bands_ref[...] / L
    fr = fr_ref[...]
    dot = functools.partial(jnp.dot, preferred_element_type=F32, precision=HIGHEST)
    tdot = lambda a, b: lax.dot_general(a, b, (((0,), (0,)), ((), ())), preferred_element_type=F32,
                                        precision=HIGHEST)
    pre = t * w1t_ref[...] + tdot(jnp.cos(ang_t), w1c_ref[...]) + tdot(-jnp.sin(ang_t), w1s_ref[...])
    h = jnp.sin(fr * (pre + b1_ref[...]))
    h = jnp.sin(fr * (dot(h, w2_ref[...]) + b2_ref[...]))
    h = jnp.sin(fr * (dot(h, w3_ref[...]) + b3_ref[...]))
    w4 = w4_ref[...]
    h_hi, w_hi = h.astype(BF16), w4.astype(BF16)
    h_lo, w_lo = (h - h_hi.astype(F32)).astype(BF16), (w4 - w_hi.astype(F32)).astype(BF16)
    bdot = functools.partial(jnp.dot, preferred_element_type=F32)
    h = bdot(h_hi, w_hi) + bdot(h_hi, w_lo) + bdot(h_lo, w_hi)
    decay = jnp.exp(-t * absd_ref[...])
    hf = h[:, :HY_WIDTH] * decay
    hb = jnp.where(pos == 0.0, 0.0, h[:, HY_WIDTH:] * decay)
    _to_lane_blocks(s_ref, jnp.concatenate([hf + hb, hb - hf], axis=1))
    x_ref[...] = _split_even_odd(s_ref).astype(BF16)


def _hy_filter(L, w1, b1, w2, b2, w3, b3, w4, freq):
    tl = min(512, L)
    t = jnp.linspace(0.0, 1.0, L, dtype=F32)[:, None]
    bands = jnp.linspace(1e-4, HY_EMB_BANDS - 1, HY_EMB_BANDS, dtype=F32)[:, None]
    max_decay = math.log(HY_DECAY_TARGET) / HY_FAST_DECAY
    min_decay = math.log(HY_DECAY_TARGET) / HY_SLOW_DECAY
    absd = jnp.abs(jnp.linspace(min_decay, max_decay, HY_WIDTH, dtype=F32))[None, :]
    w1 = w1.astype(F32)
    full = lambda a: pl.BlockSpec(a.shape, lambda i: (0,) * a.ndim)
    args = [t, bands, w1[0:1], w1[1:1 + HY_EMB_BANDS], w1[1 + HY_EMB_BANDS:], b1[None], w2, b2[None],
            w3, b3[None], w4, freq[None], absd]
    in_specs = [pl.BlockSpec((tl, 1), lambda i: (i, 0))] + [full(a) for a in args[1:]]
    return pl.pallas_call(
        functools.partial(_hy_filter_kernel, L=L, tl=tl),
        grid=(L // tl,),
        in_specs=in_specs,
        out_specs=pl.BlockSpec((tl // 2, 4 * HY_WIDTH), lambda i: (i, 0)),
        out_shape=jax.ShapeDtypeStruct((L // 2, 4 * HY_WIDTH), BF16),
        scratch_shapes=[pltpu.VMEM((2 * HY_WIDTH // LANES, tl, LANES), F32)],
        compiler_params=_cparams(("parallel",)),
        name="hy_filter",
    )(*args)


def _dft_tables(L, hm):
    L2 = L // 2
    g = jnp.arange(L2, dtype=jnp.int32)
    s = jnp.arange(L2, dtype=jnp.int32)
    ph = ((2 * g + 1)[:, None] * s[None, :]) % (2 * L)
    ang = ph.astype(F32) * (math.pi / L)
    c = jnp.cos(ang).astype(BF16).reshape(L2 // hm, 1, hm, L2)
    sn = jnp.sin(ang).astype(BF16).reshape(L2 // hm, 1, hm, L2)
    a_fwd = jnp.concatenate([c, sn], axis=1).reshape(L, L2)
    a_inv = jnp.concatenate([c, -sn], axis=1).reshape(L, L2).T
    return a_fwd, a_inv


def _hy_kspec_kernel(a_ref, x_ref, wc_ref, ws_ref, f_ref, *, hm, scale):
    W = HY_WIDTH
    pq = jnp.dot(a_ref[...], x_ref[...], preferred_element_type=F32)
    ea, eb, oa, ob = [(pq[0:hm, i * W:(i + 1) * W], -pq[hm:, i * W:(i + 1) * W]) for i in range(4)]
    w = (wc_ref[...], -ws_ref[...])
    cmul = lambda x, y: (x[0] * y[0] - x[1] * y[1], x[0] * y[1] + x[1] * y[0])
    woa, wob = cmul(w, oa), cmul(w, ob)
    k1 = (ea[0] + woa[0], -(eb[1] + wob[1]))
    k2 = (ea[0] - woa[0], eb[1] - wob[1])
    kp = (k1[0] + k2[0], k1[1] - k2[1])
    km = (k1[0] - k2[0], k1[1] + k2[1])
    wkm = cmul(w, km)
    vkm = cmul((w[0], -w[1]), km)
    for i, part in enumerate((kp[0], kp[1], wkm[0], wkm[1], vkm[0], vkm[1])):
        f_ref[0, i] = part * scale


def _hy_kspec(a_fwd, xf2, *, hm):
    L, L2 = a_fwd.shape
    tm = 2 * hm
    nx = xf2.shape[1]
    theta = (2.0 * jnp.arange(L2, dtype=F32) + 1.0) * (math.pi / (2 * L))
    return pl.pallas_call(
        functools.partial(_hy_kspec_kernel, hm=hm, scale=1.0 / L),
        grid=(L // tm,),
        in_specs=[pl.BlockSpec((tm, L2), lambda i: (i, 0)),
                  pl.BlockSpec((L2, nx), lambda i: (0, 0)),
                  pl.BlockSpec((hm, 1), lambda i: (i, 0)),
                  pl.BlockSpec((hm, 1), lambda i: (i, 0))],
        out_specs=pl.BlockSpec((1, 6, hm, HY_WIDTH), lambda i: (i, 0, 0, 0)),
        out_shape=jax.ShapeDtypeStruct((L2 // hm, 6, hm, HY_WIDTH), F32),
        compiler_params=_cparams(("parallel",)),
        name="hy_kspec",
    )(a_fwd, xf2, jnp.cos(theta)[:, None], jnp.sin(theta)[:, None])


def _to_lane_blocks(s_ref, x):
    for c in range(s_ref.shape[0]):
        s_ref[c] = x[:, c * LANES:(c + 1) * LANES]


def _split_even_odd(s_ref):
    k, rows, _ = s_ref.shape
    return jnp.concatenate([s_ref[c, pl.ds(p, rows // 2, stride=2), :]
                            for p in range(2) for c in range(k)], axis=1)


def _hy_pre_kernel(xm_ref, xp_ref, xn_ref, cw_ref, cb_ref, u_ref, xau_ref, xpad_ref, s_ref, *, nt, tl):
    i = pl.program_id(1)
    _load_padded(xpad_ref, xm_ref, xp_ref, xn_ref, i > 0, i < nt - 1, tl)
    xa = _dwconv(xpad_ref, cw_ref, cb_ref, tl, HY_CONV, 0, HY_WIDTH)
    xb = _dwconv(xpad_ref, cw_ref, cb_ref, tl, HY_CONV, HY_WIDTH, 2 * HY_WIDTH)
    v = _dwconv(xpad_ref, cw_ref, cb_ref, tl, HY_CONV, 2 * HY_WIDTH, 3 * HY_WIDTH)
    u = xb * v
    xau_ref[0, :, :HY_WIDTH] = xa.astype(BF16)
    xau_ref[0, :, HY_WIDTH:] = u.astype(BF16)
    _to_lane_blocks(s_ref, u)
    u_ref[0] = _split_even_odd(s_ref).astype(BF16)


def _hy_tile_specs(L, tl):
    hb = tl // HALO
    nhb = L // HALO
    w = 3 * HY_WIDTH
    return [pl.BlockSpec((1, tl, w), lambda b, i: (b, i, C_HY // w)),
            pl.BlockSpec((1, HALO, w), lambda b, i: (b, jnp.maximum(i * hb - 1, 0), C_HY // w)),
            pl.BlockSpec((1, HALO, w), lambda b, i: (b, jnp.minimum((i + 1) * hb, nhb - 1), C_HY // w))]


def _hy_pre(u3, cw, cb, *, tl):
    B, L, _ = u3.shape
    nt = L // tl
    w = 3 * HY_WIDTH
    return pl.pallas_call(
        functools.partial(_hy_pre_kernel, nt=nt, tl=tl),
        grid=(B, nt),
        in_specs=_hy_tile_specs(L, tl) + [pl.BlockSpec((8, w), lambda b, i: (0, 0)),
                                          pl.BlockSpec((1, w), lambda b, i: (0, 0))],
        out_specs=[pl.BlockSpec((1, tl // 2, 2 * HY_WIDTH), lambda b, i: (b, i, 0)),
                   pl.BlockSpec((1, tl, 2 * HY_WIDTH), lambda b, i: (b, i, 0))],
        out_shape=[jax.ShapeDtypeStruct((B, L // 2, 2 * HY_WIDTH), BF16),
                   jax.ShapeDtypeStruct((B, L, 2 * HY_WIDTH), BF16)],
        scratch_shapes=[pltpu.VMEM((tl + 2 * HALO, w), BF16),
                        pltpu.VMEM((HY_WIDTH // LANES, tl, LANES), F32)],
        compiler_params=_cparams(("parallel", "parallel")),
        name="hy_pre",
    )(u3, u3, u3, cw, cb)


def _hy_fwd_kernel(a_ref, u_ref, f_ref, y_ref, *, hm):
    W = HY_WIDTH
    pq = jnp.dot(a_ref[...], u_ref[0], preferred_element_type=F32)
    pe, po = pq[0:hm, :W], pq[0:hm, W:]
    qe, qo = pq[hm:, :W], pq[hm:, W:]
    kpr, kpi, wr, wi, vr, vi = [f_ref[0, i] for i in range(6)]
    y_ref[0, 0:hm, :W] = (pe * kpr + qe * kpi + po * wr + qo * wi).astype(BF16)
    y_ref[0, hm:, :W] = (pe * kpi - qe * kpr + po * wi - qo * wr).astype(BF16)
    y_ref[0, 0:hm, W:] = (pe * vr + qe * vi + po * kpr + qo * kpi).astype(BF16)
    y_ref[0, hm:, W:] = (pe * vi - qe * vr + po * kpi - qo * kpr).astype(BF16)


def _hy_fwd(a_fwd, ueo, filt, *, hm):
    B, L2, W2 = ueo.shape
    L = 2 * L2
    tm = 2 * hm
    return pl.pallas_call(
        functools.partial(_hy_fwd_kernel, hm=hm),
        grid=(L // tm, B),
        in_specs=[pl.BlockSpec((tm, L2), lambda i, b: (i, 0)),
                  pl.BlockSpec((1, L2, W2), lambda i, b: (b, 0, 0)),
                  pl.BlockSpec((1, 6, hm, HY_WIDTH), lambda i, b: (i, 0, 0, 0))],
        out_specs=pl.BlockSpec((1, tm, W2), lambda i, b: (b, i, 0)),
        out_shape=jax.ShapeDtypeStruct((B, L, W2), BF16),
        compiler_params=_cparams(("parallel", "parallel")),
        name="hy_fwd",
    )(a_fwd, ueo, filt)


def _hy_inv_kernel(a_ref, y_ref, o_ref):
    o_ref[0] = jnp.dot(a_ref[...], y_ref[0], preferred_element_type=F32)


def _hy_inv(a_inv, yspec, *, tm):
    B, L, W2 = yspec.shape
    L2 = L // 2
    return pl.pallas_call(
        _hy_inv_kernel,
        grid=(L2 // tm, B),
        in_specs=[pl.BlockSpec((tm, L), lambda i, b: (i, 0)),
                  pl.BlockSpec((1, L, W2), lambda i, b: (b, 0, 0))],
        out_specs=pl.BlockSpec((1, tm, W2), lambda i, b: (b, i, 0)),
        out_shape=jax.ShapeDtypeStruct((B, L2, W2), F32),
        compiler_params=_cparams(("parallel", "parallel")),
        name="hy_inv",
    )(a_inv, yspec)


def _pad_rows(a, rows):
    return jnp.concatenate([a, jnp.zeros((rows - a.shape[0],) + a.shape[1:], a.dtype)], axis=0)


def _pad_cols(a, cols):
    return jnp.concatenate([a, jnp.zeros(a.shape[:-1] + (cols - a.shape[-1],), a.dtype)], axis=-1)


def _static_take(a, idx, axis):
    idx = np.asarray(idx)
    cuts = np.flatnonzero(np.diff(idx) != 1) + 1
    starts = np.concatenate([[0], cuts])
    ends = np.concatenate([cuts, [len(idx)]])
    parts = [lax.slice_in_dim(a, int(idx[s]), int(idx[e - 1]) + 1, axis=axis)
             for s, e in zip(starts, ends)]
    return jnp.concatenate(parts, axis=axis)


def _tile(n, pref):
    t = min(pref, n)
    assert n % t == 0
    return t


def _layer(x3, p, tables, *, final, final_g):
    B, L, _ = x3.shape
    T = B * L
    x2 = x3.reshape(T, D_MODEL)
    u, dt = _inproj(x2, p["norm_g"], p["w_in"], tm=_tile(T, 512), tn=512)
    u3 = u.reshape(B, L, N_IN)
    dt3 = dt.reshape(B, L, DT_PAD)

    ys = _ssd(u3, dt3, p["ssd_cw"], p["ssd_cb"], p["ssd_dtb"], p["ssd_alog"], p["rexp_f"],
              p["rexp_b"], p["ssd_dskip"], p["ssd_ng"], qb=_tile(L // CHUNK, 4))

    ya = _attention(u3, p["att_bias"], p["att_sink"], p["att_ng"], qb=_tile(L // ATT_BLOCK, 8))

    a_fwd, a_inv, filt, hm = tables
    tl = _tile(L, 1024)
    ueo, xau = _hy_pre(u3, p["hy_cw"], p["hy_cb"], tl=tl)
    yspec = _hy_fwd(a_fwd, ueo, filt, hm=hm)
    conv = _hy_inv(a_inv, yspec, tm=_tile(L // 2, 512))

    out = _outproj(x2, ys.reshape(T, SSD_WIDTH), ya.reshape(T, ATT_WIDTH),
                   conv.reshape(T // 2, 2 * HY_WIDTH), xau.reshape(T, 2 * HY_WIDTH), u,
                   p["hy_d"], p["hy_ng"], p["w_out"], final_g, tm=_tile(T, 512), final=final)
    return out.reshape(B, L, D_MODEL)


def kernel(x_prompt, x_sample, rel_bias, norm_g, w_in, ssd_conv_w, ssd_conv_b, ssd_dt_bias, ssd_a_log, ssd_d, ssd_norm_g, att_sink, att_norm_g, hy_conv_w, hy_conv_b, hy_w1, hy_b1, hy_w2, hy_b2, hy_w3, hy_b3, hy_w4, hy_freq, hy_d, hy_norm_g, w_out, final_norm_g):
    depth = w_in.shape[0]
    perm, col_scale = _in_perm()
    n_real = C_DT + 2 * SSD_HEADS
    scale_old = np.ones((IN_COLS,), np.float32)
    scale_old[perm[:n_real]] = col_scale[:n_real]
    w_in_p = [_pad_cols(_static_take((w_in[i] * jnp.asarray(scale_old)).astype(BF16), perm[:n_real], 1),
                        N_IN) for i in range(depth)]
    att_perm = _att_col_perm()
    out_rows = np.concatenate([np.arange(SSD_WIDTH), SSD_WIDTH + att_perm,
                               np.arange(SSD_WIDTH + ATT_WIDTH, D_MODEL)])
    w_out_b = [_static_take(w_out[i].astype(BF16), out_rows, 0) for i in range(depth)]
    att_bias = _attn_bias(rel_bias)
    rexp_f, rexp_b = _ssd_expand_matrix(False), _ssd_expand_matrix(True)
    final_g = final_norm_g.astype(F32)[None, :]

    layers = []
    for i in range(depth):
        layers.append(dict(
            norm_g=norm_g[i].astype(F32)[None, :],
            w_in=w_in_p[i],
            ssd_cw=_pad_rows(ssd_conv_w[i].astype(F32), 8),
            ssd_cb=ssd_conv_b[i].astype(F32)[None, :],
            ssd_dtb=_pad_cols(ssd_dt_bias[i].astype(F32).reshape(1, 2 * SSD_HEADS), DT_PAD),
            ssd_alog=_pad_cols(ssd_a_log[i].astype(F32).reshape(1, 2 * SSD_HEADS), DT_PAD),
            ssd_dskip=jnp.repeat(ssd_d[i].astype(F32), SSD_HEAD_DIM)[None, :],
            ssd_ng=ssd_norm_g[i].astype(F32)[None, :],
            rexp_f=rexp_f, rexp_b=rexp_b,
            att_bias=att_bias,
            att_sink=jnp.broadcast_to(att_sink[i].astype(F32)[:, None] * LOG2E, (ATT_HEADS, LANES)),
            att_ng=att_norm_g[i].astype(F32)[att_perm][None, :],
            hy_cw=_pad_rows(hy_conv_w[i].astype(F32), 8),
            hy_cb=hy_conv_b[i].astype(F32)[None, :],
            hy_d=hy_d[i].astype(F32)[None, :],
            hy_ng=hy_norm_g[i].astype(F32)[None, :],
            w_out=w_out_b[i],
        ))

    def trunk(x):
        L = x.shape[1]
        hm = min(512, L // 4)
        a_fwd, a_inv = _dft_tables(L, hm)
        for i in range(depth):
            xf = _hy_filter(L, hy_w1[i], hy_b1[i].astype(F32), hy_w2[i].astype(F32),
                            hy_b2[i].astype(F32), hy_w3[i].astype(F32), hy_b3[i].astype(F32),
                            hy_w4[i].astype(F32), hy_freq[i].astype(F32))
            filt = _hy_kspec(a_fwd, xf, hm=hm)
            x = _layer(x, layers[i], (a_fwd, a_inv, filt, hm), final=(i == depth - 1),
                       final_g=final_g)
        return x

    return (trunk(x_prompt), trunk(x_sample))
```

```python
import functools
import math

import jax
import jax.numpy as jnp
import numpy as np
from jax import lax
from jax.experimental import pallas as pl
from jax.experimental.pallas import tpu as pltpu

F32 = jnp.float32
BF16 = jnp.bfloat16
HIGHEST = lax.Precision.HIGHEST

D_MODEL = 2048
SSD_WIDTH = 1024
ATT_WIDTH = 512
HY_WIDTH = 512
SSD_HEAD_DIM = 64
SSD_HEADS = 16
SSD_GROUPS = 2
SSD_STATE = 128
SSD_CONV = 5
CHUNK = 128
SSD_XBC = SSD_WIDTH + 2 * SSD_GROUPS * SSD_STATE
ATT_HEAD_DIM = 64
ATT_HEADS = 8
ATT_KV_HEADS = 2
ATT_REP = ATT_HEADS // ATT_KV_HEADS
ATT_WINDOW = 128
ATT_BLOCK = 128
REL_BUCKETS = 32
REL_MAX_DIST = 128
HY_CONV = 3
HY_EMB_BANDS = 16
HY_FF = 64
HY_FAST_DECAY = 0.3
HY_SLOW_DECAY = 1.5
HY_DECAY_TARGET = 1e-2
EPS = 1e-6
NEG_BIG = -1e30

LANES = 128
BF16_SUBLANES = 16
VMEM_LIMIT = 56 * 1024 * 1024

C_XBC = 0
C_HY = 1536
C_Z = 3072
C_Q = 4096
C_GATT = 4608
C_GHY = 5120
C_K = 5632
C_V = 5760
C_DT = 5888
DT_PAD = 128
N_IN = C_DT + DT_PAD

_OLD_SIZES = [SSD_WIDTH, SSD_XBC, 2 * SSD_HEADS, ATT_WIDTH, ATT_KV_HEADS * ATT_HEAD_DIM,
              ATT_KV_HEADS * ATT_HEAD_DIM, ATT_WIDTH, 3 * HY_WIDTH, HY_WIDTH]
_OLD_OFF = np.concatenate([[0], np.cumsum(_OLD_SIZES)])
IN_COLS = int(_OLD_OFF[-1])


LOG2E = math.log2(math.e)
Q_SCALE = ATT_HEAD_DIM ** -0.5 * LOG2E


def _att_col_perm():
    order = [h for j in range(ATT_REP) for h in (j, ATT_REP + j)]
    return np.concatenate([np.arange(h * ATT_HEAD_DIM, (h + 1) * ATT_HEAD_DIM) for h in order])


def _in_perm():
    perm = np.full((N_IN,), IN_COLS, np.int32)
    scale = np.ones((N_IN,), np.float32)
    o = {n: int(_OLD_OFF[i]) for i, n in enumerate(
        ["z", "xbc", "dt", "q", "k", "v", "gatt", "hy", "ghy"])}
    def put(new, old, width):
        perm[new:new + width] = np.arange(old, old + width)
    put(C_XBC, o["xbc"], SSD_XBC)
    put(C_HY, o["hy"], 3 * HY_WIDTH)
    put(C_Z, o["z"], SSD_WIDTH)
    perm[C_Q:C_Q + ATT_WIDTH] = o["q"] + _att_col_perm()
    scale[C_Q:C_Q + ATT_WIDTH] = Q_SCALE
    perm[C_GATT:C_GATT + ATT_WIDTH] = o["gatt"] + _att_col_perm()
    put(C_GHY, o["ghy"], HY_WIDTH)
    put(C_K, o["k"], 128)
    put(C_V, o["v"], 128)
    put(C_DT, o["dt"], 2 * SSD_HEADS)
    return perm, scale


def _cparams(sem):
    return pltpu.CompilerParams(dimension_semantics=sem, vmem_limit_bytes=VMEM_LIMIT)


def _silu(x):
    return x * (1.0 / (1.0 + jnp.exp(-x)))


def _softplus(x):
    return jnp.maximum(x, 0.0) + jnp.log1p(jnp.exp(-jnp.abs(x)))


def _inproj_kernel(x_ref, g_ref, w_ref, u_ref, dt_ref, h_ref, *, rows, tn):
    for r in range(x_ref.shape[0] // rows):
        sl = slice(r * rows, (r + 1) * rows)
        x = x_ref[sl, :]
        ms = jnp.mean(x * x, axis=-1, keepdims=True)
        h_ref[sl, :] = (x * lax.rsqrt(ms + EPS) * g_ref[...]).astype(BF16)
        h = h_ref[sl, :]
        for c0 in range(0, N_IN, tn):
            c1 = min(c0 + tn, N_IN)
            acc = jnp.dot(h, w_ref[:, c0:c1], preferred_element_type=F32)
            u_ref[sl, c0:c1] = acc.astype(BF16)
            if c0 <= C_DT < c1:
                dt_ref[sl, :] = acc[:, C_DT - c0:C_DT - c0 + DT_PAD]


def _inproj(x2, g, w, *, tm, tn):
    T = x2.shape[0]
    assert T % tm == 0
    return pl.pallas_call(
        functools.partial(_inproj_kernel, rows=min(256, tm), tn=tn),
        grid=(T // tm,),
        in_specs=[pl.BlockSpec((tm, D_MODEL), lambda i: (i, 0)),
                  pl.BlockSpec((1, D_MODEL), lambda i: (0, 0)),
                  pl.BlockSpec((D_MODEL, N_IN), lambda i: (0, 0), pipeline_mode=pl.Buffered(1))],
        out_specs=[pl.BlockSpec((tm, N_IN), lambda i: (i, 0)),
                   pl.BlockSpec((tm, DT_PAD), lambda i: (i, 0))],
        out_shape=[jax.ShapeDtypeStruct((T, N_IN), BF16),
                   jax.ShapeDtypeStruct((T, DT_PAD), F32)],
        scratch_shapes=[pltpu.VMEM((tm, D_MODEL), BF16)],
        compiler_params=_cparams(("parallel",)),
        name="inproj",
    )(x2, g, w)


def _outproj_kernel(x_ref, ys_ref, ya_ref, c_ref, xau_ref, g_ref, d_ref, hng_ref, w_ref, fg_ref,
                    o_ref, s_ref, *, rows, final):
    nblk = HY_WIDTH // LANES
    for r in range(x_ref.shape[0] // rows):
        sl = slice(r * rows, (r + 1) * rows)
        hs = slice(r * rows // 2, (r + 1) * rows // 2)
        for p in range(2):
            for c in range(nblk):
                s_ref[c, pl.ds(p, rows // 2, stride=2), :] = c_ref[hs, (p * nblk + c) * LANES:
                                                                   (p * nblk + c + 1) * LANES]
        conv = jnp.concatenate([s_ref[c] for c in range(nblk)], axis=1)
        xa = xau_ref[sl, :HY_WIDTH].astype(F32)
        u = xau_ref[sl, HY_WIDTH:].astype(F32)
        y = xa * (conv + u * d_ref[...])
        y = y * _silu(g_ref[sl, :].astype(F32))
        ms = jnp.mean(y * y, axis=-1, keepdims=True)
        yh = (y * lax.rsqrt(ms + EPS) * hng_ref[...]).astype(BF16)

        acc = x_ref[sl, :]
        acc = acc + jnp.dot(ys_ref[sl, :], w_ref[0:SSD_WIDTH, :], preferred_element_type=F32)
        acc = acc + jnp.dot(ya_ref[sl, :], w_ref[SSD_WIDTH:SSD_WIDTH + ATT_WIDTH, :],
                            preferred_element_type=F32)
        acc = acc + jnp.dot(yh, w_ref[SSD_WIDTH + ATT_WIDTH:, :], preferred_element_type=F32)
        if final:
            ms = jnp.mean(acc * acc, axis=-1, keepdims=True)
            acc = acc * lax.rsqrt(ms + EPS) * fg_ref[...]
        o_ref[sl, :] = acc


def _outproj(x2, ys, ya, conv2, xau2, u2, d, hng, w, fg, *, tm, final):
    T = x2.shape[0]
    assert T % tm == 0
    rows = min(256, tm)
    return pl.pallas_call(
        functools.partial(_outproj_kernel, rows=rows, final=final),
        grid=(T // tm,),
        in_specs=[pl.BlockSpec((tm, D_MODEL), lambda i: (i, 0)),
                  pl.BlockSpec((tm, SSD_WIDTH), lambda i: (i, 0)),
                  pl.BlockSpec((tm, ATT_WIDTH), lambda i: (i, 0)),
                  pl.BlockSpec((tm // 2, 2 * HY_WIDTH), lambda i: (i, 0)),
                  pl.BlockSpec((tm, 2 * HY_WIDTH), lambda i: (i, 0)),
                  pl.BlockSpec((tm, HY_WIDTH), lambda i: (i, C_GHY // HY_WIDTH)),
                  pl.BlockSpec((1, HY_WIDTH), lambda i: (0, 0)),
                  pl.BlockSpec((1, HY_WIDTH), lambda i: (0, 0)),
                  pl.BlockSpec((D_MODEL, D_MODEL), lambda i: (0, 0)),
                  pl.BlockSpec((1, D_MODEL), lambda i: (0, 0))],
        out_specs=pl.BlockSpec((tm, D_MODEL), lambda i: (i, 0)),
        out_shape=jax.ShapeDtypeStruct((T, D_MODEL), F32),
        scratch_shapes=[pltpu.VMEM((HY_WIDTH // LANES, rows, LANES), F32)],
        compiler_params=_cparams(("parallel",)),
        name="outproj",
    )(x2, ys, ya, conv2, xau2, u2, d, hng, w, fg)


def _t5_buckets(rel):
    nb = REL_BUCKETS // 2
    max_exact = nb // 2
    ret = (rel > 0).astype(np.int32) * nb
    n = np.abs(rel)
    large = max_exact + (np.log(np.maximum(n, 1) / max_exact) / math.log(REL_MAX_DIST / max_exact)
                         * (nb - max_exact)).astype(np.int32)
    large = np.minimum(large, nb - 1)
    return ret + np.where(n < max_exact, n, large)


def _attn_bias(rel_bias):
    qi = np.arange(ATT_BLOCK)[:, None]
    kj = np.arange(3 * ATT_BLOCK)[None, :]
    rel = kj - ATT_BLOCK - qi
    onehot = (_t5_buckets(rel)[None] == np.arange(REL_BUCKETS)[:, None, None]).astype(np.float32)
    bias = jnp.einsum("bqk,bh->hqk", jnp.asarray(onehot), rel_bias.astype(F32),
                      precision=HIGHEST) * LOG2E
    window = np.abs(rel) <= ATT_WINDOW
    variants = []
    for last in (False, True):
        for first in (False, True):
            ok = window & ~(first & (kj < ATT_BLOCK)) & ~(last & (kj >= 2 * ATT_BLOCK))
            variants.append(jnp.where(ok[None], bias, NEG_BIG))
    return jnp.transpose(jnp.stack(variants), (0, 1, 3, 2))


def _attn_kernel(q_ref, kp_ref, kc_ref, kn_ref, vp_ref, vc_ref, vn_ref, g_ref, bias_ref, sink_ref,
                 ng_ref, o_ref, klo_ref, khi_ref, vt_ref, *, nt, qb):
    n = pl.program_id(1)
    lo = lax.broadcasted_iota(jnp.int32, (1, LANES), 1) < ATT_HEAD_DIM
    zero = jnp.zeros((), BF16)
    kext = jnp.concatenate([kp_ref[0], kc_ref[0], kn_ref[0]], axis=0)
    klo_ref[...] = jnp.where(lo, kext, zero)
    khi_ref[...] = jnp.where(lo, zero, kext)
    for t, ref, cnt in ((0, vp_ref, 1), (1, vc_ref, qb), (qb + 1, vn_ref, 1)):
        for i in range(cnt):
            blk = ref[0, i * ATT_BLOCK:(i + 1) * ATT_BLOCK, :]
            vt_ref[t + i] = blk.astype(F32).T.astype(BF16)
    row_lo = lax.broadcasted_iota(jnp.int32, (LANES, ATT_BLOCK), 0) < ATT_HEAD_DIM
    nk = 3 * ATT_BLOCK

    def body(i, carry):
        r0 = pl.multiple_of(i * ATT_BLOCK, ATT_BLOCK)
        q = q_ref[0, pl.ds(r0, ATT_BLOCK), :]
        kst = jnp.concatenate([klo_ref[pl.ds(r0, nk), :], khi_ref[pl.ds(r0, nk), :]], axis=0)
        vt = jnp.concatenate([vt_ref[i], vt_ref[i + 1], vt_ref[i + 2]], axis=1)
        first = jnp.logical_and(n == 0, i == 0)
        last = jnp.logical_and(n == nt - 1, i == qb - 1)
        variant = first.astype(jnp.int32) + 2 * last.astype(jnp.int32)
        outs = []
        for j in range(ATT_REP):
            qp = q[:, j * LANES:(j + 1) * LANES]
            st = lax.dot_general(kst, qp, (((1,), (1,)), ((), ())), preferred_element_type=F32)
            halves = []
            for e, h in enumerate((j, ATT_REP + j)):
                s = st[e * nk:(e + 1) * nk] + bias_ref[variant, h]
                sk = sink_ref[h:h + 1, :]
                m = jnp.maximum(jnp.max(s, axis=0, keepdims=True), sk)
                p = jnp.exp2(s - m)
                den = jnp.sum(p, axis=0, keepdims=True) + jnp.exp2(sk - m)
                ot = jnp.dot(vt, p.astype(BF16), preferred_element_type=F32)
                halves.append(ot * (1.0 / den))
            outs.append(jnp.where(row_lo, halves[0], halves[1]).T)
        o = jnp.concatenate(outs, axis=-1)
        y = o * _silu(g_ref[0, pl.ds(r0, ATT_BLOCK), :].astype(F32))
        ms = jnp.mean(y * y, axis=-1, keepdims=True)
        o_ref[0, pl.ds(r0, ATT_BLOCK), :] = (y * lax.rsqrt(ms + EPS) * ng_ref[...]).astype(BF16)
        return carry

    lax.fori_loop(0, qb, body, 0, unroll=True)


def _attention(u3, bias, sink, ng, *, qb):
    B, L, _ = u3.shape
    tq = qb * ATT_BLOCK
    assert L % tq == 0
    nt = L // tq
    nb = L // ATT_BLOCK
    kcol, vcol = C_K // 128, C_V // 128
    def kv_specs(colblk):
        return [pl.BlockSpec((1, ATT_BLOCK, 128), lambda b, n: (b, jnp.maximum(n * qb - 1, 0), colblk)),
                pl.BlockSpec((1, tq, 128), lambda b, n: (b, n, colblk)),
                pl.BlockSpec((1, ATT_BLOCK, 128),
                             lambda b, n: (b, jnp.minimum((n + 1) * qb, nb - 1), colblk))]
    return pl.pallas_call(
        functools.partial(_attn_kernel, nt=nt, qb=qb),
        grid=(B, nt),
        in_specs=[pl.BlockSpec((1, tq, ATT_WIDTH), lambda b, n: (b, n, C_Q // ATT_WIDTH))]
                 + kv_specs(kcol) + kv_specs(vcol)
                 + [pl.BlockSpec((1, tq, ATT_WIDTH), lambda b, n: (b, n, C_GATT // ATT_WIDTH)),
                    pl.BlockSpec((4, ATT_HEADS, 3 * ATT_BLOCK, ATT_BLOCK), lambda b, n: (0, 0, 0, 0)),
                    pl.BlockSpec((ATT_HEADS, LANES), lambda b, n: (0, 0)),
                    pl.BlockSpec((1, ATT_WIDTH), lambda b, n: (0, 0))],
        out_specs=pl.BlockSpec((1, tq, ATT_WIDTH), lambda b, n: (b, n, 0)),
        out_shape=jax.ShapeDtypeStruct((B, L, ATT_WIDTH), BF16),
        scratch_shapes=[pltpu.VMEM(((qb + 2) * ATT_BLOCK, LANES), BF16),
                        pltpu.VMEM(((qb + 2) * ATT_BLOCK, LANES), BF16),
                        pltpu.VMEM((qb + 2, LANES, ATT_BLOCK), BF16)],
        compiler_params=_cparams(("parallel", "parallel")),
        name="attn",
    )(u3, u3, u3, u3, u3, u3, u3, u3, bias, sink, ng)


HALO = BF16_SUBLANES


CONV_BLK = 128


def _load_padded(xpad_ref, xm_ref, xp_ref, xn_ref, has_prev, has_next, rows):
    zero = jnp.zeros((), BF16)
    xpad_ref[pl.ds(0, HALO), :] = jnp.where(has_prev, xp_ref[0], zero)
    xpad_ref[pl.ds(HALO, rows), :] = xm_ref[0]
    xpad_ref[pl.ds(HALO + rows, HALO), :] = jnp.where(has_next, xn_ref[0], zero)


def _shift_matrix(width):
    offs = [k - width // 2 for k in range(width) if k != width // 2]
    r = lax.broadcasted_iota(jnp.int32, (CONV_BLK, CONV_BLK + 2 * HALO), 0)
    c = lax.broadcasted_iota(jnp.int32, (CONV_BLK, CONV_BLK + 2 * HALO), 1)
    return jnp.concatenate([(c == r + HALO + d).astype(BF16) for d in offs], axis=0)


def _dwconv_block(xpad_ref, shifts, w_ref, b_ref, j, width, c0, c1):
    win = xpad_ref[j * CONV_BLK:(j + 1) * CONV_BLK + 2 * HALO, c0:c1]
    moved = jnp.dot(shifts, win, preferred_element_type=F32)
    acc = win[HALO:HALO + CONV_BLK].astype(F32) * w_ref[width // 2:width // 2 + 1, c0:c1]
    i = 0
    for k in range(width):
        if k == width // 2:
            continue
        acc = acc + moved[i * CONV_BLK:(i + 1) * CONV_BLK] * w_ref[k:k + 1, c0:c1]
        i += 1
    return acc + b_ref[:, c0:c1]


def _dwconv(xpad_ref, w_ref, b_ref, rows, width, c0, c1):
    shifts = _shift_matrix(width)
    return jnp.concatenate([_dwconv_block(xpad_ref, shifts, w_ref, b_ref, j, width, c0, c1)
                            for j in range(rows // CONV_BLK)], axis=0)


SSD_HPG = SSD_HEADS // SSD_GROUPS
SSD_GW = SSD_HPG * SSD_HEAD_DIM
SSD_BC = SSD_GROUPS * SSD_STATE


def _ssd_decay(dt_raw, dtb_ref, alog_ref):
    dt = _softplus(dt_raw + dtb_ref[...])
    dta = dt * (-jnp.exp(alog_ref[...]))
    row = lax.broadcasted_iota(jnp.int32, (CHUNK, CHUNK), 0)
    col = lax.broadcasted_iota(jnp.int32, (CHUNK, CHUNK), 1)
    tril = (row >= col).astype(BF16)
    hi = dta.astype(BF16)
    r1 = dta - hi.astype(F32)
    mid = r1.astype(BF16)
    lo = (r1 - mid.astype(F32)).astype(BF16)
    pre = (jnp.dot(tril, hi, preferred_element_type=F32) + jnp.dot(tril, mid, preferred_element_type=F32)
           + jnp.dot(tril, lo, preferred_element_type=F32))
    tot = pre[CHUNK - 1:CHUNK, :]
    is_bwd = lax.broadcasted_iota(jnp.int32, (1, LANES), 1) >= SSD_HEADS
    acs = jnp.where(is_bwd, tot - pre + dta, pre)
    return dt, acs, tot


def _ssd_bwd_kernel(xm_ref, xp_ref, xn_ref, dt_ref, cw_ref, cb_ref, dtb_ref, alog_ref, rexp_ref,
                    xact_ref, prev_ref, xpad_ref, xf_ref, st_ref, *, nt, qb):
    n = pl.program_id(1)
    tile = nt - 1 - n
    rows = qb * CHUNK

    @pl.when(n == 0)
    def _():
        st_ref[...] = jnp.zeros(st_ref.shape, F32)

    _load_padded(xpad_ref, xm_ref, xp_ref, xn_ref, tile > 0, tile < nt - 1, rows)
    xact = _silu(_dwconv(xpad_ref, cw_ref, cb_ref, rows, SSD_CONV, 0, SSD_XBC))
    xf_ref[...] = xact
    xact_ref[0] = xact.astype(BF16)

    def body(j, carry):
        i = qb - 1 - j
        r0 = pl.multiple_of(i * CHUNK, CHUNK)
        xs = xf_ref[pl.ds(r0, CHUNK), 0:SSD_WIDTH]
        bm = xf_ref[pl.ds(r0, CHUNK), SSD_WIDTH:SSD_WIDTH + SSD_BC].astype(BF16)
        dt, acs, tot = _ssd_decay(dt_ref[0, pl.ds(r0, CHUNK), :], dtb_ref, alog_ref)
        e_in = jnp.concatenate([jnp.exp(tot - acs) * dt, jnp.broadcast_to(jnp.exp(tot), (8, LANES))],
                               axis=0).astype(BF16)
        e_out = jnp.dot(e_in, rexp_ref[...], preferred_element_type=F32)
        xd = (xs * e_out[0:CHUNK]).astype(BF16)
        cdec = e_out[CHUNK:CHUNK + 1]
        for g in range(SSD_GROUPS):
            prev = st_ref[g]
            prev_ref[0, i, g] = prev.astype(BF16)
            s_new = lax.dot_general(bm[:, g * SSD_STATE:(g + 1) * SSD_STATE],
                                    xd[:, g * SSD_GW:(g + 1) * SSD_GW], (((0,), (0,)), ((), ())),
                                    preferred_element_type=F32)
            st_ref[g] = prev * cdec[:, g * SSD_GW:(g + 1) * SSD_GW] + s_new
        return carry

    lax.fori_loop(0, qb, body, 0, unroll=True)


def _ssd_fwd_kernel(xa_ref, dt_ref, prev_ref, z_ref, dtb_ref, alog_ref, rexpf_ref, rexpb_ref,
                    dsk_ref, ng_ref, o_ref, st_ref, *, qb):
    n = pl.program_id(1)

    @pl.when(n == 0)
    def _():
        st_ref[...] = jnp.zeros(st_ref.shape, F32)

    row = lax.broadcasted_iota(jnp.int32, (CHUNK, CHUNK), 0)
    col = lax.broadcasted_iota(jnp.int32, (CHUNK, CHUNK), 1)
    fwd_part = row > col
    diag = row == col
    lane_lo = lax.broadcasted_iota(jnp.int32, (1, LANES), 1) < SSD_HEAD_DIM
    zero_b = jnp.zeros((), BF16)

    def body(i, carry):
        r0 = pl.multiple_of(i * CHUNK, CHUNK)
        xs_b = xa_ref[0, pl.ds(r0, CHUNK), 0:SSD_WIDTH]
        xs = xs_b.astype(F32)
        bm = xa_ref[0, pl.ds(r0, CHUNK), SSD_WIDTH:SSD_WIDTH + SSD_BC]
        cm = xa_ref[0, pl.ds(r0, CHUNK), SSD_WIDTH + SSD_BC:SSD_XBC]
        dt, acs, tot = _ssd_decay(dt_ref[0, pl.ds(r0, CHUNK), :], dtb_ref, alog_ref)
        acs2 = acs * LOG2E
        rt = (acs2 - jnp.log2(dt)).T
        dsum_t = jnp.log2(dt + pltpu.roll(dt, LANES - SSD_HEADS, axis=1)).T
        eacs = jnp.exp(acs)
        ef_in = jnp.concatenate([jnp.exp(tot - acs) * dt, eacs,
                                 jnp.broadcast_to(jnp.exp(tot), (8, LANES))], axis=0).astype(BF16)
        ef = jnp.dot(ef_in, rexpf_ref[...], preferred_element_type=F32)
        eb = jnp.dot(eacs.astype(BF16), rexpb_ref[...], preferred_element_type=F32)
        xd = (xs * ef[0:CHUNK]).astype(BF16)
        eacs_f = ef[CHUNK:2 * CHUNK]
        cdec = ef[2 * CHUNK:2 * CHUNK + 1]
        ys = []
        for g in range(SSD_GROUPS):
            bg = bm[:, g * SSD_STATE:(g + 1) * SSD_STATE]
            cg = cm[:, g * SSD_STATE:(g + 1) * SSD_STATE]
            gs = slice(g * SSD_GW, (g + 1) * SSD_GW)
            cb = lax.dot_general(cg, bg, (((1,), (1,)), ((), ())), preferred_element_type=F32)
            prev = st_ref[g]
            y_off = (jnp.dot(cg, prev.astype(BF16), preferred_element_type=F32) * eacs_f[:, gs]
                     + jnp.dot(cg, prev_ref[0, i, g], preferred_element_type=F32) * eb[:, gs])
            s_new = lax.dot_general(bg, xd[:, gs], (((0,), (0,)), ((), ())),
                                    preferred_element_type=F32)
            st_ref[g] = prev * cdec[:, gs] + s_new
            for pr in range(SSD_HPG // 2):
                c0 = g * SSD_GW + pr * LANES
                xpair = xs_b[:, c0:c0 + LANES]
                xbd = jnp.concatenate([jnp.where(lane_lo, xpair, zero_b),
                                       jnp.where(lane_lo, zero_b, xpair)], axis=0)
                mats = []
                for e in range(2):
                    h = g * SSD_HPG + pr * 2 + e
                    hb = SSD_HEADS + h
                    sel = jnp.where(fwd_part, acs2[:, h:h + 1] - rt[h:h + 1, :],
                                    acs2[:, hb:hb + 1] - rt[hb:hb + 1, :])
                    sel = jnp.where(diag, dsum_t[h:h + 1, :], sel)
                    mats.append((cb * jnp.exp2(sel)).astype(BF16))
                yd = jnp.dot(jnp.concatenate(mats, axis=1), xbd, preferred_element_type=F32)
                ys.append(yd + y_off[:, pr * LANES:(pr + 1) * LANES])
        y = jnp.concatenate(ys, axis=-1) + xs * dsk_ref[...]
        y = y * _silu(z_ref[0, pl.ds(r0, CHUNK), :].astype(F32))
        ms = jnp.mean(y * y, axis=-1, keepdims=True)
        o_ref[0, pl.ds(r0, CHUNK), :] = (y * lax.rsqrt(ms + EPS) * ng_ref[...]).astype(BF16)
        return carry

    lax.fori_loop(0, qb, body, 0, unroll=True)


def _ssd(u3, dt3, cw, cb, dtb, alog, rexp_f, rexp_b, dsk, ng, *, qb):
    B, L, _ = u3.shape
    rows = qb * CHUNK
    assert L % rows == 0
    nt = L // rows
    nc = L // CHUNK
    hb = rows // HALO
    nhb = L // HALO
    state = pltpu.VMEM((SSD_GROUPS, SSD_STATE, SSD_GW), F32)
    const = lambda shape: pl.BlockSpec(shape, lambda b, n: (0,) * len(shape))
    rt = lambda n: nt - 1 - n

    xact, prevb = pl.pallas_call(
        functools.partial(_ssd_bwd_kernel, nt=nt, qb=qb),
        grid=(B, nt),
        in_specs=[
            pl.BlockSpec((1, rows, SSD_XBC), lambda b, n: (b, rt(n), C_XBC // SSD_XBC)),
            pl.BlockSpec((1, HALO, SSD_XBC),
                         lambda b, n: (b, jnp.maximum(rt(n) * hb - 1, 0), C_XBC // SSD_XBC)),
            pl.BlockSpec((1, HALO, SSD_XBC),
                         lambda b, n: (b, jnp.minimum((rt(n) + 1) * hb, nhb - 1), C_XBC // SSD_XBC)),
            pl.BlockSpec((1, rows, DT_PAD), lambda b, n: (b, rt(n), 0)),
            const((8, SSD_XBC)), const((1, SSD_XBC)), const((1, DT_PAD)), const((1, DT_PAD)),
            const((LANES, SSD_WIDTH)),
        ],
        out_specs=[pl.BlockSpec((1, rows, SSD_XBC), lambda b, n: (b, rt(n), 0)),
                   pl.BlockSpec((1, qb, SSD_GROUPS, SSD_STATE, SSD_GW),
                                lambda b, n: (b, rt(n), 0, 0, 0))],
        out_shape=[jax.ShapeDtypeStruct((B, L, SSD_XBC), BF16),
                   jax.ShapeDtypeStruct((B, nc, SSD_GROUPS, SSD_STATE, SSD_GW), BF16)],
        scratch_shapes=[pltpu.VMEM((rows + 2 * HALO, SSD_XBC), BF16),
                        pltpu.VMEM((rows, SSD_XBC), F32), state],
        compiler_params=_cparams(("parallel", "arbitrary")),
        name="ssd_bwd",
    )(u3, u3, u3, dt3, cw, cb, dtb, alog, rexp_b)

    return pl.pallas_call(
        functools.partial(_ssd_fwd_kernel, qb=qb),
        grid=(B, nt),
        in_specs=[
            pl.BlockSpec((1, rows, SSD_XBC), lambda b, n: (b, n, 0)),
            pl.BlockSpec((1, rows, DT_PAD), lambda b, n: (b, n, 0)),
            pl.BlockSpec((1, qb, SSD_GROUPS, SSD_STATE, SSD_GW), lambda b, n: (b, n, 0, 0, 0)),
            pl.BlockSpec((1, rows, SSD_WIDTH), lambda b, n: (b, n, C_Z // SSD_WIDTH)),
            const((1, DT_PAD)), const((1, DT_PAD)),
            const((LANES, SSD_WIDTH)), const((LANES, SSD_WIDTH)),
            const((1, SSD_WIDTH)), const((1, SSD_WIDTH)),
        ],
        out_specs=pl.BlockSpec((1, rows, SSD_WIDTH), lambda b, n: (b, n, 0)),
        out_shape=jax.ShapeDtypeStruct((B, L, SSD_WIDTH), BF16),
        scratch_shapes=[state],
        compiler_params=_cparams(("parallel", "arbitrary")),
        name="ssd_fwd",
    )(xact, dt3, prevb, u3, dtb, alog, rexp_f, rexp_b, dsk, ng)


def _ssd_expand_matrix(rev):
    m = np.zeros((LANES, SSD_WIDTH), np.float32)
    hoff = SSD_HEADS if rev else 0
    for h in range(SSD_HEADS):
        m[hoff + h, h * SSD_HEAD_DIM:(h + 1) * SSD_HEAD_DIM] = 1.0
    return jnp.asarray(m, BF16)


def _hy_filter_kernel(t_ref, bands_ref, w1t_ref, w1c_ref, w1s_ref, b1_ref, w2_ref, b2_ref, w3_ref,
                      b3_ref, w4_ref, fr_ref, absd_ref, x_ref, s_ref, *, L, tl):
    i = pl.program_id(0)
    pos = (lax.broadcasted_iota(jnp.int32, (tl, 1), 0) + i * tl).astype(F32)
    t = t_ref[...]
    pos_row = (lax.broadcasted_iota(jnp.int32, (1, tl), 1) + i * tl).astype(F32)
    ang_t = 2.0 * math.pi * pos_row * bands_ref[...] / L
    fr = fr_ref[...]
    dot = functools.partial(jnp.dot, preferred_element_type=F32, precision=HIGHEST)
    tdot = lambda a, b: lax.dot_general(a, b, (((0,), (0,)), ((), ())), preferred_element_type=F32,
                                        precision=HIGHEST)
    pre = t * w1t_ref[...] + tdot(jnp.cos(ang_t), w1c_ref[...]) + tdot(-jnp.sin(ang_t), w1s_ref[...])
    h = jnp.sin(fr * (pre + b1_ref[...]))
    h = jnp.sin(fr * (dot(h, w2_ref[...]) + b2_ref[...]))
    h = jnp.sin(fr * (dot(h, w3_ref[...]) + b3_ref[...]))
    w4 = w4_ref[...]
    h_hi, w_hi = h.astype(BF16), w4.astype(BF16)
    h_lo, w_lo = (h - h_hi.astype(F32)).astype(BF16), (w4 - w_hi.astype(F32)).astype(BF16)
    bdot = functools.partial(jnp.dot, preferred_element_type=F32)
    h = bdot(h_hi, w_hi) + bdot(h_hi, w_lo) + bdot(h_lo, w_hi)
    decay = jnp.exp(-t * absd_ref[...])
    hf = h[:, :HY_WIDTH] * decay
    hb = jnp.where(pos == 0.0, 0.0, h[:, HY_WIDTH:] * decay)
    _to_lane_blocks(s_ref, jnp.concatenate([hf + hb, hb - hf], axis=1))
    x_ref[...] = _split_even_odd(s_ref).astype(BF16)


def _hy_filter(L, w1, b1, w2, b2, w3, b3, w4, freq):
    tl = min(512, L)
    t = jnp.linspace(0.0, 1.0, L, dtype=F32)[:, None]
    bands = jnp.linspace(1e-4, HY_EMB_BANDS - 1, HY_EMB_BANDS, dtype=F32)[:, None]
    max_decay = math.log(HY_DECAY_TARGET) / HY_FAST_DECAY
    min_decay = math.log(HY_DECAY_TARGET) / HY_SLOW_DECAY
    absd = jnp.abs(jnp.linspace(min_decay, max_decay, HY_WIDTH, dtype=F32))[None, :]
    w1 = w1.astype(F32)
    full = lambda a: pl.BlockSpec(a.shape, lambda i: (0,) * a.ndim)
    args = [t, bands, w1[0:1], w1[1:1 + HY_EMB_BANDS], w1[1 + HY_EMB_BANDS:], b1[None], w2, b2[None],
            w3, b3[None], w4, freq[None], absd]
    in_specs = [pl.BlockSpec((tl, 1), lambda i: (i, 0))] + [full(a) for a in args[1:]]
    return pl.pallas_call(
        functools.partial(_hy_filter_kernel, L=L, tl=tl),
        grid=(L // tl,),
        in_specs=in_specs,
        out_specs=pl.BlockSpec((tl // 2, 4 * HY_WIDTH), lambda i: (i, 0)),
        out_shape=jax.ShapeDtypeStruct((L // 2, 4 * HY_WIDTH), BF16),
        scratch_shapes=[pltpu.VMEM((2 * HY_WIDTH // LANES, tl, LANES), F32)],
        compiler_params=_cparams(("parallel",)),
        name="hy_filter",
    )(*args)


def _dft_tables(L, hm):
    L2 = L // 2
    g = jnp.arange(L2, dtype=jnp.int32)
    s = jnp.arange(L2, dtype=jnp.int32)
    ph = ((2 * g + 1)[:, None] * s[None, :]) % (2 * L)
    ang = ph.astype(F32) * (math.pi / L)
    c = jnp.cos(ang).astype(BF16).reshape(L2 // hm, 1, hm, L2)
    sn = jnp.sin(ang).astype(BF16).reshape(L2 // hm, 1, hm, L2)
    a_fwd = jnp.concatenate([c, sn], axis=1).reshape(L, L2)
    a_inv = jnp.concatenate([c, -sn], axis=1).reshape(L, L2).T
    return a_fwd, a_inv


def _hy_kspec_kernel(a_ref, x_ref, wc_ref, ws_ref, f_ref, *, hm, scale):
    W = HY_WIDTH
    pq = jnp.dot(a_ref[...], x_ref[...], preferred_element_type=F32)
    ea, eb, oa, ob = [(pq[0:hm, i * W:(i + 1) * W], -pq[hm:, i * W:(i + 1) * W]) for i in range(4)]
    w = (wc_ref[...], -ws_ref[...])
    cmul = lambda x, y: (x[0] * y[0] - x[1] * y[1], x[0] * y[1] + x[1] * y[0])
    woa, wob = cmul(w, oa), cmul(w, ob)
    k1 = (ea[0] + woa[0], -(eb[1] + wob[1]))
    k2 = (ea[0] - woa[0], eb[1] - wob[1])
    kp = (k1[0] + k2[0], k1[1] - k2[1])
    km = (k1[0] - k2[0], k1[1] + k2[1])
    wkm = cmul(w, km)
    vkm = cmul((w[0], -w[1]), km)
    for i, part in enumerate((kp[0], kp[1], wkm[0], wkm[1], vkm[0], vkm[1])):
        f_ref[0, i] = part * scale


def _hy_kspec(a_fwd, xf2, *, hm):
    L, L2 = a_fwd.shape
    tm = 2 * hm
    nx = xf2.shape[1]
    theta = (2.0 * jnp.arange(L2, dtype=F32) + 1.0) * (math.pi / (2 * L))
    return pl.pallas_call(
        functools.partial(_hy_kspec_kernel, hm=hm, scale=1.0 / L),
        grid=(L // tm,),
        in_specs=[pl.BlockSpec((tm, L2), lambda i: (i, 0)),
                  pl.BlockSpec((L2, nx), lambda i: (0, 0)),
                  pl.BlockSpec((hm, 1), lambda i: (i, 0)),
                  pl.BlockSpec((hm, 1), lambda i: (i, 0))],
        out_specs=pl.BlockSpec((1, 6, hm, HY_WIDTH), lambda i: (i, 0, 0, 0)),
        out_shape=jax.ShapeDtypeStruct((L2 // hm, 6, hm, HY_WIDTH), F32),
        compiler_params=_cparams(("parallel",)),
        name="hy_kspec",
    )(a_fwd, xf2, jnp.cos(theta)[:, None], jnp.sin(theta)[:, None])


def _to_lane_blocks(s_ref, x):
    for c in range(s_ref.shape[0]):
        s_ref[c] = x[:, c * LANES:(c + 1) * LANES]


def _split_even_odd(s_ref):
    k, rows, _ = s_ref.shape
    return jnp.concatenate([s_ref[c, pl.ds(p, rows // 2, stride=2), :]
                            for p in range(2) for c in range(k)], axis=1)


def _hy_pre_kernel(xm_ref, xp_ref, xn_ref, cw_ref, cb_ref, u_ref, xau_ref, xpad_ref, s_ref, *, nt, tl):
    i = pl.program_id(1)
    _load_padded(xpad_ref, xm_ref, xp_ref, xn_ref, i > 0, i < nt - 1, tl)
    xa = _dwconv(xpad_ref, cw_ref, cb_ref, tl, HY_CONV, 0, HY_WIDTH)
    xb = _dwconv(xpad_ref, cw_ref, cb_ref, tl, HY_CONV, HY_WIDTH, 2 * HY_WIDTH)
    v = _dwconv(xpad_ref, cw_ref, cb_ref, tl, HY_CONV, 2 * HY_WIDTH, 3 * HY_WIDTH)
    u = xb * v
    xau_ref[0, :, :HY_WIDTH] = xa.astype(BF16)
    xau_ref[0, :, HY_WIDTH:] = u.astype(BF16)
    _to_lane_blocks(s_ref, u)
    u_ref[0] = _split_even_odd(s_ref).astype(BF16)


def _hy_tile_specs(L, tl):
    hb = tl // HALO
    nhb = L // HALO
    w = 3 * HY_WIDTH
    return [pl.BlockSpec((1, tl, w), lambda b, i: (b, i, C_HY // w)),
            pl.BlockSpec((1, HALO, w), lambda b, i: (b, jnp.maximum(i * hb - 1, 0), C_HY // w)),
            pl.BlockSpec((1, HALO, w), lambda b, i: (b, jnp.minimum((i + 1) * hb, nhb - 1), C_HY // w))]


def _hy_pre(u3, cw, cb, *, tl):
    B, L, _ = u3.shape
    nt = L // tl
    w = 3 * HY_WIDTH
    return pl.pallas_call(
        functools.partial(_hy_pre_kernel, nt=nt, tl=tl),
        grid=(B, nt),
        in_specs=_hy_tile_specs(L, tl) + [pl.BlockSpec((8, w), lambda b, i: (0, 0)),
                                          pl.BlockSpec((1, w), lambda b, i: (0, 0))],
        out_specs=[pl.BlockSpec((1, tl // 2, 2 * HY_WIDTH), lambda b, i: (b, i, 0)),
                   pl.BlockSpec((1, tl, 2 * HY_WIDTH), lambda b, i: (b, i, 0))],
        out_shape=[jax.ShapeDtypeStruct((B, L // 2, 2 * HY_WIDTH), BF16),
                   jax.ShapeDtypeStruct((B, L, 2 * HY_WIDTH), BF16)],
        scratch_shapes=[pltpu.VMEM((tl + 2 * HALO, w), BF16),
                        pltpu.VMEM((HY_WIDTH // LANES, tl, LANES), F32)],
        compiler_params=_cparams(("parallel", "parallel")),
        name="hy_pre",
    )(u3, u3, u3, cw, cb)


def _hy_fwd_kernel(a_ref, u_ref, f_ref, y_ref, *, hm):
    W = HY_WIDTH
    pq = jnp.dot(a_ref[...], u_ref[0], preferred_element_type=F32)
    pe, po = pq[0:hm, :W], pq[0:hm, W:]
    qe, qo = pq[hm:, :W], pq[hm:, W:]
    kpr, kpi, wr, wi, vr, vi = [f_ref[0, i] for i in range(6)]
    y_ref[0, 0:hm, :W] = (pe * kpr + qe * kpi + po * wr + qo * wi).astype(BF16)
    y_ref[0, hm:, :W] = (pe * kpi - qe * kpr + po * wi - qo * wr).astype(BF16)
    y_ref[0, 0:hm, W:] = (pe * vr + qe * vi + po * kpr + qo * kpi).astype(BF16)
    y_ref[0, hm:, W:] = (pe * vi - qe * vr + po * kpi - qo * kpr).astype(BF16)


def _hy_fwd(a_fwd, ueo, filt, *, hm):
    B, L2, W2 = ueo.shape
    L = 2 * L2
    tm = 2 * hm
    return pl.pallas_call(
        functools.partial(_hy_fwd_kernel, hm=hm),
        grid=(L // tm, B),
        in_specs=[pl.BlockSpec((tm, L2), lambda i, b: (i, 0)),
                  pl.BlockSpec((1, L2, W2), lambda i, b: (b, 0, 0)),
                  pl.BlockSpec((1, 6, hm, HY_WIDTH), lambda i, b: (i, 0, 0, 0))],
        out_specs=pl.BlockSpec((1, tm, W2), lambda i, b: (b, i, 0)),
        out_shape=jax.ShapeDtypeStruct((B, L, W2), BF16),
        compiler_params=_cparams(("parallel", "parallel")),
        name="hy_fwd",
    )(a_fwd, ueo, filt)


def _hy_inv_kernel(a_ref, y_ref, o_ref):
    o_ref[0] = jnp.dot(a_ref[...], y_ref[0], preferred_element_type=F32)


def _hy_inv(a_inv, yspec, *, tm):
    B, L, W2 = yspec.shape
    L2 = L // 2
    return pl.pallas_call(
        _hy_inv_kernel,
        grid=(L2 // tm, B),
        in_specs=[pl.BlockSpec((tm, L), lambda i, b: (i, 0)),
                  pl.BlockSpec((1, L, W2), lambda i, b: (b, 0, 0))],
        out_specs=pl.BlockSpec((1, tm, W2), lambda i, b: (b, i, 0)),
        out_shape=jax.ShapeDtypeStruct((B, L2, W2), F32),
        compiler_params=_cparams(("parallel", "parallel")),
        name="hy_inv",
    )(a_inv, yspec)


def _pad_rows(a, rows):
    return jnp.concatenate([a, jnp.zeros((rows - a.shape[0],) + a.shape[1:], a.dtype)], axis=0)


def _pad_cols(a, cols):
    return jnp.concatenate([a, jnp.zeros(a.shape[:-1] + (cols - a.shape[-1],), a.dtype)], axis=-1)


def _static_take(a, idx, axis):
    idx = np.asarray(idx)
    cuts = np.flatnonzero(np.diff(idx) != 1) + 1
    starts = np.concatenate([[0], cuts])
    ends = np.concatenate([cuts, [len(idx)]])
    parts = [lax.slice_in_dim(a, int(idx[s]), int(idx[e - 1]) + 1, axis=axis)
             for s, e in zip(starts, ends)]
    return jnp.concatenate(parts, axis=axis)


def _tile(n, pref):
    t = min(pref, n)
    assert n % t == 0
    return t


def _layer(x3, p, tables, *, final, final_g):
    B, L, _ = x3.shape
    T = B * L
    x2 = x3.reshape(T, D_MODEL)
    u, dt = _inproj(x2, p["norm_g"], p["w_in"], tm=_tile(T, 512), tn=512)
    u3 = u.reshape(B, L, N_IN)
    dt3 = dt.reshape(B, L, DT_PAD)

    ys = _ssd(u3, dt3, p["ssd_cw"], p["ssd_cb"], p["ssd_dtb"], p["ssd_alog"], p["rexp_f"],
              p["rexp_b"], p["ssd_dskip"], p["ssd_ng"], qb=_tile(L // CHUNK, 8))

    ya = _attention(u3, p["att_bias"], p["att_sink"], p["att_ng"], qb=_tile(L // ATT_BLOCK, 8))

    a_fwd, a_inv, filt, hm = tables
    tl = _tile(L, 1024)
    ueo, xau = _hy_pre(u3, p["hy_cw"], p["hy_cb"], tl=tl)
    yspec = _hy_fwd(a_fwd, ueo, filt, hm=hm)
    conv = _hy_inv(a_inv, yspec, tm=_tile(L // 2, 512))

    out = _outproj(x2, ys.reshape(T, SSD_WIDTH), ya.reshape(T, ATT_WIDTH),
                   conv.reshape(T // 2, 2 * HY_WIDTH), xau.reshape(T, 2 * HY_WIDTH), u,
                   p["hy_d"], p["hy_ng"], p["w_out"], final_g, tm=_tile(T, 512), final=final)
    return out.reshape(B, L, D_MODEL)


def kernel(x_prompt, x_sample, rel_bias, norm_g, w_in, ssd_conv_w, ssd_conv_b, ssd_dt_bias, ssd_a_log, ssd_d, ssd_norm_g, att_sink, att_norm_g, hy_conv_w, hy_conv_b, hy_w1, hy_b1, hy_w2, hy_b2, hy_w3, hy_b3, hy_w4, hy_freq, hy_d, hy_norm_g, w_out, final_norm_g):
    depth = w_in.shape[0]
    perm, col_scale = _in_perm()
    n_real = C_DT + 2 * SSD_HEADS
    scale_old = np.ones((IN_COLS,), np.float32)
    scale_old[perm[:n_real]] = col_scale[:n_real]
    w_in_p = [_pad_cols(_static_take((w_in[i] * jnp.asarray(scale_old)).astype(BF16), perm[:n_real], 1),
                        N_IN) for i in range(depth)]
    att_perm = _att_col_perm()
    out_rows = np.concatenate([np.arange(SSD_WIDTH), SSD_WIDTH + att_perm,
                               np.arange(SSD_WIDTH + ATT_WIDTH, D_MODEL)])
    w_out_b = [_static_take(w_out[i].astype(BF16), out_rows, 0) for i in range(depth)]
    att_bias = _attn_bias(rel_bias)
    rexp_f, rexp_b = _ssd_expand_matrix(False), _ssd_expand_matrix(True)
    final_g = final_norm_g.astype(F32)[None, :]

    layers = []
    for i in range(depth):
        layers.append(dict(
            norm_g=norm_g[i].astype(F32)[None, :],
            w_in=w_in_p[i],
            ssd_cw=_pad_rows(ssd_conv_w[i].astype(F32), 8),
            ssd_cb=ssd_conv_b[i].astype(F32)[None, :],
            ssd_dtb=_pad_cols(ssd_dt_bias[i].astype(F32).reshape(1, 2 * SSD_HEADS), DT_PAD),
            ssd_alog=_pad_cols(ssd_a_log[i].astype(F32).reshape(1, 2 * SSD_HEADS), DT_PAD),
            ssd_dskip=jnp.repeat(ssd_d[i].astype(F32), SSD_HEAD_DIM)[None, :],
            ssd_ng=ssd_norm_g[i].astype(F32)[None, :],
            rexp_f=rexp_f, rexp_b=rexp_b,
            att_bias=att_bias,
            att_sink=jnp.broadcast_to(att_sink[i].astype(F32)[:, None] * LOG2E, (ATT_HEADS, LANES)),
            att_ng=att_norm_g[i].astype(F32)[att_perm][None, :],
            hy_cw=_pad_rows(hy_conv_w[i].astype(F32), 8),
            hy_cb=hy_conv_b[i].astype(F32)[None, :],
            hy_d=hy_d[i].astype(F32)[None, :],
            hy_ng=hy_norm_g[i].astype(F32)[None, :],
            w_out=w_out_b[i],
        ))

    def trunk(x):
        L = x.shape[1]
        hm = min(512, L // 4)
        a_fwd, a_inv = _dft_tables(L, hm)
        for i in range(depth):
            xf = _hy_filter(L, hy_w1[i], hy_b1[i].astype(F32), hy_w2[i].astype(F32),
                            hy_b2[i].astype(F32), hy_w3[i].astype(F32), hy_b3[i].astype(F32),
                            hy_w4[i].astype(F32), hy_freq[i].astype(F32))
            filt = _hy_kspec(a_fwd, xf, hm=hm)
            x = _layer(x, layers[i], (a_fwd, a_inv, filt, hm), final=(i == depth - 1),
                       final_g=final_g)
        return x

    return (trunk(x_prompt), trunk(x_sample))
```

```python
import functools
import math

import jax
import jax.numpy as jnp
import numpy as np
from jax import lax
from jax.experimental import pallas as pl
from jax.experimental.pallas import tpu as pltpu

F32 = jnp.float32
BF16 = jnp.bfloat16
HIGHEST = lax.Precision.HIGHEST

D_MODEL = 2048
SSD_WIDTH = 1024
ATT_WIDTH = 512
HY_WIDTH = 512
SSD_HEAD_DIM = 64
SSD_HEADS = 16
SSD_GROUPS = 2
SSD_STATE = 128
SSD_CONV = 5
CHUNK = 128
SSD_XBC = SSD_WIDTH + 2 * SSD_GROUPS * SSD_STATE
ATT_HEAD_DIM = 64
ATT_HEADS = 8
ATT_KV_HEADS = 2
ATT_REP = ATT_HEADS // ATT_KV_HEADS
ATT_WINDOW = 128
ATT_BLOCK = 128
REL_BUCKETS = 32
REL_MAX_DIST = 128
HY_CONV = 3
HY_EMB_BANDS = 16
HY_FF = 64
HY_FAST_DECAY = 0.3
HY_SLOW_DECAY = 1.5
HY_DECAY_TARGET = 1e-2
EPS = 1e-6
NEG_BIG = -1e30

LANES = 128
BF16_SUBLANES = 16
VMEM_LIMIT = 56 * 1024 * 1024

C_XBC = 0
C_HY = 1536
C_Z = 3072
C_Q = 4096
C_GATT = 4608
C_GHY = 5120
C_K = 5632
C_V = 5760
C_DT = 5888
DT_PAD = 128
N_IN = C_DT + DT_PAD

_OLD_SIZES = [SSD_WIDTH, SSD_XBC, 2 * SSD_HEADS, ATT_WIDTH, ATT_KV_HEADS * ATT_HEAD_DIM,
              ATT_KV_HEADS * ATT_HEAD_DIM, ATT_WIDTH, 3 * HY_WIDTH, HY_WIDTH]
_OLD_OFF = np.concatenate([[0], np.cumsum(_OLD_SIZES)])
IN_COLS = int(_OLD_OFF[-1])


LOG2E = math.log2(math.e)
Q_SCALE = ATT_HEAD_DIM ** -0.5 * LOG2E


def _att_col_perm():
    order = [h for j in range(ATT_REP) for h in (j, ATT_REP + j)]
    return np.concatenate([np.arange(h * ATT_HEAD_DIM, (h + 1) * ATT_HEAD_DIM) for h in order])


def _in_perm():
    perm = np.full((N_IN,), IN_COLS, np.int32)
    scale = np.ones((N_IN,), np.float32)
    o = {n: int(_OLD_OFF[i]) for i, n in enumerate(
        ["z", "xbc", "dt", "q", "k", "v", "gatt", "hy", "ghy"])}
    def put(new, old, width):
        perm[new:new + width] = np.arange(old, old + width)
    put(C_XBC, o["xbc"], SSD_XBC)
    put(C_HY, o["hy"], 3 * HY_WIDTH)
    put(C_Z, o["z"], SSD_WIDTH)
    perm[C_Q:C_Q + ATT_WIDTH] = o["q"] + _att_col_perm()
    scale[C_Q:C_Q + ATT_WIDTH] = Q_SCALE
    perm[C_GATT:C_GATT + ATT_WIDTH] = o["gatt"] + _att_col_perm()
    put(C_GHY, o["ghy"], HY_WIDTH)
    put(C_K, o["k"], 128)
    put(C_V, o["v"], 128)
    put(C_DT, o["dt"], 2 * SSD_HEADS)
    return perm, scale


def _cparams(sem):
    return pltpu.CompilerParams(dimension_semantics=sem, vmem_limit_bytes=VMEM_LIMIT)


def _silu(x):
    return x * (1.0 / (1.0 + jnp.exp(-x)))


def _softplus(x):
    return jnp.maximum(x, 0.0) + jnp.log1p(jnp.exp(-jnp.abs(x)))


def _inproj_kernel(x_ref, g_ref, w_ref, u_ref, dt_ref, h_ref, *, rows, tn):
    for r in range(x_ref.shape[0] // rows):
        sl = slice(r * rows, (r + 1) * rows)
        x = x_ref[sl, :]
        ms = jnp.mean(x * x, axis=-1, keepdims=True)
        h_ref[sl, :] = (x * lax.rsqrt(ms + EPS) * g_ref[...]).astype(BF16)
        h = h_ref[sl, :]
        for c0 in range(0, N_IN, tn):
            c1 = min(c0 + tn, N_IN)
            acc = jnp.dot(h, w_ref[:, c0:c1], preferred_element_type=F32)
            u_ref[sl, c0:c1] = acc.astype(BF16)
            if c0 <= C_DT < c1:
                dt_ref[sl, :] = acc[:, C_DT - c0:C_DT - c0 + DT_PAD]


def _inproj(x2, g, w, *, tm, tn):
    T = x2.shape[0]
    assert T % tm == 0
    return pl.pallas_call(
        functools.partial(_inproj_kernel, rows=min(256, tm), tn=tn),
        grid=(T // tm,),
        in_specs=[pl.BlockSpec((tm, D_MODEL), lambda i: (i, 0)),
                  pl.BlockSpec((1, D_MODEL), lambda i: (0, 0)),
                  pl.BlockSpec((D_MODEL, N_IN), lambda i: (0, 0), pipeline_mode=pl.Buffered(1))],
        out_specs=[pl.BlockSpec((tm, N_IN), lambda i: (i, 0)),
                   pl.BlockSpec((tm, DT_PAD), lambda i: (i, 0))],
        out_shape=[jax.ShapeDtypeStruct((T, N_IN), BF16),
                   jax.ShapeDtypeStruct((T, DT_PAD), F32)],
        scratch_shapes=[pltpu.VMEM((tm, D_MODEL), BF16)],
        compiler_params=_cparams(("parallel",)),
        name="inproj",
    )(x2, g, w)


def _outproj_kernel(x_ref, ys_ref, ya_ref, c_ref, ueo_ref, xa_ref, g_ref, d_ref, hng_ref, w_ref, fg_ref,
                    o_ref, s_ref, *, rows, final):
    nblk = HY_WIDTH // LANES

    def interleave(ref, hs):
        for p in range(2):
            for c in range(nblk):
                s_ref[c, pl.ds(p, rows // 2, stride=2), :] = ref[
                    hs, (p * nblk + c) * LANES:(p * nblk + c + 1) * LANES].astype(F32)
        return jnp.concatenate([s_ref[c] for c in range(nblk)], axis=1)

    for r in range(x_ref.shape[0] // rows):
        sl = slice(r * rows, (r + 1) * rows)
        hs = slice(r * rows // 2, (r + 1) * rows // 2)
        conv = interleave(c_ref, hs)
        u = interleave(ueo_ref, hs)
        xa = xa_ref[sl, :].astype(F32)
        y = xa * (conv + u * d_ref[...])
        y = y * _silu(g_ref[sl, :].astype(F32))
        ms = jnp.mean(y * y, axis=-1, keepdims=True)
        yh = (y * lax.rsqrt(ms + EPS) * hng_ref[...]).astype(BF16)

        acc = x_ref[sl, :]
        acc = acc + jnp.dot(ys_ref[sl, :], w_ref[0:SSD_WIDTH, :], preferred_element_type=F32)
        acc = acc + jnp.dot(ya_ref[sl, :], w_ref[SSD_WIDTH:SSD_WIDTH + ATT_WIDTH, :],
                            preferred_element_type=F32)
        acc = acc + jnp.dot(yh, w_ref[SSD_WIDTH + ATT_WIDTH:, :], preferred_element_type=F32)
        if final:
            ms = jnp.mean(acc * acc, axis=-1, keepdims=True)
            acc = acc * lax.rsqrt(ms + EPS) * fg_ref[...]
        o_ref[sl, :] = acc


def _outproj(x2, ys, ya, conv2, ueo2, xa2, u2, d, hng, w, fg, *, tm, final):
    T = x2.shape[0]
    assert T % tm == 0
    rows = min(256, tm)
    return pl.pallas_call(
        functools.partial(_outproj_kernel, rows=rows, final=final),
        grid=(T // tm,),
        in_specs=[pl.BlockSpec((tm, D_MODEL), lambda i: (i, 0)),
                  pl.BlockSpec((tm, SSD_WIDTH), lambda i: (i, 0)),
                  pl.BlockSpec((tm, ATT_WIDTH), lambda i: (i, 0)),
                  pl.BlockSpec((tm // 2, 2 * HY_WIDTH), lambda i: (i, 0)),
                  pl.BlockSpec((tm // 2, 2 * HY_WIDTH), lambda i: (i, 0)),
                  pl.BlockSpec((tm, HY_WIDTH), lambda i: (i, 0)),
                  pl.BlockSpec((tm, HY_WIDTH), lambda i: (i, C_GHY // HY_WIDTH)),
                  pl.BlockSpec((1, HY_WIDTH), lambda i: (0, 0)),
                  pl.BlockSpec((1, HY_WIDTH), lambda i: (0, 0)),
                  pl.BlockSpec((D_MODEL, D_MODEL), lambda i: (0, 0)),
                  pl.BlockSpec((1, D_MODEL), lambda i: (0, 0))],
        out_specs=pl.BlockSpec((tm, D_MODEL), lambda i: (i, 0)),
        out_shape=jax.ShapeDtypeStruct((T, D_MODEL), F32),
        scratch_shapes=[pltpu.VMEM((HY_WIDTH // LANES, rows, LANES), F32)],
        compiler_params=_cparams(("parallel",)),
        name="outproj",
    )(x2, ys, ya, conv2, ueo2, xa2, u2, d, hng, w, fg)


def _t5_buckets(rel):
    nb = REL_BUCKETS // 2
    max_exact = nb // 2
    ret = (rel > 0).astype(np.int32) * nb
    n = np.abs(rel)
    large = max_exact + (np.log(np.maximum(n, 1) / max_exact) / math.log(REL_MAX_DIST / max_exact)
                         * (nb - max_exact)).astype(np.int32)
    large = np.minimum(large, nb - 1)
    return ret + np.where(n < max_exact, n, large)


def _attn_bias(rel_bias):
    qi = np.arange(ATT_BLOCK)[:, None]
    kj = np.arange(3 * ATT_BLOCK)[None, :]
    rel = kj - ATT_BLOCK - qi
    onehot = (_t5_buckets(rel)[None] == np.arange(REL_BUCKETS)[:, None, None]).astype(np.float32)
    bias = jnp.einsum("bqk,bh->hqk", jnp.asarray(onehot), rel_bias.astype(F32),
                      precision=HIGHEST) * LOG2E
    window = np.abs(rel) <= ATT_WINDOW
    variants = []
    for last in (False, True):
        for first in (False, True):
            ok = window & ~(first & (kj < ATT_BLOCK)) & ~(last & (kj >= 2 * ATT_BLOCK))
            variants.append(jnp.where(ok[None], bias, NEG_BIG))
    return jnp.transpose(jnp.stack(variants), (0, 1, 3, 2))


def _attn_kernel(q_ref, kp_ref, kc_ref, kn_ref, vp_ref, vc_ref, vn_ref, g_ref, bias_ref, sink_ref,
                 ng_ref, o_ref, klo_ref, khi_ref, vt_ref, *, nt, qb):
    n = pl.program_id(1)
    lo = lax.broadcasted_iota(jnp.int32, (1, LANES), 1) < ATT_HEAD_DIM
    zero = jnp.zeros((), BF16)
    kext = jnp.concatenate([kp_ref[0], kc_ref[0], kn_ref[0]], axis=0)
    klo_ref[...] = jnp.where(lo, kext, zero)
    khi_ref[...] = jnp.where(lo, zero, kext)
    for t, ref, cnt in ((0, vp_ref, 1), (1, vc_ref, qb), (qb + 1, vn_ref, 1)):
        for i in range(cnt):
            blk = ref[0, i * ATT_BLOCK:(i + 1) * ATT_BLOCK, :]
            vt_ref[t + i] = blk.astype(F32).T.astype(BF16)
    row_lo = lax.broadcasted_iota(jnp.int32, (LANES, ATT_BLOCK), 0) < ATT_HEAD_DIM
    nk = 3 * ATT_BLOCK

    def body(i, carry):
        r0 = pl.multiple_of(i * ATT_BLOCK, ATT_BLOCK)
        q = q_ref[0, pl.ds(r0, ATT_BLOCK), :]
        kst = jnp.concatenate([klo_ref[pl.ds(r0, nk), :], khi_ref[pl.ds(r0, nk), :]], axis=0)
        vt = jnp.concatenate([vt_ref[i], vt_ref[i + 1], vt_ref[i + 2]], axis=1)
        first = jnp.logical_and(n == 0, i == 0)
        last = jnp.logical_and(n == nt - 1, i == qb - 1)
        variant = first.astype(jnp.int32) + 2 * last.astype(jnp.int32)
        outs = []
        for j in range(ATT_REP):
            qp = q[:, j * LANES:(j + 1) * LANES]
            st = lax.dot_general(kst, qp, (((1,), (1,)), ((), ())), preferred_element_type=F32)
            halves = []
            for e, h in enumerate((j, ATT_REP + j)):
                s = st[e * nk:(e + 1) * nk] + bias_ref[variant, h]
                sk = sink_ref[h:h + 1, :]
                m = jnp.maximum(jnp.max(s, axis=0, keepdims=True), sk)
                p = jnp.exp2(s - m)
                den = jnp.sum(p, axis=0, keepdims=True) + jnp.exp2(sk - m)
                ot = jnp.dot(vt, p.astype(BF16), preferred_element_type=F32)
                halves.append(ot * (1.0 / den))
            outs.append(jnp.where(row_lo, halves[0], halves[1]).T)
        o = jnp.concatenate(outs, axis=-1)
        y = o * _silu(g_ref[0, pl.ds(r0, ATT_BLOCK), :].astype(F32))
        ms = jnp.mean(y * y, axis=-1, keepdims=True)
        o_ref[0, pl.ds(r0, ATT_BLOCK), :] = (y * lax.rsqrt(ms + EPS) * ng_ref[...]).astype(BF16)
        return carry

    lax.fori_loop(0, qb, body, 0, unroll=True)


def _attention(u3, bias, sink, ng, *, qb):
    B, L, _ = u3.shape
    tq = qb * ATT_BLOCK
    assert L % tq == 0
    nt = L // tq
    nb = L // ATT_BLOCK
    kcol, vcol = C_K // 128, C_V // 128
    def kv_specs(colblk):
        return [pl.BlockSpec((1, ATT_BLOCK, 128), lambda b, n: (b, jnp.maximum(n * qb - 1, 0), colblk)),
                pl.BlockSpec((1, tq, 128), lambda b, n: (b, n, colblk)),
                pl.BlockSpec((1, ATT_BLOCK, 128),
                             lambda b, n: (b, jnp.minimum((n + 1) * qb, nb - 1), colblk))]
    return pl.pallas_call(
        functools.partial(_attn_kernel, nt=nt, qb=qb),
        grid=(B, nt),
        in_specs=[pl.BlockSpec((1, tq, ATT_WIDTH), lambda b, n: (b, n, C_Q // ATT_WIDTH))]
                 + kv_specs(kcol) + kv_specs(vcol)
                 + [pl.BlockSpec((1, tq, ATT_WIDTH), lambda b, n: (b, n, C_GATT // ATT_WIDTH)),
                    pl.BlockSpec((4, ATT_HEADS, 3 * ATT_BLOCK, ATT_BLOCK), lambda b, n: (0, 0, 0, 0)),
                    pl.BlockSpec((ATT_HEADS, LANES), lambda b, n: (0, 0)),
                    pl.BlockSpec((1, ATT_WIDTH), lambda b, n: (0, 0))],
        out_specs=pl.BlockSpec((1, tq, ATT_WIDTH), lambda b, n: (b, n, 0)),
        out_shape=jax.ShapeDtypeStruct((B, L, ATT_WIDTH), BF16),
        scratch_shapes=[pltpu.VMEM(((qb + 2) * ATT_BLOCK, LANES), BF16),
                        pltpu.VMEM(((qb + 2) * ATT_BLOCK, LANES), BF16),
                        pltpu.VMEM((qb + 2, LANES, ATT_BLOCK), BF16)],
        compiler_params=_cparams(("parallel", "parallel")),
        name="attn",
    )(u3, u3, u3, u3, u3, u3, u3, u3, bias, sink, ng)


HALO = BF16_SUBLANES


CONV_BLK = 128


def _load_padded(xpad_ref, xm_ref, xp_ref, xn_ref, has_prev, has_next, rows):
    zero = jnp.zeros((), BF16)
    xpad_ref[pl.ds(0, HALO), :] = jnp.where(has_prev, xp_ref[0], zero)
    xpad_ref[pl.ds(HALO, rows), :] = xm_ref[0]
    xpad_ref[pl.ds(HALO + rows, HALO), :] = jnp.where(has_next, xn_ref[0], zero)


def _shift_matrix(width):
    offs = [k - width // 2 for k in range(width) if k != width // 2]
    r = lax.broadcasted_iota(jnp.int32, (CONV_BLK, CONV_BLK + 2 * HALO), 0)
    c = lax.broadcasted_iota(jnp.int32, (CONV_BLK, CONV_BLK + 2 * HALO), 1)
    return jnp.concatenate([(c == r + HALO + d).astype(BF16) for d in offs], axis=0)


def _dwconv_block(xpad_ref, shifts, w_ref, b_ref, j, width, c0, c1):
    win = xpad_ref[j * CONV_BLK:(j + 1) * CONV_BLK + 2 * HALO, c0:c1]
    moved = jnp.dot(shifts, win, preferred_element_type=F32)
    acc = win[HALO:HALO + CONV_BLK].astype(F32) * w_ref[width // 2:width // 2 + 1, c0:c1]
    i = 0
    for k in range(width):
        if k == width // 2:
            continue
        acc = acc + moved[i * CONV_BLK:(i + 1) * CONV_BLK] * w_ref[k:k + 1, c0:c1]
        i += 1
    return acc + b_ref[:, c0:c1]


def _dwconv(xpad_ref, w_ref, b_ref, rows, width, c0, c1):
    shifts = _shift_matrix(width)
    return jnp.concatenate([_dwconv_block(xpad_ref, shifts, w_ref, b_ref, j, width, c0, c1)
                            for j in range(rows // CONV_BLK)], axis=0)


SSD_HPG = SSD_HEADS // SSD_GROUPS
SSD_GW = SSD_HPG * SSD_HEAD_DIM
SSD_BC = SSD_GROUPS * SSD_STATE


def _ssd_decay(dt_raw, dtb_ref, alog_ref):
    dt = _softplus(dt_raw + dtb_ref[...])
    dta = dt * (-jnp.exp(alog_ref[...]))
    row = lax.broadcasted_iota(jnp.int32, (CHUNK, CHUNK), 0)
    col = lax.broadcasted_iota(jnp.int32, (CHUNK, CHUNK), 1)
    tril = (row >= col).astype(BF16)
    hi = dta.astype(BF16)
    r1 = dta - hi.astype(F32)
    mid = r1.astype(BF16)
    lo = (r1 - mid.astype(F32)).astype(BF16)
    pre = (jnp.dot(tril, hi, preferred_element_type=F32) + jnp.dot(tril, mid, preferred_element_type=F32)
           + jnp.dot(tril, lo, preferred_element_type=F32))
    tot = pre[CHUNK - 1:CHUNK, :]
    is_bwd = lax.broadcasted_iota(jnp.int32, (1, LANES), 1) >= SSD_HEADS
    acs = jnp.where(is_bwd, tot - pre + dta, pre)
    return dt, acs, tot


def _ssd_bwd_kernel(xm_ref, xp_ref, xn_ref, dt_ref, cw_ref, cb_ref, dtb_ref, alog_ref, rexp_ref,
                    xact_ref, prev_ref, xpad_ref, xf_ref, st_ref, *, nt, qb):
    n = pl.program_id(1)
    tile = nt - 1 - n
    rows = qb * CHUNK

    @pl.when(n == 0)
    def _():
        st_ref[...] = jnp.zeros(st_ref.shape, F32)

    _load_padded(xpad_ref, xm_ref, xp_ref, xn_ref, tile > 0, tile < nt - 1, rows)
    xact = _silu(_dwconv(xpad_ref, cw_ref, cb_ref, rows, SSD_CONV, 0, SSD_XBC))
    xf_ref[...] = xact
    xact_ref[0] = xact.astype(BF16)

    def body(j, carry):
        i = qb - 1 - j
        r0 = pl.multiple_of(i * CHUNK, CHUNK)
        xs = xf_ref[pl.ds(r0, CHUNK), 0:SSD_WIDTH]
        bm = xf_ref[pl.ds(r0, CHUNK), SSD_WIDTH:SSD_WIDTH + SSD_BC].astype(BF16)
        dt, acs, tot = _ssd_decay(dt_ref[0, pl.ds(r0, CHUNK), :], dtb_ref, alog_ref)
        e_in = jnp.concatenate([jnp.exp(tot - acs) * dt, jnp.broadcast_to(jnp.exp(tot), (8, LANES))],
                               axis=0).astype(BF16)
        e_out = jnp.dot(e_in, rexp_ref[...], preferred_element_type=F32)
        xd = (xs * e_out[0:CHUNK]).astype(BF16)
        cdec = e_out[CHUNK:CHUNK + 1]
        for g in range(SSD_GROUPS):
            prev = st_ref[g]
            prev_ref[0, i, g] = prev.astype(BF16)
            s_new = lax.dot_general(bm[:, g * SSD_STATE:(g + 1) * SSD_STATE],
                                    xd[:, g * SSD_GW:(g + 1) * SSD_GW], (((0,), (0,)), ((), ())),
                                    preferred_element_type=F32)
            st_ref[g] = prev * cdec[:, g * SSD_GW:(g + 1) * SSD_GW] + s_new
        return carry

    lax.fori_loop(0, qb, body, 0, unroll=True)


def _ssd_fwd_kernel(xa_ref, dt_ref, prev_ref, z_ref, dtb_ref, alog_ref, rexpf_ref, rexpb_ref,
                    dsk_ref, ng_ref, o_ref, st_ref, *, qb):
    n = pl.program_id(1)

    @pl.when(n == 0)
    def _():
        st_ref[...] = jnp.zeros(st_ref.shape, F32)

    row = lax.broadcasted_iota(jnp.int32, (CHUNK, CHUNK), 0)
    col = lax.broadcasted_iota(jnp.int32, (CHUNK, CHUNK), 1)
    fwd_part = row > col
    diag = row == col
    lane_lo = lax.broadcasted_iota(jnp.int32, (1, LANES), 1) < SSD_HEAD_DIM
    zero_b = jnp.zeros((), BF16)

    def body(i, carry):
        r0 = pl.multiple_of(i * CHUNK, CHUNK)
        xs_b = xa_ref[0, pl.ds(r0, CHUNK), 0:SSD_WIDTH]
        xs = xs_b.astype(F32)
        bm = xa_ref[0, pl.ds(r0, CHUNK), SSD_WIDTH:SSD_WIDTH + SSD_BC]
        cm = xa_ref[0, pl.ds(r0, CHUNK), SSD_WIDTH + SSD_BC:SSD_XBC]
        dt, acs, tot = _ssd_decay(dt_ref[0, pl.ds(r0, CHUNK), :], dtb_ref, alog_ref)
        acs2 = acs * LOG2E
        rt = (acs2 - jnp.log2(dt)).T
        dsum_t = jnp.log2(dt + pltpu.roll(dt, LANES - SSD_HEADS, axis=1)).T
        eacs = jnp.exp(acs)
        ef_in = jnp.concatenate([jnp.exp(tot - acs) * dt, eacs,
                                 jnp.broadcast_to(jnp.exp(tot), (8, LANES))], axis=0).astype(BF16)
        ef = jnp.dot(ef_in, rexpf_ref[...], preferred_element_type=F32)
        eb = jnp.dot(eacs.astype(BF16), rexpb_ref[...], preferred_element_type=F32)
        xd = (xs * ef[0:CHUNK]).astype(BF16)
        eacs_f = ef[CHUNK:2 * CHUNK]
        cdec = ef[2 * CHUNK:2 * CHUNK + 1]
        ys = []
        for g in range(SSD_GROUPS):
            bg = bm[:, g * SSD_STATE:(g + 1) * SSD_STATE]
            cg = cm[:, g * SSD_STATE:(g + 1) * SSD_STATE]
            gs = slice(g * SSD_GW, (g + 1) * SSD_GW)
            cb = lax.dot_general(cg, bg, (((1,), (1,)), ((), ())), preferred_element_type=F32)
            prev = st_ref[g]
            y_off = (jnp.dot(cg, prev.astype(BF16), preferred_element_type=F32) * eacs_f[:, gs]
                     + jnp.dot(cg, prev_ref[0, i, g], preferred_element_type=F32) * eb[:, gs])
            s_new = lax.dot_general(bg, xd[:, gs], (((0,), (0,)), ((), ())),
                                    preferred_element_type=F32)
            st_ref[g] = prev * cdec[:, gs] + s_new
            for pr in range(SSD_HPG // 2):
                c0 = g * SSD_GW + pr * LANES
                xpair = xs_b[:, c0:c0 + LANES]
                xbd = jnp.concatenate([jnp.where(lane_lo, xpair, zero_b),
                                       jnp.where(lane_lo, zero_b, xpair)], axis=0)
                mats = []
                for e in range(2):
                    h = g * SSD_HPG + pr * 2 + e
                    hb = SSD_HEADS + h
                    sel = jnp.where(fwd_part, acs2[:, h:h + 1] - rt[h:h + 1, :],
                                    acs2[:, hb:hb + 1] - rt[hb:hb + 1, :])
                    sel = jnp.where(diag, dsum_t[h:h + 1, :], sel)
                    mats.append((cb * jnp.exp2(sel)).astype(BF16))
                yd = jnp.dot(jnp.concatenate(mats, axis=1), xbd, preferred_element_type=F32)
                ys.append(yd + y_off[:, pr * LANES:(pr + 1) * LANES])
        y = jnp.concatenate(ys, axis=-1) + xs * dsk_ref[...]
        y = y * _silu(z_ref[0, pl.ds(r0, CHUNK), :].astype(F32))
        ms = jnp.mean(y * y, axis=-1, keepdims=True)
        o_ref[0, pl.ds(r0, CHUNK), :] = (y * lax.rsqrt(ms + EPS) * ng_ref[...]).astype(BF16)
        return carry

    lax.fori_loop(0, qb, body, 0, unroll=True)


def _ssd(u3, dt3, cw, cb, dtb, alog, rexp_f, rexp_b, dsk, ng, *, qb):
    B, L, _ = u3.shape
    rows = qb * CHUNK
    assert L % rows == 0
    nt = L // rows
    nc = L // CHUNK
    hb = rows // HALO
    nhb = L // HALO
    state = pltpu.VMEM((SSD_GROUPS, SSD_STATE, SSD_GW), F32)
    const = lambda shape: pl.BlockSpec(shape, lambda b, n: (0,) * len(shape))
    rt = lambda n: nt - 1 - n

    xact, prevb = pl.pallas_call(
        functools.partial(_ssd_bwd_kernel, nt=nt, qb=qb),
        grid=(B, nt),
        in_specs=[
            pl.BlockSpec((1, rows, SSD_XBC), lambda b, n: (b, rt(n), C_XBC // SSD_XBC)),
            pl.BlockSpec((1, HALO, SSD_XBC),
                         lambda b, n: (b, jnp.maximum(rt(n) * hb - 1, 0), C_XBC // SSD_XBC)),
            pl.BlockSpec((1, HALO, SSD_XBC),
                         lambda b, n: (b, jnp.minimum((rt(n) + 1) * hb, nhb - 1), C_XBC // SSD_XBC)),
            pl.BlockSpec((1, rows, DT_PAD), lambda b, n: (b, rt(n), 0)),
            const((8, SSD_XBC)), const((1, SSD_XBC)), const((1, DT_PAD)), const((1, DT_PAD)),
            const((LANES, SSD_WIDTH)),
        ],
        out_specs=[pl.BlockSpec((1, rows, SSD_XBC), lambda b, n: (b, rt(n), 0)),
                   pl.BlockSpec((1, qb, SSD_GROUPS, SSD_STATE, SSD_GW),
                                lambda b, n: (b, rt(n), 0, 0, 0))],
        out_shape=[jax.ShapeDtypeStruct((B, L, SSD_XBC), BF16),
                   jax.ShapeDtypeStruct((B, nc, SSD_GROUPS, SSD_STATE, SSD_GW), BF16)],
        scratch_shapes=[pltpu.VMEM((rows + 2 * HALO, SSD_XBC), BF16),
                        pltpu.VMEM((rows, SSD_XBC), F32), state],
        compiler_params=_cparams(("parallel", "arbitrary")),
        name="ssd_bwd",
    )(u3, u3, u3, dt3, cw, cb, dtb, alog, rexp_b)

    return pl.pallas_call(
        functools.partial(_ssd_fwd_kernel, qb=qb),
        grid=(B, nt),
        in_specs=[
            pl.BlockSpec((1, rows, SSD_XBC), lambda b, n: (b, n, 0)),
            pl.BlockSpec((1, rows, DT_PAD), lambda b, n: (b, n, 0)),
            pl.BlockSpec((1, qb, SSD_GROUPS, SSD_STATE, SSD_GW), lambda b, n: (b, n, 0, 0, 0)),
            pl.BlockSpec((1, rows, SSD_WIDTH), lambda b, n: (b, n, C_Z // SSD_WIDTH)),
            const((1, DT_PAD)), const((1, DT_PAD)),
            const((LANES, SSD_WIDTH)), const((LANES, SSD_WIDTH)),
            const((1, SSD_WIDTH)), const((1, SSD_WIDTH)),
        ],
        out_specs=pl.BlockSpec((1, rows, SSD_WIDTH), lambda b, n: (b, n, 0)),
        out_shape=jax.ShapeDtypeStruct((B, L, SSD_WIDTH), BF16),
        scratch_shapes=[state],
        compiler_params=_cparams(("parallel", "arbitrary")),
        name="ssd_fwd",
    )(xact, dt3, prevb, u3, dtb, alog, rexp_f, rexp_b, dsk, ng)


def _ssd_expand_matrix(rev):
    m = np.zeros((LANES, SSD_WIDTH), np.float32)
    hoff = SSD_HEADS if rev else 0
    for h in range(SSD_HEADS):
        m[hoff + h, h * SSD_HEAD_DIM:(h + 1) * SSD_HEAD_DIM] = 1.0
    return jnp.asarray(m, BF16)


def _hy_filter_kernel(t_ref, bands_ref, w1t_ref, w1c_ref, w1s_ref, b1_ref, w2_ref, b2_ref, w3_ref,
                      b3_ref, w4_ref, fr_ref, absd_ref, x_ref, s_ref, *, L, tl):
    i = pl.program_id(0)
    pos = (lax.broadcasted_iota(jnp.int32, (tl, 1), 0) + i * tl).astype(F32)
    t = t_ref[...]
    pos_row = (lax.broadcasted_iota(jnp.int32, (1, tl), 1) + i * tl).astype(F32)
    ang_t = 2.0 * math.pi * pos_row * bands_ref[...] / L
    fr = fr_ref[...]
    dot = functools.partial(jnp.dot, preferred_element_type=F32, precision=HIGHEST)
    tdot = lambda a, b: lax.dot_general(a, b, (((0,), (0,)), ((), ())), preferred_element_type=F32,
                                        precision=HIGHEST)
    pre = t * w1t_ref[...] + tdot(jnp.cos(ang_t), w1c_ref[...]) + tdot(-jnp.sin(ang_t), w1s_ref[...])
    h = jnp.sin(fr * (pre + b1_ref[...]))
    h = jnp.sin(fr * (dot(h, w2_ref[...]) + b2_ref[...]))
    h = jnp.sin(fr * (dot(h, w3_ref[...]) + b3_ref[...]))
    w4 = w4_ref[...]
    h_hi, w_hi = h.astype(BF16), w4.astype(BF16)
    h_lo, w_lo = (h - h_hi.astype(F32)).astype(BF16), (w4 - w_hi.astype(F32)).astype(BF16)
    bdot = functools.partial(jnp.dot, preferred_element_type=F32)
    h = bdot(h_hi, w_hi) + bdot(h_hi, w_lo) + bdot(h_lo, w_hi)
    decay = jnp.exp(-t * absd_ref[...])
    hf = h[:, :HY_WIDTH] * decay
    hb = jnp.where(pos == 0.0, 0.0, h[:, HY_WIDTH:] * decay)
    _to_lane_blocks(s_ref, jnp.concatenate([hf + hb, hb - hf], axis=1))
    x_ref[...] = _split_even_odd(s_ref).astype(BF16)


def _hy_filter(L, w1, b1, w2, b2, w3, b3, w4, freq):
    tl = min(512, L)
    t = jnp.linspace(0.0, 1.0, L, dtype=F32)[:, None]
    bands = jnp.linspace(1e-4, HY_EMB_BANDS - 1, HY_EMB_BANDS, dtype=F32)[:, None]
    max_decay = math.log(HY_DECAY_TARGET) / HY_FAST_DECAY
    min_decay = math.log(HY_DECAY_TARGET) / HY_SLOW_DECAY
    absd = jnp.abs(jnp.linspace(min_decay, max_decay, HY_WIDTH, dtype=F32))[None, :]
    w1 = w1.astype(F32)
    full = lambda a: pl.BlockSpec(a.shape, lambda i: (0,) * a.ndim)
    args = [t, bands, w1[0:1], w1[1:1 + HY_EMB_BANDS], w1[1 + HY_EMB_BANDS:], b1[None], w2, b2[None],
            w3, b3[None], w4, freq[None], absd]
    in_specs = [pl.BlockSpec((tl, 1), lambda i: (i, 0))] + [full(a) for a in args[1:]]
    return pl.pallas_call(
        functools.partial(_hy_filter_kernel, L=L, tl=tl),
        grid=(L // tl,),
        in_specs=in_specs,
        out_specs=pl.BlockSpec((tl // 2, 4 * HY_WIDTH), lambda i: (i, 0)),
        out_shape=jax.ShapeDtypeStruct((L // 2, 4 * HY_WIDTH), BF16),
        scratch_shapes=[pltpu.VMEM((2 * HY_WIDTH // LANES, tl, LANES), F32)],
        compiler_params=_cparams(("parallel",)),
        name="hy_filter",
    )(*args)


def _dft_tables(L, hm):
    L2 = L // 2
    g = jnp.arange(L2, dtype=jnp.int32)
    s = jnp.arange(L2, dtype=jnp.int32)
    ph = ((2 * g + 1)[:, None] * s[None, :]) % (2 * L)
    ang = ph.astype(F32) * (math.pi / L)
    c = jnp.cos(ang).astype(BF16).reshape(L2 // hm, 1, hm, L2)
    sn = jnp.sin(ang).astype(BF16).reshape(L2 // hm, 1, hm, L2)
    a_fwd = jnp.concatenate([c, sn], axis=1).reshape(L, L2)
    a_inv = jnp.concatenate([c, -sn], axis=1).reshape(L, L2).T
    return a_fwd, a_inv


def _hy_kspec_kernel(a_ref, x_ref, wc_ref, ws_ref, f_ref, *, hm, scale):
    W = HY_WIDTH
    pq = jnp.dot(a_ref[...], x_ref[...], preferred_element_type=F32)
    ea, eb, oa, ob = [(pq[0:hm, i * W:(i + 1) * W], -pq[hm:, i * W:(i + 1) * W]) for i in range(4)]
    w = (wc_ref[...], -ws_ref[...])
    cmul = lambda x, y: (x[0] * y[0] - x[1] * y[1], x[0] * y[1] + x[1] * y[0])
    woa, wob = cmul(w, oa), cmul(w, ob)
    k1 = (ea[0] + woa[0], -(eb[1] + wob[1]))
    k2 = (ea[0] - woa[0], eb[1] - wob[1])
    kp = (k1[0] + k2[0], k1[1] - k2[1])
    km = (k1[0] - k2[0], k1[1] + k2[1])
    wkm = cmul(w, km)
    vkm = cmul((w[0], -w[1]), km)
    for i, part in enumerate((kp[0], kp[1], wkm[0], wkm[1], vkm[0], vkm[1])):
        f_ref[0, i] = part * scale


def _hy_kspec(a_fwd, xf2, *, hm):
    L, L2 = a_fwd.shape
    tm = 2 * hm
    nx = xf2.shape[1]
    theta = (2.0 * jnp.arange(L2, dtype=F32) + 1.0) * (math.pi / (2 * L))
    return pl.pallas_call(
        functools.partial(_hy_kspec_kernel, hm=hm, scale=1.0 / L),
        grid=(L // tm,),
        in_specs=[pl.BlockSpec((tm, L2), lambda i: (i, 0)),
                  pl.BlockSpec((L2, nx), lambda i: (0, 0)),
                  pl.BlockSpec((hm, 1), lambda i: (i, 0)),
                  pl.BlockSpec((hm, 1), lambda i: (i, 0))],
        out_specs=pl.BlockSpec((1, 6, hm, HY_WIDTH), lambda i: (i, 0, 0, 0)),
        out_shape=jax.ShapeDtypeStruct((L2 // hm, 6, hm, HY_WIDTH), F32),
        compiler_params=_cparams(("parallel",)),
        name="hy_kspec",
    )(a_fwd, xf2, jnp.cos(theta)[:, None], jnp.sin(theta)[:, None])


def _to_lane_blocks(s_ref, x):
    for c in range(s_ref.shape[0]):
        s_ref[c] = x[:, c * LANES:(c + 1) * LANES]


def _split_even_odd(s_ref):
    k, rows, _ = s_ref.shape
    return jnp.concatenate([s_ref[c, pl.ds(p, rows // 2, stride=2), :]
                            for p in range(2) for c in range(k)], axis=1)


def _hy_pre_kernel(xm_ref, xp_ref, xn_ref, cw_ref, cb_ref, u_ref, xa_ref, xpad_ref, s_ref, *, nt, tl):
    i = pl.program_id(1)
    _load_padded(xpad_ref, xm_ref, xp_ref, xn_ref, i > 0, i < nt - 1, tl)
    xa = _dwconv(xpad_ref, cw_ref, cb_ref, tl, HY_CONV, 0, HY_WIDTH)
    xb = _dwconv(xpad_ref, cw_ref, cb_ref, tl, HY_CONV, HY_WIDTH, 2 * HY_WIDTH)
    v = _dwconv(xpad_ref, cw_ref, cb_ref, tl, HY_CONV, 2 * HY_WIDTH, 3 * HY_WIDTH)
    u = xb * v
    xa_ref[0] = xa.astype(BF16)
    _to_lane_blocks(s_ref, u)
    u_ref[0] = _split_even_odd(s_ref).astype(BF16)


def _hy_tile_specs(L, tl):
    hb = tl // HALO
    nhb = L // HALO
    w = 3 * HY_WIDTH
    return [pl.BlockSpec((1, tl, w), lambda b, i: (b, i, C_HY // w)),
            pl.BlockSpec((1, HALO, w), lambda b, i: (b, jnp.maximum(i * hb - 1, 0), C_HY // w)),
            pl.BlockSpec((1, HALO, w), lambda b, i: (b, jnp.minimum((i + 1) * hb, nhb - 1), C_HY // w))]


def _hy_pre(u3, cw, cb, *, tl):
    B, L, _ = u3.shape
    nt = L // tl
    w = 3 * HY_WIDTH
    return pl.pallas_call(
        functools.partial(_hy_pre_kernel, nt=nt, tl=tl),
        grid=(B, nt),
        in_specs=_hy_tile_specs(L, tl) + [pl.BlockSpec((8, w), lambda b, i: (0, 0)),
                                          pl.BlockSpec((1, w), lambda b, i: (0, 0))],
        out_specs=[pl.BlockSpec((1, tl // 2, 2 * HY_WIDTH), lambda b, i: (b, i, 0)),
                   pl.BlockSpec((1, tl, HY_WIDTH), lambda b, i: (b, i, 0))],
        out_shape=[jax.ShapeDtypeStruct((B, L // 2, 2 * HY_WIDTH), BF16),
                   jax.ShapeDtypeStruct((B, L, HY_WIDTH), BF16)],
        scratch_shapes=[pltpu.VMEM((tl + 2 * HALO, w), BF16),
                        pltpu.VMEM((HY_WIDTH // LANES, tl, LANES), F32)],
        compiler_params=_cparams(("parallel", "parallel")),
        name="hy_pre",
    )(u3, u3, u3, cw, cb)


def _hy_fwd_kernel(a_ref, u_ref, f_ref, y_ref, *, hm):
    W = HY_WIDTH
    pq = jnp.dot(a_ref[...], u_ref[0], preferred_element_type=F32)
    pe, po = pq[0:hm, :W], pq[0:hm, W:]
    qe, qo = pq[hm:, :W], pq[hm:, W:]
    kpr, kpi, wr, wi, vr, vi = [f_ref[0, i] for i in range(6)]
    y_ref[0, 0:hm, :W] = (pe * kpr + qe * kpi + po * wr + qo * wi).astype(BF16)
    y_ref[0, hm:, :W] = (pe * kpi - qe * kpr + po * wi - qo * wr).astype(BF16)
    y_ref[0, 0:hm, W:] = (pe * vr + qe * vi + po * kpr + qo * kpi).astype(BF16)
    y_ref[0, hm:, W:] = (pe * vi - qe * vr + po * kpi - qo * kpr).astype(BF16)


def _hy_fwd(a_fwd, ueo, filt, *, hm):
    B, L2, W2 = ueo.shape
    L = 2 * L2
    tm = 2 * hm
    return pl.pallas_call(
        functools.partial(_hy_fwd_kernel, hm=hm),
        grid=(L // tm, B),
        in_specs=[pl.BlockSpec((tm, L2), lambda i, b: (i, 0)),
                  pl.BlockSpec((1, L2, W2), lambda i, b: (b, 0, 0)),
                  pl.BlockSpec((1, 6, hm, HY_WIDTH), lambda i, b: (i, 0, 0, 0))],
        out_specs=pl.BlockSpec((1, tm, W2), lambda i, b: (b, i, 0)),
        out_shape=jax.ShapeDtypeStruct((B, L, W2), BF16),
        compiler_params=_cparams(("parallel", "parallel")),
        name="hy_fwd",
    )(a_fwd, ueo, filt)


def _hy_inv_kernel(a_ref, y_ref, o_ref):
    o_ref[0] = jnp.dot(a_ref[...], y_ref[0], preferred_element_type=F32).astype(BF16)


def _hy_inv(a_inv, yspec, *, tm):
    B, L, W2 = yspec.shape
    L2 = L // 2
    return pl.pallas_call(
        _hy_inv_kernel,
        grid=(L2 // tm, B),
        in_specs=[pl.BlockSpec((tm, L), lambda i, b: (i, 0)),
                  pl.BlockSpec((1, L, W2), lambda i, b: (b, 0, 0))],
        out_specs=pl.BlockSpec((1, tm, W2), lambda i, b: (b, i, 0)),
        out_shape=jax.ShapeDtypeStruct((B, L2, W2), BF16),
        compiler_params=_cparams(("parallel", "parallel")),
        name="hy_inv",
    )(a_inv, yspec)


def _pad_rows(a, rows):
    return jnp.concatenate([a, jnp.zeros((rows - a.shape[0],) + a.shape[1:], a.dtype)], axis=0)


def _pad_cols(a, cols):
    return jnp.concatenate([a, jnp.zeros(a.shape[:-1] + (cols - a.shape[-1],), a.dtype)], axis=-1)


def _static_take(a, idx, axis):
    idx = np.asarray(idx)
    cuts = np.flatnonzero(np.diff(idx) != 1) + 1
    starts = np.concatenate([[0], cuts])
    ends = np.concatenate([cuts, [len(idx)]])
    parts = [lax.slice_in_dim(a, int(idx[s]), int(idx[e - 1]) + 1, axis=axis)
             for s, e in zip(starts, ends)]
    return jnp.concatenate(parts, axis=axis)


def _tile(n, pref):
    t = min(pref, n)
    assert n % t == 0
    return t


def _layer(x3, p, tables, *, final, final_g):
    B, L, _ = x3.shape
    T = B * L
    x2 = x3.reshape(T, D_MODEL)
    u, dt = _inproj(x2, p["norm_g"], p["w_in"], tm=_tile(T, 512), tn=512)
    u3 = u.reshape(B, L, N_IN)
    dt3 = dt.reshape(B, L, DT_PAD)

    ys = _ssd(u3, dt3, p["ssd_cw"], p["ssd_cb"], p["ssd_dtb"], p["ssd_alog"], p["rexp_f"],
              p["rexp_b"], p["ssd_dskip"], p["ssd_ng"], qb=_tile(L // CHUNK, 8))

    ya = _attention(u3, p["att_bias"], p["att_sink"], p["att_ng"], qb=_tile(L // ATT_BLOCK, 8))

    a_fwd, a_inv, filt, hm = tables
    tl = _tile(L, 1024)
    ueo, xa = _hy_pre(u3, p["hy_cw"], p["hy_cb"], tl=tl)
    yspec = _hy_fwd(a_fwd, ueo, filt, hm=hm)
    conv = _hy_inv(a_inv, yspec, tm=_tile(L // 2, 512))

    out = _outproj(x2, ys.reshape(T, SSD_WIDTH), ya.reshape(T, ATT_WIDTH),
                   conv.reshape(T // 2, 2 * HY_WIDTH), ueo.reshape(T // 2, 2 * HY_WIDTH),
                   xa.reshape(T, HY_WIDTH), u, p["hy_d"], p["hy_ng"], p["w_out"], final_g,
                   tm=_tile(T, 512), final=final)
    return out.reshape(B, L, D_MODEL)


def kernel(x_prompt, x_sample, rel_bias, norm_g, w_in, ssd_conv_w, ssd_conv_b, ssd_dt_bias, ssd_a_log, ssd_d, ssd_norm_g, att_sink, att_norm_g, hy_conv_w, hy_conv_b, hy_w1, hy_b1, hy_w2, hy_b2, hy_w3, hy_b3, hy_w4, hy_freq, hy_d, hy_norm_g, w_out, final_norm_g):
    depth = w_in.shape[0]
    perm, col_scale = _in_perm()
    n_real = C_DT + 2 * SSD_HEADS
    scale_old = np.ones((IN_COLS,), np.float32)
    scale_old[perm[:n_real]] = col_scale[:n_real]
    w_in_p = [_pad_cols(_static_take((w_in[i] * jnp.asarray(scale_old)).astype(BF16), perm[:n_real], 1),
                        N_IN) for i in range(depth)]
    att_perm = _att_col_perm()
    out_rows = np.concatenate([np.arange(SSD_WIDTH), SSD_WIDTH + att_perm,
                               np.arange(SSD_WIDTH + ATT_WIDTH, D_MODEL)])
    w_out_b = [_static_take(w_out[i].astype(BF16), out_rows, 0) for i in range(depth)]
    att_bias = _attn_bias(rel_bias)
    rexp_f, rexp_b = _ssd_expand_matrix(False), _ssd_expand_matrix(True)
    final_g = final_norm_g.astype(F32)[None, :]

    layers = []
    for i in range(depth):
        layers.append(dict(
            norm_g=norm_g[i].astype(F32)[None, :],
            w_in=w_in_p[i],
            ssd_cw=_pad_rows(ssd_conv_w[i].astype(F32), 8),
            ssd_cb=ssd_conv_b[i].astype(F32)[None, :],
            ssd_dtb=_pad_cols(ssd_dt_bias[i].astype(F32).reshape(1, 2 * SSD_HEADS), DT_PAD),
            ssd_alog=_pad_cols(ssd_a_log[i].astype(F32).reshape(1, 2 * SSD_HEADS), DT_PAD),
            ssd_dskip=jnp.repeat(ssd_d[i].astype(F32), SSD_HEAD_DIM)[None, :],
            ssd_ng=ssd_norm_g[i].astype(F32)[None, :],
            rexp_f=rexp_f, rexp_b=rexp_b,
            att_bias=att_bias,
            att_sink=jnp.broadcast_to(att_sink[i].astype(F32)[:, None] * LOG2E, (ATT_HEADS, LANES)),
            att_ng=att_norm_g[i].astype(F32)[att_perm][None, :],
            hy_cw=_pad_rows(hy_conv_w[i].astype(F32), 8),
            hy_cb=hy_conv_b[i].astype(F32)[None, :],
            hy_d=hy_d[i].astype(F32)[None, :],
            hy_ng=hy_norm_g[i].astype(F32)[None, :],
            w_out=w_out_b[i],
        ))

    def trunk(x):
        L = x.shape[1]
        hm = min(512, L // 4)
        a_fwd, a_inv = _dft_tables(L, hm)
        for i in range(depth):
            xf = _hy_filter(L, hy_w1[i], hy_b1[i].astype(F32), hy_w2[i].astype(F32),
                            hy_b2[i].astype(F32), hy_w3[i].astype(F32), hy_b3[i].astype(F32),
                            hy_w4[i].astype(F32), hy_freq[i].astype(F32))
            filt = _hy_kspec(a_fwd, xf, hm=hm)
            x = _layer(x, layers[i], (a_fwd, a_inv, filt, hm), final=(i == depth - 1),
                       final_g=final_g)
        return x

    return (trunk(x_prompt), trunk(x_sample))
```

```python
import functools
import math

import jax
import jax.numpy as jnp
import numpy as np
from jax import lax
from jax.experimental import pallas as pl
from jax.experimental.pallas import tpu as pltpu

F32 = jnp.float32
BF16 = jnp.bfloat16
HIGHEST = lax.Precision.HIGHEST

D_MODEL = 2048
SSD_WIDTH = 1024
ATT_WIDTH = 512
HY_WIDTH = 512
SSD_HEAD_DIM = 64
SSD_HEADS = 16
SSD_GROUPS = 2
SSD_STATE = 128
SSD_CONV = 5
CHUNK = 128
SSD_XBC = SSD_WIDTH + 2 * SSD_GROUPS * SSD_STATE
ATT_HEAD_DIM = 64
ATT_HEADS = 8
ATT_KV_HEADS = 2
ATT_REP = ATT_HEADS // ATT_KV_HEADS
ATT_WINDOW = 128
ATT_BLOCK = 128
REL_BUCKETS = 32
REL_MAX_DIST = 128
HY_CONV = 3
HY_EMB_BANDS = 16
HY_FF = 64
HY_FAST_DECAY = 0.3
HY_SLOW_DECAY = 1.5
HY_DECAY_TARGET = 1e-2
EPS = 1e-6
NEG_BIG = -1e30

LANES = 128
BF16_SUBLANES = 16
VMEM_LIMIT = 56 * 1024 * 1024

C_XBC = 0
C_HY = 1536
C_Z = 3072
C_Q = 4096
C_GATT = 4608
C_GHY = 5120
C_K = 5632
C_V = 5760
C_DT = 5888
DT_PAD = 128
N_IN = C_DT + DT_PAD

_OLD_SIZES = [SSD_WIDTH, SSD_XBC, 2 * SSD_HEADS, ATT_WIDTH, ATT_KV_HEADS * ATT_HEAD_DIM,
              ATT_KV_HEADS * ATT_HEAD_DIM, ATT_WIDTH, 3 * HY_WIDTH, HY_WIDTH]
_OLD_OFF = np.concatenate([[0], np.cumsum(_OLD_SIZES)])
IN_COLS = int(_OLD_OFF[-1])


LOG2E = math.log2(math.e)
Q_SCALE = ATT_HEAD_DIM ** -0.5 * LOG2E


def _att_col_perm():
    order = [h for j in range(ATT_REP) for h in (j, ATT_REP + j)]
    return np.concatenate([np.arange(h * ATT_HEAD_DIM, (h + 1) * ATT_HEAD_DIM) for h in order])


def _in_perm():
    perm = np.full((N_IN,), IN_COLS, np.int32)
    scale = np.ones((N_IN,), np.float32)
    o = {n: int(_OLD_OFF[i]) for i, n in enumerate(
        ["z", "xbc", "dt", "q", "k", "v", "gatt", "hy", "ghy"])}
    def put(new, old, width):
        perm[new:new + width] = np.arange(old, old + width)
    put(C_XBC, o["xbc"], SSD_XBC)
    put(C_HY, o["hy"], 3 * HY_WIDTH)
    put(C_Z, o["z"], SSD_WIDTH)
    perm[C_Q:C_Q + ATT_WIDTH] = o["q"] + _att_col_perm()
    scale[C_Q:C_Q + ATT_WIDTH] = Q_SCALE
    perm[C_GATT:C_GATT + ATT_WIDTH] = o["gatt"] + _att_col_perm()
    put(C_GHY, o["ghy"], HY_WIDTH)
    put(C_K, o["k"], 128)
    put(C_V, o["v"], 128)
    put(C_DT, o["dt"], 2 * SSD_HEADS)
    return perm, scale


def _col_runs(perm):
    cuts = np.flatnonzero(np.diff(perm) != 1) + 1
    starts = np.concatenate([[0], cuts])
    ends = np.concatenate([cuts, [len(perm)]])
    return [(int(s), int(perm[s]), int(e - s)) for s, e in zip(starts, ends)]


def _prep_w_in_kernel(w_ref, o_ref, *, runs, n_real):
    for new0, old0, n in runs:
        piece = w_ref[0, :, old0:old0 + n]
        if C_Q <= new0 < C_Q + ATT_WIDTH:
            piece = piece * Q_SCALE
        o_ref[0, :, new0:new0 + n] = piece.astype(BF16)
    o_ref[0, :, n_real:] = jnp.zeros((o_ref.shape[1], N_IN - n_real), BF16)


def _prep_w_in(w_in):
    depth = w_in.shape[0]
    perm, _ = _in_perm()
    n_real = C_DT + 2 * SSD_HEADS
    tr = 256
    return pl.pallas_call(
        functools.partial(_prep_w_in_kernel, runs=_col_runs(perm[:n_real]), n_real=n_real),
        grid=(depth, D_MODEL // tr),
        in_specs=[pl.BlockSpec((1, tr, IN_COLS), lambda l, r: (l, r, 0))],
        out_specs=pl.BlockSpec((1, tr, N_IN), lambda l, r: (l, r, 0)),
        out_shape=jax.ShapeDtypeStruct((depth, D_MODEL, N_IN), BF16),
        compiler_params=_cparams(("parallel", "parallel")),
        name="prep_w_in",
    )(w_in)


def _cparams(sem):
    return pltpu.CompilerParams(dimension_semantics=sem, vmem_limit_bytes=VMEM_LIMIT)


def _silu(x):
    return x * (1.0 / (1.0 + jnp.exp(-x)))


def _softplus(x):
    return jnp.maximum(x, 0.0) + jnp.log1p(jnp.exp(-jnp.abs(x)))


def _inproj_kernel(x_ref, g_ref, w_ref, u_ref, dt_ref, h_ref, *, rows, tn):
    for r in range(x_ref.shape[0] // rows):
        sl = slice(r * rows, (r + 1) * rows)
        x = x_ref[sl, :]
        ms = jnp.mean(x * x, axis=-1, keepdims=True)
        h_ref[sl, :] = (x * lax.rsqrt(ms + EPS) * g_ref[...]).astype(BF16)
        h = h_ref[sl, :]
        for c0 in range(0, N_IN, tn):
            c1 = min(c0 + tn, N_IN)
            acc = jnp.dot(h, w_ref[:, c0:c1], preferred_element_type=F32)
            u_ref[sl, c0:c1] = acc.astype(BF16)
            if c0 <= C_DT < c1:
                dt_ref[sl, :] = acc[:, C_DT - c0:C_DT - c0 + DT_PAD]


def _inproj(x2, g, w_all, layer, *, tm, tn):
    T = x2.shape[0]
    assert T % tm == 0
    return pl.pallas_call(
        functools.partial(_inproj_kernel, rows=min(256, tm), tn=tn),
        grid=(T // tm,),
        in_specs=[pl.BlockSpec((tm, D_MODEL), lambda i: (i, 0)),
                  pl.BlockSpec((1, D_MODEL), lambda i: (0, 0)),
                  pl.BlockSpec((None, D_MODEL, N_IN), lambda i: (layer, 0, 0),
                               pipeline_mode=pl.Buffered(1))],
        out_specs=[pl.BlockSpec((tm, N_IN), lambda i: (i, 0)),
                   pl.BlockSpec((tm, DT_PAD), lambda i: (i, 0))],
        out_shape=[jax.ShapeDtypeStruct((T, N_IN), BF16),
                   jax.ShapeDtypeStruct((T, DT_PAD), F32)],
        scratch_shapes=[pltpu.VMEM((tm, D_MODEL), BF16)],
        compiler_params=_cparams(("parallel",)),
        name="inproj",
    )(x2, g, w_all)


def _outproj_kernel(x_ref, ys_ref, ya_ref, c_ref, ueo_ref, xa_ref, g_ref, d_ref, hng_ref, w_ref, fg_ref,
                    o_ref, s_ref, *, rows, final):
    nblk = HY_WIDTH // LANES

    def interleave(ref, hs):
        for p in range(2):
            for c in range(nblk):
                s_ref[c, pl.ds(p, rows // 2, stride=2), :] = ref[
                    hs, (p * nblk + c) * LANES:(p * nblk + c + 1) * LANES].astype(F32)
        return jnp.concatenate([s_ref[c] for c in range(nblk)], axis=1)

    for r in range(x_ref.shape[0] // rows):
        sl = slice(r * rows, (r + 1) * rows)
        hs = slice(r * rows // 2, (r + 1) * rows // 2)
        conv = interleave(c_ref, hs)
        u = interleave(ueo_ref, hs)
        xa = xa_ref[sl, :].astype(F32)
        y = xa * (conv + u * d_ref[...])
        y = y * _silu(g_ref[sl, :].astype(F32))
        ms = jnp.mean(y * y, axis=-1, keepdims=True)
        yh = (y * lax.rsqrt(ms + EPS) * hng_ref[...]).astype(BF16)

        acc = x_ref[sl, :]
        acc = acc + jnp.dot(ys_ref[sl, :], w_ref[0:SSD_WIDTH, :], preferred_element_type=F32)
        acc = acc + jnp.dot(ya_ref[sl, :], w_ref[SSD_WIDTH:SSD_WIDTH + ATT_WIDTH, :],
                            preferred_element_type=F32)
        acc = acc + jnp.dot(yh, w_ref[SSD_WIDTH + ATT_WIDTH:, :], preferred_element_type=F32)
        if final:
            ms = jnp.mean(acc * acc, axis=-1, keepdims=True)
            acc = acc * lax.rsqrt(ms + EPS) * fg_ref[...]
        o_ref[sl, :] = acc


def _outproj(x2, ys, ya, conv2, ueo2, xa2, u2, d, hng, w, fg, *, tm, final):
    T = x2.shape[0]
    assert T % tm == 0
    rows = min(256, tm)
    return pl.pallas_call(
        functools.partial(_outproj_kernel, rows=rows, final=final),
        grid=(T // tm,),
        in_specs=[pl.BlockSpec((tm, D_MODEL), lambda i: (i, 0)),
                  pl.BlockSpec((tm, SSD_WIDTH), lambda i: (i, 0)),
                  pl.BlockSpec((tm, ATT_WIDTH), lambda i: (i, 0)),
                  pl.BlockSpec((tm // 2, 2 * HY_WIDTH), lambda i: (i, 0)),
                  pl.BlockSpec((tm // 2, 2 * HY_WIDTH), lambda i: (i, 0)),
                  pl.BlockSpec((tm, HY_WIDTH), lambda i: (i, 0)),
                  pl.BlockSpec((tm, HY_WIDTH), lambda i: (i, C_GHY // HY_WIDTH)),
                  pl.BlockSpec((1, HY_WIDTH), lambda i: (0, 0)),
                  pl.BlockSpec((1, HY_WIDTH), lambda i: (0, 0)),
                  pl.BlockSpec((D_MODEL, D_MODEL), lambda i: (0, 0)),
                  pl.BlockSpec((1, D_MODEL), lambda i: (0, 0))],
        out_specs=pl.BlockSpec((tm, D_MODEL), lambda i: (i, 0)),
        out_shape=jax.ShapeDtypeStruct((T, D_MODEL), F32),
        scratch_shapes=[pltpu.VMEM((HY_WIDTH // LANES, rows, LANES), F32)],
        compiler_params=_cparams(("parallel",)),
        name="outproj",
    )(x2, ys, ya, conv2, ueo2, xa2, u2, d, hng, w, fg)


def _t5_buckets(rel):
    nb = REL_BUCKETS // 2
    max_exact = nb // 2
    ret = (rel > 0).astype(np.int32) * nb
    n = np.abs(rel)
    large = max_exact + (np.log(np.maximum(n, 1) / max_exact) / math.log(REL_MAX_DIST / max_exact)
                         * (nb - max_exact)).astype(np.int32)
    large = np.minimum(large, nb - 1)
    return ret + np.where(n < max_exact, n, large)


def _attn_bias(rel_bias):
    qi = np.arange(ATT_BLOCK)[:, None]
    kj = np.arange(3 * ATT_BLOCK)[None, :]
    rel = kj - ATT_BLOCK - qi
    onehot = (_t5_buckets(rel)[None] == np.arange(REL_BUCKETS)[:, None, None]).astype(np.float32)
    bias = jnp.einsum("bqk,bh->hqk", jnp.asarray(onehot), rel_bias.astype(F32),
                      precision=HIGHEST) * LOG2E
    window = np.abs(rel) <= ATT_WINDOW
    variants = []
    for last in (False, True):
        for first in (False, True):
            ok = window & ~(first & (kj < ATT_BLOCK)) & ~(last & (kj >= 2 * ATT_BLOCK))
            variants.append(jnp.where(ok[None], bias, NEG_BIG))
    return jnp.transpose(jnp.stack(variants), (0, 1, 3, 2))


def _attn_kernel(q_ref, kp_ref, kc_ref, kn_ref, vp_ref, vc_ref, vn_ref, g_ref, bias_ref, sink_ref,
                 ng_ref, o_ref, klo_ref, khi_ref, vt_ref, *, nt, qb):
    n = pl.program_id(1)
    lo = lax.broadcasted_iota(jnp.int32, (1, LANES), 1) < ATT_HEAD_DIM
    zero = jnp.zeros((), BF16)
    kext = jnp.concatenate([kp_ref[0], kc_ref[0], kn_ref[0]], axis=0)
    klo_ref[...] = jnp.where(lo, kext, zero)
    khi_ref[...] = jnp.where(lo, zero, kext)
    for t, ref, cnt in ((0, vp_ref, 1), (1, vc_ref, qb), (qb + 1, vn_ref, 1)):
        for i in range(cnt):
            blk = ref[0, i * ATT_BLOCK:(i + 1) * ATT_BLOCK, :]
            vt_ref[t + i] = blk.astype(F32).T.astype(BF16)
    row_lo = lax.broadcasted_iota(jnp.int32, (LANES, ATT_BLOCK), 0) < ATT_HEAD_DIM
    nk = 3 * ATT_BLOCK

    def body(i, carry):
        r0 = pl.multiple_of(i * ATT_BLOCK, ATT_BLOCK)
        q = q_ref[0, pl.ds(r0, ATT_BLOCK), :]
        kst = jnp.concatenate([klo_ref[pl.ds(r0, nk), :], khi_ref[pl.ds(r0, nk), :]], axis=0)
        vt = jnp.concatenate([vt_ref[i], vt_ref[i + 1], vt_ref[i + 2]], axis=1)
        first = jnp.logical_and(n == 0, i == 0)
        last = jnp.logical_and(n == nt - 1, i == qb - 1)
        variant = first.astype(jnp.int32) + 2 * last.astype(jnp.int32)
        outs = []
        for j in range(ATT_REP):
            qp = q[:, j * LANES:(j + 1) * LANES]
            st = lax.dot_general(kst, qp, (((1,), (1,)), ((), ())), preferred_element_type=F32)
            halves = []
            for e, h in enumerate((j, ATT_REP + j)):
                s = st[e * nk:(e + 1) * nk] + bias_ref[variant, h]
                sk = sink_ref[h:h + 1, :]
                m = jnp.maximum(jnp.max(s, axis=0, keepdims=True), sk)
                p = jnp.exp2(s - m)
                den = jnp.sum(p, axis=0, keepdims=True) + jnp.exp2(sk - m)
                ot = jnp.dot(vt, p.astype(BF16), preferred_element_type=F32)
                halves.append(ot * (1.0 / den))
            outs.append(jnp.where(row_lo, halves[0], halves[1]).T)
        o = jnp.concatenate(outs, axis=-1)
        y = o * _silu(g_ref[0, pl.ds(r0, ATT_BLOCK), :].astype(F32))
        ms = jnp.mean(y * y, axis=-1, keepdims=True)
        o_ref[0, pl.ds(r0, ATT_BLOCK), :] = (y * lax.rsqrt(ms + EPS) * ng_ref[...]).astype(BF16)
        return carry

    lax.fori_loop(0, qb, body, 0, unroll=True)


def _attention(u3, bias, sink, ng, *, qb):
    B, L, _ = u3.shape
    tq = qb * ATT_BLOCK
    assert L % tq == 0
    nt = L // tq
    nb = L // ATT_BLOCK
    kcol, vcol = C_K // 128, C_V // 128
    def kv_specs(colblk):
        return [pl.BlockSpec((1, ATT_BLOCK, 128), lambda b, n: (b, jnp.maximum(n * qb - 1, 0), colblk)),
                pl.BlockSpec((1, tq, 128), lambda b, n: (b, n, colblk)),
                pl.BlockSpec((1, ATT_BLOCK, 128),
                             lambda b, n: (b, jnp.minimum((n + 1) * qb, nb - 1), colblk))]
    return pl.pallas_call(
        functools.partial(_attn_kernel, nt=nt, qb=qb),
        grid=(B, nt),
        in_specs=[pl.BlockSpec((1, tq, ATT_WIDTH), lambda b, n: (b, n, C_Q // ATT_WIDTH))]
                 + kv_specs(kcol) + kv_specs(vcol)
                 + [pl.BlockSpec((1, tq, ATT_WIDTH), lambda b, n: (b, n, C_GATT // ATT_WIDTH)),
                    pl.BlockSpec((4, ATT_HEADS, 3 * ATT_BLOCK, ATT_BLOCK), lambda b, n: (0, 0, 0, 0)),
                    pl.BlockSpec((ATT_HEADS, LANES), lambda b, n: (0, 0)),
                    pl.BlockSpec((1, ATT_WIDTH), lambda b, n: (0, 0))],
        out_specs=pl.BlockSpec((1, tq, ATT_WIDTH), lambda b, n: (b, n, 0)),
        out_shape=jax.ShapeDtypeStruct((B, L, ATT_WIDTH), BF16),
        scratch_shapes=[pltpu.VMEM(((qb + 2) * ATT_BLOCK, LANES), BF16),
                        pltpu.VMEM(((qb + 2) * ATT_BLOCK, LANES), BF16),
                        pltpu.VMEM((qb + 2, LANES, ATT_BLOCK), BF16)],
        compiler_params=_cparams(("parallel", "parallel")),
        name="attn",
    )(u3, u3, u3, u3, u3, u3, u3, u3, bias, sink, ng)


HALO = BF16_SUBLANES


CONV_BLK = 128


def _load_padded(xpad_ref, xm_ref, xp_ref, xn_ref, has_prev, has_next, rows):
    zero = jnp.zeros((), BF16)
    xpad_ref[pl.ds(0, HALO), :] = jnp.where(has_prev, xp_ref[0], zero)
    xpad_ref[pl.ds(HALO, rows), :] = xm_ref[0]
    xpad_ref[pl.ds(HALO + rows, HALO), :] = jnp.where(has_next, xn_ref[0], zero)


def _shift_matrix(width):
    offs = [k - width // 2 for k in range(width) if k != width // 2]
    r = lax.broadcasted_iota(jnp.int32, (CONV_BLK, CONV_BLK + 2 * HALO), 0)
    c = lax.broadcasted_iota(jnp.int32, (CONV_BLK, CONV_BLK + 2 * HALO), 1)
    return jnp.concatenate([(c == r + HALO + d).astype(BF16) for d in offs], axis=0)


def _dwconv_block(xpad_ref, shifts, w_ref, b_ref, j, width, c0, c1):
    win = xpad_ref[j * CONV_BLK:(j + 1) * CONV_BLK + 2 * HALO, c0:c1]
    moved = jnp.dot(shifts, win, preferred_element_type=F32)
    acc = win[HALO:HALO + CONV_BLK].astype(F32) * w_ref[width // 2:width // 2 + 1, c0:c1]
    i = 0
    for k in range(width):
        if k == width // 2:
            continue
        acc = acc + moved[i * CONV_BLK:(i + 1) * CONV_BLK] * w_ref[k:k + 1, c0:c1]
        i += 1
    return acc + b_ref[:, c0:c1]


def _dwconv(xpad_ref, w_ref, b_ref, rows, width, c0, c1):
    shifts = _shift_matrix(width)
    return jnp.concatenate([_dwconv_block(xpad_ref, shifts, w_ref, b_ref, j, width, c0, c1)
                            for j in range(rows // CONV_BLK)], axis=0)


SSD_HPG = SSD_HEADS // SSD_GROUPS
SSD_GW = SSD_HPG * SSD_HEAD_DIM
SSD_BC = SSD_GROUPS * SSD_STATE


def _ssd_decay(dt_raw, dtb_ref, alog_ref):
    dt = _softplus(dt_raw + dtb_ref[...])
    dta = dt * (-jnp.exp(alog_ref[...]))
    row = lax.broadcasted_iota(jnp.int32, (CHUNK, CHUNK), 0)
    col = lax.broadcasted_iota(jnp.int32, (CHUNK, CHUNK), 1)
    tril = (row >= col).astype(BF16)
    hi = dta.astype(BF16)
    r1 = dta - hi.astype(F32)
    mid = r1.astype(BF16)
    lo = (r1 - mid.astype(F32)).astype(BF16)
    pre = (jnp.dot(tril, hi, preferred_element_type=F32) + jnp.dot(tril, mid, preferred_element_type=F32)
           + jnp.dot(tril, lo, preferred_element_type=F32))
    tot = pre[CHUNK - 1:CHUNK, :]
    is_bwd = lax.broadcasted_iota(jnp.int32, (1, LANES), 1) >= SSD_HEADS
    acs = jnp.where(is_bwd, tot - pre + dta, pre)
    return dt, acs, tot


def _ssd_bwd_kernel(xm_ref, xp_ref, xn_ref, dt_ref, cw_ref, cb_ref, dtb_ref, alog_ref, rexp_ref,
                    xact_ref, prev_ref, xpad_ref, xf_ref, st_ref, *, nt, qb):
    n = pl.program_id(1)
    tile = nt - 1 - n
    rows = qb * CHUNK

    @pl.when(n == 0)
    def _():
        st_ref[...] = jnp.zeros(st_ref.shape, F32)

    _load_padded(xpad_ref, xm_ref, xp_ref, xn_ref, tile > 0, tile < nt - 1, rows)
    xact = _silu(_dwconv(xpad_ref, cw_ref, cb_ref, rows, SSD_CONV, 0, SSD_XBC))
    xf_ref[...] = xact
    xact_ref[0] = xact.astype(BF16)

    def body(j, carry):
        i = qb - 1 - j
        r0 = pl.multiple_of(i * CHUNK, CHUNK)
        xs = xf_ref[pl.ds(r0, CHUNK), 0:SSD_WIDTH]
        bm = xf_ref[pl.ds(r0, CHUNK), SSD_WIDTH:SSD_WIDTH + SSD_BC].astype(BF16)
        dt, acs, tot = _ssd_decay(dt_ref[0, pl.ds(r0, CHUNK), :], dtb_ref, alog_ref)
        e_in = jnp.concatenate([jnp.exp(tot - acs) * dt, jnp.broadcast_to(jnp.exp(tot), (8, LANES))],
                               axis=0).astype(BF16)
        e_out = jnp.dot(e_in, rexp_ref[...], preferred_element_type=F32)
        xd = (xs * e_out[0:CHUNK]).astype(BF16)
        cdec = e_out[CHUNK:CHUNK + 1]
        for g in range(SSD_GROUPS):
            prev = st_ref[g]
            prev_ref[0, i, g] = prev.astype(BF16)
            s_new = lax.dot_general(bm[:, g * SSD_STATE:(g + 1) * SSD_STATE],
                                    xd[:, g * SSD_GW:(g + 1) * SSD_GW], (((0,), (0,)), ((), ())),
                                    preferred_element_type=F32)
            st_ref[g] = prev * cdec[:, g * SSD_GW:(g + 1) * SSD_GW] + s_new
        return carry

    lax.fori_loop(0, qb, body, 0, unroll=True)


def _ssd_fwd_kernel(xa_ref, dt_ref, prev_ref, z_ref, dtb_ref, alog_ref, rexpf_ref, rexpb_ref,
                    dsk_ref, ng_ref, o_ref, st_ref, *, qb):
    n = pl.program_id(1)

    @pl.when(n == 0)
    def _():
        st_ref[...] = jnp.zeros(st_ref.shape, F32)

    row = lax.broadcasted_iota(jnp.int32, (CHUNK, CHUNK), 0)
    col = lax.broadcasted_iota(jnp.int32, (CHUNK, CHUNK), 1)
    fwd_part = row > col
    diag = row == col
    lane_lo = lax.broadcasted_iota(jnp.int32, (1, LANES), 1) < SSD_HEAD_DIM
    zero_b = jnp.zeros((), BF16)

    def body(i, carry):
        r0 = pl.multiple_of(i * CHUNK, CHUNK)
        xs_b = xa_ref[0, pl.ds(r0, CHUNK), 0:SSD_WIDTH]
        xs = xs_b.astype(F32)
        bm = xa_ref[0, pl.ds(r0, CHUNK), SSD_WIDTH:SSD_WIDTH + SSD_BC]
        cm = xa_ref[0, pl.ds(r0, CHUNK), SSD_WIDTH + SSD_BC:SSD_XBC]
        dt, acs, tot = _ssd_decay(dt_ref[0, pl.ds(r0, CHUNK), :], dtb_ref, alog_ref)
        acs2 = acs * LOG2E
        rt = (acs2 - jnp.log2(dt)).T
        dsum_t = jnp.log2(dt + pltpu.roll(dt, LANES - SSD_HEADS, axis=1)).T
        eacs = jnp.exp(acs)
        ef_in = jnp.concatenate([jnp.exp(tot - acs) * dt, eacs,
                                 jnp.broadcast_to(jnp.exp(tot), (8, LANES))], axis=0).astype(BF16)
        ef = jnp.dot(ef_in, rexpf_ref[...], preferred_element_type=F32)
        eb = jnp.dot(eacs.astype(BF16), rexpb_ref[...], preferred_element_type=F32)
        xd = (xs * ef[0:CHUNK]).astype(BF16)
        eacs_f = ef[CHUNK:2 * CHUNK]
        cdec = ef[2 * CHUNK:2 * CHUNK + 1]
        ys = []
        for g in range(SSD_GROUPS):
            bg = bm[:, g * SSD_STATE:(g + 1) * SSD_STATE]
            cg = cm[:, g * SSD_STATE:(g + 1) * SSD_STATE]
            gs = slice(g * SSD_GW, (g + 1) * SSD_GW)
            cb = lax.dot_general(cg, bg, (((1,), (1,)), ((), ())), preferred_element_type=F32)
            prev = st_ref[g]
            y_off = (jnp.dot(cg, prev.astype(BF16), preferred_element_type=F32) * eacs_f[:, gs]
                     + jnp.dot(cg, prev_ref[0, i, g], preferred_element_type=F32) * eb[:, gs])
            s_new = lax.dot_general(bg, xd[:, gs], (((0,), (0,)), ((), ())),
                                    preferred_element_type=F32)
            st_ref[g] = prev * cdec[:, gs] + s_new
            for pr in range(SSD_HPG // 2):
                c0 = g * SSD_GW + pr * LANES
                xpair = xs_b[:, c0:c0 + LANES]
                xbd = jnp.concatenate([jnp.where(lane_lo, xpair, zero_b),
                                       jnp.where(lane_lo, zero_b, xpair)], axis=0)
                mats = []
                for e in range(2):
                    h = g * SSD_HPG + pr * 2 + e
                    hb = SSD_HEADS + h
                    sel = jnp.where(fwd_part, acs2[:, h:h + 1] - rt[h:h + 1, :],
                                    acs2[:, hb:hb + 1] - rt[hb:hb + 1, :])
                    sel = jnp.where(diag, dsum_t[h:h + 1, :], sel)
                    mats.append((cb * jnp.exp2(sel)).astype(BF16))
                yd = jnp.dot(jnp.concatenate(mats, axis=1), xbd, preferred_element_type=F32)
                ys.append(yd + y_off[:, pr * LANES:(pr + 1) * LANES])
        y = jnp.concatenate(ys, axis=-1) + xs * dsk_ref[...]
        y = y * _silu(z_ref[0, pl.ds(r0, CHUNK), :].astype(F32))
        ms = jnp.mean(y * y, axis=-1, keepdims=True)
        o_ref[0, pl.ds(r0, CHUNK), :] = (y * lax.rsqrt(ms + EPS) * ng_ref[...]).astype(BF16)
        return carry

    lax.fori_loop(0, qb, body, 0, unroll=True)


def _ssd(u3, dt3, cw, cb, dtb, alog, rexp_f, rexp_b, dsk, ng, *, qb):
    B, L, _ = u3.shape
    rows = qb * CHUNK
    assert L % rows == 0
    nt = L // rows
    nc = L // CHUNK
    hb = rows // HALO
    nhb = L // HALO
    state = pltpu.VMEM((SSD_GROUPS, SSD_STATE, SSD_GW), F32)
    const = lambda shape: pl.BlockSpec(shape, lambda b, n: (0,) * len(shape))
    rt = lambda n: nt - 1 - n

    xact, prevb = pl.pallas_call(
        functools.partial(_ssd_bwd_kernel, nt=nt, qb=qb),
        grid=(B, nt),
        in_specs=[
            pl.BlockSpec((1, rows, SSD_XBC), lambda b, n: (b, rt(n), C_XBC // SSD_XBC)),
            pl.BlockSpec((1, HALO, SSD_XBC),
                         lambda b, n: (b, jnp.maximum(rt(n) * hb - 1, 0), C_XBC // SSD_XBC)),
            pl.BlockSpec((1, HALO, SSD_XBC),
                         lambda b, n: (b, jnp.minimum((rt(n) + 1) * hb, nhb - 1), C_XBC // SSD_XBC)),
            pl.BlockSpec((1, rows, DT_PAD), lambda b, n: (b, rt(n), 0)),
            const((8, SSD_XBC)), const((1, SSD_XBC)), const((1, DT_PAD)), const((1, DT_PAD)),
            const((LANES, SSD_WIDTH)),
        ],
        out_specs=[pl.BlockSpec((1, rows, SSD_XBC), lambda b, n: (b, rt(n), 0)),
                   pl.BlockSpec((1, qb, SSD_GROUPS, SSD_STATE, SSD_GW),
                                lambda b, n: (b, rt(n), 0, 0, 0))],
        out_shape=[jax.ShapeDtypeStruct((B, L, SSD_XBC), BF16),
                   jax.ShapeDtypeStruct((B, nc, SSD_GROUPS, SSD_STATE, SSD_GW), BF16)],
        scratch_shapes=[pltpu.VMEM((rows + 2 * HALO, SSD_XBC), BF16),
                        pltpu.VMEM((rows, SSD_XBC), F32), state],
        compiler_params=_cparams(("parallel", "arbitrary")),
        name="ssd_bwd",
    )(u3, u3, u3, dt3, cw, cb, dtb, alog, rexp_b)

    return pl.pallas_call(
        functools.partial(_ssd_fwd_kernel, qb=qb),
        grid=(B, nt),
        in_specs=[
            pl.BlockSpec((1, rows, SSD_XBC), lambda b, n: (b, n, 0)),
            pl.BlockSpec((1, rows, DT_PAD), lambda b, n: (b, n, 0)),
            pl.BlockSpec((1, qb, SSD_GROUPS, SSD_STATE, SSD_GW), lambda b, n: (b, n, 0, 0, 0)),
            pl.BlockSpec((1, rows, SSD_WIDTH), lambda b, n: (b, n, C_Z // SSD_WIDTH)),
            const((1, DT_PAD)), const((1, DT_PAD)),
            const((LANES, SSD_WIDTH)), const((LANES, SSD_WIDTH)),
            const((1, SSD_WIDTH)), const((1, SSD_WIDTH)),
        ],
        out_specs=pl.BlockSpec((1, rows, SSD_WIDTH), lambda b, n: (b, n, 0)),
        out_shape=jax.ShapeDtypeStruct((B, L, SSD_WIDTH), BF16),
        scratch_shapes=[state],
        compiler_params=_cparams(("parallel", "arbitrary")),
        name="ssd_fwd",
    )(xact, dt3, prevb, u3, dtb, alog, rexp_f, rexp_b, dsk, ng)


def _ssd_expand_matrix(rev):
    m = np.zeros((LANES, SSD_WIDTH), np.float32)
    hoff = SSD_HEADS if rev else 0
    for h in range(SSD_HEADS):
        m[hoff + h, h * SSD_HEAD_DIM:(h + 1) * SSD_HEAD_DIM] = 1.0
    return jnp.asarray(m, BF16)


def _hy_filter_kernel(t_ref, bands_ref, w1t_ref, w1c_ref, w1s_ref, b1_ref, w2_ref, b2_ref, w3_ref,
                      b3_ref, w4_ref, fr_ref, absd_ref, x_ref, s_ref, *, L, tl):
    i = pl.program_id(0)
    pos = (lax.broadcasted_iota(jnp.int32, (tl, 1), 0) + i * tl).astype(F32)
    t = t_ref[...]
    pos_row = (lax.broadcasted_iota(jnp.int32, (1, tl), 1) + i * tl).astype(F32)
    ang_t = 2.0 * math.pi * pos_row * bands_ref[...] / L
    fr = fr_ref[...]
    dot = functools.partial(jnp.dot, preferred_element_type=F32, precision=HIGHEST)
    tdot = lambda a, b: lax.dot_general(a, b, (((0,), (0,)), ((), ())), preferred_element_type=F32,
                                        precision=HIGHEST)
    pre = t * w1t_ref[...] + tdot(jnp.cos(ang_t), w1c_ref[...]) + tdot(-jnp.sin(ang_t), w1s_ref[...])
    h = jnp.sin(fr * (pre + b1_ref[...]))
    h = jnp.sin(fr * (dot(h, w2_ref[...]) + b2_ref[...]))
    h = jnp.sin(fr * (dot(h, w3_ref[...]) + b3_ref[...]))
    w4 = w4_ref[...]
    h_hi, w_hi = h.astype(BF16), w4.astype(BF16)
    h_lo, w_lo = (h - h_hi.astype(F32)).astype(BF16), (w4 - w_hi.astype(F32)).astype(BF16)
    bdot = functools.partial(jnp.dot, preferred_element_type=F32)
    h = bdot(h_hi, w_hi) + bdot(h_hi, w_lo) + bdot(h_lo, w_hi)
    decay = jnp.exp(-t * absd_ref[...])
    hf = h[:, :HY_WIDTH] * decay
    hb = jnp.where(pos == 0.0, 0.0, h[:, HY_WIDTH:] * decay)
    _to_lane_blocks(s_ref, jnp.concatenate([hf + hb, hb - hf], axis=1))
    x_ref[...] = _split_even_odd(s_ref).astype(BF16)


def _hy_filter(L, w1, b1, w2, b2, w3, b3, w4, freq):
    tl = min(512, L)
    t = jnp.linspace(0.0, 1.0, L, dtype=F32)[:, None]
    bands = jnp.linspace(1e-4, HY_EMB_BANDS - 1, HY_EMB_BANDS, dtype=F32)[:, None]
    max_decay = math.log(HY_DECAY_TARGET) / HY_FAST_DECAY
    min_decay = math.log(HY_DECAY_TARGET) / HY_SLOW_DECAY
    absd = jnp.abs(jnp.linspace(min_decay, max_decay, HY_WIDTH, dtype=F32))[None, :]
    w1 = w1.astype(F32)
    full = lambda a: pl.BlockSpec(a.shape, lambda i: (0,) * a.ndim)
    args = [t, bands, w1[0:1], w1[1:1 + HY_EMB_BANDS], w1[1 + HY_EMB_BANDS:], b1[None], w2, b2[None],
            w3, b3[None], w4, freq[None], absd]
    in_specs = [pl.BlockSpec((tl, 1), lambda i: (i, 0))] + [full(a) for a in args[1:]]
    return pl.pallas_call(
        functools.partial(_hy_filter_kernel, L=L, tl=tl),
        grid=(L // tl,),
        in_specs=in_specs,
        out_specs=pl.BlockSpec((tl // 2, 4 * HY_WIDTH), lambda i: (i, 0)),
        out_shape=jax.ShapeDtypeStruct((L // 2, 4 * HY_WIDTH), BF16),
        scratch_shapes=[pltpu.VMEM((2 * HY_WIDTH // LANES, tl, LANES), F32)],
        compiler_params=_cparams(("parallel",)),
        name="hy_filter",
    )(*args)


def _dft_tables(L, hm):
    L2 = L // 2
    g = jnp.arange(L2, dtype=jnp.int32)
    s = jnp.arange(L2, dtype=jnp.int32)
    ph = ((2 * g + 1)[:, None] * s[None, :]) % (2 * L)
    ang = ph.astype(F32) * (math.pi / L)
    c = jnp.cos(ang).astype(BF16).reshape(L2 // hm, 1, hm, L2)
    sn = jnp.sin(ang).astype(BF16).reshape(L2 // hm, 1, hm, L2)
    a_fwd = jnp.concatenate([c, sn], axis=1).reshape(L, L2)
    a_inv = jnp.concatenate([c, -sn], axis=1).reshape(L, L2).T
    return a_fwd, a_inv


def _hy_kspec_kernel(a_ref, x_ref, wc_ref, ws_ref, f_ref, *, hm, scale):
    W = HY_WIDTH
    pq = jnp.dot(a_ref[...], x_ref[...], preferred_element_type=F32)
    ea, eb, oa, ob = [(pq[0:hm, i * W:(i + 1) * W], -pq[hm:, i * W:(i + 1) * W]) for i in range(4)]
    w = (wc_ref[...], -ws_ref[...])
    cmul = lambda x, y: (x[0] * y[0] - x[1] * y[1], x[0] * y[1] + x[1] * y[0])
    woa, wob = cmul(w, oa), cmul(w, ob)
    k1 = (ea[0] + woa[0], -(eb[1] + wob[1]))
    k2 = (ea[0] - woa[0], eb[1] - wob[1])
    kp = (k1[0] + k2[0], k1[1] - k2[1])
    km = (k1[0] - k2[0], k1[1] + k2[1])
    wkm = cmul(w, km)
    vkm = cmul((w[0], -w[1]), km)
    for i, part in enumerate((kp[0], kp[1], wkm[0], wkm[1], vkm[0], vkm[1])):
        f_ref[0, i] = part * scale


def _hy_kspec(a_fwd, xf2, *, hm):
    L, L2 = a_fwd.shape
    tm = 2 * hm
    nx = xf2.shape[1]
    theta = (2.0 * jnp.arange(L2, dtype=F32) + 1.0) * (math.pi / (2 * L))
    return pl.pallas_call(
        functools.partial(_hy_kspec_kernel, hm=hm, scale=1.0 / L),
        grid=(L // tm,),
        in_specs=[pl.BlockSpec((tm, L2), lambda i: (i, 0)),
                  pl.BlockSpec((L2, nx), lambda i: (0, 0)),
                  pl.BlockSpec((hm, 1), lambda i: (i, 0)),
                  pl.BlockSpec((hm, 1), lambda i: (i, 0))],
        out_specs=pl.BlockSpec((1, 6, hm, HY_WIDTH), lambda i: (i, 0, 0, 0)),
        out_shape=jax.ShapeDtypeStruct((L2 // hm, 6, hm, HY_WIDTH), F32),
        compiler_params=_cparams(("parallel",)),
        name="hy_kspec",
    )(a_fwd, xf2, jnp.cos(theta)[:, None], jnp.sin(theta)[:, None])


def _to_lane_blocks(s_ref, x):
    for c in range(s_ref.shape[0]):
        s_ref[c] = x[:, c * LANES:(c + 1) * LANES]


def _split_even_odd(s_ref):
    k, rows, _ = s_ref.shape
    return jnp.concatenate([s_ref[c, pl.ds(p, rows // 2, stride=2), :]
                            for p in range(2) for c in range(k)], axis=1)


def _hy_pre_kernel(xm_ref, xp_ref, xn_ref, cw_ref, cb_ref, u_ref, xa_ref, xpad_ref, s_ref, *, nt, tl):
    i = pl.program_id(1)
    _load_padded(xpad_ref, xm_ref, xp_ref, xn_ref, i > 0, i < nt - 1, tl)
    xa = _dwconv(xpad_ref, cw_ref, cb_ref, tl, HY_CONV, 0, HY_WIDTH)
    xb = _dwconv(xpad_ref, cw_ref, cb_ref, tl, HY_CONV, HY_WIDTH, 2 * HY_WIDTH)
    v = _dwconv(xpad_ref, cw_ref, cb_ref, tl, HY_CONV, 2 * HY_WIDTH, 3 * HY_WIDTH)
    u = xb * v
    xa_ref[0] = xa.astype(BF16)
    _to_lane_blocks(s_ref, u)
    u_ref[0] = _split_even_odd(s_ref).astype(BF16)


def _hy_tile_specs(L, tl):
    hb = tl // HALO
    nhb = L // HALO
    w = 3 * HY_WIDTH
    return [pl.BlockSpec((1, tl, w), lambda b, i: (b, i, C_HY // w)),
            pl.BlockSpec((1, HALO, w), lambda b, i: (b, jnp.maximum(i * hb - 1, 0), C_HY // w)),
            pl.BlockSpec((1, HALO, w), lambda b, i: (b, jnp.minimum((i + 1) * hb, nhb - 1), C_HY // w))]


def _hy_pre(u3, cw, cb, *, tl):
    B, L, _ = u3.shape
    nt = L // tl
    w = 3 * HY_WIDTH
    return pl.pallas_call(
        functools.partial(_hy_pre_kernel, nt=nt, tl=tl),
        grid=(B, nt),
        in_specs=_hy_tile_specs(L, tl) + [pl.BlockSpec((8, w), lambda b, i: (0, 0)),
                                          pl.BlockSpec((1, w), lambda b, i: (0, 0))],
        out_specs=[pl.BlockSpec((1, tl // 2, 2 * HY_WIDTH), lambda b, i: (b, i, 0)),
                   pl.BlockSpec((1, tl, HY_WIDTH), lambda b, i: (b, i, 0))],
        out_shape=[jax.ShapeDtypeStruct((B, L // 2, 2 * HY_WIDTH), BF16),
                   jax.ShapeDtypeStruct((B, L, HY_WIDTH), BF16)],
        scratch_shapes=[pltpu.VMEM((tl + 2 * HALO, w), BF16),
                        pltpu.VMEM((HY_WIDTH // LANES, tl, LANES), F32)],
        compiler_params=_cparams(("parallel", "parallel")),
        name="hy_pre",
    )(u3, u3, u3, cw, cb)


def _hy_fwd_kernel(a_ref, u_ref, f_ref, y_ref, *, hm):
    W = HY_WIDTH
    pq = jnp.dot(a_ref[...], u_ref[0], preferred_element_type=F32)
    pe, po = pq[0:hm, :W], pq[0:hm, W:]
    qe, qo = pq[hm:, :W], pq[hm:, W:]
    kpr, kpi, wr, wi, vr, vi = [f_ref[0, i] for i in range(6)]
    y_ref[0, 0:hm, :W] = (pe * kpr + qe * kpi + po * wr + qo * wi).astype(BF16)
    y_ref[0, hm:, :W] = (pe * kpi - qe * kpr + po * wi - qo * wr).astype(BF16)
    y_ref[0, 0:hm, W:] = (pe * vr + qe * vi + po * kpr + qo * kpi).astype(BF16)
    y_ref[0, hm:, W:] = (pe * vi - qe * vr + po * kpi - qo * kpr).astype(BF16)


def _hy_fwd(a_fwd, ueo, filt, *, hm):
    B, L2, W2 = ueo.shape
    L = 2 * L2
    tm = 2 * hm
    return pl.pallas_call(
        functools.partial(_hy_fwd_kernel, hm=hm),
        grid=(L // tm, B),
        in_specs=[pl.BlockSpec((tm, L2), lambda i, b: (i, 0)),
                  pl.BlockSpec((1, L2, W2), lambda i, b: (b, 0, 0)),
                  pl.BlockSpec((1, 6, hm, HY_WIDTH), lambda i, b: (i, 0, 0, 0))],
        out_specs=pl.BlockSpec((1, tm, W2), lambda i, b: (b, i, 0)),
        out_shape=jax.ShapeDtypeStruct((B, L, W2), BF16),
        compiler_params=_cparams(("parallel", "parallel")),
        name="hy_fwd",
    )(a_fwd, ueo, filt)


def _hy_inv_kernel(a_ref, y_ref, o_ref):
    o_ref[0] = jnp.dot(a_ref[...], y_ref[0], preferred_element_type=F32).astype(BF16)


def _hy_inv(a_inv, yspec, *, tm):
    B, L, W2 = yspec.shape
    L2 = L // 2
    return pl.pallas_call(
        _hy_inv_kernel,
        grid=(L2 // tm, B),
        in_specs=[pl.BlockSpec((tm, L), lambda i, b: (i, 0)),
                  pl.BlockSpec((1, L, W2), lambda i, b: (b, 0, 0))],
        out_specs=pl.BlockSpec((1, tm, W2), lambda i, b: (b, i, 0)),
        out_shape=jax.ShapeDtypeStruct((B, L2, W2), BF16),
        compiler_params=_cparams(("parallel", "parallel")),
        name="hy_inv",
    )(a_inv, yspec)


def _pad_rows(a, rows):
    return jnp.concatenate([a, jnp.zeros((rows - a.shape[0],) + a.shape[1:], a.dtype)], axis=0)


def _pad_cols(a, cols):
    return jnp.concatenate([a, jnp.zeros(a.shape[:-1] + (cols - a.shape[-1],), a.dtype)], axis=-1)


def _static_take(a, idx, axis):
    idx = np.asarray(idx)
    cuts = np.flatnonzero(np.diff(idx) != 1) + 1
    starts = np.concatenate([[0], cuts])
    ends = np.concatenate([cuts, [len(idx)]])
    parts = [lax.slice_in_dim(a, int(idx[s]), int(idx[e - 1]) + 1, axis=axis)
             for s, e in zip(starts, ends)]
    return jnp.concatenate(parts, axis=axis)


def _tile(n, pref):
    t = min(pref, n)
    assert n % t == 0
    return t


def _layer(x3, p, tables, *, final, final_g):
    B, L, _ = x3.shape
    T = B * L
    x2 = x3.reshape(T, D_MODEL)
    u, dt = _inproj(x2, p["norm_g"], p["w_in"], p["layer"], tm=_tile(T, 512), tn=512)
    u3 = u.reshape(B, L, N_IN)
    dt3 = dt.reshape(B, L, DT_PAD)

    ys = _ssd(u3, dt3, p["ssd_cw"], p["ssd_cb"], p["ssd_dtb"], p["ssd_alog"], p["rexp_f"],
              p["rexp_b"], p["ssd_dskip"], p["ssd_ng"], qb=_tile(L // CHUNK, 8))

    ya = _attention(u3, p["att_bias"], p["att_sink"], p["att_ng"], qb=_tile(L // ATT_BLOCK, 8))

    a_fwd, a_inv, filt, hm = tables
    tl = _tile(L, 1024)
    ueo, xa = _hy_pre(u3, p["hy_cw"], p["hy_cb"], tl=tl)
    yspec = _hy_fwd(a_fwd, ueo, filt, hm=hm)
    conv = _hy_inv(a_inv, yspec, tm=_tile(L // 2, 512))

    out = _outproj(x2, ys.reshape(T, SSD_WIDTH), ya.reshape(T, ATT_WIDTH),
                   conv.reshape(T // 2, 2 * HY_WIDTH), ueo.reshape(T // 2, 2 * HY_WIDTH),
                   xa.reshape(T, HY_WIDTH), u, p["hy_d"], p["hy_ng"], p["w_out"], final_g,
                   tm=_tile(T, 512), final=final)
    return out.reshape(B, L, D_MODEL)


def kernel(x_prompt, x_sample, rel_bias, norm_g, w_in, ssd_conv_w, ssd_conv_b, ssd_dt_bias, ssd_a_log, ssd_d, ssd_norm_g, att_sink, att_norm_g, hy_conv_w, hy_conv_b, hy_w1, hy_b1, hy_w2, hy_b2, hy_w3, hy_b3, hy_w4, hy_freq, hy_d, hy_norm_g, w_out, final_norm_g):
    depth = w_in.shape[0]
    w_in_p = _prep_w_in(w_in)
    att_perm = _att_col_perm()
    out_rows = np.concatenate([np.arange(SSD_WIDTH), SSD_WIDTH + att_perm,
                               np.arange(SSD_WIDTH + ATT_WIDTH, D_MODEL)])
    w_out_b = [_static_take(w_out[i].astype(BF16), out_rows, 0) for i in range(depth)]
    att_bias = _attn_bias(rel_bias)
    rexp_f, rexp_b = _ssd_expand_matrix(False), _ssd_expand_matrix(True)
    final_g = final_norm_g.astype(F32)[None, :]

    layers = []
    for i in range(depth):
        layers.append(dict(
            norm_g=norm_g[i].astype(F32)[None, :],
            w_in=w_in_p, layer=i,
            ssd_cw=_pad_rows(ssd_conv_w[i].astype(F32), 8),
            ssd_cb=ssd_conv_b[i].astype(F32)[None, :],
            ssd_dtb=_pad_cols(ssd_dt_bias[i].astype(F32).reshape(1, 2 * SSD_HEADS), DT_PAD),
            ssd_alog=_pad_cols(ssd_a_log[i].astype(F32).reshape(1, 2 * SSD_HEADS), DT_PAD),
            ssd_dskip=jnp.repeat(ssd_d[i].astype(F32), SSD_HEAD_DIM)[None, :],
            ssd_ng=ssd_norm_g[i].astype(F32)[None, :],
            rexp_f=rexp_f, rexp_b=rexp_b,
            att_bias=att_bias,
            att_sink=jnp.broadcast_to(att_sink[i].astype(F32)[:, None] * LOG2E, (ATT_HEADS, LANES)),
            att_ng=att_norm_g[i].astype(F32)[att_perm][None, :],
            hy_cw=_pad_rows(hy_conv_w[i].astype(F32), 8),
            hy_cb=hy_conv_b[i].astype(F32)[None, :],
            hy_d=hy_d[i].astype(F32)[None, :],
            hy_ng=hy_norm_g[i].astype(F32)[None, :],
            w_out=w_out_b[i],
        ))

    def trunk(x):
        L = x.shape[1]
        hm = min(512, L // 4)
        a_fwd, a_inv = _dft_tables(L, hm)
        for i in range(depth):
            xf = _hy_filter(L, hy_w1[i], hy_b1[i].astype(F32), hy_w2[i].astype(F32),
                            hy_b2[i].astype(F32), hy_w3[i].astype(F32), hy_b3[i].astype(F32),
                            hy_w4[i].astype(F32), hy_freq[i].astype(F32))
            filt = _hy_kspec(a_fwd, xf, hm=hm)
            x = _layer(x, layers[i], (a_fwd, a_inv, filt, hm), final=(i == depth - 1),
                       final_g=final_g)
        return x

    return (trunk(x_prompt), trunk(x_sample))
```

```python
import functools
import math

import jax
import jax.numpy as jnp
import numpy as np
from jax import lax
from jax.experimental import pallas as pl
from jax.experimental.pallas import tpu as pltpu

F32 = jnp.float32
BF16 = jnp.bfloat16
HIGHEST = lax.Precision.HIGHEST

D_MODEL = 2048
SSD_WIDTH = 1024
ATT_WIDTH = 512
HY_WIDTH = 512
SSD_HEAD_DIM = 64
SSD_HEADS = 16
SSD_GROUPS = 2
SSD_STATE = 128
SSD_CONV = 5
CHUNK = 128
SSD_XBC = SSD_WIDTH + 2 * SSD_GROUPS * SSD_STATE
ATT_HEAD_DIM = 64
ATT_HEADS = 8
ATT_KV_HEADS = 2
ATT_REP = ATT_HEADS // ATT_KV_HEADS
ATT_WINDOW = 128
ATT_BLOCK = 128
REL_BUCKETS = 32
REL_MAX_DIST = 128
HY_CONV = 3
HY_EMB_BANDS = 16
HY_FF = 64
HY_FAST_DECAY = 0.3
HY_SLOW_DECAY = 1.5
HY_DECAY_TARGET = 1e-2
EPS = 1e-6
NEG_BIG = -1e30

LANES = 128
BF16_SUBLANES = 16
VMEM_LIMIT = 56 * 1024 * 1024

C_XBC = 0
C_HY = 1536
C_Z = 3072
C_Q = 4096
C_GATT = 4608
C_GHY = 5120
C_K = 5632
C_V = 5760
C_DT = 5888
DT_PAD = 128
N_IN = C_DT + DT_PAD

_OLD_SIZES = [SSD_WIDTH, SSD_XBC, 2 * SSD_HEADS, ATT_WIDTH, ATT_KV_HEADS * ATT_HEAD_DIM,
              ATT_KV_HEADS * ATT_HEAD_DIM, ATT_WIDTH, 3 * HY_WIDTH, HY_WIDTH]
_OLD_OFF = np.concatenate([[0], np.cumsum(_OLD_SIZES)])
IN_COLS = int(_OLD_OFF[-1])


LOG2E = math.log2(math.e)
Q_SCALE = ATT_HEAD_DIM ** -0.5 * LOG2E


def _att_col_perm():
    order = [h for j in range(ATT_REP) for h in (j, ATT_REP + j)]
    return np.concatenate([np.arange(h * ATT_HEAD_DIM, (h + 1) * ATT_HEAD_DIM) for h in order])


def _in_perm():
    perm = np.full((N_IN,), IN_COLS, np.int32)
    scale = np.ones((N_IN,), np.float32)
    o = {n: int(_OLD_OFF[i]) for i, n in enumerate(
        ["z", "xbc", "dt", "q", "k", "v", "gatt", "hy", "ghy"])}
    def put(new, old, width):
        perm[new:new + width] = np.arange(old, old + width)
    put(C_XBC, o["xbc"], SSD_XBC)
    put(C_HY, o["hy"], 3 * HY_WIDTH)
    put(C_Z, o["z"], SSD_WIDTH)
    perm[C_Q:C_Q + ATT_WIDTH] = o["q"] + _att_col_perm()
    scale[C_Q:C_Q + ATT_WIDTH] = Q_SCALE
    perm[C_GATT:C_GATT + ATT_WIDTH] = o["gatt"] + _att_col_perm()
    put(C_GHY, o["ghy"], HY_WIDTH)
    put(C_K, o["k"], 128)
    put(C_V, o["v"], 128)
    put(C_DT, o["dt"], 2 * SSD_HEADS)
    return perm, scale


def _col_runs(perm):
    cuts = np.flatnonzero(np.diff(perm) != 1) + 1
    starts = np.concatenate([[0], cuts])
    ends = np.concatenate([cuts, [len(perm)]])
    return [(int(s), int(perm[s]), int(e - s)) for s, e in zip(starts, ends)]


def _prep_w_in_kernel(w_ref, o_ref, *, runs, n_real):
    for new0, old0, n in runs:
        piece = w_ref[:, old0:old0 + n]
        if C_Q <= new0 < C_Q + ATT_WIDTH:
            piece = piece * Q_SCALE
        o_ref[:, new0:new0 + n] = piece.astype(BF16)
    o_ref[:, n_real:] = jnp.zeros((o_ref.shape[0], N_IN - n_real), BF16)


def _prep_w_in(w_in, layer):
    perm, _ = _in_perm()
    n_real = C_DT + 2 * SSD_HEADS
    tr = 256
    return pl.pallas_call(
        functools.partial(_prep_w_in_kernel, runs=_col_runs(perm[:n_real]), n_real=n_real),
        grid=(D_MODEL // tr,),
        in_specs=[pl.BlockSpec((None, tr, IN_COLS), lambda r: (layer, r, 0))],
        out_specs=pl.BlockSpec((tr, N_IN), lambda r: (r, 0)),
        out_shape=jax.ShapeDtypeStruct((D_MODEL, N_IN), BF16),
        compiler_params=_cparams(("parallel",)),
        name="prep_w_in",
    )(w_in)


def _prep_w_out_kernel(w_ref, o_ref, *, runs):
    for new0, old0, n in runs:
        o_ref[new0:new0 + n, :] = w_ref[old0:old0 + n, :].astype(BF16)


def _prep_w_out(w_out, layer):
    rows = np.concatenate([np.arange(SSD_WIDTH), SSD_WIDTH + _att_col_perm(),
                           np.arange(SSD_WIDTH + ATT_WIDTH, D_MODEL)])
    tc = 512
    return pl.pallas_call(
        functools.partial(_prep_w_out_kernel, runs=_col_runs(rows)),
        grid=(D_MODEL // tc,),
        in_specs=[pl.BlockSpec((None, D_MODEL, tc), lambda c: (layer, 0, c))],
        out_specs=pl.BlockSpec((D_MODEL, tc), lambda c: (0, c)),
        out_shape=jax.ShapeDtypeStruct((D_MODEL, D_MODEL), BF16),
        compiler_params=_cparams(("parallel",)),
        name="prep_w_out",
    )(w_out)


def _cparams(sem):
    return pltpu.CompilerParams(dimension_semantics=sem, vmem_limit_bytes=VMEM_LIMIT)


def _silu(x):
    return x * (1.0 / (1.0 + jnp.exp(-x)))


def _softplus(x):
    return jnp.maximum(x, 0.0) + jnp.log1p(jnp.exp(-jnp.abs(x)))


def _inproj_kernel(x_ref, g_ref, w_ref, u_ref, dt_ref, h_ref, *, rows, tn):
    for r in range(x_ref.shape[0] // rows):
        sl = slice(r * rows, (r + 1) * rows)
        x = x_ref[sl, :]
        ms = jnp.mean(x * x, axis=-1, keepdims=True)
        h_ref[sl, :] = (x * lax.rsqrt(ms + EPS) * g_ref[...]).astype(BF16)
        h = h_ref[sl, :]
        for c0 in range(0, N_IN, tn):
            c1 = min(c0 + tn, N_IN)
            acc = jnp.dot(h, w_ref[:, c0:c1], preferred_element_type=F32)
            u_ref[sl, c0:c1] = acc.astype(BF16)
            if c0 <= C_DT < c1:
                dt_ref[sl, :] = acc[:, C_DT - c0:C_DT - c0 + DT_PAD]


def _inproj(x2, g, w, *, tm, tn):
    T = x2.shape[0]
    assert T % tm == 0
    return pl.pallas_call(
        functools.partial(_inproj_kernel, rows=min(256, tm), tn=tn),
        grid=(T // tm,),
        in_specs=[pl.BlockSpec((tm, D_MODEL), lambda i: (i, 0)),
                  pl.BlockSpec((1, D_MODEL), lambda i: (0, 0)),
                  pl.BlockSpec((D_MODEL, N_IN), lambda i: (0, 0), pipeline_mode=pl.Buffered(1))],
        out_specs=[pl.BlockSpec((tm, N_IN), lambda i: (i, 0)),
                   pl.BlockSpec((tm, DT_PAD), lambda i: (i, 0))],
        out_shape=[jax.ShapeDtypeStruct((T, N_IN), BF16),
                   jax.ShapeDtypeStruct((T, DT_PAD), F32)],
        scratch_shapes=[pltpu.VMEM((tm, D_MODEL), BF16)],
        compiler_params=_cparams(("parallel",)),
        name="inproj",
    )(x2, g, w)


def _outproj_kernel(x_ref, ys_ref, ya_ref, c_ref, ueo_ref, xa_ref, g_ref, d_ref, hng_ref, w_ref, fg_ref,
                    o_ref, s_ref, *, rows, final):
    nblk = HY_WIDTH // LANES

    def interleave(ref, hs):
        for p in range(2):
            for c in range(nblk):
                s_ref[c, pl.ds(p, rows // 2, stride=2), :] = ref[
                    hs, (p * nblk + c) * LANES:(p * nblk + c + 1) * LANES].astype(F32)
        return jnp.concatenate([s_ref[c] for c in range(nblk)], axis=1)

    for r in range(x_ref.shape[0] // rows):
        sl = slice(r * rows, (r + 1) * rows)
        hs = slice(r * rows // 2, (r + 1) * rows // 2)
        conv = interleave(c_ref, hs)
        u = interleave(ueo_ref, hs)
        xa = xa_ref[sl, :].astype(F32)
        y = xa * (conv + u * d_ref[...])
        y = y * _silu(g_ref[sl, :].astype(F32))
        ms = jnp.mean(y * y, axis=-1, keepdims=True)
        yh = (y * lax.rsqrt(ms + EPS) * hng_ref[...]).astype(BF16)

        acc = x_ref[sl, :]
        acc = acc + jnp.dot(ys_ref[sl, :], w_ref[0:SSD_WIDTH, :], preferred_element_type=F32)
        acc = acc + jnp.dot(ya_ref[sl, :], w_ref[SSD_WIDTH:SSD_WIDTH + ATT_WIDTH, :],
                            preferred_element_type=F32)
        acc = acc + jnp.dot(yh, w_ref[SSD_WIDTH + ATT_WIDTH:, :], preferred_element_type=F32)
        if final:
            ms = jnp.mean(acc * acc, axis=-1, keepdims=True)
            acc = acc * lax.rsqrt(ms + EPS) * fg_ref[...]
        o_ref[sl, :] = acc


def _outproj(x2, ys, ya, conv2, ueo2, xa2, u2, d, hng, w, fg, *, tm, final):
    T = x2.shape[0]
    assert T % tm == 0
    rows = min(256, tm)
    return pl.pallas_call(
        functools.partial(_outproj_kernel, rows=rows, final=final),
        grid=(T // tm,),
        in_specs=[pl.BlockSpec((tm, D_MODEL), lambda i: (i, 0)),
                  pl.BlockSpec((tm, SSD_WIDTH), lambda i: (i, 0)),
                  pl.BlockSpec((tm, ATT_WIDTH), lambda i: (i, 0)),
                  pl.BlockSpec((tm // 2, 2 * HY_WIDTH), lambda i: (i, 0)),
                  pl.BlockSpec((tm // 2, 2 * HY_WIDTH), lambda i: (i, 0)),
                  pl.BlockSpec((tm, HY_WIDTH), lambda i: (i, 0)),
                  pl.BlockSpec((tm, HY_WIDTH), lambda i: (i, C_GHY // HY_WIDTH)),
                  pl.BlockSpec((1, HY_WIDTH), lambda i: (0, 0)),
                  pl.BlockSpec((1, HY_WIDTH), lambda i: (0, 0)),
                  pl.BlockSpec((D_MODEL, D_MODEL), lambda i: (0, 0)),
                  pl.BlockSpec((1, D_MODEL), lambda i: (0, 0))],
        out_specs=pl.BlockSpec((tm, D_MODEL), lambda i: (i, 0)),
        out_shape=jax.ShapeDtypeStruct((T, D_MODEL), F32),
        scratch_shapes=[pltpu.VMEM((HY_WIDTH // LANES, rows, LANES), F32)],
        compiler_params=_cparams(("parallel",)),
        name="outproj",
    )(x2, ys, ya, conv2, ueo2, xa2, u2, d, hng, w, fg)


def _t5_buckets(rel):
    nb = REL_BUCKETS // 2
    max_exact = nb // 2
    ret = (rel > 0).astype(np.int32) * nb
    n = np.abs(rel)
    large = max_exact + (np.log(np.maximum(n, 1) / max_exact) / math.log(REL_MAX_DIST / max_exact)
                         * (nb - max_exact)).astype(np.int32)
    large = np.minimum(large, nb - 1)
    return ret + np.where(n < max_exact, n, large)


def _attn_bias(rel_bias):
    qi = np.arange(ATT_BLOCK)[:, None]
    kj = np.arange(3 * ATT_BLOCK)[None, :]
    rel = kj - ATT_BLOCK - qi
    onehot = (_t5_buckets(rel)[None] == np.arange(REL_BUCKETS)[:, None, None]).astype(np.float32)
    bias = jnp.einsum("bqk,bh->hqk", jnp.asarray(onehot), rel_bias.astype(F32),
                      precision=HIGHEST) * LOG2E
    window = np.abs(rel) <= ATT_WINDOW
    variants = []
    for last in (False, True):
        for first in (False, True):
            ok = window & ~(first & (kj < ATT_BLOCK)) & ~(last & (kj >= 2 * ATT_BLOCK))
            variants.append(jnp.where(ok[None], bias, NEG_BIG))
    return jnp.transpose(jnp.stack(variants), (0, 1, 3, 2))


def _attn_kernel(q_ref, kp_ref, kc_ref, kn_ref, vp_ref, vc_ref, vn_ref, g_ref, bias_ref, sink_ref,
                 ng_ref, o_ref, klo_ref, khi_ref, vt_ref, *, nt, qb):
    n = pl.program_id(1)
    lo = lax.broadcasted_iota(jnp.int32, (1, LANES), 1) < ATT_HEAD_DIM
    zero = jnp.zeros((), BF16)
    kext = jnp.concatenate([kp_ref[0], kc_ref[0], kn_ref[0]], axis=0)
    klo_ref[...] = jnp.where(lo, kext, zero)
    khi_ref[...] = jnp.where(lo, zero, kext)
    for t, ref, cnt in ((0, vp_ref, 1), (1, vc_ref, qb), (qb + 1, vn_ref, 1)):
        for i in range(cnt):
            blk = ref[0, i * ATT_BLOCK:(i + 1) * ATT_BLOCK, :]
            vt_ref[t + i] = blk.astype(F32).T.astype(BF16)
    row_lo = lax.broadcasted_iota(jnp.int32, (LANES, ATT_BLOCK), 0) < ATT_HEAD_DIM
    nk = 3 * ATT_BLOCK

    def body(i, carry):
        r0 = pl.multiple_of(i * ATT_BLOCK, ATT_BLOCK)
        q = q_ref[0, pl.ds(r0, ATT_BLOCK), :]
        kst = jnp.concatenate([klo_ref[pl.ds(r0, nk), :], khi_ref[pl.ds(r0, nk), :]], axis=0)
        vt = jnp.concatenate([vt_ref[i], vt_ref[i + 1], vt_ref[i + 2]], axis=1)
        first = jnp.logical_and(n == 0, i == 0)
        last = jnp.logical_and(n == nt - 1, i == qb - 1)
        variant = first.astype(jnp.int32) + 2 * last.astype(jnp.int32)
        outs = []
        for j in range(ATT_REP):
            qp = q[:, j * LANES:(j + 1) * LANES]
            st = lax.dot_general(kst, qp, (((1,), (1,)), ((), ())), preferred_element_type=F32)
            halves = []
            for e, h in enumerate((j, ATT_REP + j)):
                s = st[e * nk:(e + 1) * nk] + bias_ref[variant, h]
                sk = sink_ref[h:h + 1, :]
                m = jnp.maximum(jnp.max(s, axis=0, keepdims=True), sk)
                p = jnp.exp2(s - m)
                den = jnp.sum(p, axis=0, keepdims=True) + jnp.exp2(sk - m)
                ot = jnp.dot(vt, p.astype(BF16), preferred_element_type=F32)
                halves.append(ot * (1.0 / den))
            outs.append(jnp.where(row_lo, halves[0], halves[1]).T)
        o = jnp.concatenate(outs, axis=-1)
        y = o * _silu(g_ref[0, pl.ds(r0, ATT_BLOCK), :].astype(F32))
        ms = jnp.mean(y * y, axis=-1, keepdims=True)
        o_ref[0, pl.ds(r0, ATT_BLOCK), :] = (y * lax.rsqrt(ms + EPS) * ng_ref[...]).astype(BF16)
        return carry

    lax.fori_loop(0, qb, body, 0, unroll=True)


def _attention(u3, bias, sink, ng, *, qb):
    B, L, _ = u3.shape
    tq = qb * ATT_BLOCK
    assert L % tq == 0
    nt = L // tq
    nb = L // ATT_BLOCK
    kcol, vcol = C_K // 128, C_V // 128
    def kv_specs(colblk):
        return [pl.BlockSpec((1, ATT_BLOCK, 128), lambda b, n: (b, jnp.maximum(n * qb - 1, 0), colblk)),
                pl.BlockSpec((1, tq, 128), lambda b, n: (b, n, colblk)),
                pl.BlockSpec((1, ATT_BLOCK, 128),
                             lambda b, n: (b, jnp.minimum((n + 1) * qb, nb - 1), colblk))]
    return pl.pallas_call(
        functools.partial(_attn_kernel, nt=nt, qb=qb),
        grid=(B, nt),
        in_specs=[pl.BlockSpec((1, tq, ATT_WIDTH), lambda b, n: (b, n, C_Q // ATT_WIDTH))]
                 + kv_specs(kcol) + kv_specs(vcol)
                 + [pl.BlockSpec((1, tq, ATT_WIDTH), lambda b, n: (b, n, C_GATT // ATT_WIDTH)),
                    pl.BlockSpec((4, ATT_HEADS, 3 * ATT_BLOCK, ATT_BLOCK), lambda b, n: (0, 0, 0, 0)),
                    pl.BlockSpec((ATT_HEADS, LANES), lambda b, n: (0, 0)),
                    pl.BlockSpec((1, ATT_WIDTH), lambda b, n: (0, 0))],
        out_specs=pl.BlockSpec((1, tq, ATT_WIDTH), lambda b, n: (b, n, 0)),
        out_shape=jax.ShapeDtypeStruct((B, L, ATT_WIDTH), BF16),
        scratch_shapes=[pltpu.VMEM(((qb + 2) * ATT_BLOCK, LANES), BF16),
                        pltpu.VMEM(((qb + 2) * ATT_BLOCK, LANES), BF16),
                        pltpu.VMEM((qb + 2, LANES, ATT_BLOCK), BF16)],
        compiler_params=_cparams(("parallel", "parallel")),
        name="attn",
    )(u3, u3, u3, u3, u3, u3, u3, u3, bias, sink, ng)


HALO = BF16_SUBLANES


CONV_BLK = 128


def _load_padded(xpad_ref, xm_ref, xp_ref, xn_ref, has_prev, has_next, rows):
    zero = jnp.zeros((), BF16)
    xpad_ref[pl.ds(0, HALO), :] = jnp.where(has_prev, xp_ref[0], zero)
    xpad_ref[pl.ds(HALO, rows), :] = xm_ref[0]
    xpad_ref[pl.ds(HALO + rows, HALO), :] = jnp.where(has_next, xn_ref[0], zero)


def _shift_matrix(width):
    offs = [k - width // 2 for k in range(width) if k != width // 2]
    r = lax.broadcasted_iota(jnp.int32, (CONV_BLK, CONV_BLK + 2 * HALO), 0)
    c = lax.broadcasted_iota(jnp.int32, (CONV_BLK, CONV_BLK + 2 * HALO), 1)
    return jnp.concatenate([(c == r + HALO + d).astype(BF16) for d in offs], axis=0)


def _dwconv_block(xpad_ref, shifts, w_ref, b_ref, j, width, c0, c1):
    win = xpad_ref[j * CONV_BLK:(j + 1) * CONV_BLK + 2 * HALO, c0:c1]
    moved = jnp.dot(shifts, win, preferred_element_type=F32)
    acc = win[HALO:HALO + CONV_BLK].astype(F32) * w_ref[width // 2:width // 2 + 1, c0:c1]
    i = 0
    for k in range(width):
        if k == width // 2:
            continue
        acc = acc + moved[i * CONV_BLK:(i + 1) * CONV_BLK] * w_ref[k:k + 1, c0:c1]
        i += 1
    return acc + b_ref[:, c0:c1]


def _dwconv(xpad_ref, w_ref, b_ref, rows, width, c0, c1):
    shifts = _shift_matrix(width)
    return jnp.concatenate([_dwconv_block(xpad_ref, shifts, w_ref, b_ref, j, width, c0, c1)
                            for j in range(rows // CONV_BLK)], axis=0)


SSD_HPG = SSD_HEADS // SSD_GROUPS
SSD_GW = SSD_HPG * SSD_HEAD_DIM
SSD_BC = SSD_GROUPS * SSD_STATE


def _ssd_decay(dt_raw, dtb_ref, alog_ref):
    dt = _softplus(dt_raw + dtb_ref[...])
    dta = dt * (-jnp.exp(alog_ref[...]))
    row = lax.broadcasted_iota(jnp.int32, (CHUNK, CHUNK), 0)
    col = lax.broadcasted_iota(jnp.int32, (CHUNK, CHUNK), 1)
    tril = (row >= col).astype(BF16)
    hi = dta.astype(BF16)
    r1 = dta - hi.astype(F32)
    mid = r1.astype(BF16)
    lo = (r1 - mid.astype(F32)).astype(BF16)
    pre = (jnp.dot(tril, hi, preferred_element_type=F32) + jnp.dot(tril, mid, preferred_element_type=F32)
           + jnp.dot(tril, lo, preferred_element_type=F32))
    tot = pre[CHUNK - 1:CHUNK, :]
    is_bwd = lax.broadcasted_iota(jnp.int32, (1, LANES), 1) >= SSD_HEADS
    acs = jnp.where(is_bwd, tot - pre + dta, pre)
    return dt, acs, tot


def _ssd_bwd_kernel(xm_ref, xp_ref, xn_ref, dt_ref, cw_ref, cb_ref, dtb_ref, alog_ref, rexp_ref,
                    xact_ref, prev_ref, xpad_ref, xf_ref, st_ref, *, nt, qb):
    n = pl.program_id(1)
    tile = nt - 1 - n
    rows = qb * CHUNK

    @pl.when(n == 0)
    def _():
        st_ref[...] = jnp.zeros(st_ref.shape, F32)

    _load_padded(xpad_ref, xm_ref, xp_ref, xn_ref, tile > 0, tile < nt - 1, rows)
    xact = _silu(_dwconv(xpad_ref, cw_ref, cb_ref, rows, SSD_CONV, 0, SSD_XBC))
    xf_ref[...] = xact
    xact_ref[0] = xact.astype(BF16)

    def body(j, carry):
        i = qb - 1 - j
        r0 = pl.multiple_of(i * CHUNK, CHUNK)
        xs = xf_ref[pl.ds(r0, CHUNK), 0:SSD_WIDTH]
        bm = xf_ref[pl.ds(r0, CHUNK), SSD_WIDTH:SSD_WIDTH + SSD_BC].astype(BF16)
        dt, acs, tot = _ssd_decay(dt_ref[0, pl.ds(r0, CHUNK), :], dtb_ref, alog_ref)
        e_in = jnp.concatenate([jnp.exp(tot - acs) * dt, jnp.broadcast_to(jnp.exp(tot), (8, LANES))],
                               axis=0).astype(BF16)
        e_out = jnp.dot(e_in, rexp_ref[...], preferred_element_type=F32)
        xd = (xs * e_out[0:CHUNK]).astype(BF16)
        cdec = e_out[CHUNK:CHUNK + 1]
        for g in range(SSD_GROUPS):
            prev = st_ref[g]
            prev_ref[0, i, g] = prev.astype(BF16)
            s_new = lax.dot_general(bm[:, g * SSD_STATE:(g + 1) * SSD_STATE],
                                    xd[:, g * SSD_GW:(g + 1) * SSD_GW], (((0,), (0,)), ((), ())),
                                    preferred_element_type=F32)
            st_ref[g] = prev * cdec[:, g * SSD_GW:(g + 1) * SSD_GW] + s_new
        return carry

    lax.fori_loop(0, qb, body, 0, unroll=True)


def _ssd_fwd_kernel(xa_ref, dt_ref, prev_ref, z_ref, dtb_ref, alog_ref, rexpf_ref, rexpb_ref,
                    dsk_ref, ng_ref, o_ref, st_ref, *, qb):
    n = pl.program_id(1)

    @pl.when(n == 0)
    def _():
        st_ref[...] = jnp.zeros(st_ref.shape, F32)

    row = lax.broadcasted_iota(jnp.int32, (CHUNK, CHUNK), 0)
    col = lax.broadcasted_iota(jnp.int32, (CHUNK, CHUNK), 1)
    fwd_part = row > col
    diag = row == col
    lane_lo = lax.broadcasted_iota(jnp.int32, (1, LANES), 1) < SSD_HEAD_DIM
    zero_b = jnp.zeros((), BF16)

    def body(i, carry):
        r0 = pl.multiple_of(i * CHUNK, CHUNK)
        xs_b = xa_ref[0, pl.ds(r0, CHUNK), 0:SSD_WIDTH]
        xs = xs_b.astype(F32)
        bm = xa_ref[0, pl.ds(r0, CHUNK), SSD_WIDTH:SSD_WIDTH + SSD_BC]
        cm = xa_ref[0, pl.ds(r0, CHUNK), SSD_WIDTH + SSD_BC:SSD_XBC]
        dt, acs, tot = _ssd_decay(dt_ref[0, pl.ds(r0, CHUNK), :], dtb_ref, alog_ref)
        acs2 = acs * LOG2E
        rt = (acs2 - jnp.log2(dt)).T
        dsum_t = jnp.log2(dt + pltpu.roll(dt, LANES - SSD_HEADS, axis=1)).T
        eacs = jnp.exp(acs)
        ef_in = jnp.concatenate([jnp.exp(tot - acs) * dt, eacs,
                                 jnp.broadcast_to(jnp.exp(tot), (8, LANES))], axis=0).astype(BF16)
        ef = jnp.dot(ef_in, rexpf_ref[...], preferred_element_type=F32)
        eb = jnp.dot(eacs.astype(BF16), rexpb_ref[...], preferred_element_type=F32)
        xd = (xs * ef[0:CHUNK]).astype(BF16)
        eacs_f = ef[CHUNK:2 * CHUNK]
        cdec = ef[2 * CHUNK:2 * CHUNK + 1]
        ys = []
        for g in range(SSD_GROUPS):
            bg = bm[:, g * SSD_STATE:(g + 1) * SSD_STATE]
            cg = cm[:, g * SSD_STATE:(g + 1) * SSD_STATE]
            gs = slice(g * SSD_GW, (g + 1) * SSD_GW)
            cb = lax.dot_general(cg, bg, (((1,), (1,)), ((), ())), preferred_element_type=F32)
            prev = st_ref[g]
            y_off = (jnp.dot(cg, prev.astype(BF16), preferred_element_type=F32) * eacs_f[:, gs]
                     + jnp.dot(cg, prev_ref[0, i, g], preferred_element_type=F32) * eb[:, gs])
            s_new = lax.dot_general(bg, xd[:, gs], (((0,), (0,)), ((), ())),
                                    preferred_element_type=F32)
            st_ref[g] = prev * cdec[:, gs] + s_new
            for pr in range(SSD_HPG // 2):
                c0 = g * SSD_GW + pr * LANES
                xpair = xs_b[:, c0:c0 + LANES]
                xbd = jnp.concatenate([jnp.where(lane_lo, xpair, zero_b),
                                       jnp.where(lane_lo, zero_b, xpair)], axis=0)
                mats = []
                for e in range(2):
                    h = g * SSD_HPG + pr * 2 + e
                    hb = SSD_HEADS + h
                    sel = jnp.where(fwd_part, acs2[:, h:h + 1] - rt[h:h + 1, :],
                                    acs2[:, hb:hb + 1] - rt[hb:hb + 1, :])
                    sel = jnp.where(diag, dsum_t[h:h + 1, :], sel)
                    mats.append((cb * jnp.exp2(sel)).astype(BF16))
                yd = jnp.dot(jnp.concatenate(mats, axis=1), xbd, preferred_element_type=F32)
                ys.append(yd + y_off[:, pr * LANES:(pr + 1) * LANES])
        y = jnp.concatenate(ys, axis=-1) + xs * dsk_ref[...]
        y = y * _silu(z_ref[0, pl.ds(r0, CHUNK), :].astype(F32))
        ms = jnp.mean(y * y, axis=-1, keepdims=True)
        o_ref[0, pl.ds(r0, CHUNK), :] = (y * lax.rsqrt(ms + EPS) * ng_ref[...]).astype(BF16)
        return carry

    lax.fori_loop(0, qb, body, 0, unroll=True)


def _ssd(u3, dt3, cw, cb, dtb, alog, rexp_f, rexp_b, dsk, ng, *, qb):
    B, L, _ = u3.shape
    rows = qb * CHUNK
    assert L % rows == 0
    nt = L // rows
    nc = L // CHUNK
    hb = rows // HALO
    nhb = L // HALO
    state = pltpu.VMEM((SSD_GROUPS, SSD_STATE, SSD_GW), F32)
    const = lambda shape: pl.BlockSpec(shape, lambda b, n: (0,) * len(shape))
    rt = lambda n: nt - 1 - n

    xact, prevb = pl.pallas_call(
        functools.partial(_ssd_bwd_kernel, nt=nt, qb=qb),
        grid=(B, nt),
        in_specs=[
            pl.BlockSpec((1, rows, SSD_XBC), lambda b, n: (b, rt(n), C_XBC // SSD_XBC)),
            pl.BlockSpec((1, HALO, SSD_XBC),
                         lambda b, n: (b, jnp.maximum(rt(n) * hb - 1, 0), C_XBC // SSD_XBC)),
            pl.BlockSpec((1, HALO, SSD_XBC),
                         lambda b, n: (b, jnp.minimum((rt(n) + 1) * hb, nhb - 1), C_XBC // SSD_XBC)),
            pl.BlockSpec((1, rows, DT_PAD), lambda b, n: (b, rt(n), 0)),
            const((8, SSD_XBC)), const((1, SSD_XBC)), const((1, DT_PAD)), const((1, DT_PAD)),
            const((LANES, SSD_WIDTH)),
        ],
        out_specs=[pl.BlockSpec((1, rows, SSD_XBC), lambda b, n: (b, rt(n), 0)),
                   pl.BlockSpec((1, qb, SSD_GROUPS, SSD_STATE, SSD_GW),
                                lambda b, n: (b, rt(n), 0, 0, 0))],
        out_shape=[jax.ShapeDtypeStruct((B, L, SSD_XBC), BF16),
                   jax.ShapeDtypeStruct((B, nc, SSD_GROUPS, SSD_STATE, SSD_GW), BF16)],
        scratch_shapes=[pltpu.VMEM((rows + 2 * HALO, SSD_XBC), BF16),
                        pltpu.VMEM((rows, SSD_XBC), F32), state],
        compiler_params=_cparams(("parallel", "arbitrary")),
        name="ssd_bwd",
    )(u3, u3, u3, dt3, cw, cb, dtb, alog, rexp_b)

    return pl.pallas_call(
        functools.partial(_ssd_fwd_kernel, qb=qb),
        grid=(B, nt),
        in_specs=[
            pl.BlockSpec((1, rows, SSD_XBC), lambda b, n: (b, n, 0)),
            pl.BlockSpec((1, rows, DT_PAD), lambda b, n: (b, n, 0)),
            pl.BlockSpec((1, qb, SSD_GROUPS, SSD_STATE, SSD_GW), lambda b, n: (b, n, 0, 0, 0)),
            pl.BlockSpec((1, rows, SSD_WIDTH), lambda b, n: (b, n, C_Z // SSD_WIDTH)),
            const((1, DT_PAD)), const((1, DT_PAD)),
            const((LANES, SSD_WIDTH)), const((LANES, SSD_WIDTH)),
            const((1, SSD_WIDTH)), const((1, SSD_WIDTH)),
        ],
        out_specs=pl.BlockSpec((1, rows, SSD_WIDTH), lambda b, n: (b, n, 0)),
        out_shape=jax.ShapeDtypeStruct((B, L, SSD_WIDTH), BF16),
        scratch_shapes=[state],
        compiler_params=_cparams(("parallel", "arbitrary")),
        name="ssd_fwd",
    )(xact, dt3, prevb, u3, dtb, alog, rexp_f, rexp_b, dsk, ng)


def _ssd_expand_matrix(rev):
    m = np.zeros((LANES, SSD_WIDTH), np.float32)
    hoff = SSD_HEADS if rev else 0
    for h in range(SSD_HEADS):
        m[hoff + h, h * SSD_HEAD_DIM:(h + 1) * SSD_HEAD_DIM] = 1.0
    return jnp.asarray(m, BF16)


def _hy_filter_kernel(t_ref, bands_ref, w1t_ref, w1c_ref, w1s_ref, b1_ref, w2_ref, b2_ref, w3_ref,
                      b3_ref, w4_ref, fr_ref, absd_ref, x_ref, s_ref, *, L, tl):
    i = pl.program_id(0)
    pos = (lax.broadcasted_iota(jnp.int32, (tl, 1), 0) + i * tl).astype(F32)
    t = t_ref[...]
    pos_row = (lax.broadcasted_iota(jnp.int32, (1, tl), 1) + i * tl).astype(F32)
    ang_t = 2.0 * math.pi * pos_row * bands_ref[...] / L
    fr = fr_ref[...]
    dot = functools.partial(jnp.dot, preferred_element_type=F32, precision=HIGHEST)
    tdot = lambda a, b: lax.dot_general(a, b, (((0,), (0,)), ((), ())), preferred_element_type=F32,
                                        precision=HIGHEST)
    pre = t * w1t_ref[...] + tdot(jnp.cos(ang_t), w1c_ref[...]) + tdot(-jnp.sin(ang_t), w1s_ref[...])
    h = jnp.sin(fr * (pre + b1_ref[...]))
    h = jnp.sin(fr * (dot(h, w2_ref[...]) + b2_ref[...]))
    h = jnp.sin(fr * (dot(h, w3_ref[...]) + b3_ref[...]))
    w4 = w4_ref[...]
    h_hi, w_hi = h.astype(BF16), w4.astype(BF16)
    h_lo, w_lo = (h - h_hi.astype(F32)).astype(BF16), (w4 - w_hi.astype(F32)).astype(BF16)
    bdot = functools.partial(jnp.dot, preferred_element_type=F32)
    h = bdot(h_hi, w_hi) + bdot(h_hi, w_lo) + bdot(h_lo, w_hi)
    decay = jnp.exp(-t * absd_ref[...])
    hf = h[:, :HY_WIDTH] * decay
    hb = jnp.where(pos == 0.0, 0.0, h[:, HY_WIDTH:] * decay)
    _to_lane_blocks(s_ref, jnp.concatenate([hf + hb, hb - hf], axis=1))
    x_ref[...] = _split_even_odd(s_ref).astype(BF16)


def _hy_filter(L, w1, b1, w2, b2, w3, b3, w4, freq):
    tl = min(512, L)
    t = jnp.linspace(0.0, 1.0, L, dtype=F32)[:, None]
    bands = jnp.linspace(1e-4, HY_EMB_BANDS - 1, HY_EMB_BANDS, dtype=F32)[:, None]
    max_decay = math.log(HY_DECAY_TARGET) / HY_FAST_DECAY
    min_decay = math.log(HY_DECAY_TARGET) / HY_SLOW_DECAY
    absd = jnp.abs(jnp.linspace(min_decay, max_decay, HY_WIDTH, dtype=F32))[None, :]
    w1 = w1.astype(F32)
    full = lambda a: pl.BlockSpec(a.shape, lambda i: (0,) * a.ndim)
    args = [t, bands, w1[0:1], w1[1:1 + HY_EMB_BANDS], w1[1 + HY_EMB_BANDS:], b1[None], w2, b2[None],
            w3, b3[None], w4, freq[None], absd]
    in_specs = [pl.BlockSpec((tl, 1), lambda i: (i, 0))] + [full(a) for a in args[1:]]
    return pl.pallas_call(
        functools.partial(_hy_filter_kernel, L=L, tl=tl),
        grid=(L // tl,),
        in_specs=in_specs,
        out_specs=pl.BlockSpec((tl // 2, 4 * HY_WIDTH), lambda i: (i, 0)),
        out_shape=jax.ShapeDtypeStruct((L // 2, 4 * HY_WIDTH), BF16),
        scratch_shapes=[pltpu.VMEM((2 * HY_WIDTH // LANES, tl, LANES), F32)],
        compiler_params=_cparams(("parallel",)),
        name="hy_filter",
    )(*args)


def _dft_tables(L, hm):
    L2 = L // 2
    g = jnp.arange(L2, dtype=jnp.int32)
    s = jnp.arange(L2, dtype=jnp.int32)
    ph = ((2 * g + 1)[:, None] * s[None, :]) % (2 * L)
    ang = ph.astype(F32) * (math.pi / L)
    c = jnp.cos(ang).astype(BF16).reshape(L2 // hm, 1, hm, L2)
    sn = jnp.sin(ang).astype(BF16).reshape(L2 // hm, 1, hm, L2)
    a_fwd = jnp.concatenate([c, sn], axis=1).reshape(L, L2)
    a_inv = jnp.concatenate([c, -sn], axis=1).reshape(L, L2).T
    return a_fwd, a_inv


def _hy_kspec_kernel(a_ref, x_ref, wc_ref, ws_ref, f_ref, *, hm, scale):
    W = HY_WIDTH
    pq = jnp.dot(a_ref[...], x_ref[...], preferred_element_type=F32)
    ea, eb, oa, ob = [(pq[0:hm, i * W:(i + 1) * W], -pq[hm:, i * W:(i + 1) * W]) for i in range(4)]
    w = (wc_ref[...], -ws_ref[...])
    cmul = lambda x, y: (x[0] * y[0] - x[1] * y[1], x[0] * y[1] + x[1] * y[0])
    woa, wob = cmul(w, oa), cmul(w, ob)
    k1 = (ea[0] + woa[0], -(eb[1] + wob[1]))
    k2 = (ea[0] - woa[0], eb[1] - wob[1])
    kp = (k1[0] + k2[0], k1[1] - k2[1])
    km = (k1[0] - k2[0], k1[1] + k2[1])
    wkm = cmul(w, km)
    vkm = cmul((w[0], -w[1]), km)
    for i, part in enumerate((kp[0], kp[1], wkm[0], wkm[1], vkm[0], vkm[1])):
        f_ref[0, i] = part * scale


def _hy_kspec(a_fwd, xf2, *, hm):
    L, L2 = a_fwd.shape
    tm = 2 * hm
    nx = xf2.shape[1]
    theta = (2.0 * jnp.arange(L2, dtype=F32) + 1.0) * (math.pi / (2 * L))
    return pl.pallas_call(
        functools.partial(_hy_kspec_kernel, hm=hm, scale=1.0 / L),
        grid=(L // tm,),
        in_specs=[pl.BlockSpec((tm, L2), lambda i: (i, 0)),
                  pl.BlockSpec((L2, nx), lambda i: (0, 0)),
                  pl.BlockSpec((hm, 1), lambda i: (i, 0)),
                  pl.BlockSpec((hm, 1), lambda i: (i, 0))],
        out_specs=pl.BlockSpec((1, 6, hm, HY_WIDTH), lambda i: (i, 0, 0, 0)),
        out_shape=jax.ShapeDtypeStruct((L2 // hm, 6, hm, HY_WIDTH), F32),
        compiler_params=_cparams(("parallel",)),
        name="hy_kspec",
    )(a_fwd, xf2, jnp.cos(theta)[:, None], jnp.sin(theta)[:, None])


def _to_lane_blocks(s_ref, x):
    for c in range(s_ref.shape[0]):
        s_ref[c] = x[:, c * LANES:(c + 1) * LANES]


def _split_even_odd(s_ref):
    k, rows, _ = s_ref.shape
    return jnp.concatenate([s_ref[c, pl.ds(p, rows // 2, stride=2), :]
                            for p in range(2) for c in range(k)], axis=1)


def _hy_pre_kernel(xm_ref, xp_ref, xn_ref, cw_ref, cb_ref, u_ref, xa_ref, xpad_ref, s_ref, *, nt, tl):
    i = pl.program_id(1)
    _load_padded(xpad_ref, xm_ref, xp_ref, xn_ref, i > 0, i < nt - 1, tl)
    xa = _dwconv(xpad_ref, cw_ref, cb_ref, tl, HY_CONV, 0, HY_WIDTH)
    xb = _dwconv(xpad_ref, cw_ref, cb_ref, tl, HY_CONV, HY_WIDTH, 2 * HY_WIDTH)
    v = _dwconv(xpad_ref, cw_ref, cb_ref, tl, HY_CONV, 2 * HY_WIDTH, 3 * HY_WIDTH)
    u = xb * v
    xa_ref[0] = xa.astype(BF16)
    _to_lane_blocks(s_ref, u)
    u_ref[0] = _split_even_odd(s_ref).astype(BF16)


def _hy_tile_specs(L, tl):
    hb = tl // HALO
    nhb = L // HALO
    w = 3 * HY_WIDTH
    return [pl.BlockSpec((1, tl, w), lambda b, i: (b, i, C_HY // w)),
            pl.BlockSpec((1, HALO, w), lambda b, i: (b, jnp.maximum(i * hb - 1, 0), C_HY // w)),
            pl.BlockSpec((1, HALO, w), lambda b, i: (b, jnp.minimum((i + 1) * hb, nhb - 1), C_HY // w))]


def _hy_pre(u3, cw, cb, *, tl):
    B, L, _ = u3.shape
    nt = L // tl
    w = 3 * HY_WIDTH
    return pl.pallas_call(
        functools.partial(_hy_pre_kernel, nt=nt, tl=tl),
        grid=(B, nt),
        in_specs=_hy_tile_specs(L, tl) + [pl.BlockSpec((8, w), lambda b, i: (0, 0)),
                                          pl.BlockSpec((1, w), lambda b, i: (0, 0))],
        out_specs=[pl.BlockSpec((1, tl // 2, 2 * HY_WIDTH), lambda b, i: (b, i, 0)),
                   pl.BlockSpec((1, tl, HY_WIDTH), lambda b, i: (b, i, 0))],
        out_shape=[jax.ShapeDtypeStruct((B, L // 2, 2 * HY_WIDTH), BF16),
                   jax.ShapeDtypeStruct((B, L, HY_WIDTH), BF16)],
        scratch_shapes=[pltpu.VMEM((tl + 2 * HALO, w), BF16),
                        pltpu.VMEM((HY_WIDTH // LANES, tl, LANES), F32)],
        compiler_params=_cparams(("parallel", "parallel")),
        name="hy_pre",
    )(u3, u3, u3, cw, cb)


def _hy_fwd_kernel(a_ref, u_ref, f_ref, y_ref, *, hm):
    W = HY_WIDTH
    pq = jnp.dot(a_ref[...], u_ref[0], preferred_element_type=F32)
    pe, po = pq[0:hm, :W], pq[0:hm, W:]
    qe, qo = pq[hm:, :W], pq[hm:, W:]
    kpr, kpi, wr, wi, vr, vi = [f_ref[0, i] for i in range(6)]
    y_ref[0, 0:hm, :W] = (pe * kpr + qe * kpi + po * wr + qo * wi).astype(BF16)
    y_ref[0, hm:, :W] = (pe * kpi - qe * kpr + po * wi - qo * wr).astype(BF16)
    y_ref[0, 0:hm, W:] = (pe * vr + qe * vi + po * kpr + qo * kpi).astype(BF16)
    y_ref[0, hm:, W:] = (pe * vi - qe * vr + po * kpi - qo * kpr).astype(BF16)


def _hy_fwd(a_fwd, ueo, filt, *, hm):
    B, L2, W2 = ueo.shape
    L = 2 * L2
    tm = 2 * hm
    return pl.pallas_call(
        functools.partial(_hy_fwd_kernel, hm=hm),
        grid=(L // tm, B),
        in_specs=[pl.BlockSpec((tm, L2), lambda i, b: (i, 0)),
                  pl.BlockSpec((1, L2, W2), lambda i, b: (b, 0, 0)),
                  pl.BlockSpec((1, 6, hm, HY_WIDTH), lambda i, b: (i, 0, 0, 0))],
        out_specs=pl.BlockSpec((1, tm, W2), lambda i, b: (b, i, 0)),
        out_shape=jax.ShapeDtypeStruct((B, L, W2), BF16),
        compiler_params=_cparams(("parallel", "parallel")),
        name="hy_fwd",
    )(a_fwd, ueo, filt)


def _hy_inv_kernel(a_ref, y_ref, o_ref):
    o_ref[0] = jnp.dot(a_ref[...], y_ref[0], preferred_element_type=F32).astype(BF16)


def _hy_inv(a_inv, yspec, *, tm):
    B, L, W2 = yspec.shape
    L2 = L // 2
    return pl.pallas_call(
        _hy_inv_kernel,
        grid=(L2 // tm, B),
        in_specs=[pl.BlockSpec((tm, L), lambda i, b: (i, 0)),
                  pl.BlockSpec((1, L, W2), lambda i, b: (b, 0, 0))],
        out_specs=pl.BlockSpec((1, tm, W2), lambda i, b: (b, i, 0)),
        out_shape=jax.ShapeDtypeStruct((B, L2, W2), BF16),
        compiler_params=_cparams(("parallel", "parallel")),
        name="hy_inv",
    )(a_inv, yspec)


def _pad_rows(a, rows):
    return jnp.concatenate([a, jnp.zeros((rows - a.shape[0],) + a.shape[1:], a.dtype)], axis=0)


def _pad_cols(a, cols):
    return jnp.concatenate([a, jnp.zeros(a.shape[:-1] + (cols - a.shape[-1],), a.dtype)], axis=-1)


def _tile(n, pref):
    t = min(pref, n)
    assert n % t == 0
    return t


def _layer(x3, p, tables, *, final, final_g):
    B, L, _ = x3.shape
    T = B * L
    x2 = x3.reshape(T, D_MODEL)
    u, dt = _inproj(x2, p["norm_g"], p["w_in"], tm=_tile(T, 512), tn=512)
    u3 = u.reshape(B, L, N_IN)
    dt3 = dt.reshape(B, L, DT_PAD)

    ys = _ssd(u3, dt3, p["ssd_cw"], p["ssd_cb"], p["ssd_dtb"], p["ssd_alog"], p["rexp_f"],
              p["rexp_b"], p["ssd_dskip"], p["ssd_ng"], qb=_tile(L // CHUNK, 8))

    ya = _attention(u3, p["att_bias"], p["att_sink"], p["att_ng"], qb=_tile(L // ATT_BLOCK, 8))

    a_fwd, a_inv, filt, hm = tables
    tl = _tile(L, 1024)
    ueo, xa = _hy_pre(u3, p["hy_cw"], p["hy_cb"], tl=tl)
    yspec = _hy_fwd(a_fwd, ueo, filt, hm=hm)
    conv = _hy_inv(a_inv, yspec, tm=_tile(L // 2, 512))

    out = _outproj(x2, ys.reshape(T, SSD_WIDTH), ya.reshape(T, ATT_WIDTH),
                   conv.reshape(T // 2, 2 * HY_WIDTH), ueo.reshape(T // 2, 2 * HY_WIDTH),
                   xa.reshape(T, HY_WIDTH), u, p["hy_d"], p["hy_ng"], p["w_out"], final_g,
                   tm=_tile(T, 512), final=final)
    return out.reshape(B, L, D_MODEL)


def kernel(x_prompt, x_sample, rel_bias, norm_g, w_in, ssd_conv_w, ssd_conv_b, ssd_dt_bias, ssd_a_log, ssd_d, ssd_norm_g, att_sink, att_norm_g, hy_conv_w, hy_conv_b, hy_w1, hy_b1, hy_w2, hy_b2, hy_w3, hy_b3, hy_w4, hy_freq, hy_d, hy_norm_g, w_out, final_norm_g):
    depth = w_in.shape[0]
    att_perm = _att_col_perm()
    att_bias = _attn_bias(rel_bias)
    rexp_f, rexp_b = _ssd_expand_matrix(False), _ssd_expand_matrix(True)
    final_g = final_norm_g.astype(F32)[None, :]

    layers = []
    for i in range(depth):
        layers.append(dict(
            norm_g=norm_g[i].astype(F32)[None, :],
            w_in=_prep_w_in(w_in, i),
            ssd_cw=_pad_rows(ssd_conv_w[i].astype(F32), 8),
            ssd_cb=ssd_conv_b[i].astype(F32)[None, :],
            ssd_dtb=_pad_cols(ssd_dt_bias[i].astype(F32).reshape(1, 2 * SSD_HEADS), DT_PAD),
            ssd_alog=_pad_cols(ssd_a_log[i].astype(F32).reshape(1, 2 * SSD_HEADS), DT_PAD),
            ssd_dskip=jnp.repeat(ssd_d[i].astype(F32), SSD_HEAD_DIM)[None, :],
            ssd_ng=ssd_norm_g[i].astype(F32)[None, :],
            rexp_f=rexp_f, rexp_b=rexp_b,
            att_bias=att_bias,
            att_sink=jnp.broadcast_to(att_sink[i].astype(F32)[:, None] * LOG2E, (ATT_HEADS, LANES)),
            att_ng=att_norm_g[i].astype(F32)[att_perm][None, :],
            hy_cw=_pad_rows(hy_conv_w[i].astype(F32), 8),
            hy_cb=hy_conv_b[i].astype(F32)[None, :],
            hy_d=hy_d[i].astype(F32)[None, :],
            hy_ng=hy_norm_g[i].astype(F32)[None, :],
            w_out=_prep_w_out(w_out, i),
        ))

    def trunk(x):
        L = x.shape[1]
        hm = min(512, L // 4)
        a_fwd, a_inv = _dft_tables(L, hm)
        for i in range(depth):
            xf = _hy_filter(L, hy_w1[i], hy_b1[i].astype(F32), hy_w2[i].astype(F32),
                            hy_b2[i].astype(F32), hy_w3[i].astype(F32), hy_b3[i].astype(F32),
                            hy_w4[i].astype(F32), hy_freq[i].astype(F32))
            filt = _hy_kspec(a_fwd, xf, hm=hm)
            x = _layer(x, layers[i], (a_fwd, a_inv, filt, hm), final=(i == depth - 1),
                       final_g=final_g)
        return x

    return (trunk(x_prompt), trunk(x_sample))
```

```python
import functools
import math

import jax
import jax.numpy as jnp
import numpy as np
from jax import lax
from jax.experimental import pallas as pl
from jax.experimental.pallas import tpu as pltpu

F32 = jnp.float32
BF16 = jnp.bfloat16
HIGHEST = lax.Precision.HIGHEST

D_MODEL = 2048
SSD_WIDTH = 1024
ATT_WIDTH = 512
HY_WIDTH = 512
SSD_HEAD_DIM = 64
SSD_HEADS = 16
SSD_GROUPS = 2
SSD_STATE = 128
SSD_CONV = 5
CHUNK = 128
SSD_XBC = SSD_WIDTH + 2 * SSD_GROUPS * SSD_STATE
ATT_HEAD_DIM = 64
ATT_HEADS = 8
ATT_KV_HEADS = 2
ATT_REP = ATT_HEADS // ATT_KV_HEADS
ATT_WINDOW = 128
ATT_BLOCK = 128
REL_BUCKETS = 32
REL_MAX_DIST = 128
HY_CONV = 3
HY_EMB_BANDS = 16
HY_FF = 64
HY_FAST_DECAY = 0.3
HY_SLOW_DECAY = 1.5
HY_DECAY_TARGET = 1e-2
EPS = 1e-6
NEG_BIG = -1e30

LANES = 128
BF16_SUBLANES = 16
VMEM_LIMIT = 56 * 1024 * 1024

C_XBC = 0
C_HY = 1536
C_Z = 3072
C_Q = 4096
C_GATT = 4608
C_GHY = 5120
C_K = 5632
C_V = 5760
C_DT = 5888
DT_PAD = 128
N_IN = C_DT + DT_PAD

_OLD_SIZES = [SSD_WIDTH, SSD_XBC, 2 * SSD_HEADS, ATT_WIDTH, ATT_KV_HEADS * ATT_HEAD_DIM,
              ATT_KV_HEADS * ATT_HEAD_DIM, ATT_WIDTH, 3 * HY_WIDTH, HY_WIDTH]
_OLD_OFF = np.concatenate([[0], np.cumsum(_OLD_SIZES)])
IN_COLS = int(_OLD_OFF[-1])


LOG2E = math.log2(math.e)
Q_SCALE = ATT_HEAD_DIM ** -0.5 * LOG2E


def _att_col_perm():
    order = [h for j in range(ATT_REP) for h in (j, ATT_REP + j)]
    return np.concatenate([np.arange(h * ATT_HEAD_DIM, (h + 1) * ATT_HEAD_DIM) for h in order])


def _in_perm():
    perm = np.full((N_IN,), IN_COLS, np.int32)
    scale = np.ones((N_IN,), np.float32)
    o = {n: int(_OLD_OFF[i]) for i, n in enumerate(
        ["z", "xbc", "dt", "q", "k", "v", "gatt", "hy", "ghy"])}
    def put(new, old, width):
        perm[new:new + width] = np.arange(old, old + width)
    put(C_XBC, o["xbc"], SSD_XBC)
    put(C_HY, o["hy"], 3 * HY_WIDTH)
    put(C_Z, o["z"], SSD_WIDTH)
    perm[C_Q:C_Q + ATT_WIDTH] = o["q"] + _att_col_perm()
    scale[C_Q:C_Q + ATT_WIDTH] = Q_SCALE
    perm[C_GATT:C_GATT + ATT_WIDTH] = o["gatt"] + _att_col_perm()
    put(C_GHY, o["ghy"], HY_WIDTH)
    put(C_K, o["k"], 128)
    put(C_V, o["v"], 128)
    put(C_DT, o["dt"], 2 * SSD_HEADS)
    return perm, scale


def _col_runs(perm):
    cuts = np.flatnonzero(np.diff(perm) != 1) + 1
    starts = np.concatenate([[0], cuts])
    ends = np.concatenate([cuts, [len(perm)]])
    return [(int(s), int(perm[s]), int(e - s)) for s, e in zip(starts, ends)]


def _prep_w_in_kernel(w_ref, o_ref, *, runs, n_real):
    for new0, old0, n in runs:
        o_ref[:, new0:new0 + n] = w_ref[:, old0:old0 + n]
    o_ref[:, n_real:] = jnp.zeros((o_ref.shape[0], N_IN - n_real), BF16)


def _prep_w_in(w_in_b, layer):
    perm, _ = _in_perm()
    n_real = C_DT + 2 * SSD_HEADS
    tr = 512
    return pl.pallas_call(
        functools.partial(_prep_w_in_kernel, runs=_col_runs(perm[:n_real]), n_real=n_real),
        grid=(D_MODEL // tr,),
        in_specs=[pl.BlockSpec((None, tr, IN_COLS), lambda r: (layer, r, 0))],
        out_specs=pl.BlockSpec((tr, N_IN), lambda r: (r, 0)),
        out_shape=jax.ShapeDtypeStruct((D_MODEL, N_IN), BF16),
        compiler_params=_cparams(("parallel",)),
        name="prep_w_in",
    )(w_in_b)


def _prep_w_out_kernel(w_ref, o_ref, *, runs):
    for new0, old0, n in runs:
        o_ref[new0:new0 + n, :] = w_ref[old0:old0 + n, :].astype(BF16)


def _prep_w_out(w_out, layer):
    rows = np.concatenate([np.arange(SSD_WIDTH), SSD_WIDTH + _att_col_perm(),
                           np.arange(SSD_WIDTH + ATT_WIDTH, D_MODEL)])
    tc = 512
    return pl.pallas_call(
        functools.partial(_prep_w_out_kernel, runs=_col_runs(rows)),
        grid=(D_MODEL // tc,),
        in_specs=[pl.BlockSpec((None, D_MODEL, tc), lambda c: (layer, 0, c))],
        out_specs=pl.BlockSpec((D_MODEL, tc), lambda c: (0, c)),
        out_shape=jax.ShapeDtypeStruct((D_MODEL, D_MODEL), BF16),
        compiler_params=_cparams(("parallel",)),
        name="prep_w_out",
    )(w_out)


def _cparams(sem):
    return pltpu.CompilerParams(dimension_semantics=sem, vmem_limit_bytes=VMEM_LIMIT)


def _silu(x):
    return x * (1.0 / (1.0 + jnp.exp(-x)))


def _softplus(x):
    return jnp.maximum(x, 0.0) + jnp.log1p(jnp.exp(-jnp.abs(x)))


def _inproj_kernel(x_ref, g_ref, w_ref, u_ref, dt_ref, h_ref, *, rows, tn):
    for r in range(x_ref.shape[0] // rows):
        sl = slice(r * rows, (r + 1) * rows)
        x = x_ref[sl, :]
        ms = jnp.mean(x * x, axis=-1, keepdims=True)
        h_ref[sl, :] = (x * lax.rsqrt(ms + EPS) * g_ref[...]).astype(BF16)
        h = h_ref[sl, :]
        for c0 in range(0, N_IN, tn):
            c1 = min(c0 + tn, N_IN)
            acc = jnp.dot(h, w_ref[:, c0:c1], preferred_element_type=F32)
            u_ref[sl, c0:c1] = acc.astype(BF16)
            if c0 <= C_DT < c1:
                dt_ref[sl, :] = acc[:, C_DT - c0:C_DT - c0 + DT_PAD]


def _inproj(x2, g, w, *, tm, tn):
    T = x2.shape[0]
    assert T % tm == 0
    return pl.pallas_call(
        functools.partial(_inproj_kernel, rows=min(256, tm), tn=tn),
        grid=(T // tm,),
        in_specs=[pl.BlockSpec((tm, D_MODEL), lambda i: (i, 0)),
                  pl.BlockSpec((1, D_MODEL), lambda i: (0, 0)),
                  pl.BlockSpec((D_MODEL, N_IN), lambda i: (0, 0), pipeline_mode=pl.Buffered(1))],
        out_specs=[pl.BlockSpec((tm, N_IN), lambda i: (i, 0)),
                   pl.BlockSpec((tm, DT_PAD), lambda i: (i, 0))],
        out_shape=[jax.ShapeDtypeStruct((T, N_IN), BF16),
                   jax.ShapeDtypeStruct((T, DT_PAD), F32)],
        scratch_shapes=[pltpu.VMEM((tm, D_MODEL), BF16)],
        compiler_params=_cparams(("parallel",)),
        name="inproj",
    )(x2, g, w)


def _outproj_kernel(x_ref, ys_ref, ya_ref, c_ref, ueo_ref, xa_ref, g_ref, d_ref, hng_ref, w_ref, fg_ref,
                    o_ref, s_ref, *, rows, final):
    nblk = HY_WIDTH // LANES

    def interleave(ref, hs):
        for p in range(2):
            for c in range(nblk):
                s_ref[c, pl.ds(p, rows // 2, stride=2), :] = ref[
                    hs, (p * nblk + c) * LANES:(p * nblk + c + 1) * LANES].astype(F32)
        return jnp.concatenate([s_ref[c] for c in range(nblk)], axis=1)

    for r in range(x_ref.shape[0] // rows):
        sl = slice(r * rows, (r + 1) * rows)
        hs = slice(r * rows // 2, (r + 1) * rows // 2)
        conv = interleave(c_ref, hs)
        u = interleave(ueo_ref, hs)
        xa = xa_ref[sl, :].astype(F32)
        y = xa * (conv + u * d_ref[...])
        y = y * _silu(g_ref[sl, :].astype(F32))
        ms = jnp.mean(y * y, axis=-1, keepdims=True)
        yh = (y * lax.rsqrt(ms + EPS) * hng_ref[...]).astype(BF16)

        acc = x_ref[sl, :]
        acc = acc + jnp.dot(ys_ref[sl, :], w_ref[0:SSD_WIDTH, :], preferred_element_type=F32)
        acc = acc + jnp.dot(ya_ref[sl, :], w_ref[SSD_WIDTH:SSD_WIDTH + ATT_WIDTH, :],
                            preferred_element_type=F32)
        acc = acc + jnp.dot(yh, w_ref[SSD_WIDTH + ATT_WIDTH:, :], preferred_element_type=F32)
        if final:
            ms = jnp.mean(acc * acc, axis=-1, keepdims=True)
            acc = acc * lax.rsqrt(ms + EPS) * fg_ref[...]
        o_ref[sl, :] = acc


def _outproj(x2, ys, ya, conv2, ueo2, xa2, u2, d, hng, w, fg, *, tm, final):
    T = x2.shape[0]
    assert T % tm == 0
    rows = min(256, tm)
    return pl.pallas_call(
        functools.partial(_outproj_kernel, rows=rows, final=final),
        grid=(T // tm,),
        in_specs=[pl.BlockSpec((tm, D_MODEL), lambda i: (i, 0)),
                  pl.BlockSpec((tm, SSD_WIDTH), lambda i: (i, 0)),
                  pl.BlockSpec((tm, ATT_WIDTH), lambda i: (i, 0)),
                  pl.BlockSpec((tm // 2, 2 * HY_WIDTH), lambda i: (i, 0)),
                  pl.BlockSpec((tm // 2, 2 * HY_WIDTH), lambda i: (i, 0)),
                  pl.BlockSpec((tm, HY_WIDTH), lambda i: (i, 0)),
                  pl.BlockSpec((tm, HY_WIDTH), lambda i: (i, C_GHY // HY_WIDTH)),
                  pl.BlockSpec((1, HY_WIDTH), lambda i: (0, 0)),
                  pl.BlockSpec((1, HY_WIDTH), lambda i: (0, 0)),
                  pl.BlockSpec((D_MODEL, D_MODEL), lambda i: (0, 0)),
                  pl.BlockSpec((1, D_MODEL), lambda i: (0, 0))],
        out_specs=pl.BlockSpec((tm, D_MODEL), lambda i: (i, 0)),
        out_shape=jax.ShapeDtypeStruct((T, D_MODEL), F32),
        scratch_shapes=[pltpu.VMEM((HY_WIDTH // LANES, rows, LANES), F32)],
        compiler_params=_cparams(("parallel",)),
        name="outproj",
    )(x2, ys, ya, conv2, ueo2, xa2, u2, d, hng, w, fg)


def _t5_buckets(rel):
    nb = REL_BUCKETS // 2
    max_exact = nb // 2
    ret = (rel > 0).astype(np.int32) * nb
    n = np.abs(rel)
    large = max_exact + (np.log(np.maximum(n, 1) / max_exact) / math.log(REL_MAX_DIST / max_exact)
                         * (nb - max_exact)).astype(np.int32)
    large = np.minimum(large, nb - 1)
    return ret + np.where(n < max_exact, n, large)


def _attn_bias(rel_bias):
    qi = np.arange(ATT_BLOCK)[:, None]
    kj = np.arange(3 * ATT_BLOCK)[None, :]
    rel = kj - ATT_BLOCK - qi
    onehot = (_t5_buckets(rel)[None] == np.arange(REL_BUCKETS)[:, None, None]).astype(np.float32)
    bias = jnp.einsum("bqk,bh->hqk", jnp.asarray(onehot), rel_bias.astype(F32),
                      precision=HIGHEST) * LOG2E
    window = np.abs(rel) <= ATT_WINDOW
    variants = []
    for last in (False, True):
        for first in (False, True):
            ok = window & ~(first & (kj < ATT_BLOCK)) & ~(last & (kj >= 2 * ATT_BLOCK))
            variants.append(jnp.where(ok[None], bias, NEG_BIG))
    return jnp.transpose(jnp.stack(variants), (0, 1, 3, 2))


def _attn_kernel(q_ref, kp_ref, kc_ref, kn_ref, vp_ref, vc_ref, vn_ref, g_ref, bias_ref, sink_ref,
                 ng_ref, o_ref, klo_ref, khi_ref, vt_ref, *, nt, qb):
    n = pl.program_id(1)
    lo = lax.broadcasted_iota(jnp.int32, (1, LANES), 1) < ATT_HEAD_DIM
    zero = jnp.zeros((), BF16)
    kext = jnp.concatenate([kp_ref[0], kc_ref[0], kn_ref[0]], axis=0)
    klo_ref[...] = jnp.where(lo, kext, zero)
    khi_ref[...] = jnp.where(lo, zero, kext)
    for t, ref, cnt in ((0, vp_ref, 1), (1, vc_ref, qb), (qb + 1, vn_ref, 1)):
        for i in range(cnt):
            blk = ref[0, i * ATT_BLOCK:(i + 1) * ATT_BLOCK, :]
            vt_ref[t + i] = blk.astype(F32).T.astype(BF16)
    row_lo = lax.broadcasted_iota(jnp.int32, (LANES, ATT_BLOCK), 0) < ATT_HEAD_DIM
    nk = 3 * ATT_BLOCK

    def body(i, carry):
        r0 = pl.multiple_of(i * ATT_BLOCK, ATT_BLOCK)
        q = q_ref[0, pl.ds(r0, ATT_BLOCK), :]
        kst = jnp.concatenate([klo_ref[pl.ds(r0, nk), :], khi_ref[pl.ds(r0, nk), :]], axis=0)
        vt = jnp.concatenate([vt_ref[i], vt_ref[i + 1], vt_ref[i + 2]], axis=1)
        first = jnp.logical_and(n == 0, i == 0)
        last = jnp.logical_and(n == nt - 1, i == qb - 1)
        variant = first.astype(jnp.int32) + 2 * last.astype(jnp.int32)
        outs = []
        for j in range(ATT_REP):
            qp = q[:, j * LANES:(j + 1) * LANES]
            st = lax.dot_general(kst, qp, (((1,), (1,)), ((), ())), preferred_element_type=F32)
            halves = []
            for e, h in enumerate((j, ATT_REP + j)):
                s = st[e * nk:(e + 1) * nk] + bias_ref[variant, h]
                sk = sink_ref[h:h + 1, :]
                m = jnp.maximum(jnp.max(s, axis=0, keepdims=True), sk)
                p = jnp.exp2(s - m)
                den = jnp.sum(p, axis=0, keepdims=True) + jnp.exp2(sk - m)
                ot = jnp.dot(vt, p.astype(BF16), preferred_element_type=F32)
                halves.append(ot * (1.0 / den))
            outs.append(jnp.where(row_lo, halves[0], halves[1]).T)
        o = jnp.concatenate(outs, axis=-1)
        y = o * _silu(g_ref[0, pl.ds(r0, ATT_BLOCK), :].astype(F32))
        ms = jnp.mean(y * y, axis=-1, keepdims=True)
        o_ref[0, pl.ds(r0, ATT_BLOCK), :] = (y * lax.rsqrt(ms + EPS) * ng_ref[...]).astype(BF16)
        return carry

    lax.fori_loop(0, qb, body, 0, unroll=True)


def _attention(u3, bias, sink, ng, *, qb):
    B, L, _ = u3.shape
    tq = qb * ATT_BLOCK
    assert L % tq == 0
    nt = L // tq
    nb = L // ATT_BLOCK
    kcol, vcol = C_K // 128, C_V // 128
    def kv_specs(colblk):
        return [pl.BlockSpec((1, ATT_BLOCK, 128), lambda b, n: (b, jnp.maximum(n * qb - 1, 0), colblk)),
                pl.BlockSpec((1, tq, 128), lambda b, n: (b, n, colblk)),
                pl.BlockSpec((1, ATT_BLOCK, 128),
                             lambda b, n: (b, jnp.minimum((n + 1) * qb, nb - 1), colblk))]
    return pl.pallas_call(
        functools.partial(_attn_kernel, nt=nt, qb=qb),
        grid=(B, nt),
        in_specs=[pl.BlockSpec((1, tq, ATT_WIDTH), lambda b, n: (b, n, C_Q // ATT_WIDTH))]
                 + kv_specs(kcol) + kv_specs(vcol)
                 + [pl.BlockSpec((1, tq, ATT_WIDTH), lambda b, n: (b, n, C_GATT // ATT_WIDTH)),
                    pl.BlockSpec((4, ATT_HEADS, 3 * ATT_BLOCK, ATT_BLOCK), lambda b, n: (0, 0, 0, 0)),
                    pl.BlockSpec((ATT_HEADS, LANES), lambda b, n: (0, 0)),
                    pl.BlockSpec((1, ATT_WIDTH), lambda b, n: (0, 0))],
        out_specs=pl.BlockSpec((1, tq, ATT_WIDTH), lambda b, n: (b, n, 0)),
        out_shape=jax.ShapeDtypeStruct((B, L, ATT_WIDTH), BF16),
        scratch_shapes=[pltpu.VMEM(((qb + 2) * ATT_BLOCK, LANES), BF16),
                        pltpu.VMEM(((qb + 2) * ATT_BLOCK, LANES), BF16),
                        pltpu.VMEM((qb + 2, LANES, ATT_BLOCK), BF16)],
        compiler_params=_cparams(("parallel", "parallel")),
        name="attn",
    )(u3, u3, u3, u3, u3, u3, u3, u3, bias, sink, ng)


HALO = BF16_SUBLANES


CONV_BLK = 128


def _load_padded(xpad_ref, xm_ref, xp_ref, xn_ref, has_prev, has_next, rows):
    zero = jnp.zeros((), BF16)
    xpad_ref[pl.ds(0, HALO), :] = jnp.where(has_prev, xp_ref[0], zero)
    xpad_ref[pl.ds(HALO, rows), :] = xm_ref[0]
    xpad_ref[pl.ds(HALO + rows, HALO), :] = jnp.where(has_next, xn_ref[0], zero)


def _shift_matrix(width):
    offs = [k - width // 2 for k in range(width) if k != width // 2]
    r = lax.broadcasted_iota(jnp.int32, (CONV_BLK, CONV_BLK + 2 * HALO), 0)
    c = lax.broadcasted_iota(jnp.int32, (CONV_BLK, CONV_BLK + 2 * HALO), 1)
    return jnp.concatenate([(c == r + HALO + d).astype(BF16) for d in offs], axis=0)


def _dwconv_block(xpad_ref, shifts, w_ref, b_ref, j, width, c0, c1):
    win = xpad_ref[j * CONV_BLK:(j + 1) * CONV_BLK + 2 * HALO, c0:c1]
    moved = jnp.dot(shifts, win, preferred_element_type=F32)
    acc = win[HALO:HALO + CONV_BLK].astype(F32) * w_ref[width // 2:width // 2 + 1, c0:c1]
    i = 0
    for k in range(width):
        if k == width // 2:
            continue
        acc = acc + moved[i * CONV_BLK:(i + 1) * CONV_BLK] * w_ref[k:k + 1, c0:c1]
        i += 1
    return acc + b_ref[:, c0:c1]


def _dwconv(xpad_ref, w_ref, b_ref, rows, width, c0, c1):
    shifts = _shift_matrix(width)
    return jnp.concatenate([_dwconv_block(xpad_ref, shifts, w_ref, b_ref, j, width, c0, c1)
                            for j in range(rows // CONV_BLK)], axis=0)


SSD_HPG = SSD_HEADS // SSD_GROUPS
SSD_GW = SSD_HPG * SSD_HEAD_DIM
SSD_BC = SSD_GROUPS * SSD_STATE


def _ssd_decay(dt_raw, dtb_ref, alog_ref):
    dt = _softplus(dt_raw + dtb_ref[...])
    dta = dt * (-jnp.exp(alog_ref[...]))
    row = lax.broadcasted_iota(jnp.int32, (CHUNK, CHUNK), 0)
    col = lax.broadcasted_iota(jnp.int32, (CHUNK, CHUNK), 1)
    tril = (row >= col).astype(BF16)
    hi = dta.astype(BF16)
    r1 = dta - hi.astype(F32)
    mid = r1.astype(BF16)
    lo = (r1 - mid.astype(F32)).astype(BF16)
    pre = (jnp.dot(tril, hi, preferred_element_type=F32) + jnp.dot(tril, mid, preferred_element_type=F32)
           + jnp.dot(tril, lo, preferred_element_type=F32))
    tot = pre[CHUNK - 1:CHUNK, :]
    is_bwd = lax.broadcasted_iota(jnp.int32, (1, LANES), 1) >= SSD_HEADS
    acs = jnp.where(is_bwd, tot - pre + dta, pre)
    return dt, acs, tot


def _ssd_bwd_kernel(xm_ref, xp_ref, xn_ref, dt_ref, cw_ref, cb_ref, dtb_ref, alog_ref, rexp_ref,
                    xact_ref, prev_ref, xpad_ref, xf_ref, st_ref, *, nt, qb):
    n = pl.program_id(1)
    tile = nt - 1 - n
    rows = qb * CHUNK

    @pl.when(n == 0)
    def _():
        st_ref[...] = jnp.zeros(st_ref.shape, F32)

    _load_padded(xpad_ref, xm_ref, xp_ref, xn_ref, tile > 0, tile < nt - 1, rows)
    xact = _silu(_dwconv(xpad_ref, cw_ref, cb_ref, rows, SSD_CONV, 0, SSD_XBC))
    xf_ref[...] = xact
    xact_ref[0] = xact.astype(BF16)

    def body(j, carry):
        i = qb - 1 - j
        r0 = pl.multiple_of(i * CHUNK, CHUNK)
        xs = xf_ref[pl.ds(r0, CHUNK), 0:SSD_WIDTH]
        bm = xf_ref[pl.ds(r0, CHUNK), SSD_WIDTH:SSD_WIDTH + SSD_BC].astype(BF16)
        dt, acs, tot = _ssd_decay(dt_ref[0, pl.ds(r0, CHUNK), :], dtb_ref, alog_ref)
        e_in = jnp.concatenate([jnp.exp(tot - acs) * dt, jnp.broadcast_to(jnp.exp(tot), (8, LANES))],
                               axis=0).astype(BF16)
        e_out = jnp.dot(e_in, rexp_ref[...], preferred_element_type=F32)
        xd = (xs * e_out[0:CHUNK]).astype(BF16)
        cdec = e_out[CHUNK:CHUNK + 1]
        for g in range(SSD_GROUPS):
            prev = st_ref[g]
            prev_ref[0, i, g] = prev.astype(BF16)
            s_new = lax.dot_general(bm[:, g * SSD_STATE:(g + 1) * SSD_STATE],
                                    xd[:, g * SSD_GW:(g + 1) * SSD_GW], (((0,), (0,)), ((), ())),
                                    preferred_element_type=F32)
            st_ref[g] = prev * cdec[:, g * SSD_GW:(g + 1) * SSD_GW] + s_new
        return carry

    lax.fori_loop(0, qb, body, 0, unroll=True)


def _ssd_fwd_kernel(xa_ref, dt_ref, prev_ref, z_ref, dtb_ref, alog_ref, rexpf_ref, rexpb_ref,
                    dsk_ref, ng_ref, o_ref, st_ref, *, qb):
    n = pl.program_id(1)

    @pl.when(n == 0)
    def _():
        st_ref[...] = jnp.zeros(st_ref.shape, F32)

    row = lax.broadcasted_iota(jnp.int32, (CHUNK, CHUNK), 0)
    col = lax.broadcasted_iota(jnp.int32, (CHUNK, CHUNK), 1)
    fwd_part = row > col
    diag = row == col
    lane_lo = lax.broadcasted_iota(jnp.int32, (1, LANES), 1) < SSD_HEAD_DIM
    zero_b = jnp.zeros((), BF16)

    def body(i, carry):
        r0 = pl.multiple_of(i * CHUNK, CHUNK)
        xs_b = xa_ref[0, pl.ds(r0, CHUNK), 0:SSD_WIDTH]
        xs = xs_b.astype(F32)
        bm = xa_ref[0, pl.ds(r0, CHUNK), SSD_WIDTH:SSD_WIDTH + SSD_BC]
        cm = xa_ref[0, pl.ds(r0, CHUNK), SSD_WIDTH + SSD_BC:SSD_XBC]
        dt, acs, tot = _ssd_decay(dt_ref[0, pl.ds(r0, CHUNK), :], dtb_ref, alog_ref)
        acs2 = acs * LOG2E
        rt = (acs2 - jnp.log2(dt)).T
        dsum_t = jnp.log2(dt + pltpu.roll(dt, LANES - SSD_HEADS, axis=1)).T
        eacs = jnp.exp(acs)
        ef_in = jnp.concatenate([jnp.exp(tot - acs) * dt, eacs,
                                 jnp.broadcast_to(jnp.exp(tot), (8, LANES))], axis=0).astype(BF16)
        ef = jnp.dot(ef_in, rexpf_ref[...], preferred_element_type=F32)
        eb = jnp.dot(eacs.astype(BF16), rexpb_ref[...], preferred_element_type=F32)
        xd = (xs * ef[0:CHUNK]).astype(BF16)
        eacs_f = ef[CHUNK:2 * CHUNK]
        cdec = ef[2 * CHUNK:2 * CHUNK + 1]
        ys = []
        for g in range(SSD_GROUPS):
            bg = bm[:, g * SSD_STATE:(g + 1) * SSD_STATE]
            cg = cm[:, g * SSD_STATE:(g + 1) * SSD_STATE]
            gs = slice(g * SSD_GW, (g + 1) * SSD_GW)
            cb = lax.dot_general(cg, bg, (((1,), (1,)), ((), ())), preferred_element_type=F32)
            prev = st_ref[g]
            y_off = (jnp.dot(cg, prev.astype(BF16), preferred_element_type=F32) * eacs_f[:, gs]
                     + jnp.dot(cg, prev_ref[0, i, g], preferred_element_type=F32) * eb[:, gs])
            s_new = lax.dot_general(bg, xd[:, gs], (((0,), (0,)), ((), ())),
                                    preferred_element_type=F32)
            st_ref[g] = prev * cdec[:, gs] + s_new
            for pr in range(SSD_HPG // 2):
                c0 = g * SSD_GW + pr * LANES
                xpair = xs_b[:, c0:c0 + LANES]
                xbd = jnp.concatenate([jnp.where(lane_lo, xpair, zero_b),
                                       jnp.where(lane_lo, zero_b, xpair)], axis=0)
                mats = []
                for e in range(2):
                    h = g * SSD_HPG + pr * 2 + e
                    hb = SSD_HEADS + h
                    sel = jnp.where(fwd_part, acs2[:, h:h + 1] - rt[h:h + 1, :],
                                    acs2[:, hb:hb + 1] - rt[hb:hb + 1, :])
                    sel = jnp.where(diag, dsum_t[h:h + 1, :], sel)
                    mats.append((cb * jnp.exp2(sel)).astype(BF16))
                yd = jnp.dot(jnp.concatenate(mats, axis=1), xbd, preferred_element_type=F32)
                ys.append(yd + y_off[:, pr * LANES:(pr + 1) * LANES])
        y = jnp.concatenate(ys, axis=-1) + xs * dsk_ref[...]
        y = y * _silu(z_ref[0, pl.ds(r0, CHUNK), :].astype(F32))
        ms = jnp.mean(y * y, axis=-1, keepdims=True)
        o_ref[0, pl.ds(r0, CHUNK), :] = (y * lax.rsqrt(ms + EPS) * ng_ref[...]).astype(BF16)
        return carry

    lax.fori_loop(0, qb, body, 0, unroll=True)


def _ssd(u3, dt3, cw, cb, dtb, alog, rexp_f, rexp_b, dsk, ng, *, qb):
    B, L, _ = u3.shape
    rows = qb * CHUNK
    assert L % rows == 0
    nt = L // rows
    nc = L // CHUNK
    hb = rows // HALO
    nhb = L // HALO
    state = pltpu.VMEM((SSD_GROUPS, SSD_STATE, SSD_GW), F32)
    const = lambda shape: pl.BlockSpec(shape, lambda b, n: (0,) * len(shape))
    rt = lambda n: nt - 1 - n

    xact, prevb = pl.pallas_call(
        functools.partial(_ssd_bwd_kernel, nt=nt, qb=qb),
        grid=(B, nt),
        in_specs=[
            pl.BlockSpec((1, rows, SSD_XBC), lambda b, n: (b, rt(n), C_XBC // SSD_XBC)),
            pl.BlockSpec((1, HALO, SSD_XBC),
                         lambda b, n: (b, jnp.maximum(rt(n) * hb - 1, 0), C_XBC // SSD_XBC)),
            pl.BlockSpec((1, HALO, SSD_XBC),
                         lambda b, n: (b, jnp.minimum((rt(n) + 1) * hb, nhb - 1), C_XBC // SSD_XBC)),
            pl.BlockSpec((1, rows, DT_PAD), lambda b, n: (b, rt(n), 0)),
            const((8, SSD_XBC)), const((1, SSD_XBC)), const((1, DT_PAD)), const((1, DT_PAD)),
            const((LANES, SSD_WIDTH)),
        ],
        out_specs=[pl.BlockSpec((1, rows, SSD_XBC), lambda b, n: (b, rt(n), 0)),
                   pl.BlockSpec((1, qb, SSD_GROUPS, SSD_STATE, SSD_GW),
                                lambda b, n: (b, rt(n), 0, 0, 0))],
        out_shape=[jax.ShapeDtypeStruct((B, L, SSD_XBC), BF16),
                   jax.ShapeDtypeStruct((B, nc, SSD_GROUPS, SSD_STATE, SSD_GW), BF16)],
        scratch_shapes=[pltpu.VMEM((rows + 2 * HALO, SSD_XBC), BF16),
                        pltpu.VMEM((rows, SSD_XBC), F32), state],
        compiler_params=_cparams(("parallel", "arbitrary")),
        name="ssd_bwd",
    )(u3, u3, u3, dt3, cw, cb, dtb, alog, rexp_b)

    return pl.pallas_call(
        functools.partial(_ssd_fwd_kernel, qb=qb),
        grid=(B, nt),
        in_specs=[
            pl.BlockSpec((1, rows, SSD_XBC), lambda b, n: (b, n, 0)),
            pl.BlockSpec((1, rows, DT_PAD), lambda b, n: (b, n, 0)),
            pl.BlockSpec((1, qb, SSD_GROUPS, SSD_STATE, SSD_GW), lambda b, n: (b, n, 0, 0, 0)),
            pl.BlockSpec((1, rows, SSD_WIDTH), lambda b, n: (b, n, C_Z // SSD_WIDTH)),
            const((1, DT_PAD)), const((1, DT_PAD)),
            const((LANES, SSD_WIDTH)), const((LANES, SSD_WIDTH)),
            const((1, SSD_WIDTH)), const((1, SSD_WIDTH)),
        ],
        out_specs=pl.BlockSpec((1, rows, SSD_WIDTH), lambda b, n: (b, n, 0)),
        out_shape=jax.ShapeDtypeStruct((B, L, SSD_WIDTH), BF16),
        scratch_shapes=[state],
        compiler_params=_cparams(("parallel", "arbitrary")),
        name="ssd_fwd",
    )(xact, dt3, prevb, u3, dtb, alog, rexp_f, rexp_b, dsk, ng)


def _ssd_expand_matrix(rev):
    m = np.zeros((LANES, SSD_WIDTH), np.float32)
    hoff = SSD_HEADS if rev else 0
    for h in range(SSD_HEADS):
        m[hoff + h, h * SSD_HEAD_DIM:(h + 1) * SSD_HEAD_DIM] = 1.0
    return jnp.asarray(m, BF16)


def _hy_filter_kernel(t_ref, bands_ref, w1t_ref, w1c_ref, w1s_ref, b1_ref, w2_ref, b2_ref, w3_ref,
                      b3_ref, w4_ref, fr_ref, absd_ref, x_ref, s_ref, *, L, tl):
    i = pl.program_id(0)
    pos = (lax.broadcasted_iota(jnp.int32, (tl, 1), 0) + i * tl).astype(F32)
    t = t_ref[...]
    pos_row = (lax.broadcasted_iota(jnp.int32, (1, tl), 1) + i * tl).astype(F32)
    ang_t = 2.0 * math.pi * pos_row * bands_ref[...] / L
    fr = fr_ref[...]
    dot = functools.partial(jnp.dot, preferred_element_type=F32, precision=HIGHEST)
    tdot = lambda a, b: lax.dot_general(a, b, (((0,), (0,)), ((), ())), preferred_element_type=F32,
                                        precision=HIGHEST)
    pre = t * w1t_ref[...] + tdot(jnp.cos(ang_t), w1c_ref[...]) + tdot(-jnp.sin(ang_t), w1s_ref[...])
    h = jnp.sin(fr * (pre + b1_ref[...]))
    h = jnp.sin(fr * (dot(h, w2_ref[...]) + b2_ref[...]))
    h = jnp.sin(fr * (dot(h, w3_ref[...]) + b3_ref[...]))
    w4 = w4_ref[...]
    h_hi, w_hi = h.astype(BF16), w4.astype(BF16)
    h_lo, w_lo = (h - h_hi.astype(F32)).astype(BF16), (w4 - w_hi.astype(F32)).astype(BF16)
    bdot = functools.partial(jnp.dot, preferred_element_type=F32)
    h = bdot(h_hi, w_hi) + bdot(h_hi, w_lo) + bdot(h_lo, w_hi)
    decay = jnp.exp(-t * absd_ref[...])
    hf = h[:, :HY_WIDTH] * decay
    hb = jnp.where(pos == 0.0, 0.0, h[:, HY_WIDTH:] * decay)
    _to_lane_blocks(s_ref, jnp.concatenate([hf + hb, hb - hf], axis=1))
    x_ref[...] = _split_even_odd(s_ref).astype(BF16)


def _hy_filter(L, w1, b1, w2, b2, w3, b3, w4, freq):
    tl = min(512, L)
    t = jnp.linspace(0.0, 1.0, L, dtype=F32)[:, None]
    bands = jnp.linspace(1e-4, HY_EMB_BANDS - 1, HY_EMB_BANDS, dtype=F32)[:, None]
    max_decay = math.log(HY_DECAY_TARGET) / HY_FAST_DECAY
    min_decay = math.log(HY_DECAY_TARGET) / HY_SLOW_DECAY
    absd = jnp.abs(jnp.linspace(min_decay, max_decay, HY_WIDTH, dtype=F32))[None, :]
    w1 = w1.astype(F32)
    full = lambda a: pl.BlockSpec(a.shape, lambda i: (0,) * a.ndim)
    args = [t, bands, w1[0:1], w1[1:1 + HY_EMB_BANDS], w1[1 + HY_EMB_BANDS:], b1[None], w2, b2[None],
            w3, b3[None], w4, freq[None], absd]
    in_specs = [pl.BlockSpec((tl, 1), lambda i: (i, 0))] + [full(a) for a in args[1:]]
    return pl.pallas_call(
        functools.partial(_hy_filter_kernel, L=L, tl=tl),
        grid=(L // tl,),
        in_specs=in_specs,
        out_specs=pl.BlockSpec((tl // 2, 4 * HY_WIDTH), lambda i: (i, 0)),
        out_shape=jax.ShapeDtypeStruct((L // 2, 4 * HY_WIDTH), BF16),
        scratch_shapes=[pltpu.VMEM((2 * HY_WIDTH // LANES, tl, LANES), F32)],
        compiler_params=_cparams(("parallel",)),
        name="hy_filter",
    )(*args)


def _dft_tables(L, hm):
    L2 = L // 2
    g = jnp.arange(L2, dtype=jnp.int32)
    s = jnp.arange(L2, dtype=jnp.int32)
    ph = ((2 * g + 1)[:, None] * s[None, :]) % (2 * L)
    ang = ph.astype(F32) * (math.pi / L)
    c = jnp.cos(ang).astype(BF16).reshape(L2 // hm, 1, hm, L2)
    sn = jnp.sin(ang).astype(BF16).reshape(L2 // hm, 1, hm, L2)
    a_fwd = jnp.concatenate([c, sn], axis=1).reshape(L, L2)
    a_inv = jnp.concatenate([c, -sn], axis=1).reshape(L, L2).T
    return a_fwd, a_inv


def _hy_kspec_kernel(a_ref, x_ref, wc_ref, ws_ref, f_ref, *, hm, scale):
    W = HY_WIDTH
    pq = jnp.dot(a_ref[...], x_ref[...], preferred_element_type=F32)
    ea, eb, oa, ob = [(pq[0:hm, i * W:(i + 1) * W], -pq[hm:, i * W:(i + 1) * W]) for i in range(4)]
    w = (wc_ref[...], -ws_ref[...])
    cmul = lambda x, y: (x[0] * y[0] - x[1] * y[1], x[0] * y[1] + x[1] * y[0])
    woa, wob = cmul(w, oa), cmul(w, ob)
    k1 = (ea[0] + woa[0], -(eb[1] + wob[1]))
    k2 = (ea[0] - woa[0], eb[1] - wob[1])
    kp = (k1[0] + k2[0], k1[1] - k2[1])
    km = (k1[0] - k2[0], k1[1] + k2[1])
    wkm = cmul(w, km)
    vkm = cmul((w[0], -w[1]), km)
    for i, part in enumerate((kp[0], kp[1], wkm[0], wkm[1], vkm[0], vkm[1])):
        f_ref[0, i] = part * scale


def _hy_kspec(a_fwd, xf2, *, hm):
    L, L2 = a_fwd.shape
    tm = 2 * hm
    nx = xf2.shape[1]
    theta = (2.0 * jnp.arange(L2, dtype=F32) + 1.0) * (math.pi / (2 * L))
    return pl.pallas_call(
        functools.partial(_hy_kspec_kernel, hm=hm, scale=1.0 / L),
        grid=(L // tm,),
        in_specs=[pl.BlockSpec((tm, L2), lambda i: (i, 0)),
                  pl.BlockSpec((L2, nx), lambda i: (0, 0)),
                  pl.BlockSpec((hm, 1), lambda i: (i, 0)),
                  pl.BlockSpec((hm, 1), lambda i: (i, 0))],
        out_specs=pl.BlockSpec((1, 6, hm, HY_WIDTH), lambda i: (i, 0, 0, 0)),
        out_shape=jax.ShapeDtypeStruct((L2 // hm, 6, hm, HY_WIDTH), F32),
        compiler_params=_cparams(("parallel",)),
        name="hy_kspec",
    )(a_fwd, xf2, jnp.cos(theta)[:, None], jnp.sin(theta)[:, None])


def _to_lane_blocks(s_ref, x):
    for c in range(s_ref.shape[0]):
        s_ref[c] = x[:, c * LANES:(c + 1) * LANES]


def _split_even_odd(s_ref):
    k, rows, _ = s_ref.shape
    return jnp.concatenate([s_ref[c, pl.ds(p, rows // 2, stride=2), :]
                            for p in range(2) for c in range(k)], axis=1)


def _hy_pre_kernel(xm_ref, xp_ref, xn_ref, cw_ref, cb_ref, u_ref, xa_ref, xpad_ref, s_ref, *, nt, tl):
    i = pl.program_id(1)
    _load_padded(xpad_ref, xm_ref, xp_ref, xn_ref, i > 0, i < nt - 1, tl)
    xa = _dwconv(xpad_ref, cw_ref, cb_ref, tl, HY_CONV, 0, HY_WIDTH)
    xb = _dwconv(xpad_ref, cw_ref, cb_ref, tl, HY_CONV, HY_WIDTH, 2 * HY_WIDTH)
    v = _dwconv(xpad_ref, cw_ref, cb_ref, tl, HY_CONV, 2 * HY_WIDTH, 3 * HY_WIDTH)
    u = xb * v
    xa_ref[0] = xa.astype(BF16)
    _to_lane_blocks(s_ref, u)
    u_ref[0] = _split_even_odd(s_ref).astype(BF16)


def _hy_tile_specs(L, tl):
    hb = tl // HALO
    nhb = L // HALO
    w = 3 * HY_WIDTH
    return [pl.BlockSpec((1, tl, w), lambda b, i: (b, i, C_HY // w)),
            pl.BlockSpec((1, HALO, w), lambda b, i: (b, jnp.maximum(i * hb - 1, 0), C_HY // w)),
            pl.BlockSpec((1, HALO, w), lambda b, i: (b, jnp.minimum((i + 1) * hb, nhb - 1), C_HY // w))]


def _hy_pre(u3, cw, cb, *, tl):
    B, L, _ = u3.shape
    nt = L // tl
    w = 3 * HY_WIDTH
    return pl.pallas_call(
        functools.partial(_hy_pre_kernel, nt=nt, tl=tl),
        grid=(B, nt),
        in_specs=_hy_tile_specs(L, tl) + [pl.BlockSpec((8, w), lambda b, i: (0, 0)),
                                          pl.BlockSpec((1, w), lambda b, i: (0, 0))],
        out_specs=[pl.BlockSpec((1, tl // 2, 2 * HY_WIDTH), lambda b, i: (b, i, 0)),
                   pl.BlockSpec((1, tl, HY_WIDTH), lambda b, i: (b, i, 0))],
        out_shape=[jax.ShapeDtypeStruct((B, L // 2, 2 * HY_WIDTH), BF16),
                   jax.ShapeDtypeStruct((B, L, HY_WIDTH), BF16)],
        scratch_shapes=[pltpu.VMEM((tl + 2 * HALO, w), BF16),
                        pltpu.VMEM((HY_WIDTH // LANES, tl, LANES), F32)],
        compiler_params=_cparams(("parallel", "parallel")),
        name="hy_pre",
    )(u3, u3, u3, cw, cb)


def _hy_fwd_kernel(a_ref, u_ref, f_ref, y_ref, *, hm):
    W = HY_WIDTH
    pq = jnp.dot(a_ref[...], u_ref[0], preferred_element_type=F32)
    pe, po = pq[0:hm, :W], pq[0:hm, W:]
    qe, qo = pq[hm:, :W], pq[hm:, W:]
    kpr, kpi, wr, wi, vr, vi = [f_ref[0, i] for i in range(6)]
    y_ref[0, 0:hm, :W] = (pe * kpr + qe * kpi + po * wr + qo * wi).astype(BF16)
    y_ref[0, hm:, :W] = (pe * kpi - qe * kpr + po * wi - qo * wr).astype(BF16)
    y_ref[0, 0:hm, W:] = (pe * vr + qe * vi + po * kpr + qo * kpi).astype(BF16)
    y_ref[0, hm:, W:] = (pe * vi - qe * vr + po * kpi - qo * kpr).astype(BF16)


def _hy_fwd(a_fwd, ueo, filt, *, hm):
    B, L2, W2 = ueo.shape
    L = 2 * L2
    tm = 2 * hm
    return pl.pallas_call(
        functools.partial(_hy_fwd_kernel, hm=hm),
        grid=(L // tm, B),
        in_specs=[pl.BlockSpec((tm, L2), lambda i, b: (i, 0)),
                  pl.BlockSpec((1, L2, W2), lambda i, b: (b, 0, 0)),
                  pl.BlockSpec((1, 6, hm, HY_WIDTH), lambda i, b: (i, 0, 0, 0))],
        out_specs=pl.BlockSpec((1, tm, W2), lambda i, b: (b, i, 0)),
        out_shape=jax.ShapeDtypeStruct((B, L, W2), BF16),
        compiler_params=_cparams(("parallel", "parallel")),
        name="hy_fwd",
    )(a_fwd, ueo, filt)


def _hy_inv_kernel(a_ref, y_ref, o_ref):
    o_ref[0] = jnp.dot(a_ref[...], y_ref[0], preferred_element_type=F32).astype(BF16)


def _hy_inv(a_inv, yspec, *, tm):
    B, L, W2 = yspec.shape
    L2 = L // 2
    return pl.pallas_call(
        _hy_inv_kernel,
        grid=(L2 // tm, B),
        in_specs=[pl.BlockSpec((tm, L), lambda i, b: (i, 0)),
                  pl.BlockSpec((1, L, W2), lambda i, b: (b, 0, 0))],
        out_specs=pl.BlockSpec((1, tm, W2), lambda i, b: (b, i, 0)),
        out_shape=jax.ShapeDtypeStruct((B, L2, W2), BF16),
        compiler_params=_cparams(("parallel", "parallel")),
        name="hy_inv",
    )(a_inv, yspec)


def _pad_rows(a, rows):
    return jnp.concatenate([a, jnp.zeros((rows - a.shape[0],) + a.shape[1:], a.dtype)], axis=0)


def _pad_cols(a, cols):
    return jnp.concatenate([a, jnp.zeros(a.shape[:-1] + (cols - a.shape[-1],), a.dtype)], axis=-1)


def _tile(n, pref):
    t = min(pref, n)
    assert n % t == 0
    return t


def _layer(x3, p, tables, *, final, final_g):
    B, L, _ = x3.shape
    T = B * L
    x2 = x3.reshape(T, D_MODEL)
    u, dt = _inproj(x2, p["norm_g"], p["w_in"], tm=_tile(T, 512), tn=512)
    u3 = u.reshape(B, L, N_IN)
    dt3 = dt.reshape(B, L, DT_PAD)

    ys = _ssd(u3, dt3, p["ssd_cw"], p["ssd_cb"], p["ssd_dtb"], p["ssd_alog"], p["rexp_f"],
              p["rexp_b"], p["ssd_dskip"], p["ssd_ng"], qb=_tile(L // CHUNK, 8))

    ya = _attention(u3, p["att_bias"], p["att_sink"], p["att_ng"], qb=_tile(L // ATT_BLOCK, 8))

    a_fwd, a_inv, filt, hm = tables
    tl = _tile(L, 1024)
    ueo, xa = _hy_pre(u3, p["hy_cw"], p["hy_cb"], tl=tl)
    yspec = _hy_fwd(a_fwd, ueo, filt, hm=hm)
    conv = _hy_inv(a_inv, yspec, tm=_tile(L // 2, 512))

    out = _outproj(x2, ys.reshape(T, SSD_WIDTH), ya.reshape(T, ATT_WIDTH),
                   conv.reshape(T // 2, 2 * HY_WIDTH), ueo.reshape(T // 2, 2 * HY_WIDTH),
                   xa.reshape(T, HY_WIDTH), u, p["hy_d"], p["hy_ng"], p["w_out"], final_g,
                   tm=_tile(T, 512), final=final)
    return out.reshape(B, L, D_MODEL)


def kernel(x_prompt, x_sample, rel_bias, norm_g, w_in, ssd_conv_w, ssd_conv_b, ssd_dt_bias, ssd_a_log, ssd_d, ssd_norm_g, att_sink, att_norm_g, hy_conv_w, hy_conv_b, hy_w1, hy_b1, hy_w2, hy_b2, hy_w3, hy_b3, hy_w4, hy_freq, hy_d, hy_norm_g, w_out, final_norm_g):
    depth = w_in.shape[0]
    att_perm = _att_col_perm()
    perm, col_scale = _in_perm()
    scale_old = np.ones((IN_COLS,), np.float32)
    scale_old[perm[:C_DT + 2 * SSD_HEADS]] = col_scale[:C_DT + 2 * SSD_HEADS]
    w_in_b = (w_in * jnp.asarray(scale_old)).astype(BF16)
    att_bias = _attn_bias(rel_bias)
    rexp_f, rexp_b = _ssd_expand_matrix(False), _ssd_expand_matrix(True)
    final_g = final_norm_g.astype(F32)[None, :]

    layers = []
    for i in range(depth):
        layers.append(dict(
            norm_g=norm_g[i].astype(F32)[None, :],
            w_in=_prep_w_in(w_in_b, i),
            ssd_cw=_pad_rows(ssd_conv_w[i].astype(F32), 8),
            ssd_cb=ssd_conv_b[i].astype(F32)[None, :],
            ssd_dtb=_pad_cols(ssd_dt_bias[i].astype(F32).reshape(1, 2 * SSD_HEADS), DT_PAD),
            ssd_alog=_pad_cols(ssd_a_log[i].astype(F32).reshape(1, 2 * SSD_HEADS), DT_PAD),
            ssd_dskip=jnp.repeat(ssd_d[i].astype(F32), SSD_HEAD_DIM)[None, :],
            ssd_ng=ssd_norm_g[i].astype(F32)[None, :],
            rexp_f=rexp_f, rexp_b=rexp_b,
            att_bias=att_bias,
            att_sink=jnp.broadcast_to(att_sink[i].astype(F32)[:, None] * LOG2E, (ATT_HEADS, LANES)),
            att_ng=att_norm_g[i].astype(F32)[att_perm][None, :],
            hy_cw=_pad_rows(hy_conv_w[i].astype(F32), 8),
            hy_cb=hy_conv_b[i].astype(F32)[None, :],
            hy_d=hy_d[i].astype(F32)[None, :],
            hy_ng=hy_norm_g[i].astype(F32)[None, :],
            w_out=_prep_w_out(w_out, i),
        ))

    def trunk(x):
        L = x.shape[1]
        hm = min(512, L // 4)
        a_fwd, a_inv = _dft_tables(L, hm)
        for i in range(depth):
            xf = _hy_filter(L, hy_w1[i], hy_b1[i].astype(F32), hy_w2[i].astype(F32),
                            hy_b2[i].astype(F32), hy_w3[i].astype(F32), hy_b3[i].astype(F32),
                            hy_w4[i].astype(F32), hy_freq[i].astype(F32))
            filt = _hy_kspec(a_fwd, xf, hm=hm)
            x = _layer(x, layers[i], (a_fwd, a_inv, filt, hm), final=(i == depth - 1),
                       final_g=final_g)
        return x

    return (trunk(x_prompt), trunk(x_sample))
```

```python
import functools
import math

import jax
import jax.numpy as jnp
import numpy as np
from jax import lax
from jax.experimental import pallas as pl
from jax.experimental.pallas import tpu as pltpu

F32 = jnp.float32
BF16 = jnp.bfloat16
HIGHEST = lax.Precision.HIGHEST

D_MODEL = 2048
SSD_WIDTH = 1024
ATT_WIDTH = 512
HY_WIDTH = 512
SSD_HEAD_DIM = 64
SSD_HEADS = 16
SSD_GROUPS = 2
SSD_STATE = 128
SSD_CONV = 5
CHUNK = 128
SSD_XBC = SSD_WIDTH + 2 * SSD_GROUPS * SSD_STATE
ATT_HEAD_DIM = 64
ATT_HEADS = 8
ATT_KV_HEADS = 2
ATT_REP = ATT_HEADS // ATT_KV_HEADS
ATT_WINDOW = 128
ATT_BLOCK = 128
REL_BUCKETS = 32
REL_MAX_DIST = 128
HY_CONV = 3
HY_EMB_BANDS = 16
HY_FF = 64
HY_FAST_DECAY = 0.3
HY_SLOW_DECAY = 1.5
HY_DECAY_TARGET = 1e-2
EPS = 1e-6
NEG_BIG = -1e30

LANES = 128
BF16_SUBLANES = 16
VMEM_LIMIT = 56 * 1024 * 1024

C_XBC = 0
C_HY = 1536
C_Z = 3072
C_Q = 4096
C_GATT = 4608
C_GHY = 5120
C_K = 5632
C_V = 5760
C_DT = 5888
DT_PAD = 128
N_IN = C_DT + DT_PAD

_OLD_SIZES = [SSD_WIDTH, SSD_XBC, 2 * SSD_HEADS, ATT_WIDTH, ATT_KV_HEADS * ATT_HEAD_DIM,
              ATT_KV_HEADS * ATT_HEAD_DIM, ATT_WIDTH, 3 * HY_WIDTH, HY_WIDTH]
_OLD_OFF = np.concatenate([[0], np.cumsum(_OLD_SIZES)])
IN_COLS = int(_OLD_OFF[-1])


LOG2E = math.log2(math.e)
Q_SCALE = ATT_HEAD_DIM ** -0.5 * LOG2E


def _att_col_perm():
    order = [h for j in range(ATT_REP) for h in (j, ATT_REP + j)]
    return np.concatenate([np.arange(h * ATT_HEAD_DIM, (h + 1) * ATT_HEAD_DIM) for h in order])


def _in_perm():
    perm = np.full((N_IN,), IN_COLS, np.int32)
    scale = np.ones((N_IN,), np.float32)
    o = {n: int(_OLD_OFF[i]) for i, n in enumerate(
        ["z", "xbc", "dt", "q", "k", "v", "gatt", "hy", "ghy"])}
    def put(new, old, width):
        perm[new:new + width] = np.arange(old, old + width)
    put(C_XBC, o["xbc"], SSD_XBC)
    put(C_HY, o["hy"], 3 * HY_WIDTH)
    put(C_Z, o["z"], SSD_WIDTH)
    perm[C_Q:C_Q + ATT_WIDTH] = o["q"] + _att_col_perm()
    scale[C_Q:C_Q + ATT_WIDTH] = Q_SCALE
    perm[C_GATT:C_GATT + ATT_WIDTH] = o["gatt"] + _att_col_perm()
    put(C_GHY, o["ghy"], HY_WIDTH)
    put(C_K, o["k"], 128)
    put(C_V, o["v"], 128)
    put(C_DT, o["dt"], 2 * SSD_HEADS)
    return perm, scale


def _col_runs(perm):
    cuts = np.flatnonzero(np.diff(perm) != 1) + 1
    starts = np.concatenate([[0], cuts])
    ends = np.concatenate([cuts, [len(perm)]])
    return [(int(s), int(perm[s]), int(e - s)) for s, e in zip(starts, ends)]


def _prep_w_in_kernel(w_ref, o_ref, *, runs, n_real):
    for new0, old0, n in runs:
        piece = w_ref[:, old0:old0 + n]
        if C_Q <= new0 < C_Q + ATT_WIDTH:
            piece = piece * Q_SCALE
        o_ref[:, new0:new0 + n] = piece.astype(BF16)
    o_ref[:, n_real:] = jnp.zeros((o_ref.shape[0], N_IN - n_real), BF16)


def _prep_w_in(w_in, layer):
    perm, _ = _in_perm()
    n_real = C_DT + 2 * SSD_HEADS
    tr = 256
    return pl.pallas_call(
        functools.partial(_prep_w_in_kernel, runs=_col_runs(perm[:n_real]), n_real=n_real),
        grid=(D_MODEL // tr,),
        in_specs=[pl.BlockSpec((None, tr, IN_COLS), lambda r: (layer, r, 0))],
        out_specs=pl.BlockSpec((tr, N_IN), lambda r: (r, 0)),
        out_shape=jax.ShapeDtypeStruct((D_MODEL, N_IN), BF16),
        compiler_params=_cparams(("parallel",)),
        name="prep_w_in",
    )(w_in)


def _prep_w_out_kernel(w_ref, o_ref, *, runs):
    for new0, old0, n in runs:
        o_ref[new0:new0 + n, :] = w_ref[old0:old0 + n, :].astype(BF16)


def _prep_w_out(w_out, layer):
    rows = np.concatenate([np.arange(SSD_WIDTH), SSD_WIDTH + _att_col_perm(),
                           np.arange(SSD_WIDTH + ATT_WIDTH, D_MODEL)])
    tc = 512
    return pl.pallas_call(
        functools.partial(_prep_w_out_kernel, runs=_col_runs(rows)),
        grid=(D_MODEL // tc,),
        in_specs=[pl.BlockSpec((None, D_MODEL, tc), lambda c: (layer, 0, c))],
        out_specs=pl.BlockSpec((D_MODEL, tc), lambda c: (0, c)),
        out_shape=jax.ShapeDtypeStruct((D_MODEL, D_MODEL), BF16),
        compiler_params=_cparams(("parallel",)),
        name="prep_w_out",
    )(w_out)


def _cparams(sem):
    return pltpu.CompilerParams(dimension_semantics=sem, vmem_limit_bytes=VMEM_LIMIT)


def _silu(x):
    return x * (1.0 / (1.0 + jnp.exp(-x)))


def _softplus(x):
    return jnp.maximum(x, 0.0) + jnp.log1p(jnp.exp(-jnp.abs(x)))


def _inproj_kernel(x_ref, g_ref, w_ref, u_ref, dt_ref, h_ref, *, rows, tn):
    for r in range(x_ref.shape[0] // rows):
        sl = slice(r * rows, (r + 1) * rows)
        x = x_ref[sl, :]
        ms = jnp.mean(x * x, axis=-1, keepdims=True)
        h_ref[sl, :] = (x * lax.rsqrt(ms + EPS) * g_ref[...]).astype(BF16)
        h = h_ref[sl, :]
        for c0 in range(0, N_IN, tn):
            c1 = min(c0 + tn, N_IN)
            acc = jnp.dot(h, w_ref[:, c0:c1], preferred_element_type=F32)
            u_ref[sl, c0:c1] = acc.astype(BF16)
            if c0 <= C_DT < c1:
                dt_ref[sl, :] = acc[:, C_DT - c0:C_DT - c0 + DT_PAD]


def _inproj(x2, g, w, *, tm, tn):
    T = x2.shape[0]
    assert T % tm == 0
    return pl.pallas_call(
        functools.partial(_inproj_kernel, rows=min(256, tm), tn=tn),
        grid=(T // tm,),
        in_specs=[pl.BlockSpec((tm, D_MODEL), lambda i: (i, 0)),
                  pl.BlockSpec((1, D_MODEL), lambda i: (0, 0)),
                  pl.BlockSpec((D_MODEL, N_IN), lambda i: (0, 0), pipeline_mode=pl.Buffered(1))],
        out_specs=[pl.BlockSpec((tm, N_IN), lambda i: (i, 0)),
                   pl.BlockSpec((tm, DT_PAD), lambda i: (i, 0))],
        out_shape=[jax.ShapeDtypeStruct((T, N_IN), BF16),
                   jax.ShapeDtypeStruct((T, DT_PAD), F32)],
        scratch_shapes=[pltpu.VMEM((tm, D_MODEL), BF16)],
        compiler_params=_cparams(("parallel",)),
        name="inproj",
    )(x2, g, w)


def _outproj_kernel(x_ref, ys_ref, ya_ref, c_ref, ueo_ref, xa_ref, g_ref, d_ref, hng_ref, w_ref, fg_ref,
                    o_ref, s_ref, *, rows, final):
    nblk = HY_WIDTH // LANES

    def interleave(ref, hs):
        for p in range(2):
            for c in range(nblk):
                s_ref[c, pl.ds(p, rows // 2, stride=2), :] = ref[
                    hs, (p * nblk + c) * LANES:(p * nblk + c + 1) * LANES].astype(F32)
        return jnp.concatenate([s_ref[c] for c in range(nblk)], axis=1)

    for r in range(x_ref.shape[0] // rows):
        sl = slice(r * rows, (r + 1) * rows)
        hs = slice(r * rows // 2, (r + 1) * rows // 2)
        conv = interleave(c_ref, hs)
        u = interleave(ueo_ref, hs)
        xa = xa_ref[sl, :].astype(F32)
        y = xa * (conv + u * d_ref[...])
        y = y * _silu(g_ref[sl, :].astype(F32))
        ms = jnp.mean(y * y, axis=-1, keepdims=True)
        yh = (y * lax.rsqrt(ms + EPS) * hng_ref[...]).astype(BF16)

        acc = x_ref[sl, :]
        acc = acc + jnp.dot(ys_ref[sl, :], w_ref[0:SSD_WIDTH, :], preferred_element_type=F32)
        acc = acc + jnp.dot(ya_ref[sl, :], w_ref[SSD_WIDTH:SSD_WIDTH + ATT_WIDTH, :],
                            preferred_element_type=F32)
        acc = acc + jnp.dot(yh, w_ref[SSD_WIDTH + ATT_WIDTH:, :], preferred_element_type=F32)
        if final:
            ms = jnp.mean(acc * acc, axis=-1, keepdims=True)
            acc = acc * lax.rsqrt(ms + EPS) * fg_ref[...]
        o_ref[sl, :] = acc


def _outproj(x2, ys, ya, conv2, ueo2, xa2, u2, d, hng, w, fg, *, tm, final):
    T = x2.shape[0]
    assert T % tm == 0
    rows = min(256, tm)
    return pl.pallas_call(
        functools.partial(_outproj_kernel, rows=rows, final=final),
        grid=(T // tm,),
        in_specs=[pl.BlockSpec((tm, D_MODEL), lambda i: (i, 0)),
                  pl.BlockSpec((tm, SSD_WIDTH), lambda i: (i, 0)),
                  pl.BlockSpec((tm, ATT_WIDTH), lambda i: (i, 0)),
                  pl.BlockSpec((tm // 2, 2 * HY_WIDTH), lambda i: (i, 0)),
                  pl.BlockSpec((tm // 2, 2 * HY_WIDTH), lambda i: (i, 0)),
                  pl.BlockSpec((tm, HY_WIDTH), lambda i: (i, 0)),
                  pl.BlockSpec((tm, HY_WIDTH), lambda i: (i, C_GHY // HY_WIDTH)),
                  pl.BlockSpec((1, HY_WIDTH), lambda i: (0, 0)),
                  pl.BlockSpec((1, HY_WIDTH), lambda i: (0, 0)),
                  pl.BlockSpec((D_MODEL, D_MODEL), lambda i: (0, 0)),
                  pl.BlockSpec((1, D_MODEL), lambda i: (0, 0))],
        out_specs=pl.BlockSpec((tm, D_MODEL), lambda i: (i, 0)),
        out_shape=jax.ShapeDtypeStruct((T, D_MODEL), F32),
        scratch_shapes=[pltpu.VMEM((HY_WIDTH // LANES, rows, LANES), F32)],
        compiler_params=_cparams(("parallel",)),
        name="outproj",
    )(x2, ys, ya, conv2, ueo2, xa2, u2, d, hng, w, fg)


def _t5_buckets(rel):
    nb = REL_BUCKETS // 2
    max_exact = nb // 2
    ret = (rel > 0).astype(np.int32) * nb
    n = np.abs(rel)
    large = max_exact + (np.log(np.maximum(n, 1) / max_exact) / math.log(REL_MAX_DIST / max_exact)
                         * (nb - max_exact)).astype(np.int32)
    large = np.minimum(large, nb - 1)
    return ret + np.where(n < max_exact, n, large)


def _attn_bias(rel_bias):
    qi = np.arange(ATT_BLOCK)[:, None]
    kj = np.arange(3 * ATT_BLOCK)[None, :]
    rel = kj - ATT_BLOCK - qi
    onehot = (_t5_buckets(rel)[None] == np.arange(REL_BUCKETS)[:, None, None]).astype(np.float32)
    bias = jnp.einsum("bqk,bh->hqk", jnp.asarray(onehot), rel_bias.astype(F32),
                      precision=HIGHEST) * LOG2E
    window = np.abs(rel) <= ATT_WINDOW
    variants = []
    for last in (False, True):
        for first in (False, True):
            ok = window & ~(first & (kj < ATT_BLOCK)) & ~(last & (kj >= 2 * ATT_BLOCK))
            variants.append(jnp.where(ok[None], bias, NEG_BIG))
    return jnp.transpose(jnp.stack(variants), (0, 1, 3, 2))


def _attn_kernel(q_ref, kp_ref, kc_ref, kn_ref, vp_ref, vc_ref, vn_ref, g_ref, bias_ref, sink_ref,
                 ng_ref, o_ref, klo_ref, khi_ref, vt_ref, *, nt, qb):
    n = pl.program_id(1)
    lo = lax.broadcasted_iota(jnp.int32, (1, LANES), 1) < ATT_HEAD_DIM
    zero = jnp.zeros((), BF16)
    kext = jnp.concatenate([kp_ref[0], kc_ref[0], kn_ref[0]], axis=0)
    klo_ref[...] = jnp.where(lo, kext, zero)
    khi_ref[...] = jnp.where(lo, zero, kext)
    for t, ref, cnt in ((0, vp_ref, 1), (1, vc_ref, qb), (qb + 1, vn_ref, 1)):
        for i in range(cnt):
            blk = ref[0, i * ATT_BLOCK:(i + 1) * ATT_BLOCK, :]
            vt_ref[t + i] = blk.astype(F32).T.astype(BF16)
    row_lo = lax.broadcasted_iota(jnp.int32, (LANES, ATT_BLOCK), 0) < ATT_HEAD_DIM
    nk = 3 * ATT_BLOCK

    def body(i, carry):
        r0 = pl.multiple_of(i * ATT_BLOCK, ATT_BLOCK)
        q = q_ref[0, pl.ds(r0, ATT_BLOCK), :]
        kst = jnp.concatenate([klo_ref[pl.ds(r0, nk), :], khi_ref[pl.ds(r0, nk), :]], axis=0)
        vt = jnp.concatenate([vt_ref[i], vt_ref[i + 1], vt_ref[i + 2]], axis=1)
        first = jnp.logical_and(n == 0, i == 0)
        last = jnp.logical_and(n == nt - 1, i == qb - 1)
        variant = first.astype(jnp.int32) + 2 * last.astype(jnp.int32)
        outs = []
        for j in range(ATT_REP):
            qp = q[:, j * LANES:(j + 1) * LANES]
            st = lax.dot_general(kst, qp, (((1,), (1,)), ((), ())), preferred_element_type=F32)
            halves = []
            for e, h in enumerate((j, ATT_REP + j)):
                s = st[e * nk:(e + 1) * nk] + bias_ref[variant, h]
                sk = sink_ref[h:h + 1, :]
                m = jnp.maximum(jnp.max(s, axis=0, keepdims=True), sk)
                p = jnp.exp2(s - m)
                den = jnp.sum(p, axis=0, keepdims=True) + jnp.exp2(sk - m)
                ot = jnp.dot(vt, p.astype(BF16), preferred_element_type=F32)
                halves.append(ot * (1.0 / den))
            outs.append(jnp.where(row_lo, halves[0], halves[1]).T)
        o = jnp.concatenate(outs, axis=-1)
        y = o * _silu(g_ref[0, pl.ds(r0, ATT_BLOCK), :].astype(F32))
        ms = jnp.mean(y * y, axis=-1, keepdims=True)
        o_ref[0, pl.ds(r0, ATT_BLOCK), :] = (y * lax.rsqrt(ms + EPS) * ng_ref[...]).astype(BF16)
        return carry

    lax.fori_loop(0, qb, body, 0, unroll=True)


def _attention(u3, bias, sink, ng, *, qb):
    B, L, _ = u3.shape
    tq = qb * ATT_BLOCK
    assert L % tq == 0
    nt = L // tq
    nb = L // ATT_BLOCK
    kcol, vcol = C_K // 128, C_V // 128
    def kv_specs(colblk):
        return [pl.BlockSpec((1, ATT_BLOCK, 128), lambda b, n: (b, jnp.maximum(n * qb - 1, 0), colblk)),
                pl.BlockSpec((1, tq, 128), lambda b, n: (b, n, colblk)),
                pl.BlockSpec((1, ATT_BLOCK, 128),
                             lambda b, n: (b, jnp.minimum((n + 1) * qb, nb - 1), colblk))]
    return pl.pallas_call(
        functools.partial(_attn_kernel, nt=nt, qb=qb),
        grid=(B, nt),
        in_specs=[pl.BlockSpec((1, tq, ATT_WIDTH), lambda b, n: (b, n, C_Q // ATT_WIDTH))]
                 + kv_specs(kcol) + kv_specs(vcol)
                 + [pl.BlockSpec((1, tq, ATT_WIDTH), lambda b, n: (b, n, C_GATT // ATT_WIDTH)),
                    pl.BlockSpec((4, ATT_HEADS, 3 * ATT_BLOCK, ATT_BLOCK), lambda b, n: (0, 0, 0, 0)),
                    pl.BlockSpec((ATT_HEADS, LANES), lambda b, n: (0, 0)),
                    pl.BlockSpec((1, ATT_WIDTH), lambda b, n: (0, 0))],
        out_specs=pl.BlockSpec((1, tq, ATT_WIDTH), lambda b, n: (b, n, 0)),
        out_shape=jax.ShapeDtypeStruct((B, L, ATT_WIDTH), BF16),
        scratch_shapes=[pltpu.VMEM(((qb + 2) * ATT_BLOCK, LANES), BF16),
                        pltpu.VMEM(((qb + 2) * ATT_BLOCK, LANES), BF16),
                        pltpu.VMEM((qb + 2, LANES, ATT_BLOCK), BF16)],
        compiler_params=_cparams(("parallel", "parallel")),
        name="attn",
    )(u3, u3, u3, u3, u3, u3, u3, u3, bias, sink, ng)


HALO = BF16_SUBLANES


CONV_BLK = 128


def _load_padded(xpad_ref, xm_ref, xp_ref, xn_ref, has_prev, has_next, rows):
    zero = jnp.zeros((), BF16)
    xpad_ref[pl.ds(0, HALO), :] = jnp.where(has_prev, xp_ref[0], zero)
    xpad_ref[pl.ds(HALO, rows), :] = xm_ref[0]
    xpad_ref[pl.ds(HALO + rows, HALO), :] = jnp.where(has_next, xn_ref[0], zero)


def _shift_matrix(width):
    offs = [k - width // 2 for k in range(width) if k != width // 2]
    r = lax.broadcasted_iota(jnp.int32, (CONV_BLK, CONV_BLK + 2 * HALO), 0)
    c = lax.broadcasted_iota(jnp.int32, (CONV_BLK, CONV_BLK + 2 * HALO), 1)
    return jnp.concatenate([(c == r + HALO + d).astype(BF16) for d in offs], axis=0)


def _dwconv_block(xpad_ref, shifts, w_ref, b_ref, j, width, c0, c1):
    win = xpad_ref[j * CONV_BLK:(j + 1) * CONV_BLK + 2 * HALO, c0:c1]
    moved = jnp.dot(shifts, win, preferred_element_type=F32)
    acc = win[HALO:HALO + CONV_BLK].astype(F32) * w_ref[width // 2:width // 2 + 1, c0:c1]
    i = 0
    for k in range(width):
        if k == width // 2:
            continue
        acc = acc + moved[i * CONV_BLK:(i + 1) * CONV_BLK] * w_ref[k:k + 1, c0:c1]
        i += 1
    return acc + b_ref[:, c0:c1]


def _dwconv(xpad_ref, w_ref, b_ref, rows, width, c0, c1):
    shifts = _shift_matrix(width)
    return jnp.concatenate([_dwconv_block(xpad_ref, shifts, w_ref, b_ref, j, width, c0, c1)
                            for j in range(rows // CONV_BLK)], axis=0)


SSD_HPG = SSD_HEADS // SSD_GROUPS
SSD_GW = SSD_HPG * SSD_HEAD_DIM
SSD_BC = SSD_GROUPS * SSD_STATE


def _ssd_decay(dt_raw, dtb_ref, alog_ref):
    dt = _softplus(dt_raw + dtb_ref[...])
    dta = dt * (-jnp.exp(alog_ref[...]))
    row = lax.broadcasted_iota(jnp.int32, (CHUNK, CHUNK), 0)
    col = lax.broadcasted_iota(jnp.int32, (CHUNK, CHUNK), 1)
    tril = (row >= col).astype(BF16)
    hi = dta.astype(BF16)
    r1 = dta - hi.astype(F32)
    mid = r1.astype(BF16)
    lo = (r1 - mid.astype(F32)).astype(BF16)
    pre = (jnp.dot(tril, hi, preferred_element_type=F32) + jnp.dot(tril, mid, preferred_element_type=F32)
           + jnp.dot(tril, lo, preferred_element_type=F32))
    tot = pre[CHUNK - 1:CHUNK, :]
    is_bwd = lax.broadcasted_iota(jnp.int32, (1, LANES), 1) >= SSD_HEADS
    acs = jnp.where(is_bwd, tot - pre + dta, pre)
    return dt, acs, tot


def _ssd_bwd_kernel(xm_ref, xp_ref, xn_ref, dt_ref, cw_ref, cb_ref, dtb_ref, alog_ref, rexp_ref,
                    xact_ref, prev_ref, xpad_ref, xf_ref, st_ref, *, nt, qb):
    n = pl.program_id(1)
    tile = nt - 1 - n
    rows = qb * CHUNK

    @pl.when(n == 0)
    def _():
        st_ref[...] = jnp.zeros(st_ref.shape, F32)

    _load_padded(xpad_ref, xm_ref, xp_ref, xn_ref, tile > 0, tile < nt - 1, rows)
    xact = _silu(_dwconv(xpad_ref, cw_ref, cb_ref, rows, SSD_CONV, 0, SSD_XBC))
    xf_ref[...] = xact
    xact_ref[0] = xact.astype(BF16)

    def body(j, carry):
        i = qb - 1 - j
        r0 = pl.multiple_of(i * CHUNK, CHUNK)
        xs = xf_ref[pl.ds(r0, CHUNK), 0:SSD_WIDTH]
        bm = xf_ref[pl.ds(r0, CHUNK), SSD_WIDTH:SSD_WIDTH + SSD_BC].astype(BF16)
        dt, acs, tot = _ssd_decay(dt_ref[0, pl.ds(r0, CHUNK), :], dtb_ref, alog_ref)
        e_in = jnp.concatenate([jnp.exp(tot - acs) * dt, jnp.broadcast_to(jnp.exp(tot), (8, LANES))],
                               axis=0).astype(BF16)
        e_out = jnp.dot(e_in, rexp_ref[...], preferred_element_type=F32)
        xd = (xs * e_out[0:CHUNK]).astype(BF16)
        cdec = e_out[CHUNK:CHUNK + 1]
        for g in range(SSD_GROUPS):
            prev = st_ref[g]
            prev_ref[0, i, g] = prev.astype(BF16)
            s_new = lax.dot_general(bm[:, g * SSD_STATE:(g + 1) * SSD_STATE],
                                    xd[:, g * SSD_GW:(g + 1) * SSD_GW], (((0,), (0,)), ((), ())),
                                    preferred_element_type=F32)
            st_ref[g] = prev * cdec[:, g * SSD_GW:(g + 1) * SSD_GW] + s_new
        return carry

    lax.fori_loop(0, qb, body, 0, unroll=True)


def _ssd_fwd_kernel(xa_ref, dt_ref, prev_ref, z_ref, dtb_ref, alog_ref, rexpf_ref, rexpb_ref,
                    dsk_ref, ng_ref, o_ref, st_ref, *, qb):
    n = pl.program_id(1)

    @pl.when(n == 0)
    def _():
        st_ref[...] = jnp.zeros(st_ref.shape, F32)

    row = lax.broadcasted_iota(jnp.int32, (CHUNK, CHUNK), 0)
    col = lax.broadcasted_iota(jnp.int32, (CHUNK, CHUNK), 1)
    fwd_part = row > col
    diag = row == col
    lane_lo = lax.broadcasted_iota(jnp.int32, (1, LANES), 1) < SSD_HEAD_DIM
    zero_b = jnp.zeros((), BF16)

    def body(i, carry):
        r0 = pl.multiple_of(i * CHUNK, CHUNK)
        xs_b = xa_ref[0, pl.ds(r0, CHUNK), 0:SSD_WIDTH]
        xs = xs_b.astype(F32)
        bm = xa_ref[0, pl.ds(r0, CHUNK), SSD_WIDTH:SSD_WIDTH + SSD_BC]
        cm = xa_ref[0, pl.ds(r0, CHUNK), SSD_WIDTH + SSD_BC:SSD_XBC]
        dt, acs, tot = _ssd_decay(dt_ref[0, pl.ds(r0, CHUNK), :], dtb_ref, alog_ref)
        acs2 = acs * LOG2E
        rt = (acs2 - jnp.log2(dt)).T
        dsum_t = jnp.log2(dt + pltpu.roll(dt, LANES - SSD_HEADS, axis=1)).T
        eacs = jnp.exp(acs)
        ef_in = jnp.concatenate([jnp.exp(tot - acs) * dt, eacs,
                                 jnp.broadcast_to(jnp.exp(tot), (8, LANES))], axis=0).astype(BF16)
        ef = jnp.dot(ef_in, rexpf_ref[...], preferred_element_type=F32)
        eb = jnp.dot(eacs.astype(BF16), rexpb_ref[...], preferred_element_type=F32)
        xd = (xs * ef[0:CHUNK]).astype(BF16)
        eacs_f = ef[CHUNK:2 * CHUNK]
        cdec = ef[2 * CHUNK:2 * CHUNK + 1]
        ys = []
        for g in range(SSD_GROUPS):
            bg = bm[:, g * SSD_STATE:(g + 1) * SSD_STATE]
            cg = cm[:, g * SSD_STATE:(g + 1) * SSD_STATE]
            gs = slice(g * SSD_GW, (g + 1) * SSD_GW)
            cb = lax.dot_general(cg, bg, (((1,), (1,)), ((), ())), preferred_element_type=F32)
            prev = st_ref[g]
            y_off = (jnp.dot(cg, prev.astype(BF16), preferred_element_type=F32) * eacs_f[:, gs]
                     + jnp.dot(cg, prev_ref[0, i, g], preferred_element_type=F32) * eb[:, gs])
            s_new = lax.dot_general(bg, xd[:, gs], (((0,), (0,)), ((), ())),
                                    preferred_element_type=F32)
            st_ref[g] = prev * cdec[:, gs] + s_new
            for pr in range(SSD_HPG // 2):
                c0 = g * SSD_GW + pr * LANES
                xpair = xs_b[:, c0:c0 + LANES]
                xbd = jnp.concatenate([jnp.where(lane_lo, xpair, zero_b),
                                       jnp.where(lane_lo, zero_b, xpair)], axis=0)
                mats = []
                for e in range(2):
                    h = g * SSD_HPG + pr * 2 + e
                    hb = SSD_HEADS + h
                    sel = jnp.where(fwd_part, acs2[:, h:h + 1] - rt[h:h + 1, :],
                                    acs2[:, hb:hb + 1] - rt[hb:hb + 1, :])
                    sel = jnp.where(diag, dsum_t[h:h + 1, :], sel)
                    mats.append((cb * jnp.exp2(sel)).astype(BF16))
                yd = jnp.dot(jnp.concatenate(mats, axis=1), xbd, preferred_element_type=F32)
                ys.append(yd + y_off[:, pr * LANES:(pr + 1) * LANES])
        y = jnp.concatenate(ys, axis=-1) + xs * dsk_ref[...]
        y = y * _silu(z_ref[0, pl.ds(r0, CHUNK), :].astype(F32))
        ms = jnp.mean(y * y, axis=-1, keepdims=True)
        o_ref[0, pl.ds(r0, CHUNK), :] = (y * lax.rsqrt(ms + EPS) * ng_ref[...]).astype(BF16)
        return carry

    lax.fori_loop(0, qb, body, 0, unroll=True)


def _ssd(u3, dt3, cw, cb, dtb, alog, rexp_f, rexp_b, dsk, ng, *, qb):
    B, L, _ = u3.shape
    rows = qb * CHUNK
    assert L % rows == 0
    nt = L // rows
    nc = L // CHUNK
    hb = rows // HALO
    nhb = L // HALO
    state = pltpu.VMEM((SSD_GROUPS, SSD_STATE, SSD_GW), F32)
    const = lambda shape: pl.BlockSpec(shape, lambda b, n: (0,) * len(shape))
    rt = lambda n: nt - 1 - n

    xact, prevb = pl.pallas_call(
        functools.partial(_ssd_bwd_kernel, nt=nt, qb=qb),
        grid=(B, nt),
        in_specs=[
            pl.BlockSpec((1, rows, SSD_XBC), lambda b, n: (b, rt(n), C_XBC // SSD_XBC)),
            pl.BlockSpec((1, HALO, SSD_XBC),
                         lambda b, n: (b, jnp.maximum(rt(n) * hb - 1, 0), C_XBC // SSD_XBC)),
            pl.BlockSpec((1, HALO, SSD_XBC),
                         lambda b, n: (b, jnp.minimum((rt(n) + 1) * hb, nhb - 1), C_XBC // SSD_XBC)),
            pl.BlockSpec((1, rows, DT_PAD), lambda b, n: (b, rt(n), 0)),
            const((8, SSD_XBC)), const((1, SSD_XBC)), const((1, DT_PAD)), const((1, DT_PAD)),
            const((LANES, SSD_WIDTH)),
        ],
        out_specs=[pl.BlockSpec((1, rows, SSD_XBC), lambda b, n: (b, rt(n), 0)),
                   pl.BlockSpec((1, qb, SSD_GROUPS, SSD_STATE, SSD_GW),
                                lambda b, n: (b, rt(n), 0, 0, 0))],
        out_shape=[jax.ShapeDtypeStruct((B, L, SSD_XBC), BF16),
                   jax.ShapeDtypeStruct((B, nc, SSD_GROUPS, SSD_STATE, SSD_GW), BF16)],
        scratch_shapes=[pltpu.VMEM((rows + 2 * HALO, SSD_XBC), BF16),
                        pltpu.VMEM((rows, SSD_XBC), F32), state],
        compiler_params=_cparams(("parallel", "arbitrary")),
        name="ssd_bwd",
    )(u3, u3, u3, dt3, cw, cb, dtb, alog, rexp_b)

    return pl.pallas_call(
        functools.partial(_ssd_fwd_kernel, qb=qb),
        grid=(B, nt),
        in_specs=[
            pl.BlockSpec((1, rows, SSD_XBC), lambda b, n: (b, n, 0)),
            pl.BlockSpec((1, rows, DT_PAD), lambda b, n: (b, n, 0)),
            pl.BlockSpec((1, qb, SSD_GROUPS, SSD_STATE, SSD_GW), lambda b, n: (b, n, 0, 0, 0)),
            pl.BlockSpec((1, rows, SSD_WIDTH), lambda b, n: (b, n, C_Z // SSD_WIDTH)),
            const((1, DT_PAD)), const((1, DT_PAD)),
            const((LANES, SSD_WIDTH)), const((LANES, SSD_WIDTH)),
            const((1, SSD_WIDTH)), const((1, SSD_WIDTH)),
        ],
        out_specs=pl.BlockSpec((1, rows, SSD_WIDTH), lambda b, n: (b, n, 0)),
        out_shape=jax.ShapeDtypeStruct((B, L, SSD_WIDTH), BF16),
        scratch_shapes=[state],
        compiler_params=_cparams(("parallel", "arbitrary")),
        name="ssd_fwd",
    )(xact, dt3, prevb, u3, dtb, alog, rexp_f, rexp_b, dsk, ng)


def _ssd_expand_matrix(rev):
    m = np.zeros((LANES, SSD_WIDTH), np.float32)
    hoff = SSD_HEADS if rev else 0
    for h in range(SSD_HEADS):
        m[hoff + h, h * SSD_HEAD_DIM:(h + 1) * SSD_HEAD_DIM] = 1.0
    return jnp.asarray(m, BF16)


def _hy_filter_kernel(t_ref, bands_ref, w1t_ref, w1c_ref, w1s_ref, b1_ref, w2_ref, b2_ref, w3_ref,
                      b3_ref, w4_ref, fr_ref, absd_ref, x_ref, s_ref, *, L, tl):
    i = pl.program_id(0)
    pos = (lax.broadcasted_iota(jnp.int32, (tl, 1), 0) + i * tl).astype(F32)
    t = t_ref[...]
    pos_row = (lax.broadcasted_iota(jnp.int32, (1, tl), 1) + i * tl).astype(F32)
    ang_t = 2.0 * math.pi * pos_row * bands_ref[...] / L
    fr = fr_ref[...]
    dot = functools.partial(jnp.dot, preferred_element_type=F32, precision=HIGHEST)
    tdot = lambda a, b: lax.dot_general(a, b, (((0,), (0,)), ((), ())), preferred_element_type=F32,
                                        precision=HIGHEST)
    pre = t * w1t_ref[...] + tdot(jnp.cos(ang_t), w1c_ref[...]) + tdot(-jnp.sin(ang_t), w1s_ref[...])
    h = jnp.sin(fr * (pre + b1_ref[...]))
    h = jnp.sin(fr * (dot(h, w2_ref[...]) + b2_ref[...]))
    h = jnp.sin(fr * (dot(h, w3_ref[...]) + b3_ref[...]))
    w4 = w4_ref[...]
    h_hi, w_hi = h.astype(BF16), w4.astype(BF16)
    h_lo, w_lo = (h - h_hi.astype(F32)).astype(BF16), (w4 - w_hi.astype(F32)).astype(BF16)
    bdot = functools.partial(jnp.dot, preferred_element_type=F32)
    h = bdot(h_hi, w_hi) + bdot(h_hi, w_lo) + bdot(h_lo, w_hi)
    decay = jnp.exp(-t * absd_ref[...])
    hf = h[:, :HY_WIDTH] * decay
    hb = jnp.where(pos == 0.0, 0.0, h[:, HY_WIDTH:] * decay)
    _to_lane_blocks(s_ref, jnp.concatenate([hf + hb, hb - hf], axis=1))
    x_ref[...] = _split_even_odd(s_ref).astype(BF16)


def _hy_filter(L, w1, b1, w2, b2, w3, b3, w4, freq):
    tl = min(512, L)
    t = jnp.linspace(0.0, 1.0, L, dtype=F32)[:, None]
    bands = jnp.linspace(1e-4, HY_EMB_BANDS - 1, HY_EMB_BANDS, dtype=F32)[:, None]
    max_decay = math.log(HY_DECAY_TARGET) / HY_FAST_DECAY
    min_decay = math.log(HY_DECAY_TARGET) / HY_SLOW_DECAY
    absd = jnp.abs(jnp.linspace(min_decay, max_decay, HY_WIDTH, dtype=F32))[None, :]
    w1 = w1.astype(F32)
    full = lambda a: pl.BlockSpec(a.shape, lambda i: (0,) * a.ndim)
    args = [t, bands, w1[0:1], w1[1:1 + HY_EMB_BANDS], w1[1 + HY_EMB_BANDS:], b1[None], w2, b2[None],
            w3, b3[None], w4, freq[None], absd]
    in_specs = [pl.BlockSpec((tl, 1), lambda i: (i, 0))] + [full(a) for a in args[1:]]
    return pl.pallas_call(
        functools.partial(_hy_filter_kernel, L=L, tl=tl),
        grid=(L // tl,),
        in_specs=in_specs,
        out_specs=pl.BlockSpec((tl // 2, 4 * HY_WIDTH), lambda i: (i, 0)),
        out_shape=jax.ShapeDtypeStruct((L // 2, 4 * HY_WIDTH), BF16),
        scratch_shapes=[pltpu.VMEM((2 * HY_WIDTH // LANES, tl, LANES), F32)],
        compiler_params=_cparams(("parallel",)),
        name="hy_filter",
    )(*args)


def _dft_tables(L, hm):
    L2 = L // 2
    g = jnp.arange(L2, dtype=jnp.int32)
    s = jnp.arange(L2, dtype=jnp.int32)
    ph = ((2 * g + 1)[:, None] * s[None, :]) % (2 * L)
    ang = ph.astype(F32) * (math.pi / L)
    c = jnp.cos(ang).astype(BF16).reshape(L2 // hm, 1, hm, L2)
    sn = jnp.sin(ang).astype(BF16).reshape(L2 // hm, 1, hm, L2)
    a_fwd = jnp.concatenate([c, sn], axis=1).reshape(L, L2)
    a_inv = jnp.concatenate([c, -sn], axis=1).reshape(L, L2).T
    return a_fwd, a_inv


def _hy_kspec_kernel(a_ref, x_ref, wc_ref, ws_ref, f_ref, *, hm, scale):
    W = HY_WIDTH
    pq = jnp.dot(a_ref[...], x_ref[...], preferred_element_type=F32)
    ea, eb, oa, ob = [(pq[0:hm, i * W:(i + 1) * W], -pq[hm:, i * W:(i + 1) * W]) for i in range(4)]
    w = (wc_ref[...], -ws_ref[...])
    cmul = lambda x, y: (x[0] * y[0] - x[1] * y[1], x[0] * y[1] + x[1] * y[0])
    woa, wob = cmul(w, oa), cmul(w, ob)
    k1 = (ea[0] + woa[0], -(eb[1] + wob[1]))
    k2 = (ea[0] - woa[0], eb[1] - wob[1])
    kp = (k1[0] + k2[0], k1[1] - k2[1])
    km = (k1[0] - k2[0], k1[1] + k2[1])
    wkm = cmul(w, km)
    vkm = cmul((w[0], -w[1]), km)
    for i, part in enumerate((kp[0], kp[1], wkm[0], wkm[1], vkm[0], vkm[1])):
        f_ref[0, i] = part * scale


def _hy_kspec(a_fwd, xf2, *, hm):
    L, L2 = a_fwd.shape
    tm = 2 * hm
    nx = xf2.shape[1]
    theta = (2.0 * jnp.arange(L2, dtype=F32) + 1.0) * (math.pi / (2 * L))
    return pl.pallas_call(
        functools.partial(_hy_kspec_kernel, hm=hm, scale=1.0 / L),
        grid=(L // tm,),
        in_specs=[pl.BlockSpec((tm, L2), lambda i: (i, 0)),
                  pl.BlockSpec((L2, nx), lambda i: (0, 0)),
                  pl.BlockSpec((hm, 1), lambda i: (i, 0)),
                  pl.BlockSpec((hm, 1), lambda i: (i, 0))],
        out_specs=pl.BlockSpec((1, 6, hm, HY_WIDTH), lambda i: (i, 0, 0, 0)),
        out_shape=jax.ShapeDtypeStruct((L2 // hm, 6, hm, HY_WIDTH), F32),
        compiler_params=_cparams(("parallel",)),
        name="hy_kspec",
    )(a_fwd, xf2, jnp.cos(theta)[:, None], jnp.sin(theta)[:, None])


def _to_lane_blocks(s_ref, x):
    for c in range(s_ref.shape[0]):
        s_ref[c] = x[:, c * LANES:(c + 1) * LANES]


def _split_even_odd(s_ref):
    k, rows, _ = s_ref.shape
    return jnp.concatenate([s_ref[c, pl.ds(p, rows // 2, stride=2), :]
                            for p in range(2) for c in range(k)], axis=1)


def _hy_pre_kernel(xm_ref, xp_ref, xn_ref, cw_ref, cb_ref, u_ref, xa_ref, xpad_ref, s_ref, *, nt, tl):
    i = pl.program_id(1)
    _load_padded(xpad_ref, xm_ref, xp_ref, xn_ref, i > 0, i < nt - 1, tl)
    xa = _dwconv(xpad_ref, cw_ref, cb_ref, tl, HY_CONV, 0, HY_WIDTH)
    xb = _dwconv(xpad_ref, cw_ref, cb_ref, tl, HY_CONV, HY_WIDTH, 2 * HY_WIDTH)
    v = _dwconv(xpad_ref, cw_ref, cb_ref, tl, HY_CONV, 2 * HY_WIDTH, 3 * HY_WIDTH)
    u = xb * v
    xa_ref[0] = xa.astype(BF16)
    _to_lane_blocks(s_ref, u)
    u_ref[0] = _split_even_odd(s_ref).astype(BF16)


def _hy_tile_specs(L, tl):
    hb = tl // HALO
    nhb = L // HALO
    w = 3 * HY_WIDTH
    return [pl.BlockSpec((1, tl, w), lambda b, i: (b, i, C_HY // w)),
            pl.BlockSpec((1, HALO, w), lambda b, i: (b, jnp.maximum(i * hb - 1, 0), C_HY // w)),
            pl.BlockSpec((1, HALO, w), lambda b, i: (b, jnp.minimum((i + 1) * hb, nhb - 1), C_HY // w))]


def _hy_pre(u3, cw, cb, *, tl):
    B, L, _ = u3.shape
    nt = L // tl
    w = 3 * HY_WIDTH
    return pl.pallas_call(
        functools.partial(_hy_pre_kernel, nt=nt, tl=tl),
        grid=(B, nt),
        in_specs=_hy_tile_specs(L, tl) + [pl.BlockSpec((8, w), lambda b, i: (0, 0)),
                                          pl.BlockSpec((1, w), lambda b, i: (0, 0))],
        out_specs=[pl.BlockSpec((1, tl // 2, 2 * HY_WIDTH), lambda b, i: (b, i, 0)),
                   pl.BlockSpec((1, tl, HY_WIDTH), lambda b, i: (b, i, 0))],
        out_shape=[jax.ShapeDtypeStruct((B, L // 2, 2 * HY_WIDTH), BF16),
                   jax.ShapeDtypeStruct((B, L, HY_WIDTH), BF16)],
        scratch_shapes=[pltpu.VMEM((tl + 2 * HALO, w), BF16),
                        pltpu.VMEM((HY_WIDTH // LANES, tl, LANES), F32)],
        compiler_params=_cparams(("parallel", "parallel")),
        name="hy_pre",
    )(u3, u3, u3, cw, cb)


def _hy_fwd_kernel(a_ref, u_ref, f_ref, y_ref, *, hm):
    W = HY_WIDTH
    pq = jnp.dot(a_ref[...], u_ref[0], preferred_element_type=F32)
    pe, po = pq[0:hm, :W], pq[0:hm, W:]
    qe, qo = pq[hm:, :W], pq[hm:, W:]
    kpr, kpi, wr, wi, vr, vi = [f_ref[0, i] for i in range(6)]
    y_ref[0, 0:hm, :W] = (pe * kpr + qe * kpi + po * wr + qo * wi).astype(BF16)
    y_ref[0, hm:, :W] = (pe * kpi - qe * kpr + po * wi - qo * wr).astype(BF16)
    y_ref[0, 0:hm, W:] = (pe * vr + qe * vi + po * kpr + qo * kpi).astype(BF16)
    y_ref[0, hm:, W:] = (pe * vi - qe * vr + po * kpi - qo * kpr).astype(BF16)


def _hy_fwd(a_fwd, ueo, filt, *, hm):
    B, L2, W2 = ueo.shape
    L = 2 * L2
    tm = 2 * hm
    return pl.pallas_call(
        functools.partial(_hy_fwd_kernel, hm=hm),
        grid=(L // tm, B),
        in_specs=[pl.BlockSpec((tm, L2), lambda i, b: (i, 0)),
                  pl.BlockSpec((1, L2, W2), lambda i, b: (b, 0, 0)),
                  pl.BlockSpec((1, 6, hm, HY_WIDTH), lambda i, b: (i, 0, 0, 0))],
        out_specs=pl.BlockSpec((1, tm, W2), lambda i, b: (b, i, 0)),
        out_shape=jax.ShapeDtypeStruct((B, L, W2), BF16),
        compiler_params=_cparams(("parallel", "parallel")),
        name="hy_fwd",
    )(a_fwd, ueo, filt)


def _hy_inv_kernel(a_ref, y_ref, o_ref):
    o_ref[0] = jnp.dot(a_ref[...], y_ref[0], preferred_element_type=F32).astype(BF16)


def _hy_inv(a_inv, yspec, *, tm):
    B, L, W2 = yspec.shape
    L2 = L // 2
    return pl.pallas_call(
        _hy_inv_kernel,
        grid=(L2 // tm, B),
        in_specs=[pl.BlockSpec((tm, L), lambda i, b: (i, 0)),
                  pl.BlockSpec((1, L, W2), lambda i, b: (b, 0, 0))],
        out_specs=pl.BlockSpec((1, tm, W2), lambda i, b: (b, i, 0)),
        out_shape=jax.ShapeDtypeStruct((B, L2, W2), BF16),
        compiler_params=_cparams(("parallel", "parallel")),
        name="hy_inv",
    )(a_inv, yspec)


def _pad_rows(a, rows):
    return jnp.concatenate([a, jnp.zeros((rows - a.shape[0],) + a.shape[1:], a.dtype)], axis=0)


def _pad_cols(a, cols):
    return jnp.concatenate([a, jnp.zeros(a.shape[:-1] + (cols - a.shape[-1],), a.dtype)], axis=-1)


def _tile(n, pref):
    t = min(pref, n)
    assert n % t == 0
    return t


def _layer(x3, p, tables, *, final, final_g):
    B, L, _ = x3.shape
    T = B * L
    x2 = x3.reshape(T, D_MODEL)
    u, dt = _inproj(x2, p["norm_g"], p["w_in"], tm=_tile(T, 512), tn=512)
    u3 = u.reshape(B, L, N_IN)
    dt3 = dt.reshape(B, L, DT_PAD)

    ys = _ssd(u3, dt3, p["ssd_cw"], p["ssd_cb"], p["ssd_dtb"], p["ssd_alog"], p["rexp_f"],
              p["rexp_b"], p["ssd_dskip"], p["ssd_ng"], qb=_tile(L // CHUNK, 8))

    ya = _attention(u3, p["att_bias"], p["att_sink"], p["att_ng"], qb=_tile(L // ATT_BLOCK, 16))

    a_fwd, a_inv, filt, hm = tables
    tl = _tile(L, 1024)
    ueo, xa = _hy_pre(u3, p["hy_cw"], p["hy_cb"], tl=tl)
    yspec = _hy_fwd(a_fwd, ueo, filt, hm=hm)
    conv = _hy_inv(a_inv, yspec, tm=_tile(L // 2, 512))

    out = _outproj(x2, ys.reshape(T, SSD_WIDTH), ya.reshape(T, ATT_WIDTH),
                   conv.reshape(T // 2, 2 * HY_WIDTH), ueo.reshape(T // 2, 2 * HY_WIDTH),
                   xa.reshape(T, HY_WIDTH), u, p["hy_d"], p["hy_ng"], p["w_out"], final_g,
                   tm=_tile(T, 512), final=final)
    return out.reshape(B, L, D_MODEL)


def kernel(x_prompt, x_sample, rel_bias, norm_g, w_in, ssd_conv_w, ssd_conv_b, ssd_dt_bias, ssd_a_log, ssd_d, ssd_norm_g, att_sink, att_norm_g, hy_conv_w, hy_conv_b, hy_w1, hy_b1, hy_w2, hy_b2, hy_w3, hy_b3, hy_w4, hy_freq, hy_d, hy_norm_g, w_out, final_norm_g):
    depth = w_in.shape[0]
    att_perm = _att_col_perm()
    att_bias = _attn_bias(rel_bias)
    rexp_f, rexp_b = _ssd_expand_matrix(False), _ssd_expand_matrix(True)
    final_g = final_norm_g.astype(F32)[None, :]

    layers = []
    for i in range(depth):
        layers.append(dict(
            norm_g=norm_g[i].astype(F32)[None, :],
            w_in=_prep_w_in(w_in, i),
            ssd_cw=_pad_rows(ssd_conv_w[i].astype(F32), 8),
            ssd_cb=ssd_conv_b[i].astype(F32)[None, :],
            ssd_dtb=_pad_cols(ssd_dt_bias[i].astype(F32).reshape(1, 2 * SSD_HEADS), DT_PAD),
            ssd_alog=_pad_cols(ssd_a_log[i].astype(F32).reshape(1, 2 * SSD_HEADS), DT_PAD),
            ssd_dskip=jnp.repeat(ssd_d[i].astype(F32), SSD_HEAD_DIM)[None, :],
            ssd_ng=ssd_norm_g[i].astype(F32)[None, :],
            rexp_f=rexp_f, rexp_b=rexp_b,
            att_bias=att_bias,
            att_sink=jnp.broadcast_to(att_sink[i].astype(F32)[:, None] * LOG2E, (ATT_HEADS, LANES)),
            att_ng=att_norm_g[i].astype(F32)[att_perm][None, :],
            hy_cw=_pad_rows(hy_conv_w[i].astype(F32), 8),
            hy_cb=hy_conv_b[i].astype(F32)[None, :],
            hy_d=hy_d[i].astype(F32)[None, :],
            hy_ng=hy_norm_g[i].astype(F32)[None, :],
            w_out=_prep_w_out(w_out, i),
        ))

    def trunk(x):
        L = x.shape[1]
        hm = min(512, L // 4)
        a_fwd, a_inv = _dft_tables(L, hm)
        for i in range(depth):
            xf = _hy_filter(L, hy_w1[i], hy_b1[i].astype(F32), hy_w2[i].astype(F32),
                            hy_b2[i].astype(F32), hy_w3[i].astype(F32), hy_b3[i].astype(F32),
                            hy_w4[i].astype(F32), hy_freq[i].astype(F32))
            filt = _hy_kspec(a_fwd, xf, hm=hm)
            x = _layer(x, layers[i], (a_fwd, a_inv, filt, hm), final=(i == depth - 1),
                       final_g=final_g)
        return x

    return (trunk(x_prompt), trunk(x_sample))
```

```python
import functools
import math

import jax
import jax.numpy as jnp
import numpy as np
from jax import lax
from jax.experimental import pallas as pl
from jax.experimental.pallas import tpu as pltpu

F32 = jnp.float32
BF16 = jnp.bfloat16
HIGHEST = lax.Precision.HIGHEST

D_MODEL = 2048
SSD_WIDTH = 1024
ATT_WIDTH = 512
HY_WIDTH = 512
SSD_HEAD_DIM = 64
SSD_HEADS = 16
SSD_GROUPS = 2
SSD_STATE = 128
SSD_CONV = 5
CHUNK = 128
SSD_XBC = SSD_WIDTH + 2 * SSD_GROUPS * SSD_STATE
ATT_HEAD_DIM = 64
ATT_HEADS = 8
ATT_KV_HEADS = 2
ATT_REP = ATT_HEADS // ATT_KV_HEADS
ATT_WINDOW = 128
ATT_BLOCK = 128
REL_BUCKETS = 32
REL_MAX_DIST = 128
HY_CONV = 3
HY_EMB_BANDS = 16
HY_FF = 64
HY_FAST_DECAY = 0.3
HY_SLOW_DECAY = 1.5
HY_DECAY_TARGET = 1e-2
EPS = 1e-6
NEG_BIG = -1e30

LANES = 128
BF16_SUBLANES = 16
VMEM_LIMIT = 56 * 1024 * 1024

C_XBC = 0
C_HY = 1536
C_Z = 3072
C_Q = 4096
C_GATT = 4608
C_GHY = 5120
C_K = 5632
C_V = 5760
C_DT = 5888
DT_PAD = 128
N_IN = C_DT + DT_PAD

_OLD_SIZES = [SSD_WIDTH, SSD_XBC, 2 * SSD_HEADS, ATT_WIDTH, ATT_KV_HEADS * ATT_HEAD_DIM,
              ATT_KV_HEADS * ATT_HEAD_DIM, ATT_WIDTH, 3 * HY_WIDTH, HY_WIDTH]
_OLD_OFF = np.concatenate([[0], np.cumsum(_OLD_SIZES)])
IN_COLS = int(_OLD_OFF[-1])


LOG2E = math.log2(math.e)
Q_SCALE = ATT_HEAD_DIM ** -0.5 * LOG2E


def _att_col_perm():
    order = [h for j in range(ATT_REP) for h in (j, ATT_REP + j)]
    return np.concatenate([np.arange(h * ATT_HEAD_DIM, (h + 1) * ATT_HEAD_DIM) for h in order])


def _in_perm():
    perm = np.full((N_IN,), IN_COLS, np.int32)
    scale = np.ones((N_IN,), np.float32)
    o = {n: int(_OLD_OFF[i]) for i, n in enumerate(
        ["z", "xbc", "dt", "q", "k", "v", "gatt", "hy", "ghy"])}
    def put(new, old, width):
        perm[new:new + width] = np.arange(old, old + width)
    put(C_XBC, o["xbc"], SSD_XBC)
    put(C_HY, o["hy"], 3 * HY_WIDTH)
    put(C_Z, o["z"], SSD_WIDTH)
    perm[C_Q:C_Q + ATT_WIDTH] = o["q"] + _att_col_perm()
    scale[C_Q:C_Q + ATT_WIDTH] = Q_SCALE
    perm[C_GATT:C_GATT + ATT_WIDTH] = o["gatt"] + _att_col_perm()
    put(C_GHY, o["ghy"], HY_WIDTH)
    put(C_K, o["k"], 128)
    put(C_V, o["v"], 128)
    put(C_DT, o["dt"], 2 * SSD_HEADS)
    return perm, scale


def _col_runs(perm):
    cuts = np.flatnonzero(np.diff(perm) != 1) + 1
    starts = np.concatenate([[0], cuts])
    ends = np.concatenate([cuts, [len(perm)]])
    return [(int(s), int(perm[s]), int(e - s)) for s, e in zip(starts, ends)]


def _prep_w_in_kernel(w_ref, o_ref, *, runs, n_real):
    for new0, old0, n in runs:
        piece = w_ref[:, old0:old0 + n]
        if C_Q <= new0 < C_Q + ATT_WIDTH:
            piece = piece * Q_SCALE
        o_ref[:, new0:new0 + n] = piece.astype(BF16)
    o_ref[:, n_real:] = jnp.zeros((o_ref.shape[0], N_IN - n_real), BF16)


def _prep_w_in(w_in, layer):
    perm, _ = _in_perm()
    n_real = C_DT + 2 * SSD_HEADS
    tr = 256
    return pl.pallas_call(
        functools.partial(_prep_w_in_kernel, runs=_col_runs(perm[:n_real]), n_real=n_real),
        grid=(D_MODEL // tr,),
        in_specs=[pl.BlockSpec((None, tr, IN_COLS), lambda r: (layer, r, 0))],
        out_specs=pl.BlockSpec((tr, N_IN), lambda r: (r, 0)),
        out_shape=jax.ShapeDtypeStruct((D_MODEL, N_IN), BF16),
        compiler_params=_cparams(("parallel",)),
        name="prep_w_in",
    )(w_in)


def _prep_w_out_kernel(w_ref, g_ref, o_ref, *, runs):
    for new0, old0, n in runs:
        o_ref[new0:new0 + n, :] = (w_ref[old0:old0 + n, :] * g_ref[old0:old0 + n, :]).astype(BF16)


def _prep_w_out(w_out, gain, layer):
    rows = np.concatenate([np.arange(SSD_WIDTH), SSD_WIDTH + _att_col_perm(),
                           np.arange(SSD_WIDTH + ATT_WIDTH, D_MODEL)])
    tc = 512
    return pl.pallas_call(
        functools.partial(_prep_w_out_kernel, runs=_col_runs(rows)),
        grid=(D_MODEL // tc,),
        in_specs=[pl.BlockSpec((None, D_MODEL, tc), lambda c: (layer, 0, c)),
                  pl.BlockSpec((D_MODEL, 1), lambda c: (0, 0))],
        out_specs=pl.BlockSpec((D_MODEL, tc), lambda c: (0, c)),
        out_shape=jax.ShapeDtypeStruct((D_MODEL, D_MODEL), BF16),
        compiler_params=_cparams(("parallel",)),
        name="prep_w_out",
    )(w_out, gain)


def _cparams(sem):
    return pltpu.CompilerParams(dimension_semantics=sem, vmem_limit_bytes=VMEM_LIMIT)


def _silu(x):
    return x * (1.0 / (1.0 + jnp.exp(-x)))


def _softplus(x):
    return jnp.maximum(x, 0.0) + jnp.log1p(jnp.exp(-jnp.abs(x)))


def _inproj_kernel(x_ref, g_ref, w_ref, u_ref, dt_ref, h_ref, *, rows, tn):
    for r in range(x_ref.shape[0] // rows):
        sl = slice(r * rows, (r + 1) * rows)
        x = x_ref[sl, :]
        ms = jnp.mean(x * x, axis=-1, keepdims=True)
        h_ref[sl, :] = (x * lax.rsqrt(ms + EPS) * g_ref[...]).astype(BF16)
        h = h_ref[sl, :]
        for c0 in range(0, N_IN, tn):
            c1 = min(c0 + tn, N_IN)
            acc = jnp.dot(h, w_ref[:, c0:c1], preferred_element_type=F32)
            u_ref[sl, c0:c1] = acc.astype(BF16)
            if c0 <= C_DT < c1:
                dt_ref[sl, :] = acc[:, C_DT - c0:C_DT - c0 + DT_PAD]


def _inproj(x2, g, w, *, tm, tn):
    T = x2.shape[0]
    assert T % tm == 0
    return pl.pallas_call(
        functools.partial(_inproj_kernel, rows=min(256, tm), tn=tn),
        grid=(T // tm,),
        in_specs=[pl.BlockSpec((tm, D_MODEL), lambda i: (i, 0)),
                  pl.BlockSpec((1, D_MODEL), lambda i: (0, 0)),
                  pl.BlockSpec((D_MODEL, N_IN), lambda i: (0, 0), pipeline_mode=pl.Buffered(1))],
        out_specs=[pl.BlockSpec((tm, N_IN), lambda i: (i, 0)),
                   pl.BlockSpec((tm, DT_PAD), lambda i: (i, 0))],
        out_shape=[jax.ShapeDtypeStruct((T, N_IN), BF16),
                   jax.ShapeDtypeStruct((T, DT_PAD), F32)],
        scratch_shapes=[pltpu.VMEM((tm, D_MODEL), BF16)],
        compiler_params=_cparams(("parallel",)),
        name="inproj",
    )(x2, g, w)


def _outproj_kernel(x_ref, ys_ref, ya_ref, c_ref, ueo_ref, xa_ref, g_ref, d_ref, w_ref, fg_ref,
                    o_ref, s_ref, *, rows, final):
    nblk = HY_WIDTH // LANES

    def interleave(ref, hs):
        for p in range(2):
            for c in range(nblk):
                s_ref[c, pl.ds(p, rows // 2, stride=2), :] = ref[
                    hs, (p * nblk + c) * LANES:(p * nblk + c + 1) * LANES].astype(F32)
        return jnp.concatenate([s_ref[c] for c in range(nblk)], axis=1)

    for r in range(x_ref.shape[0] // rows):
        sl = slice(r * rows, (r + 1) * rows)
        hs = slice(r * rows // 2, (r + 1) * rows // 2)
        conv = interleave(c_ref, hs)
        u = interleave(ueo_ref, hs)
        xa = xa_ref[sl, :].astype(F32)
        y = xa * (conv + u * d_ref[...])
        y = y * _silu(g_ref[sl, :].astype(F32))
        ms = jnp.mean(y * y, axis=-1, keepdims=True)
        yh = (y * lax.rsqrt(ms + EPS)).astype(BF16)

        acc = x_ref[sl, :]
        acc = acc + jnp.dot(ys_ref[sl, :], w_ref[0:SSD_WIDTH, :], preferred_element_type=F32)
        acc = acc + jnp.dot(ya_ref[sl, :], w_ref[SSD_WIDTH:SSD_WIDTH + ATT_WIDTH, :],
                            preferred_element_type=F32)
        acc = acc + jnp.dot(yh, w_ref[SSD_WIDTH + ATT_WIDTH:, :], preferred_element_type=F32)
        if final:
            ms = jnp.mean(acc * acc, axis=-1, keepdims=True)
            acc = acc * lax.rsqrt(ms + EPS) * fg_ref[...]
        o_ref[sl, :] = acc


def _outproj(x2, ys, ya, conv2, ueo2, xa2, u2, d, w, fg, *, tm, final):
    T = x2.shape[0]
    assert T % tm == 0
    rows = min(256, tm)
    return pl.pallas_call(
        functools.partial(_outproj_kernel, rows=rows, final=final),
        grid=(T // tm,),
        in_specs=[pl.BlockSpec((tm, D_MODEL), lambda i: (i, 0)),
                  pl.BlockSpec((tm, SSD_WIDTH), lambda i: (i, 0)),
                  pl.BlockSpec((tm, ATT_WIDTH), lambda i: (i, 0)),
                  pl.BlockSpec((tm // 2, 2 * HY_WIDTH), lambda i: (i, 0)),
                  pl.BlockSpec((tm // 2, 2 * HY_WIDTH), lambda i: (i, 0)),
                  pl.BlockSpec((tm, HY_WIDTH), lambda i: (i, 0)),
                  pl.BlockSpec((tm, HY_WIDTH), lambda i: (i, C_GHY // HY_WIDTH)),
                  pl.BlockSpec((1, HY_WIDTH), lambda i: (0, 0)),
                  pl.BlockSpec((D_MODEL, D_MODEL), lambda i: (0, 0)),
                  pl.BlockSpec((1, D_MODEL), lambda i: (0, 0))],
        out_specs=pl.BlockSpec((tm, D_MODEL), lambda i: (i, 0)),
        out_shape=jax.ShapeDtypeStruct((T, D_MODEL), F32),
        scratch_shapes=[pltpu.VMEM((HY_WIDTH // LANES, rows, LANES), F32)],
        compiler_params=_cparams(("parallel",)),
        name="outproj",
    )(x2, ys, ya, conv2, ueo2, xa2, u2, d, w, fg)


def _t5_buckets(rel):
    nb = REL_BUCKETS // 2
    max_exact = nb // 2
    ret = (rel > 0).astype(np.int32) * nb
    n = np.abs(rel)
    large = max_exact + (np.log(np.maximum(n, 1) / max_exact) / math.log(REL_MAX_DIST / max_exact)
                         * (nb - max_exact)).astype(np.int32)
    large = np.minimum(large, nb - 1)
    return ret + np.where(n < max_exact, n, large)


def _attn_bias(rel_bias):
    qi = np.arange(ATT_BLOCK)[:, None]
    kj = np.arange(3 * ATT_BLOCK)[None, :]
    rel = kj - ATT_BLOCK - qi
    onehot = (_t5_buckets(rel)[None] == np.arange(REL_BUCKETS)[:, None, None]).astype(np.float32)
    bias = jnp.einsum("bqk,bh->hqk", jnp.asarray(onehot), rel_bias.astype(F32),
                      precision=HIGHEST) * LOG2E
    window = np.abs(rel) <= ATT_WINDOW
    variants = []
    for last in (False, True):
        for first in (False, True):
            ok = window & ~(first & (kj < ATT_BLOCK)) & ~(last & (kj >= 2 * ATT_BLOCK))
            variants.append(jnp.where(ok[None], bias, NEG_BIG))
    return jnp.transpose(jnp.stack(variants), (0, 1, 3, 2))


def _attn_kernel(q_ref, kp_ref, kc_ref, kn_ref, vp_ref, vc_ref, vn_ref, g_ref, bias_ref, sink_ref,
                 o_ref, klo_ref, khi_ref, vt_ref, *, nt, qb):
    n = pl.program_id(1)
    lo = lax.broadcasted_iota(jnp.int32, (1, LANES), 1) < ATT_HEAD_DIM
    zero = jnp.zeros((), BF16)
    kext = jnp.concatenate([kp_ref[0], kc_ref[0], kn_ref[0]], axis=0)
    klo_ref[...] = jnp.where(lo, kext, zero)
    khi_ref[...] = jnp.where(lo, zero, kext)
    for t, ref, cnt in ((0, vp_ref, 1), (1, vc_ref, qb), (qb + 1, vn_ref, 1)):
        for i in range(cnt):
            blk = ref[0, i * ATT_BLOCK:(i + 1) * ATT_BLOCK, :]
            vt_ref[t + i] = blk.astype(F32).T.astype(BF16)
    row_lo = lax.broadcasted_iota(jnp.int32, (LANES, ATT_BLOCK), 0) < ATT_HEAD_DIM
    nk = 3 * ATT_BLOCK

    def body(i, carry):
        r0 = pl.multiple_of(i * ATT_BLOCK, ATT_BLOCK)
        q = q_ref[0, pl.ds(r0, ATT_BLOCK), :]
        kst = jnp.concatenate([klo_ref[pl.ds(r0, nk), :], khi_ref[pl.ds(r0, nk), :]], axis=0)
        vt = jnp.concatenate([vt_ref[i], vt_ref[i + 1], vt_ref[i + 2]], axis=1)
        first = jnp.logical_and(n == 0, i == 0)
        last = jnp.logical_and(n == nt - 1, i == qb - 1)
        variant = first.astype(jnp.int32) + 2 * last.astype(jnp.int32)
        outs = []
        for j in range(ATT_REP):
            qp = q[:, j * LANES:(j + 1) * LANES]
            st = lax.dot_general(kst, qp, (((1,), (1,)), ((), ())), preferred_element_type=F32)
            halves = []
            for e, h in enumerate((j, ATT_REP + j)):
                s = st[e * nk:(e + 1) * nk] + bias_ref[variant, h]
                sk = sink_ref[h:h + 1, :]
                m = jnp.maximum(jnp.max(s, axis=0, keepdims=True), sk)
                p = jnp.exp2(s - m)
                den = jnp.sum(p, axis=0, keepdims=True) + jnp.exp2(sk - m)
                ot = jnp.dot(vt, p.astype(BF16), preferred_element_type=F32)
                halves.append(ot * (1.0 / den))
            outs.append(jnp.where(row_lo, halves[0], halves[1]).T)
        o = jnp.concatenate(outs, axis=-1)
        y = o * _silu(g_ref[0, pl.ds(r0, ATT_BLOCK), :].astype(F32))
        ms = jnp.mean(y * y, axis=-1, keepdims=True)
        o_ref[0, pl.ds(r0, ATT_BLOCK), :] = (y * lax.rsqrt(ms + EPS)).astype(BF16)
        return carry

    lax.fori_loop(0, qb, body, 0, unroll=True)


def _attention(u3, bias, sink, *, qb):
    B, L, _ = u3.shape
    tq = qb * ATT_BLOCK
    assert L % tq == 0
    nt = L // tq
    nb = L // ATT_BLOCK
    kcol, vcol = C_K // 128, C_V // 128
    def kv_specs(colblk):
        return [pl.BlockSpec((1, ATT_BLOCK, 128), lambda b, n: (b, jnp.maximum(n * qb - 1, 0), colblk)),
                pl.BlockSpec((1, tq, 128), lambda b, n: (b, n, colblk)),
                pl.BlockSpec((1, ATT_BLOCK, 128),
                             lambda b, n: (b, jnp.minimum((n + 1) * qb, nb - 1), colblk))]
    return pl.pallas_call(
        functools.partial(_attn_kernel, nt=nt, qb=qb),
        grid=(B, nt),
        in_specs=[pl.BlockSpec((1, tq, ATT_WIDTH), lambda b, n: (b, n, C_Q // ATT_WIDTH))]
                 + kv_specs(kcol) + kv_specs(vcol)
                 + [pl.BlockSpec((1, tq, ATT_WIDTH), lambda b, n: (b, n, C_GATT // ATT_WIDTH)),
                    pl.BlockSpec((4, ATT_HEADS, 3 * ATT_BLOCK, ATT_BLOCK), lambda b, n: (0, 0, 0, 0)),
                    pl.BlockSpec((ATT_HEADS, LANES), lambda b, n: (0, 0))],
        out_specs=pl.BlockSpec((1, tq, ATT_WIDTH), lambda b, n: (b, n, 0)),
        out_shape=jax.ShapeDtypeStruct((B, L, ATT_WIDTH), BF16),
        scratch_shapes=[pltpu.VMEM(((qb + 2) * ATT_BLOCK, LANES), BF16),
                        pltpu.VMEM(((qb + 2) * ATT_BLOCK, LANES), BF16),
                        pltpu.VMEM((qb + 2, LANES, ATT_BLOCK), BF16)],
        compiler_params=_cparams(("parallel", "parallel")),
        name="attn",
    )(u3, u3, u3, u3, u3, u3, u3, u3, bias, sink)


HALO = BF16_SUBLANES


CONV_BLK = 128


def _load_padded(xpad_ref, xm_ref, xp_ref, xn_ref, has_prev, has_next, rows):
    zero = jnp.zeros((), BF16)
    xpad_ref[pl.ds(0, HALO), :] = jnp.where(has_prev, xp_ref[0], zero)
    xpad_ref[pl.ds(HALO, rows), :] = xm_ref[0]
    xpad_ref[pl.ds(HALO + rows, HALO), :] = jnp.where(has_next, xn_ref[0], zero)


def _shift_matrix(width):
    offs = [k - width // 2 for k in range(width) if k != width // 2]
    r = lax.broadcasted_iota(jnp.int32, (CONV_BLK, CONV_BLK + 2 * HALO), 0)
    c = lax.broadcasted_iota(jnp.int32, (CONV_BLK, CONV_BLK + 2 * HALO), 1)
    return jnp.concatenate([(c == r + HALO + d).astype(BF16) for d in offs], axis=0)


def _dwconv_block(xpad_ref, shifts, w_ref, b_ref, j, width, c0, c1):
    win = xpad_ref[j * CONV_BLK:(j + 1) * CONV_BLK + 2 * HALO, c0:c1]
    moved = jnp.dot(shifts, win, preferred_element_type=F32)
    acc = win[HALO:HALO + CONV_BLK].astype(F32) * w_ref[width // 2:width // 2 + 1, c0:c1]
    i = 0
    for k in range(width):
        if k == width // 2:
            continue
        acc = acc + moved[i * CONV_BLK:(i + 1) * CONV_BLK] * w_ref[k:k + 1, c0:c1]
        i += 1
    return acc + b_ref[:, c0:c1]


def _dwconv(xpad_ref, w_ref, b_ref, rows, width, c0, c1):
    shifts = _shift_matrix(width)
    return jnp.concatenate([_dwconv_block(xpad_ref, shifts, w_ref, b_ref, j, width, c0, c1)
                            for j in range(rows // CONV_BLK)], axis=0)


SSD_HPG = SSD_HEADS // SSD_GROUPS
SSD_GW = SSD_HPG * SSD_HEAD_DIM
SSD_BC = SSD_GROUPS * SSD_STATE


def _ssd_decay(dt_raw, dtb_ref, alog_ref):
    dt = _softplus(dt_raw + dtb_ref[...])
    dta = dt * (-jnp.exp(alog_ref[...]))
    row = lax.broadcasted_iota(jnp.int32, (CHUNK, CHUNK), 0)
    col = lax.broadcasted_iota(jnp.int32, (CHUNK, CHUNK), 1)
    tril = (row >= col).astype(BF16)
    hi = dta.astype(BF16)
    r1 = dta - hi.astype(F32)
    mid = r1.astype(BF16)
    lo = (r1 - mid.astype(F32)).astype(BF16)
    pre = (jnp.dot(tril, hi, preferred_element_type=F32) + jnp.dot(tril, mid, preferred_element_type=F32)
           + jnp.dot(tril, lo, preferred_element_type=F32))
    tot = pre[CHUNK - 1:CHUNK, :]
    is_bwd = lax.broadcasted_iota(jnp.int32, (1, LANES), 1) >= SSD_HEADS
    acs = jnp.where(is_bwd, tot - pre + dta, pre)
    return dt, acs, tot


def _ssd_bwd_kernel(xm_ref, xp_ref, xn_ref, dt_ref, cw_ref, cb_ref, dtb_ref, alog_ref, rexp_ref,
                    xact_ref, prev_ref, xpad_ref, xf_ref, st_ref, *, nt, qb):
    n = pl.program_id(1)
    tile = nt - 1 - n
    rows = qb * CHUNK

    @pl.when(n == 0)
    def _():
        st_ref[...] = jnp.zeros(st_ref.shape, F32)

    _load_padded(xpad_ref, xm_ref, xp_ref, xn_ref, tile > 0, tile < nt - 1, rows)
    xact = _silu(_dwconv(xpad_ref, cw_ref, cb_ref, rows, SSD_CONV, 0, SSD_XBC))
    xf_ref[...] = xact
    xact_ref[0] = xact.astype(BF16)

    def body(j, carry):
        i = qb - 1 - j
        r0 = pl.multiple_of(i * CHUNK, CHUNK)
        xs = xf_ref[pl.ds(r0, CHUNK), 0:SSD_WIDTH]
        bm = xf_ref[pl.ds(r0, CHUNK), SSD_WIDTH:SSD_WIDTH + SSD_BC].astype(BF16)
        dt, acs, tot = _ssd_decay(dt_ref[0, pl.ds(r0, CHUNK), :], dtb_ref, alog_ref)
        e_in = jnp.concatenate([jnp.exp(tot - acs) * dt, jnp.broadcast_to(jnp.exp(tot), (8, LANES))],
                               axis=0).astype(BF16)
        e_out = jnp.dot(e_in, rexp_ref[...], preferred_element_type=F32)
        xd = (xs * e_out[0:CHUNK]).astype(BF16)
        cdec = e_out[CHUNK:CHUNK + 1]
        for g in range(SSD_GROUPS):
            prev = st_ref[g]
            prev_ref[0, i, g] = prev.astype(BF16)
            s_new = lax.dot_general(bm[:, g * SSD_STATE:(g + 1) * SSD_STATE],
                                    xd[:, g * SSD_GW:(g + 1) * SSD_GW], (((0,), (0,)), ((), ())),
                                    preferred_element_type=F32)
            st_ref[g] = prev * cdec[:, g * SSD_GW:(g + 1) * SSD_GW] + s_new
        return carry

    lax.fori_loop(0, qb, body, 0, unroll=True)


def _ssd_fwd_kernel(xa_ref, dt_ref, prev_ref, z_ref, dtb_ref, alog_ref, rexpf_ref, rexpb_ref,
                    dsk_ref, o_ref, st_ref, *, qb):
    n = pl.program_id(1)

    @pl.when(n == 0)
    def _():
        st_ref[...] = jnp.zeros(st_ref.shape, F32)

    row = lax.broadcasted_iota(jnp.int32, (CHUNK, CHUNK), 0)
    col = lax.broadcasted_iota(jnp.int32, (CHUNK, CHUNK), 1)
    fwd_part = row > col
    diag = row == col
    lane_lo = lax.broadcasted_iota(jnp.int32, (1, LANES), 1) < SSD_HEAD_DIM
    zero_b = jnp.zeros((), BF16)

    def body(i, carry):
        r0 = pl.multiple_of(i * CHUNK, CHUNK)
        xs_b = xa_ref[0, pl.ds(r0, CHUNK), 0:SSD_WIDTH]
        xs = xs_b.astype(F32)
        bm = xa_ref[0, pl.ds(r0, CHUNK), SSD_WIDTH:SSD_WIDTH + SSD_BC]
        cm = xa_ref[0, pl.ds(r0, CHUNK), SSD_WIDTH + SSD_BC:SSD_XBC]
        dt, acs, tot = _ssd_decay(dt_ref[0, pl.ds(r0, CHUNK), :], dtb_ref, alog_ref)
        acs2 = acs * LOG2E
        rt = (acs2 - jnp.log2(dt)).T
        dsum_t = jnp.log2(dt + pltpu.roll(dt, LANES - SSD_HEADS, axis=1)).T
        eacs = jnp.exp(acs)
        ef_in = jnp.concatenate([jnp.exp(tot - acs) * dt, eacs,
                                 jnp.broadcast_to(jnp.exp(tot), (8, LANES))], axis=0).astype(BF16)
        ef = jnp.dot(ef_in, rexpf_ref[...], preferred_element_type=F32)
        eb = jnp.dot(eacs.astype(BF16), rexpb_ref[...], preferred_element_type=F32)
        xd = (xs * ef[0:CHUNK]).astype(BF16)
        eacs_f = ef[CHUNK:2 * CHUNK]
        cdec = ef[2 * CHUNK:2 * CHUNK + 1]
        ys = []
        for g in range(SSD_GROUPS):
            bg = bm[:, g * SSD_STATE:(g + 1) * SSD_STATE]
            cg = cm[:, g * SSD_STATE:(g + 1) * SSD_STATE]
            gs = slice(g * SSD_GW, (g + 1) * SSD_GW)
            cb = lax.dot_general(cg, bg, (((1,), (1,)), ((), ())), preferred_element_type=F32)
            prev = st_ref[g]
            y_off = (jnp.dot(cg, prev.astype(BF16), preferred_element_type=F32) * eacs_f[:, gs]
                     + jnp.dot(cg, prev_ref[0, i, g], preferred_element_type=F32) * eb[:, gs])
            s_new = lax.dot_general(bg, xd[:, gs], (((0,), (0,)), ((), ())),
                                    preferred_element_type=F32)
            st_ref[g] = prev * cdec[:, gs] + s_new
            for pr in range(SSD_HPG // 2):
                c0 = g * SSD_GW + pr * LANES
                xpair = xs_b[:, c0:c0 + LANES]
                xbd = jnp.concatenate([jnp.where(lane_lo, xpair, zero_b),
                                       jnp.where(lane_lo, zero_b, xpair)], axis=0)
                mats = []
                for e in range(2):
                    h = g * SSD_HPG + pr * 2 + e
                    hb = SSD_HEADS + h
                    sel = jnp.where(fwd_part, acs2[:, h:h + 1] - rt[h:h + 1, :],
                                    acs2[:, hb:hb + 1] - rt[hb:hb + 1, :])
                    sel = jnp.where(diag, dsum_t[h:h + 1, :], sel)
                    mats.append((cb * jnp.exp2(sel)).astype(BF16))
                yd = jnp.dot(jnp.concatenate(mats, axis=1), xbd, preferred_element_type=F32)
                ys.append(yd + y_off[:, pr * LANES:(pr + 1) * LANES])
        y = jnp.concatenate(ys, axis=-1) + xs * dsk_ref[...]
        y = y * _silu(z_ref[0, pl.ds(r0, CHUNK), :].astype(F32))
        ms = jnp.mean(y * y, axis=-1, keepdims=True)
        o_ref[0, pl.ds(r0, CHUNK), :] = (y * lax.rsqrt(ms + EPS)).astype(BF16)
        return carry

    lax.fori_loop(0, qb, body, 0, unroll=True)


def _ssd(u3, dt3, cw, cb, dtb, alog, rexp_f, rexp_b, dsk, *, qb):
    B, L, _ = u3.shape
    rows = qb * CHUNK
    assert L % rows == 0
    nt = L // rows
    nc = L // CHUNK
    hb = rows // HALO
    nhb = L // HALO
    state = pltpu.VMEM((SSD_GROUPS, SSD_STATE, SSD_GW), F32)
    const = lambda shape: pl.BlockSpec(shape, lambda b, n: (0,) * len(shape))
    rt = lambda n: nt - 1 - n

    xact, prevb = pl.pallas_call(
        functools.partial(_ssd_bwd_kernel, nt=nt, qb=qb),
        grid=(B, nt),
        in_specs=[
            pl.BlockSpec((1, rows, SSD_XBC), lambda b, n: (b, rt(n), C_XBC // SSD_XBC)),
            pl.BlockSpec((1, HALO, SSD_XBC),
                         lambda b, n: (b, jnp.maximum(rt(n) * hb - 1, 0), C_XBC // SSD_XBC)),
            pl.BlockSpec((1, HALO, SSD_XBC),
                         lambda b, n: (b, jnp.minimum((rt(n) + 1) * hb, nhb - 1), C_XBC // SSD_XBC)),
            pl.BlockSpec((1, rows, DT_PAD), lambda b, n: (b, rt(n), 0)),
            const((8, SSD_XBC)), const((1, SSD_XBC)), const((1, DT_PAD)), const((1, DT_PAD)),
            const((LANES, SSD_WIDTH)),
        ],
        out_specs=[pl.BlockSpec((1, rows, SSD_XBC), lambda b, n: (b, rt(n), 0)),
                   pl.BlockSpec((1, qb, SSD_GROUPS, SSD_STATE, SSD_GW),
                                lambda b, n: (b, rt(n), 0, 0, 0))],
        out_shape=[jax.ShapeDtypeStruct((B, L, SSD_XBC), BF16),
                   jax.ShapeDtypeStruct((B, nc, SSD_GROUPS, SSD_STATE, SSD_GW), BF16)],
        scratch_shapes=[pltpu.VMEM((rows + 2 * HALO, SSD_XBC), BF16),
                        pltpu.VMEM((rows, SSD_XBC), F32), state],
        compiler_params=_cparams(("parallel", "arbitrary")),
        name="ssd_bwd",
    )(u3, u3, u3, dt3, cw, cb, dtb, alog, rexp_b)

    return pl.pallas_call(
        functools.partial(_ssd_fwd_kernel, qb=qb),
        grid=(B, nt),
        in_specs=[
            pl.BlockSpec((1, rows, SSD_XBC), lambda b, n: (b, n, 0)),
            pl.BlockSpec((1, rows, DT_PAD), lambda b, n: (b, n, 0)),
            pl.BlockSpec((1, qb, SSD_GROUPS, SSD_STATE, SSD_GW), lambda b, n: (b, n, 0, 0, 0)),
            pl.BlockSpec((1, rows, SSD_WIDTH), lambda b, n: (b, n, C_Z // SSD_WIDTH)),
            const((1, DT_PAD)), const((1, DT_PAD)),
            const((LANES, SSD_WIDTH)), const((LANES, SSD_WIDTH)),
            const((1, SSD_WIDTH)),
        ],
        out_specs=pl.BlockSpec((1, rows, SSD_WIDTH), lambda b, n: (b, n, 0)),
        out_shape=jax.ShapeDtypeStruct((B, L, SSD_WIDTH), BF16),
        scratch_shapes=[state],
        compiler_params=_cparams(("parallel", "arbitrary")),
        name="ssd_fwd",
    )(xact, dt3, prevb, u3, dtb, alog, rexp_f, rexp_b, dsk)


def _ssd_expand_matrix(rev):
    m = np.zeros((LANES, SSD_WIDTH), np.float32)
    hoff = SSD_HEADS if rev else 0
    for h in range(SSD_HEADS):
        m[hoff + h, h * SSD_HEAD_DIM:(h + 1) * SSD_HEAD_DIM] = 1.0
    return jnp.asarray(m, BF16)


def _hy_filter_kernel(t_ref, bands_ref, w1t_ref, w1c_ref, w1s_ref, b1_ref, w2_ref, b2_ref, w3_ref,
                      b3_ref, w4_ref, fr_ref, absd_ref, x_ref, s_ref, *, L, tl):
    i = pl.program_id(0)
    pos = (lax.broadcasted_iota(jnp.int32, (tl, 1), 0) + i * tl).astype(F32)
    t = t_ref[...]
    pos_row = (lax.broadcasted_iota(jnp.int32, (1, tl), 1) + i * tl).astype(F32)
    ang_t = 2.0 * math.pi * pos_row * bands_ref[...] / L
    fr = fr_ref[...]
    dot = functools.partial(jnp.dot, preferred_element_type=F32, precision=HIGHEST)
    tdot = lambda a, b: lax.dot_general(a, b, (((0,), (0,)), ((), ())), preferred_element_type=F32,
                                        precision=HIGHEST)
    pre = t * w1t_ref[...] + tdot(jnp.cos(ang_t), w1c_ref[...]) + tdot(-jnp.sin(ang_t), w1s_ref[...])
    h = jnp.sin(fr * (pre + b1_ref[...]))
    h = jnp.sin(fr * (dot(h, w2_ref[...]) + b2_ref[...]))
    h = jnp.sin(fr * (dot(h, w3_ref[...]) + b3_ref[...]))
    w4 = w4_ref[...]
    h_hi, w_hi = h.astype(BF16), w4.astype(BF16)
    h_lo, w_lo = (h - h_hi.astype(F32)).astype(BF16), (w4 - w_hi.astype(F32)).astype(BF16)
    bdot = functools.partial(jnp.dot, preferred_element_type=F32)
    h = bdot(h_hi, w_hi) + bdot(h_hi, w_lo) + bdot(h_lo, w_hi)
    decay = jnp.exp(-t * absd_ref[...])
    hf = h[:, :HY_WIDTH] * decay
    hb = jnp.where(pos == 0.0, 0.0, h[:, HY_WIDTH:] * decay)
    _to_lane_blocks(s_ref, jnp.concatenate([hf + hb, hb - hf], axis=1))
    x_ref[...] = _split_even_odd(s_ref).astype(BF16)


def _hy_filter(L, w1, b1, w2, b2, w3, b3, w4, freq):
    tl = min(512, L)
    t = jnp.linspace(0.0, 1.0, L, dtype=F32)[:, None]
    bands = jnp.linspace(1e-4, HY_EMB_BANDS - 1, HY_EMB_BANDS, dtype=F32)[:, None]
    max_decay = math.log(HY_DECAY_TARGET) / HY_FAST_DECAY
    min_decay = math.log(HY_DECAY_TARGET) / HY_SLOW_DECAY
    absd = jnp.abs(jnp.linspace(min_decay, max_decay, HY_WIDTH, dtype=F32))[None, :]
    w1 = w1.astype(F32)
    full = lambda a: pl.BlockSpec(a.shape, lambda i: (0,) * a.ndim)
    args = [t, bands, w1[0:1], w1[1:1 + HY_EMB_BANDS], w1[1 + HY_EMB_BANDS:], b1[None], w2, b2[None],
            w3, b3[None], w4, freq[None], absd]
    in_specs = [pl.BlockSpec((tl, 1), lambda i: (i, 0))] + [full(a) for a in args[1:]]
    return pl.pallas_call(
        functools.partial(_hy_filter_kernel, L=L, tl=tl),
        grid=(L // tl,),
        in_specs=in_specs,
        out_specs=pl.BlockSpec((tl // 2, 4 * HY_WIDTH), lambda i: (i, 0)),
        out_shape=jax.ShapeDtypeStruct((L // 2, 4 * HY_WIDTH), BF16),
        scratch_shapes=[pltpu.VMEM((2 * HY_WIDTH // LANES, tl, LANES), F32)],
        compiler_params=_cparams(("parallel",)),
        name="hy_filter",
    )(*args)


def _dft_tables(L, hm):
    L2 = L // 2
    g = jnp.arange(L2, dtype=jnp.int32)
    s = jnp.arange(L2, dtype=jnp.int32)
    ph = ((2 * g + 1)[:, None] * s[None, :]) % (2 * L)
    ang = ph.astype(F32) * (math.pi / L)
    c = jnp.cos(ang).astype(BF16).reshape(L2 // hm, 1, hm, L2)
    sn = jnp.sin(ang).astype(BF16).reshape(L2 // hm, 1, hm, L2)
    a_fwd = jnp.concatenate([c, sn], axis=1).reshape(L, L2)
    a_inv = jnp.concatenate([c, -sn], axis=1).reshape(L, L2).T
    return a_fwd, a_inv


def _hy_kspec_kernel(a_ref, x_ref, wc_ref, ws_ref, f_ref, *, hm, scale):
    W = HY_WIDTH
    pq = jnp.dot(a_ref[...], x_ref[...], preferred_element_type=F32)
    ea, eb, oa, ob = [(pq[0:hm, i * W:(i + 1) * W], -pq[hm:, i * W:(i + 1) * W]) for i in range(4)]
    w = (wc_ref[...], -ws_ref[...])
    cmul = lambda x, y: (x[0] * y[0] - x[1] * y[1], x[0] * y[1] + x[1] * y[0])
    woa, wob = cmul(w, oa), cmul(w, ob)
    k1 = (ea[0] + woa[0], -(eb[1] + wob[1]))
    k2 = (ea[0] - woa[0], eb[1] - wob[1])
    kp = (k1[0] + k2[0], k1[1] - k2[1])
    km = (k1[0] - k2[0], k1[1] + k2[1])
    wkm = cmul(w, km)
    vkm = cmul((w[0], -w[1]), km)
    for i, part in enumerate((kp[0], kp[1], wkm[0], wkm[1], vkm[0], vkm[1])):
        f_ref[0, i] = part * scale


def _hy_kspec(a_fwd, xf2, *, hm):
    L, L2 = a_fwd.shape
    tm = 2 * hm
    nx = xf2.shape[1]
    theta = (2.0 * jnp.arange(L2, dtype=F32) + 1.0) * (math.pi / (2 * L))
    return pl.pallas_call(
        functools.partial(_hy_kspec_kernel, hm=hm, scale=1.0 / L),
        grid=(L // tm,),
        in_specs=[pl.BlockSpec((tm, L2), lambda i: (i, 0)),
                  pl.BlockSpec((L2, nx), lambda i: (0, 0)),
                  pl.BlockSpec((hm, 1), lambda i: (i, 0)),
                  pl.BlockSpec((hm, 1), lambda i: (i, 0))],
        out_specs=pl.BlockSpec((1, 6, hm, HY_WIDTH), lambda i: (i, 0, 0, 0)),
        out_shape=jax.ShapeDtypeStruct((L2 // hm, 6, hm, HY_WIDTH), F32),
        compiler_params=_cparams(("parallel",)),
        name="hy_kspec",
    )(a_fwd, xf2, jnp.cos(theta)[:, None], jnp.sin(theta)[:, None])


def _to_lane_blocks(s_ref, x):
    for c in range(s_ref.shape[0]):
        s_ref[c] = x[:, c * LANES:(c + 1) * LANES]


def _split_even_odd(s_ref):
    k, rows, _ = s_ref.shape
    return jnp.concatenate([s_ref[c, pl.ds(p, rows // 2, stride=2), :]
                            for p in range(2) for c in range(k)], axis=1)


def _hy_pre_kernel(xm_ref, xp_ref, xn_ref, cw_ref, cb_ref, u_ref, xa_ref, xpad_ref, s_ref, *, nt, tl):
    i = pl.program_id(1)
    _load_padded(xpad_ref, xm_ref, xp_ref, xn_ref, i > 0, i < nt - 1, tl)
    xa = _dwconv(xpad_ref, cw_ref, cb_ref, tl, HY_CONV, 0, HY_WIDTH)
    xb = _dwconv(xpad_ref, cw_ref, cb_ref, tl, HY_CONV, HY_WIDTH, 2 * HY_WIDTH)
    v = _dwconv(xpad_ref, cw_ref, cb_ref, tl, HY_CONV, 2 * HY_WIDTH, 3 * HY_WIDTH)
    u = xb * v
    xa_ref[0] = xa.astype(BF16)
    _to_lane_blocks(s_ref, u)
    u_ref[0] = _split_even_odd(s_ref).astype(BF16)


def _hy_tile_specs(L, tl):
    hb = tl // HALO
    nhb = L // HALO
    w = 3 * HY_WIDTH
    return [pl.BlockSpec((1, tl, w), lambda b, i: (b, i, C_HY // w)),
            pl.BlockSpec((1, HALO, w), lambda b, i: (b, jnp.maximum(i * hb - 1, 0), C_HY // w)),
            pl.BlockSpec((1, HALO, w), lambda b, i: (b, jnp.minimum((i + 1) * hb, nhb - 1), C_HY // w))]


def _hy_pre(u3, cw, cb, *, tl):
    B, L, _ = u3.shape
    nt = L // tl
    w = 3 * HY_WIDTH
    return pl.pallas_call(
        functools.partial(_hy_pre_kernel, nt=nt, tl=tl),
        grid=(B, nt),
        in_specs=_hy_tile_specs(L, tl) + [pl.BlockSpec((8, w), lambda b, i: (0, 0)),
                                          pl.BlockSpec((1, w), lambda b, i: (0, 0))],
        out_specs=[pl.BlockSpec((1, tl // 2, 2 * HY_WIDTH), lambda b, i: (b, i, 0)),
                   pl.BlockSpec((1, tl, HY_WIDTH), lambda b, i: (b, i, 0))],
        out_shape=[jax.ShapeDtypeStruct((B, L // 2, 2 * HY_WIDTH), BF16),
                   jax.ShapeDtypeStruct((B, L, HY_WIDTH), BF16)],
        scratch_shapes=[pltpu.VMEM((tl + 2 * HALO, w), BF16),
                        pltpu.VMEM((HY_WIDTH // LANES, tl, LANES), F32)],
        compiler_params=_cparams(("parallel", "parallel")),
        name="hy_pre",
    )(u3, u3, u3, cw, cb)


def _hy_fwd_kernel(a_ref, u_ref, f_ref, y_ref, *, hm):
    W = HY_WIDTH
    pq = jnp.dot(a_ref[...], u_ref[0], preferred_element_type=F32)
    pe, po = pq[0:hm, :W], pq[0:hm, W:]
    qe, qo = pq[hm:, :W], pq[hm:, W:]
    kpr, kpi, wr, wi, vr, vi = [f_ref[0, i] for i in range(6)]
    y_ref[0, 0:hm, :W] = (pe * kpr + qe * kpi + po * wr + qo * wi).astype(BF16)
    y_ref[0, hm:, :W] = (pe * kpi - qe * kpr + po * wi - qo * wr).astype(BF16)
    y_ref[0, 0:hm, W:] = (pe * vr + qe * vi + po * kpr + qo * kpi).astype(BF16)
    y_ref[0, hm:, W:] = (pe * vi - qe * vr + po * kpi - qo * kpr).astype(BF16)


def _hy_fwd(a_fwd, ueo, filt, *, hm):
    B, L2, W2 = ueo.shape
    L = 2 * L2
    tm = 2 * hm
    return pl.pallas_call(
        functools.partial(_hy_fwd_kernel, hm=hm),
        grid=(L // tm, B),
        in_specs=[pl.BlockSpec((tm, L2), lambda i, b: (i, 0)),
                  pl.BlockSpec((1, L2, W2), lambda i, b: (b, 0, 0)),
                  pl.BlockSpec((1, 6, hm, HY_WIDTH), lambda i, b: (i, 0, 0, 0))],
        out_specs=pl.BlockSpec((1, tm, W2), lambda i, b: (b, i, 0)),
        out_shape=jax.ShapeDtypeStruct((B, L, W2), BF16),
        compiler_params=_cparams(("parallel", "parallel")),
        name="hy_fwd",
    )(a_fwd, ueo, filt)


def _hy_inv_kernel(a_ref, y_ref, o_ref):
    o_ref[0] = jnp.dot(a_ref[...], y_ref[0], preferred_element_type=F32).astype(BF16)


def _hy_inv(a_inv, yspec, *, tm):
    B, L, W2 = yspec.shape
    L2 = L // 2
    return pl.pallas_call(
        _hy_inv_kernel,
        grid=(L2 // tm, B),
        in_specs=[pl.BlockSpec((tm, L), lambda i, b: (i, 0)),
                  pl.BlockSpec((1, L, W2), lambda i, b: (b, 0, 0))],
        out_specs=pl.BlockSpec((1, tm, W2), lambda i, b: (b, i, 0)),
        out_shape=jax.ShapeDtypeStruct((B, L2, W2), BF16),
        compiler_params=_cparams(("parallel", "parallel")),
        name="hy_inv",
    )(a_inv, yspec)


def _pad_rows(a, rows):
    return jnp.concatenate([a, jnp.zeros((rows - a.shape[0],) + a.shape[1:], a.dtype)], axis=0)


def _pad_cols(a, cols):
    return jnp.concatenate([a, jnp.zeros(a.shape[:-1] + (cols - a.shape[-1],), a.dtype)], axis=-1)


def _tile(n, pref):
    t = min(pref, n)
    assert n % t == 0
    return t


def _layer(x3, p, tables, *, final, final_g):
    B, L, _ = x3.shape
    T = B * L
    x2 = x3.reshape(T, D_MODEL)
    u, dt = _inproj(x2, p["norm_g"], p["w_in"], tm=_tile(T, 512), tn=512)
    u3 = u.reshape(B, L, N_IN)
    dt3 = dt.reshape(B, L, DT_PAD)

    ys = _ssd(u3, dt3, p["ssd_cw"], p["ssd_cb"], p["ssd_dtb"], p["ssd_alog"], p["rexp_f"],
              p["rexp_b"], p["ssd_dskip"], qb=_tile(L // CHUNK, 8))

    ya = _attention(u3, p["att_bias"], p["att_sink"], qb=_tile(L // ATT_BLOCK, 16))

    a_fwd, a_inv, filt, hm = tables
    tl = _tile(L, 1024)
    ueo, xa = _hy_pre(u3, p["hy_cw"], p["hy_cb"], tl=tl)
    yspec = _hy_fwd(a_fwd, ueo, filt, hm=hm)
    conv = _hy_inv(a_inv, yspec, tm=_tile(L // 2, 512))

    out = _outproj(x2, ys.reshape(T, SSD_WIDTH), ya.reshape(T, ATT_WIDTH),
                   conv.reshape(T // 2, 2 * HY_WIDTH), ueo.reshape(T // 2, 2 * HY_WIDTH),
                   xa.reshape(T, HY_WIDTH), u, p["hy_d"], p["w_out"], final_g,
                   tm=_tile(T, 512), final=final)
    return out.reshape(B, L, D_MODEL)


def kernel(x_prompt, x_sample, rel_bias, norm_g, w_in, ssd_conv_w, ssd_conv_b, ssd_dt_bias, ssd_a_log, ssd_d, ssd_norm_g, att_sink, att_norm_g, hy_conv_w, hy_conv_b, hy_w1, hy_b1, hy_w2, hy_b2, hy_w3, hy_b3, hy_w4, hy_freq, hy_d, hy_norm_g, w_out, final_norm_g):
    depth = w_in.shape[0]
    att_bias = _attn_bias(rel_bias)
    rexp_f, rexp_b = _ssd_expand_matrix(False), _ssd_expand_matrix(True)
    final_g = final_norm_g.astype(F32)[None, :]

    layers = []
    for i in range(depth):
        layers.append(dict(
            norm_g=norm_g[i].astype(F32)[None, :],
            w_in=_prep_w_in(w_in, i),
            ssd_cw=_pad_rows(ssd_conv_w[i].astype(F32), 8),
            ssd_cb=ssd_conv_b[i].astype(F32)[None, :],
            ssd_dtb=_pad_cols(ssd_dt_bias[i].astype(F32).reshape(1, 2 * SSD_HEADS), DT_PAD),
            ssd_alog=_pad_cols(ssd_a_log[i].astype(F32).reshape(1, 2 * SSD_HEADS), DT_PAD),
            ssd_dskip=jnp.repeat(ssd_d[i].astype(F32), SSD_HEAD_DIM)[None, :],
            rexp_f=rexp_f, rexp_b=rexp_b,
            att_bias=att_bias,
            att_sink=jnp.broadcast_to(att_sink[i].astype(F32)[:, None] * LOG2E, (ATT_HEADS, LANES)),
            hy_cw=_pad_rows(hy_conv_w[i].astype(F32), 8),
            hy_cb=hy_conv_b[i].astype(F32)[None, :],
            hy_d=hy_d[i].astype(F32)[None, :],
            w_out=_prep_w_out(w_out, jnp.concatenate([ssd_norm_g[i], att_norm_g[i], hy_norm_g[i]])
                              .astype(F32)[:, None], i),
        ))

    def trunk(x):
        L = x.shape[1]
        hm = min(512, L // 4)
        a_fwd, a_inv = _dft_tables(L, hm)
        for i in range(depth):
            xf = _hy_filter(L, hy_w1[i], hy_b1[i].astype(F32), hy_w2[i].astype(F32),
                            hy_b2[i].astype(F32), hy_w3[i].astype(F32), hy_b3[i].astype(F32),
                            hy_w4[i].astype(F32), hy_freq[i].astype(F32))
            filt = _hy_kspec(a_fwd, xf, hm=hm)
            x = _layer(x, layers[i], (a_fwd, a_inv, filt, hm), final=(i == depth - 1),
                       final_g=final_g)
        return x

    return (trunk(x_prompt), trunk(x_sample))
```
